```python
import math
import jax, jax.numpy as jnp
from jax import lax
import numpy as np

D_MODEL = 1024
BATCH = 8
SEQ = 2048
DEPTH = 4

N_Q_HEADS = 8
N_KV_HEADS = 2
HEAD_DIM = 64
Q_GROUP = N_Q_HEADS // N_KV_HEADS
WINDOW = 128
BLOCK = 128
ROPE_THETA = 500000.0
ROT_DIM = HEAD_DIM // 4
ATTN_WIDTH = N_Q_HEADS * HEAD_DIM
KV_WIDTH = N_KV_HEADS * HEAD_DIM
NEG_INF = -1e30
CONV_WIDTH = D_MODEL // 2
CONV_K = 3
SSM_WIDTH = D_MODEL // 2
SSM_GROUP = 16
SSM_GROUPS = SSM_WIDTH // SSM_GROUP
SSM_STATE = 64
DT_MIN = 1e-3
DT_MAX = 1e-1
N_BRANCH = 3
GATE_WIDTH = N_BRANCH * D_MODEL
FFN_HIDDEN = -(-8 * D_MODEL // (3 * 256)) * 256
NORM_EPS = 1e-6

IN_SIZES = (ATTN_WIDTH, KV_WIDTH, KV_WIDTH, CONV_WIDTH, CONV_WIDTH, CONV_WIDTH, SSM_WIDTH, GATE_WIDTH)
IN_COLS = sum(IN_SIZES)
IN_SPLITS = tuple(int(v) for v in np.cumsum(IN_SIZES)[:-1])

kernel_name = "hybrid_gated_swa_conv_s5_block"


def rmsnorm(x, g):
    xf = x.astype(jnp.float32)
    y = xf * lax.rsqrt(jnp.mean(xf * xf, axis=-1, keepdims=True) + NORM_EPS)
    return (y * g.astype(jnp.float32)).astype(x.dtype)


def rope_tables(seq_len):
    pos = jnp.arange(seq_len, dtype=jnp.float32)
    inv_freq = ROPE_THETA ** (-jnp.arange(0, ROT_DIM, 2, dtype=jnp.float32) / ROT_DIM)
    ang = pos[:, None] * inv_freq[None, :]
    return jnp.cos(ang), jnp.sin(ang)


def partial_rope(t, cos, sin):
    half = ROT_DIM // 2
    tf = t.astype(jnp.float32)
    t1, t2, rest = tf[..., :half], tf[..., half:ROT_DIM], tf[..., ROT_DIM:]
    c = cos[None, :, None, :]
    s = sin[None, :, None, :]
    out = jnp.concatenate([t1 * c - t2 * s, t2 * c + t1 * s, rest], axis=-1)
    return out.astype(t.dtype)


def sliding_window_attention(q, k, v, sinks):
    b, l = q.shape[0], q.shape[1]
    nb = l // BLOCK
    qb = q.reshape(b, nb, BLOCK, N_KV_HEADS, Q_GROUP, HEAD_DIM).astype(jnp.float32)

    def band(t):
        tp = jnp.pad(t, ((0, 0), (BLOCK, 0), (0, 0), (0, 0)))
        tp = tp.reshape(b, nb + 1, BLOCK, N_KV_HEADS, HEAD_DIM)
        return jnp.concatenate([tp[:, :-1], tp[:, 1:]], axis=2).astype(jnp.float32)

    kb, vb = band(k), band(v)
    s = jnp.einsum("bnqkgd,bnskd->bnkgqs", qb, kb) * (HEAD_DIM ** -0.5)
    qi = jnp.arange(BLOCK)[:, None]
    kj = jnp.arange(2 * BLOCK)[None, :]
    delta = qi + BLOCK - kj
    band_ok = (delta >= 0) & (delta < WINDOW)
    kpos = jnp.arange(nb)[:, None] * BLOCK - BLOCK + kj
    mask = band_ok[None, :, :] & (kpos >= 0)[:, None, :]
    s = jnp.where(mask[None, :, None, None, :, :], s, NEG_INF)
    sink = sinks.astype(jnp.float32).reshape(N_KV_HEADS, Q_GROUP)[None, None, :, :, None, None]
    m = jnp.maximum(jnp.max(s, axis=-1, keepdims=True), sink)
    p = jnp.exp(s - m)
    denom = jnp.sum(p, axis=-1, keepdims=True) + jnp.exp(sink - m)
    o = jnp.einsum("bnkgqs,bnskd->bnqkgd", p / denom, vb)
    return o.reshape(b, l, ATTN_WIDTH).astype(q.dtype)


def short_conv(z, w):
    l = z.shape[1]
    zp = jnp.pad(z, ((0, 0), (CONV_K - 1, 0), (0, 0)))
    y = w[0] * zp[:, 0:l]
    for j in range(1, CONV_K):
        y = y + w[j] * zp[:, j:j + l]
    return y


def s5_ssm(u, a_re, a_im, b_re, b_im, c_re, c_im, d, log_dt):
    bsz, l = u.shape[0], u.shape[1]
    uf = u.astype(jnp.float32).reshape(bsz, l, SSM_GROUPS, SSM_GROUP)
    lam = lax.complex(a_re.astype(jnp.float32), a_im.astype(jnp.float32))
    dt = jnp.exp(log_dt.astype(jnp.float32))[:, None]
    lam_bar = jnp.exp(lam * dt)
    b_c = lax.complex(b_re.astype(jnp.float32), b_im.astype(jnp.float32))
    b_bar = ((lam_bar - 1.0) / lam)[..., None] * b_c
    bu = jnp.einsum("blgh,gph->blgp", uf.astype(jnp.complex64), b_bar)
    a_elems = jnp.broadcast_to(lam_bar, bu.shape)

    def combine(e1, e2):
        a1, x1 = e1
        a2, x2 = e2
        return a1 * a2, a2 * x1 + x2

    _, states = lax.associative_scan(combine, (a_elems, bu), axis=1)
    c_c = lax.complex(c_re.astype(jnp.float32), c_im.astype(jnp.float32))
    y = jnp.einsum("blgp,ghp->blgh", states, c_c).real
    y = y + d.astype(jnp.float32).reshape(SSM_GROUPS, SSM_GROUP) * uf
    return y.reshape(bsz, l, SSM_WIDTH).astype(u.dtype)


def _fwd_setup_inputs(seed: int = 0) -> dict:
    key = jax.random.key(seed)
    ks = jax.random.split(key, 24)
    L = DEPTH

    def nrm(k, shape, fan_in):
        return jax.random.normal(k, shape, jnp.float32) * (fan_in ** -0.5)

    x = jax.random.normal(ks[0], (BATCH, SEQ, D_MODEL), jnp.float32)
    norm_mix = 1.0 + 0.02 * jax.random.normal(ks[1], (L, D_MODEL), jnp.float32)
    w_in = nrm(ks[2], (L, D_MODEL, IN_COLS), D_MODEL)
    b_gate = 0.02 * jax.random.normal(ks[3], (L, GATE_WIDTH), jnp.float32)
    attn_sinks = 0.5 * jax.random.normal(ks[4], (L, N_Q_HEADS), jnp.float32)
    w_attn_o = nrm(ks[5], (L, ATTN_WIDTH, D_MODEL), ATTN_WIDTH)
    conv_w = nrm(ks[6], (L, CONV_K, CONV_WIDTH), CONV_K)
    w_conv_o = nrm(ks[7], (L, CONV_WIDTH, D_MODEL), CONV_WIDTH)
    ssm_a_re = -0.5 + 0.01 * jax.random.normal(ks[8], (L, SSM_GROUPS, SSM_STATE), jnp.float32)
    ssm_a_im = (math.pi * jnp.arange(SSM_STATE, dtype=jnp.float32))[None, None, :] \
        + 0.01 * jax.random.normal(ks[9], (L, SSM_GROUPS, SSM_STATE), jnp.float32)
    ssm_b_re = nrm(ks[10], (L, SSM_GROUPS, SSM_STATE, SSM_GROUP), 2 * SSM_GROUP)
    ssm_b_im = nrm(ks[11], (L, SSM_GROUPS, SSM_STATE, SSM_GROUP), 2 * SSM_GROUP)
    ssm_c_re = nrm(ks[12], (L, SSM_GROUPS, SSM_GROUP, SSM_STATE), 2 * SSM_STATE)
    ssm_c_im = nrm(ks[13], (L, SSM_GROUPS, SSM_GROUP, SSM_STATE), 2 * SSM_STATE)
    ssm_d = jax.random.normal(ks[14], (L, SSM_WIDTH), jnp.float32)
    ssm_log_dt = jax.random.uniform(ks[15], (L, SSM_GROUPS), jnp.float32,
                                    minval=math.log(DT_MIN), maxval=math.log(DT_MAX))
    w_ssm_glu = nrm(ks[16], (L, SSM_WIDTH, SSM_WIDTH), SSM_WIDTH)
    w_ssm_o = nrm(ks[17], (L, SSM_WIDTH, D_MODEL), SSM_WIDTH)
    w_mix_o = nrm(ks[18], (L, D_MODEL, D_MODEL), D_MODEL)
    norm_ffn = 1.0 + 0.02 * jax.random.normal(ks[19], (L, D_MODEL), jnp.float32)
    w_ffn_in = nrm(ks[20], (L, D_MODEL, 2 * FFN_HIDDEN), D_MODEL)
    w_ffn_out = nrm(ks[21], (L, FFN_HIDDEN, D_MODEL), FFN_HIDDEN)
    norm_final = 1.0 + 0.02 * jax.random.normal(ks[22], (D_MODEL,), jnp.float32)
    return {"x": x, "norm_mix": norm_mix, "w_in": w_in, "b_gate": b_gate,
            "attn_sinks": attn_sinks, "w_attn_o": w_attn_o, "conv_w": conv_w, "w_conv_o": w_conv_o,
            "ssm_a_re": ssm_a_re, "ssm_a_im": ssm_a_im, "ssm_b_re": ssm_b_re, "ssm_b_im": ssm_b_im,
            "ssm_c_re": ssm_c_re, "ssm_c_im": ssm_c_im, "ssm_d": ssm_d, "ssm_log_dt": ssm_log_dt,
            "w_ssm_glu": w_ssm_glu, "w_ssm_o": w_ssm_o, "w_mix_o": w_mix_o, "norm_ffn": norm_ffn,
            "w_ffn_in": w_ffn_in, "w_ffn_out": w_ffn_out, "norm_final": norm_final}


def _fwd_reference(x, norm_mix, w_in, b_gate, attn_sinks, w_attn_o, conv_w, w_conv_o,
              ssm_a_re, ssm_a_im, ssm_b_re, ssm_b_im, ssm_c_re, ssm_c_im, ssm_d, ssm_log_dt,
              w_ssm_glu, w_ssm_o, w_mix_o, norm_ffn, w_ffn_in, w_ffn_out, norm_final):
    b, l = x.shape[0], x.shape[1]
    cos, sin = rope_tables(l)
    for i in range(DEPTH):
        h = rmsnorm(x, norm_mix[i])
        proj = h @ w_in[i]
        q, k, v, cb, cc, cx, u, g = jnp.split(proj, IN_SPLITS, axis=-1)
        q = partial_rope(q.reshape(b, l, N_Q_HEADS, HEAD_DIM), cos, sin)
        k = partial_rope(k.reshape(b, l, N_KV_HEADS, HEAD_DIM), cos, sin)
        v = v.reshape(b, l, N_KV_HEADS, HEAD_DIM)
        y_attn = sliding_window_attention(q, k, v, attn_sinks[i]) @ w_attn_o[i]
        y_conv = (cb * short_conv(cc * cx, conv_w[i])) @ w_conv_o[i]
        ys = jax.nn.gelu(s5_ssm(u, ssm_a_re[i], ssm_a_im[i], ssm_b_re[i], ssm_b_im[i],
                                ssm_c_re[i], ssm_c_im[i], ssm_d[i], ssm_log_dt[i]))
        y_ssm = (ys * jax.nn.sigmoid(ys @ w_ssm_glu[i])) @ w_ssm_o[i]
        gates = jax.nn.sigmoid(g + b_gate[i]).reshape(b, l, N_BRANCH, D_MODEL)
        merged = gates[:, :, 0] * y_attn + gates[:, :, 1] * y_conv + gates[:, :, 2] * y_ssm
        x = x + merged @ w_mix_o[i]
        h = rmsnorm(x, norm_ffn[i])
        gt, up = jnp.split(h @ w_ffn_in[i], 2, axis=-1)
        x = x + (jax.nn.silu(gt) * up) @ w_ffn_out[i]
    return rmsnorm(x, norm_final)


import jax as _jax
import jax.numpy as _jnp

TWIN_FORMAT = 'train_step'
FWD_PARAMS = ['x', 'norm_mix', 'w_in', 'b_gate', 'attn_sinks', 'w_attn_o', 'conv_w', 'w_conv_o', 'ssm_a_re', 'ssm_a_im', 'ssm_b_re', 'ssm_b_im', 'ssm_c_re', 'ssm_c_im', 'ssm_d', 'ssm_log_dt', 'w_ssm_glu', 'w_ssm_o', 'w_mix_o', 'norm_ffn', 'w_ffn_in', 'w_ffn_out', 'norm_final']
TWIN_WEIGHTS = ['norm_mix', 'w_in', 'b_gate', 'attn_sinks', 'w_attn_o', 'conv_w', 'w_conv_o', 'ssm_a_re', 'ssm_a_im', 'ssm_b_re', 'ssm_b_im', 'ssm_c_re', 'ssm_c_im', 'ssm_d', 'ssm_log_dt', 'w_ssm_glu', 'w_ssm_o', 'w_mix_o', 'norm_ffn', 'w_ffn_in', 'w_ffn_out', 'norm_final']
TWIN_DIFF_INPUT = 'x'
TWIN_INPUTS = ['x', 'norm_mix', 'w_in', 'b_gate', 'attn_sinks', 'w_attn_o', 'conv_w', 'w_conv_o', 'ssm_a_re', 'ssm_a_im', 'ssm_b_re', 'ssm_b_im', 'ssm_c_re', 'ssm_c_im', 'ssm_d', 'ssm_log_dt', 'w_ssm_glu', 'w_ssm_o', 'w_mix_o', 'norm_ffn', 'w_ffn_in', 'w_ffn_out', 'norm_final', 'loss_target', 'm_norm_mix', 'm_w_in', 'm_b_gate', 'm_attn_sinks', 'm_w_attn_o', 'm_conv_w', 'm_w_conv_o', 'm_ssm_a_re', 'm_ssm_a_im', 'm_ssm_b_re', 'm_ssm_b_im', 'm_ssm_c_re', 'm_ssm_c_im', 'm_ssm_d', 'm_ssm_log_dt', 'm_w_ssm_glu', 'm_w_ssm_o', 'm_w_mix_o', 'm_norm_ffn', 'm_w_ffn_in', 'm_w_ffn_out', 'm_norm_final', 'v_norm_mix', 'v_w_in', 'v_b_gate', 'v_attn_sinks', 'v_w_attn_o', 'v_conv_w', 'v_w_conv_o', 'v_ssm_a_re', 'v_ssm_a_im', 'v_ssm_b_re', 'v_ssm_b_im', 'v_ssm_c_re', 'v_ssm_c_im', 'v_ssm_d', 'v_ssm_log_dt', 'v_w_ssm_glu', 'v_w_ssm_o', 'v_w_mix_o', 'v_norm_ffn', 'v_w_ffn_in', 'v_w_ffn_out', 'v_norm_final']
TWIN_OUTPUTS = ['loss', 'grad_x', 'grad_norm_mix', 'grad_w_in', 'grad_b_gate', 'grad_attn_sinks', 'grad_w_attn_o', 'grad_conv_w', 'grad_w_conv_o', 'grad_ssm_a_re', 'grad_ssm_a_im', 'grad_ssm_b_re', 'grad_ssm_b_im', 'grad_ssm_c_re', 'grad_ssm_c_im', 'grad_ssm_d', 'grad_ssm_log_dt', 'grad_w_ssm_glu', 'grad_w_ssm_o', 'grad_w_mix_o', 'grad_norm_ffn', 'grad_w_ffn_in', 'grad_w_ffn_out', 'grad_norm_final', 'delta_norm_mix', 'delta_w_in', 'delta_b_gate', 'delta_attn_sinks', 'delta_w_attn_o', 'delta_conv_w', 'delta_w_conv_o', 'delta_ssm_a_re', 'delta_ssm_a_im', 'delta_ssm_b_re', 'delta_ssm_b_im', 'delta_ssm_c_re', 'delta_ssm_c_im', 'delta_ssm_d', 'delta_ssm_log_dt', 'delta_w_ssm_glu', 'delta_w_ssm_o', 'delta_w_mix_o', 'delta_norm_ffn', 'delta_w_ffn_in', 'delta_w_ffn_out', 'delta_norm_final', 'new_m_norm_mix', 'new_m_w_in', 'new_m_b_gate', 'new_m_attn_sinks', 'new_m_w_attn_o', 'new_m_conv_w', 'new_m_w_conv_o', 'new_m_ssm_a_re', 'new_m_ssm_a_im', 'new_m_ssm_b_re', 'new_m_ssm_b_im', 'new_m_ssm_c_re', 'new_m_ssm_c_im', 'new_m_ssm_d', 'new_m_ssm_log_dt', 'new_m_w_ssm_glu', 'new_m_w_ssm_o', 'new_m_w_mix_o', 'new_m_norm_ffn', 'new_m_w_ffn_in', 'new_m_w_ffn_out', 'new_m_norm_final', 'new_v_norm_mix', 'new_v_w_in', 'new_v_b_gate', 'new_v_attn_sinks', 'new_v_w_attn_o', 'new_v_conv_w', 'new_v_w_conv_o', 'new_v_ssm_a_re', 'new_v_ssm_a_im', 'new_v_ssm_b_re', 'new_v_ssm_b_im', 'new_v_ssm_c_re', 'new_v_ssm_c_im', 'new_v_ssm_d', 'new_v_ssm_log_dt', 'new_v_w_ssm_glu', 'new_v_w_ssm_o', 'new_v_w_mix_o', 'new_v_norm_ffn', 'new_v_w_ffn_in', 'new_v_w_ffn_out', 'new_v_norm_final']
TWIN_LEAF_KINDS = {'loss': 'loss', 'grad_x': 'grad_x', 'grad_norm_mix': 'grad_w', 'grad_w_in': 'grad_w', 'grad_b_gate': 'grad_w', 'grad_attn_sinks': 'grad_w', 'grad_w_attn_o': 'grad_w', 'grad_conv_w': 'grad_w', 'grad_w_conv_o': 'grad_w', 'grad_ssm_a_re': 'grad_w', 'grad_ssm_a_im': 'grad_w', 'grad_ssm_b_re': 'grad_w', 'grad_ssm_b_im': 'grad_w', 'grad_ssm_c_re': 'grad_w', 'grad_ssm_c_im': 'grad_w', 'grad_ssm_d': 'grad_w', 'grad_ssm_log_dt': 'grad_w', 'grad_w_ssm_glu': 'grad_w', 'grad_w_ssm_o': 'grad_w', 'grad_w_mix_o': 'grad_w', 'grad_norm_ffn': 'grad_w', 'grad_w_ffn_in': 'grad_w', 'grad_w_ffn_out': 'grad_w', 'grad_norm_final': 'grad_w', 'delta_norm_mix': 'delta_w', 'delta_w_in': 'delta_w', 'delta_b_gate': 'delta_w', 'delta_attn_sinks': 'delta_w', 'delta_w_attn_o': 'delta_w', 'delta_conv_w': 'delta_w', 'delta_w_conv_o': 'delta_w', 'delta_ssm_a_re': 'delta_w', 'delta_ssm_a_im': 'delta_w', 'delta_ssm_b_re': 'delta_w', 'delta_ssm_b_im': 'delta_w', 'delta_ssm_c_re': 'delta_w', 'delta_ssm_c_im': 'delta_w', 'delta_ssm_d': 'delta_w', 'delta_ssm_log_dt': 'delta_w', 'delta_w_ssm_glu': 'delta_w', 'delta_w_ssm_o': 'delta_w', 'delta_w_mix_o': 'delta_w', 'delta_norm_ffn': 'delta_w', 'delta_w_ffn_in': 'delta_w', 'delta_w_ffn_out': 'delta_w', 'delta_norm_final': 'delta_w', 'new_m_norm_mix': 'new_m', 'new_m_w_in': 'new_m', 'new_m_b_gate': 'new_m', 'new_m_attn_sinks': 'new_m', 'new_m_w_attn_o': 'new_m', 'new_m_conv_w': 'new_m', 'new_m_w_conv_o': 'new_m', 'new_m_ssm_a_re': 'new_m', 'new_m_ssm_a_im': 'new_m', 'new_m_ssm_b_re': 'new_m', 'new_m_ssm_b_im': 'new_m', 'new_m_ssm_c_re': 'new_m', 'new_m_ssm_c_im': 'new_m', 'new_m_ssm_d': 'new_m', 'new_m_ssm_log_dt': 'new_m', 'new_m_w_ssm_glu': 'new_m', 'new_m_w_ssm_o': 'new_m', 'new_m_w_mix_o': 'new_m', 'new_m_norm_ffn': 'new_m', 'new_m_w_ffn_in': 'new_m', 'new_m_w_ffn_out': 'new_m', 'new_m_norm_final': 'new_m', 'new_v_norm_mix': 'new_v', 'new_v_w_in': 'new_v', 'new_v_b_gate': 'new_v', 'new_v_attn_sinks': 'new_v', 'new_v_w_attn_o': 'new_v', 'new_v_conv_w': 'new_v', 'new_v_w_conv_o': 'new_v', 'new_v_ssm_a_re': 'new_v', 'new_v_ssm_a_im': 'new_v', 'new_v_ssm_b_re': 'new_v', 'new_v_ssm_b_im': 'new_v', 'new_v_ssm_c_re': 'new_v', 'new_v_ssm_c_im': 'new_v', 'new_v_ssm_d': 'new_v', 'new_v_ssm_log_dt': 'new_v', 'new_v_w_ssm_glu': 'new_v', 'new_v_w_ssm_o': 'new_v', 'new_v_w_mix_o': 'new_v', 'new_v_norm_ffn': 'new_v', 'new_v_w_ffn_in': 'new_v', 'new_v_w_ffn_out': 'new_v', 'new_v_norm_final': 'new_v'}


def _forward(args):
    return _fwd_reference(*[args[k] for k in FWD_PARAMS])


def _output_shape():
    out = _jax.eval_shape(lambda: _forward(_fwd_setup_inputs(0)))
    return out.shape, out.dtype

N_MICROBATCH = 1
ADAM_LR = 0.001
ADAM_B1 = 0.9
ADAM_B2 = 0.999
ADAM_EPS = 1e-08
ADAM_WD = 0.01
ADAM_STEP = 10
PER_EXAMPLE_BATCH_AXIS = {'x': 0, 'loss_target': 0}
SHARED_INPUTS = []
_WEIGHT_DTYPES = {'norm_mix': _jnp.float32, 'w_in': _jnp.float32, 'b_gate': _jnp.float32, 'attn_sinks': _jnp.float32, 'w_attn_o': _jnp.float32, 'conv_w': _jnp.float32, 'w_conv_o': _jnp.float32, 'ssm_a_re': _jnp.float32, 'ssm_a_im': _jnp.float32, 'ssm_b_re': _jnp.float32, 'ssm_b_im': _jnp.float32, 'ssm_c_re': _jnp.float32, 'ssm_c_im': _jnp.float32, 'ssm_d': _jnp.float32, 'ssm_log_dt': _jnp.float32, 'w_ssm_glu': _jnp.float32, 'w_ssm_o': _jnp.float32, 'w_mix_o': _jnp.float32, 'norm_ffn': _jnp.float32, 'w_ffn_in': _jnp.float32, 'w_ffn_out': _jnp.float32, 'norm_final': _jnp.float32}
MOMENT_SCALE = {'norm_mix': 1.382723e-01, 'w_in': 5.577338e-02, 'b_gate': 1.750552e-02, 'attn_sinks': 1.853290e-02, 'w_attn_o': 1.587494e-02, 'conv_w': 1.019890e-01, 'w_conv_o': 7.221719e-02, 'ssm_a_re': 1.982498e-03, 'ssm_a_im': 1.952140e-03, 'ssm_b_re': 1.238571e-03, 'ssm_b_im': 1.265259e-03, 'ssm_c_re': 2.504470e-03, 'ssm_c_im': 2.488421e-03, 'ssm_d': 3.858084e-02, 'ssm_log_dt': 1.812337e+00, 'w_ssm_glu': 1.030928e-02, 'w_ssm_o': 2.509513e-02, 'w_mix_o': 7.777395e-02, 'norm_ffn': 9.815728e-02, 'w_ffn_in': 4.059529e-02, 'w_ffn_out': 6.622689e-02, 'norm_final': 1.600407e+01}


def _to_microbatches(a, axis):
    t = _jnp.moveaxis(a, axis, 0)
    t = t.reshape((N_MICROBATCH, t.shape[0] // N_MICROBATCH) + t.shape[1:])
    return _jnp.moveaxis(t, 1, axis + 1)


def setup_inputs(seed: int = 0) -> dict:
    inp = _fwd_setup_inputs(seed)
    key = _jax.random.fold_in(_jax.random.key(seed), 7919)
    shape, _ = _output_shape()
    out = dict(inp)
    out["loss_target"] = _jax.random.normal(_jax.random.fold_in(key, 0), shape, _jnp.float32)
    for i, name in enumerate(TWIN_WEIGHTS):
        w = inp[name].astype(_jnp.float32)
        if MOMENT_SCALE is None:
            s = _jnp.sqrt(_jnp.mean(_jnp.square(w)) + 1e-30)
        else:
            s = MOMENT_SCALE[name]
        km, kv = _jax.random.split(_jax.random.fold_in(key, i + 1))
        out[name] = w
        out["m_" + name] = s * _jax.random.normal(km, w.shape, _jnp.float32)
        out["v_" + name] = (s * s) * _jax.random.uniform(kv, w.shape, _jnp.float32, 0.5, 1.5)
    if N_MICROBATCH > 1:
        for name, axis in PER_EXAMPLE_BATCH_AXIS.items():
            out[name] = _to_microbatches(out[name], axis)
    return {'x': out['x'], 'norm_mix': out['norm_mix'], 'w_in': out['w_in'], 'b_gate': out['b_gate'], 'attn_sinks': out['attn_sinks'], 'w_attn_o': out['w_attn_o'], 'conv_w': out['conv_w'], 'w_conv_o': out['w_conv_o'], 'ssm_a_re': out['ssm_a_re'], 'ssm_a_im': out['ssm_a_im'], 'ssm_b_re': out['ssm_b_re'], 'ssm_b_im': out['ssm_b_im'], 'ssm_c_re': out['ssm_c_re'], 'ssm_c_im': out['ssm_c_im'], 'ssm_d': out['ssm_d'], 'ssm_log_dt': out['ssm_log_dt'], 'w_ssm_glu': out['w_ssm_glu'], 'w_ssm_o': out['w_ssm_o'], 'w_mix_o': out['w_mix_o'], 'norm_ffn': out['norm_ffn'], 'w_ffn_in': out['w_ffn_in'], 'w_ffn_out': out['w_ffn_out'], 'norm_final': out['norm_final'], 'loss_target': out['loss_target'], 'm_norm_mix': out['m_norm_mix'], 'm_w_in': out['m_w_in'], 'm_b_gate': out['m_b_gate'], 'm_attn_sinks': out['m_attn_sinks'], 'm_w_attn_o': out['m_w_attn_o'], 'm_conv_w': out['m_conv_w'], 'm_w_conv_o': out['m_w_conv_o'], 'm_ssm_a_re': out['m_ssm_a_re'], 'm_ssm_a_im': out['m_ssm_a_im'], 'm_ssm_b_re': out['m_ssm_b_re'], 'm_ssm_b_im': out['m_ssm_b_im'], 'm_ssm_c_re': out['m_ssm_c_re'], 'm_ssm_c_im': out['m_ssm_c_im'], 'm_ssm_d': out['m_ssm_d'], 'm_ssm_log_dt': out['m_ssm_log_dt'], 'm_w_ssm_glu': out['m_w_ssm_glu'], 'm_w_ssm_o': out['m_w_ssm_o'], 'm_w_mix_o': out['m_w_mix_o'], 'm_norm_ffn': out['m_norm_ffn'], 'm_w_ffn_in': out['m_w_ffn_in'], 'm_w_ffn_out': out['m_w_ffn_out'], 'm_norm_final': out['m_norm_final'], 'v_norm_mix': out['v_norm_mix'], 'v_w_in': out['v_w_in'], 'v_b_gate': out['v_b_gate'], 'v_attn_sinks': out['v_attn_sinks'], 'v_w_attn_o': out['v_w_attn_o'], 'v_conv_w': out['v_conv_w'], 'v_w_conv_o': out['v_w_conv_o'], 'v_ssm_a_re': out['v_ssm_a_re'], 'v_ssm_a_im': out['v_ssm_a_im'], 'v_ssm_b_re': out['v_ssm_b_re'], 'v_ssm_b_im': out['v_ssm_b_im'], 'v_ssm_c_re': out['v_ssm_c_re'], 'v_ssm_c_im': out['v_ssm_c_im'], 'v_ssm_d': out['v_ssm_d'], 'v_ssm_log_dt': out['v_ssm_log_dt'], 'v_w_ssm_glu': out['v_w_ssm_glu'], 'v_w_ssm_o': out['v_w_ssm_o'], 'v_w_mix_o': out['v_w_mix_o'], 'v_norm_ffn': out['v_norm_ffn'], 'v_w_ffn_in': out['v_w_ffn_in'], 'v_w_ffn_out': out['v_w_ffn_out'], 'v_norm_final': out['v_norm_final']}


def _loss(weights, diff, rest, loss_target):
    with _jax.named_scope("forward"):
        args = {**rest, TWIN_DIFF_INPUT: diff, **{k: w.astype(_WEIGHT_DTYPES[k]) for k, w in weights.items()}}
        y = _forward(args)
    with _jax.named_scope("loss_head"):
        err = _jnp.square(y.astype(_jnp.float32) - loss_target)
        return 0.5 * _jnp.sum(_jnp.mean(err, axis=-1)) if err.ndim else 0.5 * err


def _adamw(w, g, m, v):
    m = ADAM_B1 * m + (1.0 - ADAM_B1) * g
    v = ADAM_B2 * v + (1.0 - ADAM_B2) * _jnp.square(g)
    m_hat = m / (1.0 - ADAM_B1 ** ADAM_STEP)
    v_hat = v / (1.0 - ADAM_B2 ** ADAM_STEP)
    delta = -ADAM_LR * (m_hat / (_jnp.sqrt(v_hat) + ADAM_EPS) + ADAM_WD * w)
    return delta, m, v


def reference(x, norm_mix, w_in, b_gate, attn_sinks, w_attn_o, conv_w, w_conv_o, ssm_a_re, ssm_a_im, ssm_b_re, ssm_b_im, ssm_c_re, ssm_c_im, ssm_d, ssm_log_dt, w_ssm_glu, w_ssm_o, w_mix_o, norm_ffn, w_ffn_in, w_ffn_out, norm_final, loss_target, m_norm_mix, m_w_in, m_b_gate, m_attn_sinks, m_w_attn_o, m_conv_w, m_w_conv_o, m_ssm_a_re, m_ssm_a_im, m_ssm_b_re, m_ssm_b_im, m_ssm_c_re, m_ssm_c_im, m_ssm_d, m_ssm_log_dt, m_w_ssm_glu, m_w_ssm_o, m_w_mix_o, m_norm_ffn, m_w_ffn_in, m_w_ffn_out, m_norm_final, v_norm_mix, v_w_in, v_b_gate, v_attn_sinks, v_w_attn_o, v_conv_w, v_w_conv_o, v_ssm_a_re, v_ssm_a_im, v_ssm_b_re, v_ssm_b_im, v_ssm_c_re, v_ssm_c_im, v_ssm_d, v_ssm_log_dt, v_w_ssm_glu, v_w_ssm_o, v_w_mix_o, v_norm_ffn, v_w_ffn_in, v_w_ffn_out, v_norm_final):
    given = dict(x=x, norm_mix=norm_mix, w_in=w_in, b_gate=b_gate, attn_sinks=attn_sinks, w_attn_o=w_attn_o, conv_w=conv_w, w_conv_o=w_conv_o, ssm_a_re=ssm_a_re, ssm_a_im=ssm_a_im, ssm_b_re=ssm_b_re, ssm_b_im=ssm_b_im, ssm_c_re=ssm_c_re, ssm_c_im=ssm_c_im, ssm_d=ssm_d, ssm_log_dt=ssm_log_dt, w_ssm_glu=w_ssm_glu, w_ssm_o=w_ssm_o, w_mix_o=w_mix_o, norm_ffn=norm_ffn, w_ffn_in=w_ffn_in, w_ffn_out=w_ffn_out, norm_final=norm_final, loss_target=loss_target, m_norm_mix=m_norm_mix, m_w_in=m_w_in, m_b_gate=m_b_gate, m_attn_sinks=m_attn_sinks, m_w_attn_o=m_w_attn_o, m_conv_w=m_conv_w, m_w_conv_o=m_w_conv_o, m_ssm_a_re=m_ssm_a_re, m_ssm_a_im=m_ssm_a_im, m_ssm_b_re=m_ssm_b_re, m_ssm_b_im=m_ssm_b_im, m_ssm_c_re=m_ssm_c_re, m_ssm_c_im=m_ssm_c_im, m_ssm_d=m_ssm_d, m_ssm_log_dt=m_ssm_log_dt, m_w_ssm_glu=m_w_ssm_glu, m_w_ssm_o=m_w_ssm_o, m_w_mix_o=m_w_mix_o, m_norm_ffn=m_norm_ffn, m_w_ffn_in=m_w_ffn_in, m_w_ffn_out=m_w_ffn_out, m_norm_final=m_norm_final, v_norm_mix=v_norm_mix, v_w_in=v_w_in, v_b_gate=v_b_gate, v_attn_sinks=v_attn_sinks, v_w_attn_o=v_w_attn_o, v_conv_w=v_conv_w, v_w_conv_o=v_w_conv_o, v_ssm_a_re=v_ssm_a_re, v_ssm_a_im=v_ssm_a_im, v_ssm_b_re=v_ssm_b_re, v_ssm_b_im=v_ssm_b_im, v_ssm_c_re=v_ssm_c_re, v_ssm_c_im=v_ssm_c_im, v_ssm_d=v_ssm_d, v_ssm_log_dt=v_ssm_log_dt, v_w_ssm_glu=v_w_ssm_glu, v_w_ssm_o=v_w_ssm_o, v_w_mix_o=v_w_mix_o, v_norm_ffn=v_norm_ffn, v_w_ffn_in=v_w_ffn_in, v_w_ffn_out=v_w_ffn_out, v_norm_final=v_norm_final)
    weights = {n: given[n] for n in TWIN_WEIGHTS}
    shared = {n: given[n] for n in SHARED_INPUTS}
    per_example = {n: given[n] for n in ['x']}
    grad_fn = _jax.value_and_grad(_loss, argnums=(0, 1))

    def one_microbatch(ex, loss_target):
        ex = dict(ex)
        diff = ex.pop(TWIN_DIFF_INPUT)
        return grad_fn(weights, diff, {**shared, **ex}, loss_target)

    if N_MICROBATCH == 1:
        loss, (grad_w, grad_x) = one_microbatch(per_example, given["loss_target"])
    else:
        def body(carry, xs):
            loss_sum, grad_sum = carry
            l_k, (gw_k, gx_k) = one_microbatch(xs[0], xs[1])
            with _jax.named_scope("update"):
                return (loss_sum + l_k, _jax.tree.map(_jnp.add, grad_sum, gw_k)), gx_k

        init = (_jnp.zeros((), _jnp.float32), _jax.tree.map(_jnp.zeros_like, weights))
        (loss, grad_w), grad_x = _jax.lax.scan(body, init, (per_example, given["loss_target"]))
    with _jax.named_scope("update"):
        delta_w, new_m, new_v = {}, {}, {}
        for n in TWIN_WEIGHTS:
            delta_w[n], new_m[n], new_v[n] = _adamw(weights[n], grad_w[n], given["m_" + n], given["v_" + n])
    return (loss, grad_x, *[grad_w[n] for n in TWIN_WEIGHTS], *[delta_w[n] for n in TWIN_WEIGHTS],
            *[new_m[n] for n in TWIN_WEIGHTS], *[new_v[n] for n in TWIN_WEIGHTS])
```

```python
import functools
import math

import jax
import jax.numpy as jnp
import numpy as np
from jax import lax
from jax.experimental import pallas as pl
from jax.experimental.pallas import tpu as pltpu

F32 = jnp.float32
BF16 = jnp.bfloat16

D_MODEL = 1024
DEPTH = 4
N_Q_HEADS = 8
N_KV_HEADS = 2
HEAD_DIM = 64
Q_GROUP = N_Q_HEADS // N_KV_HEADS
WINDOW = 128
BLOCK = 128
ROPE_THETA = 500000.0
ROT_DIM = HEAD_DIM // 4
ATTN_WIDTH = N_Q_HEADS * HEAD_DIM
KV_WIDTH = N_KV_HEADS * HEAD_DIM
NEG_INF = -1e30
CONV_WIDTH = 512
CONV_K = 3
SSM_WIDTH = 512
SSM_GROUP = 16
SSM_GROUPS = 32
SSM_STATE = 64
SSM_LANES = SSM_GROUPS * SSM_STATE
GATE_WIDTH = 3 * D_MODEL
FFN_HIDDEN = 2816
NORM_EPS = 1e-6
IN_COLS = 5888
C_Q, C_K, C_V, C_CB, C_CC, C_CX, C_U, C_G = 0, 512, 640, 768, 1280, 1792, 2304, 2816

ADAM_LR = 0.001
ADAM_B1 = 0.9
ADAM_B2 = 0.999
ADAM_EPS = 1e-08
ADAM_WD = 0.01
ADAM_STEP = 10

N_CHIPS = 4
N_DEV = 8
MESH_ID = pl.DeviceIdType.MESH

VMEM_LIMIT_BYTES = 48 * 1024 * 1024
LANE = 128
SUBLANE = 8
SCAN_ROWS = 8
SCAN_CHUNK = 128

BIG = ("w_in", "w_attn_o", "w_conv_o", "w_ssm_glu", "w_ssm_o", "w_mix_o", "w_ffn_in", "w_ffn_out")
BIG_AXIS = {"w_in": 2, "w_attn_o": 2, "w_conv_o": 2, "w_ssm_glu": 1, "w_ssm_o": 2, "w_mix_o": 1,
            "w_ffn_in": 2, "w_ffn_out": 1}
SMALL = ("norm_mix", "b_gate", "attn_sinks", "ssm_a_re", "ssm_a_im", "ssm_b_re", "ssm_b_im",
         "ssm_c_re", "ssm_c_im", "ssm_d", "ssm_log_dt", "norm_ffn", "norm_final")
WEIGHTS = ("norm_mix", "w_in", "b_gate", "attn_sinks", "w_attn_o", "conv_w", "w_conv_o", "ssm_a_re",
           "ssm_a_im", "ssm_b_re", "ssm_b_im", "ssm_c_re", "ssm_c_im", "ssm_d", "ssm_log_dt",
           "w_ssm_glu", "w_ssm_o", "w_mix_o", "norm_ffn", "w_ffn_in", "w_ffn_out", "norm_final")
ARG_NAMES = ("x",) + WEIGHTS + ("loss_target",) + tuple("m_" + n for n in WEIGHTS) + tuple(
    "v_" + n for n in WEIGHTS)


def _params(*sem):
    return pltpu.CompilerParams(dimension_semantics=sem if sem else None,
                                vmem_limit_bytes=VMEM_LIMIT_BYTES)


def _tile(dim, cap, align):
    t = min(cap, dim) // align * align
    while t >= align:
        if dim % t == 0:
            return t
        t -= align
    return dim


_DOT_DIMS = {"nn": (((1,), (0,)), ((), ())), "nt": (((1,), (1,)), ((), ())), "tn": (((0,), (0,)), ((), ()))}


def _mm(a, b, mode, name, out_dtype=F32, add=None, tm_cap=512, tn_cap=3072, tk_cap=1024):
    if mode == "nn":
        (m, k), (k2, n) = a.shape, b.shape
    elif mode == "nt":
        (m, k), (n, k2) = a.shape, b.shape
    else:
        (k, m), (k2, n) = a.shape, b.shape
    assert k == k2, (name, a.shape, b.shape)
    tm, tn, tk = _tile(m, tm_cap, LANE), _tile(n, tn_cap, LANE), _tile(k, tk_cap, LANE)
    nk = k // tk
    dims = _DOT_DIMS[mode]

    def body(a_ref, b_ref, *rest):
        if add is None:
            o_ref, acc = rest
        else:
            add_ref, o_ref, acc = rest
        kk = pl.program_id(2)

        @pl.when(kk == 0)
        def _():
            acc[...] = jnp.zeros_like(acc)

        acc[...] += lax.dot_general(a_ref[...].astype(BF16), b_ref[...].astype(BF16), dims,
                                    preferred_element_type=F32)

        @pl.when(kk == nk - 1)
        def _():
            r = acc[...]
            if add is not None:
                r = r + add_ref[...]
            o_ref[...] = r.astype(out_dtype)

    if mode == "tn":
        a_spec = pl.BlockSpec((tk, tm), lambda i, j, kk: (kk, i))
    else:
        a_spec = pl.BlockSpec((tm, tk), lambda i, j, kk: (i, kk))
    if mode == "nt":
        b_spec = pl.BlockSpec((tn, tk), lambda i, j, kk: (j, kk))
    else:
        b_spec = pl.BlockSpec((tk, tn), lambda i, j, kk: (kk, j))
    o_spec = pl.BlockSpec((tm, tn), lambda i, j, kk: (i, j))
    in_specs, args = [a_spec, b_spec], [a, b]
    if add is not None:
        in_specs.append(o_spec)
        args.append(add)
    return pl.pallas_call(
        body, name=name, grid=(m // tm, n // tn, nk), in_specs=in_specs, out_specs=o_spec,
        out_shape=jax.ShapeDtypeStruct((m, n), out_dtype),
        scratch_shapes=[pltpu.VMEM((tm, tn), F32)],
        compiler_params=_params("parallel", "parallel", "arbitrary"),
    )(*args)


def _rowwise(fn, rows, pars, outs, accs, name, tm_cap=256):
    length = rows[0][0].shape[0]
    tm = _tile(length, tm_cap, LANE)
    n = length // tm
    in_specs, args, counts = [], [], []
    for arr, c0, cw, shift in rows:
        bw = math.gcd(c0, cw) if c0 else cw
        assert bw % LANE == 0 or (c0 == 0 and cw == arr.shape[1]), (name, c0, cw)
        cnt = cw // bw
        counts.append(cnt)
        for j in range(cnt):
            in_specs.append(pl.BlockSpec(
                (tm, bw), lambda i, j=j, c0=c0, bw=bw, shift=shift: (jnp.clip(i + shift, 0, n - 1), c0 // bw + j)))
            args.append(arr)
    for p in pars:
        in_specs.append(pl.BlockSpec(p.shape, lambda i: (0, 0)))
        args.append(p)
    out_shape = [jax.ShapeDtypeStruct((length, w), dt) for w, dt in outs]
    out_specs = [pl.BlockSpec((tm, w), lambda i: (i, 0)) for w, _ in outs]
    out_shape += [jax.ShapeDtypeStruct((r, w), F32) for r, w in accs]
    out_specs += [pl.BlockSpec((r, w), lambda i: (0, 0)) for r, w in accs]
    n_in, n_out = len(args), len(outs)

    def body(*refs):
        i = pl.program_id(0)
        vals, p = [], 0
        for cnt in counts:
            blocks = [refs[p + j][...] for j in range(cnt)]
            p += cnt
            vals.append(blocks[0] if cnt == 1 else jnp.concatenate(blocks, axis=1))
        for _ in pars:
            vals.append(refs[p][...])
            p += 1
        res = fn((i, n), *vals)
        out_refs = refs[n_in:n_in + n_out]
        acc_refs = refs[n_in + n_out:]
        for r, v in zip(out_refs, res[:n_out]):
            r[...] = v.astype(r.dtype)
        if acc_refs:
            @pl.when(i == 0)
            def _():
                for r in acc_refs:
                    r[...] = jnp.zeros_like(r)
            for r, v in zip(acc_refs, res[n_out:]):
                r[...] += v

    res = pl.pallas_call(
        body, name=name, grid=(n,), in_specs=in_specs, out_specs=out_specs, out_shape=out_shape,
        compiler_params=_params("arbitrary"),
    )(*args)
    return res


def _rms(x, g):
    return x * lax.rsqrt(jnp.mean(x * x, axis=-1, keepdims=True) + NORM_EPS) * g


def _rms_fwd(x, g, name):
    return _rowwise(lambda ctx, xv, gv: (_rms(xv, gv),), [(x, 0, D_MODEL, 0)], [g],
                    [(D_MODEL, BF16)], [], name)[0]


def _rms_bwd(x, g, dh, dres, name):
    def fn(ctx, xv, dhv, drv, gv):
        _, vjp = jax.vjp(_rms, xv, gv)
        dx, dg = vjp(dhv)
        return dx + drv, dg
    return _rowwise(fn, [(x, 0, D_MODEL, 0), (dh, 0, D_MODEL, 0), (dres, 0, D_MODEL, 0)], [g],
                    [(D_MODEL, F32)], [(1, D_MODEL)], name)


def _rope_tables(length):
    pos = jnp.arange(length, dtype=F32)
    inv_freq = ROPE_THETA ** (-jnp.arange(0, ROT_DIM, 2, dtype=F32) / ROT_DIM)
    ang = pos[:, None] * inv_freq[None, :]
    cos, sin = jnp.cos(ang), jnp.sin(ang)
    half = ROT_DIM // 2
    ones = jnp.ones((length, HEAD_DIM - ROT_DIM), F32)
    zeros = jnp.zeros_like(ones)
    zh = jnp.zeros((length, half), F32)
    c64 = jnp.concatenate([cos, cos, ones], axis=1)
    s1 = jnp.concatenate([-sin, zh, zeros], axis=1)
    s2 = jnp.concatenate([zh, sin, zeros], axis=1)
    tile2 = lambda t: jnp.concatenate([t, t], axis=1)
    return tile2(c64), tile2(s1), tile2(s2)


def _lane_chunks(t):
    return [t[:, j * LANE:(j + 1) * LANE] for j in range(t.shape[1] // LANE)]


def _rope(t, c, s1, s2, n_rot):
    half = ROT_DIM // 2
    out = []
    for j, ch in enumerate(_lane_chunks(t)):
        if j < n_rot:
            ch = ch * c + pltpu.roll(ch, LANE - half, 1) * s1 + pltpu.roll(ch, half, 1) * s2
        out.append(ch)
    return jnp.concatenate(out, axis=1)


def _unrope(d, c, s1, s2, n_rot):
    half = ROT_DIM // 2
    out = []
    for j, ch in enumerate(_lane_chunks(d)):
        if j < n_rot:
            ch = ch * c + pltpu.roll(ch * s1, half, 1) + pltpu.roll(ch * s2, LANE - half, 1)
        out.append(ch)
    return jnp.concatenate(out, axis=1)


N_ROT_CHUNKS = (ATTN_WIDTH + KV_WIDTH) // LANE
QKV_WIDTH = ATTN_WIDTH + 2 * KV_WIDTH


def _split_fwd(proj, tabs, name):
    def fn(ctx, t, c, s1, s2):
        return (_rope(t, c, s1, s2, N_ROT_CHUNKS),)
    rows = [(proj, 0, QKV_WIDTH, 0)] + [(t, 0, LANE, 0) for t in tabs]
    return _rowwise(fn, rows, [], [(QKV_WIDTH, BF16)], [], name, tm_cap=BLOCK)[0]


def _split_bwd(dq, dkc, dkp, dvc, dvp, tabs, name):
    def fn(ctx, dqv, dkcv, dkpv, dvcv, dvpv, c, s1, s2):
        i, n = ctx
        keep = (i < n - 1).astype(F32)
        d = jnp.concatenate([dqv, dkcv + keep * dkpv, dvcv + keep * dvpv], axis=1)
        return (_unrope(d, c, s1, s2, N_ROT_CHUNKS),)
    rows = [(dq, 0, ATTN_WIDTH, 0), (dkc, 0, KV_WIDTH, 0), (dkp, 0, KV_WIDTH, 1), (dvc, 0, KV_WIDTH, 0),
            (dvp, 0, KV_WIDTH, 1)] + [(t, 0, LANE, 0) for t in tabs]
    return _rowwise(fn, rows, [], [(QKV_WIDTH, BF16)], [], name, tm_cap=BLOCK)[0]


def _att_scores(q_ref, kp_ref, kc_ref, sink_ref):
    n = pl.program_id(1)
    j = pl.program_id(0)
    rows = Q_GROUP * BLOCK
    qs = q_ref[...].reshape(rows, HEAD_DIM)
    kb = jnp.concatenate([kp_ref[0], kc_ref[0]], axis=0)
    s = lax.dot_general(qs, kb, _DOT_DIMS["nt"], preferred_element_type=F32) * (HEAD_DIM ** -0.5)
    r = lax.broadcasted_iota(jnp.int32, (rows, 2 * BLOCK), 0)
    kj = lax.broadcasted_iota(jnp.int32, (rows, 2 * BLOCK), 1)
    delta = (r % BLOCK) + BLOCK - kj
    ok = (delta >= 0) & (delta < WINDOW) & ((kj >= BLOCK) | (n > 0))
    s = jnp.where(ok, s, NEG_INF)
    rh = lax.broadcasted_iota(jnp.int32, (rows, 1), 0) // BLOCK
    sinks = sink_ref[...]
    lane = lax.broadcasted_iota(jnp.int32, sinks.shape, 1)
    srow = lax.broadcasted_iota(jnp.int32, sinks.shape, 0)
    sink = jnp.zeros((rows, 1), F32)
    for g in range(Q_GROUP):
        val = jnp.sum(jnp.where((lane == g) & (srow == j), sinks, 0.0), keepdims=True)
        sink = jnp.where(rh == g, val, sink)
    m = jnp.maximum(jnp.max(s, axis=-1, keepdims=True), sink)
    p = jnp.exp(s - m)
    psink = jnp.exp(sink - m)
    denom = jnp.sum(p, axis=-1, keepdims=True) + psink
    return qs, kb, p / denom, psink / denom, rh


def _att_specs(length):
    nb = length // BLOCK
    q_spec = pl.BlockSpec((Q_GROUP, BLOCK, HEAD_DIM), lambda j, n: (j, n, 0))
    prev = pl.BlockSpec((1, BLOCK, HEAD_DIM), lambda j, n: (j, jnp.maximum(n - 1, 0), 0))
    cur = pl.BlockSpec((1, BLOCK, HEAD_DIM), lambda j, n: (j, n, 0))
    sink_spec = pl.BlockSpec((N_KV_HEADS, Q_GROUP), lambda j, n: (0, 0))
    return nb, q_spec, prev, cur, sink_spec


def _att_fwd(q, k, v, sinks, name):
    length = q.shape[1]
    nb, q_spec, prev, cur, sink_spec = _att_specs(length)

    def body(q_ref, kp_ref, kc_ref, vp_ref, vc_ref, sink_ref, o_ref):
        _, _, p, _, _ = _att_scores(q_ref, kp_ref, kc_ref, sink_ref)
        vb = jnp.concatenate([vp_ref[0], vc_ref[0]], axis=0)
        o = jnp.dot(p.astype(BF16), vb, preferred_element_type=F32)
        o_ref[...] = o.reshape(Q_GROUP, BLOCK, HEAD_DIM).astype(o_ref.dtype)

    return pl.pallas_call(
        body, name=name, grid=(N_KV_HEADS, nb),
        in_specs=[q_spec, prev, cur, prev, cur, sink_spec], out_specs=q_spec,
        out_shape=jax.ShapeDtypeStruct((N_Q_HEADS, length, HEAD_DIM), BF16),
        compiler_params=_params("arbitrary", "arbitrary"),
    )(q, k, k, v, v, sinks)


def _att_bwd(q, k, v, sinks, do, name):
    length = q.shape[1]
    nb, q_spec, prev, cur, sink_spec = _att_specs(length)
    ds_spec = pl.BlockSpec((1, SUBLANE, LANE), lambda j, n: (j, 0, 0))

    def body(q_ref, kp_ref, kc_ref, vp_ref, vc_ref, sink_ref, do_ref, dq_ref, dkc_ref, dkp_ref, dvc_ref,
             dvp_ref, dsink_ref):
        n = pl.program_id(1)
        qs, kb, p, psink, rh = _att_scores(q_ref, kp_ref, kc_ref, sink_ref)
        vb = jnp.concatenate([vp_ref[0], vc_ref[0]], axis=0)
        dob = do_ref[...].reshape(Q_GROUP * BLOCK, HEAD_DIM).astype(BF16)
        dv = lax.dot_general(p.astype(BF16), dob, _DOT_DIMS["tn"], preferred_element_type=F32)
        dp = lax.dot_general(dob, vb, _DOT_DIMS["nt"], preferred_element_type=F32)
        dsum = jnp.sum(p * dp, axis=-1, keepdims=True)
        ds = (p * (dp - dsum) * (HEAD_DIM ** -0.5)).astype(BF16)
        dq = jnp.dot(ds, kb, preferred_element_type=F32)
        dk = lax.dot_general(ds, qs, _DOT_DIMS["tn"], preferred_element_type=F32)
        dq_ref[...] = dq.reshape(Q_GROUP, BLOCK, HEAD_DIM)
        dkp_ref[0] = dk[:BLOCK]
        dkc_ref[0] = dk[BLOCK:]
        dvp_ref[0] = dv[:BLOCK]
        dvc_ref[0] = dv[BLOCK:]
        dsr = -psink * dsum
        row = lax.broadcasted_iota(jnp.int32, (SUBLANE, LANE), 0)
        upd = jnp.zeros((SUBLANE, LANE), F32)
        for g in range(Q_GROUP):
            val = jnp.sum(jnp.where(rh == g, dsr, 0.0), keepdims=True)
            upd = jnp.where(row == g, val, upd)

        @pl.when(n == 0)
        def _():
            dsink_ref[...] = jnp.zeros_like(dsink_ref)

        dsink_ref[0] += upd

    kv_shape = jax.ShapeDtypeStruct((N_KV_HEADS, length, HEAD_DIM), F32)
    return pl.pallas_call(
        body, name=name, grid=(N_KV_HEADS, nb),
        in_specs=[q_spec, prev, cur, prev, cur, sink_spec, q_spec],
        out_specs=[q_spec, cur, cur, cur, cur, ds_spec],
        out_shape=[jax.ShapeDtypeStruct((N_Q_HEADS, length, HEAD_DIM), F32), kv_shape, kv_shape, kv_shape,
                   kv_shape, jax.ShapeDtypeStruct((N_KV_HEADS, SUBLANE, LANE), F32)],
        compiler_params=_params("arbitrary", "arbitrary"),
    )(q, k, k, v, v, sinks, do)


def _to_heads(t, heads):
    return t.reshape(t.shape[0], heads, HEAD_DIM).transpose(1, 0, 2)


def _from_heads(t):
    return t.transpose(1, 0, 2).reshape(t.shape[1], t.shape[0] * HEAD_DIM)


def _shift_down(z, s):
    t = lax.broadcasted_iota(jnp.int32, z.shape, 0)
    return jnp.where(t >= s, pltpu.roll(z, s, 0), 0.0)


def _shift_up(z, s):
    t = lax.broadcasted_iota(jnp.int32, z.shape, 0)
    return jnp.where(t < z.shape[0] - s, pltpu.roll(z, z.shape[0] - s, 0), 0.0)


def _conv_specs(length):
    col = lambda c0: pl.BlockSpec((length, LANE), lambda j, c0=c0: (0, c0 // LANE + j))
    w_spec = pl.BlockSpec((CONV_K, LANE), lambda j: (0, j))
    o_spec = pl.BlockSpec((length, LANE), lambda j: (0, j))
    return col, w_spec, o_spec


def _conv_fwd(proj, w, name):
    length = proj.shape[0]
    col, w_spec, o_spec = _conv_specs(length)

    def body(cb_ref, cc_ref, cx_ref, w_ref, o_ref):
        z = cc_ref[...] * cx_ref[...]
        s = w_ref[0:1, :] * _shift_down(z, 2) + w_ref[1:2, :] * _shift_down(z, 1) + w_ref[2:3, :] * z
        o_ref[...] = (cb_ref[...] * s).astype(o_ref.dtype)

    return pl.pallas_call(
        body, name=name, grid=(CONV_WIDTH // LANE,),
        in_specs=[col(C_CB), col(C_CC), col(C_CX), w_spec], out_specs=o_spec,
        out_shape=jax.ShapeDtypeStruct((length, CONV_WIDTH), BF16),
        compiler_params=_params("arbitrary"),
    )(proj, proj, proj, w)


def _conv_bwd(proj, w, dy, name):
    length = proj.shape[0]
    col, w_spec, o_spec = _conv_specs(length)
    dw_spec = pl.BlockSpec((1, LANE), lambda j: (0, j))

    def body(cb_ref, cc_ref, cx_ref, w_ref, dy_ref, dcb_ref, dcc_ref, dcx_ref, dw0_ref, dw1_ref, dw2_ref):
        cc, cx, dyv = cc_ref[...], cx_ref[...], dy_ref[...]
        z = cc * cx
        w0, w1, w2 = w_ref[0:1, :], w_ref[1:2, :], w_ref[2:3, :]
        z1, z2 = _shift_down(z, 1), _shift_down(z, 2)
        s = w0 * z2 + w1 * z1 + w2 * z
        dcb_ref[...] = (dyv * s).astype(dcb_ref.dtype)
        ds = dyv * cb_ref[...]
        dw0_ref[...] = jnp.sum(ds * z2, axis=0, keepdims=True)
        dw1_ref[...] = jnp.sum(ds * z1, axis=0, keepdims=True)
        dw2_ref[...] = jnp.sum(ds * z, axis=0, keepdims=True)
        dz = w2 * ds + w1 * _shift_up(ds, 1) + w0 * _shift_up(ds, 2)
        dcc_ref[...] = (dz * cx).astype(dcc_ref.dtype)
        dcx_ref[...] = (dz * cc).astype(dcx_ref.dtype)

    act = jax.ShapeDtypeStruct((length, CONV_WIDTH), BF16)
    dws = jax.ShapeDtypeStruct((1, CONV_WIDTH), F32)
    return pl.pallas_call(
        body, name=name, grid=(CONV_WIDTH // LANE,),
        in_specs=[col(C_CB), col(C_CC), col(C_CX), w_spec, o_spec],
        out_specs=[o_spec, o_spec, o_spec, dw_spec, dw_spec, dw_spec],
        out_shape=[act, act, act, dws, dws, dws],
        compiler_params=_params("arbitrary"),
    )(proj, proj, proj, w, dy)


def _cmul(ar, ai, br, bi):
    return ar * br - ai * bi, ar * bi + ai * br


def _scan_tables(lr, li, reverse):
    pr, pi = [lr], [li]
    for _ in range(SCAN_ROWS - 1):
        nr, ni = _cmul(pr[-1], pi[-1], lr, li)
        pr.append(nr)
        pi.append(ni)
    row = jnp.arange(SCAN_ROWS)[:, None]
    mr, mi = [], []
    for s in (1, 2, 4):
        live = (row + s < SCAN_ROWS) if reverse else (row >= s)
        mr.append(jnp.where(live, pr[s - 1], 0.0))
        mi.append(jnp.where(live, pi[s - 1], 0.0))
    order = range(SCAN_ROWS - 1, -1, -1) if reverse else range(SCAN_ROWS)
    carry_r = jnp.concatenate([pr[d] for d in order], axis=0)
    carry_i = jnp.concatenate([pi[d] for d in order], axis=0)
    return jnp.stack(mr), jnp.stack(mi), carry_r, carry_i


def _scan(b, lr, li, reverse, name):
    length = b.shape[0]
    mr, mi, cr, ci = _scan_tables(lr, li, reverse)
    nchunk = length // SCAN_CHUNK
    nblk = SCAN_CHUNK // SCAN_ROWS
    half = SSM_LANES

    def body(b_ref, mr_ref, mi_ref, cr_ref, ci_ref, o_ref, carry):
        @pl.when(pl.program_id(0) == 0)
        def _():
            carry[...] = jnp.zeros_like(carry)

        c_r, c_i = carry[0:1, :], carry[1:2, :]
        blocks = range(nblk - 1, -1, -1) if reverse else range(nblk)
        for blk in blocks:
            r0 = blk * SCAN_ROWS
            xr = b_ref[r0:r0 + SCAN_ROWS, :half]
            xi = b_ref[r0:r0 + SCAN_ROWS, half:]
            for kk, s in enumerate((1, 2, 4)):
                sh = SCAN_ROWS - s if reverse else s
                rr, ri = pltpu.roll(xr, sh, 0), pltpu.roll(xi, sh, 0)
                ar, ai = _cmul(mr_ref[kk], mi_ref[kk], rr, ri)
                xr, xi = xr + ar, xi + ai
            ar, ai = _cmul(cr_ref[...], ci_ref[...], c_r, c_i)
            xr, xi = xr + ar, xi + ai
            o_ref[r0:r0 + SCAN_ROWS, :half] = xr
            o_ref[r0:r0 + SCAN_ROWS, half:] = xi
            edge = r0 if reverse else r0 + SCAN_ROWS - 1
            c_r = o_ref[edge:edge + 1, :half]
            c_i = o_ref[edge:edge + 1, half:]
        carry[0:1, :] = c_r
        carry[1:2, :] = c_i

    chunk = (lambda i: (nchunk - 1 - i, 0)) if reverse else (lambda i: (i, 0))
    blk_spec = pl.BlockSpec((SCAN_CHUNK, 2 * half), chunk)
    m_spec = pl.BlockSpec((3, SCAN_ROWS, half), lambda i: (0, 0, 0))
    c_spec = pl.BlockSpec((SCAN_ROWS, half), lambda i: (0, 0))
    return pl.pallas_call(
        body, name=name, grid=(nchunk,), in_specs=[blk_spec, m_spec, m_spec, c_spec, c_spec],
        out_specs=blk_spec, out_shape=jax.ShapeDtypeStruct(b.shape, F32),
        scratch_shapes=[pltpu.VMEM((SUBLANE, half), F32)],
        compiler_params=_params("arbitrary"),
    )(b, mr, mi, cr, ci)


def _dlam(g, states, name):
    half = SSM_LANES

    def fn(ctx, gv, xv, xprev):
        i, _ = ctx
        row = lax.broadcasted_iota(jnp.int32, xv.shape, 0)
        last = jnp.sum(jnp.where(lax.broadcasted_iota(jnp.int32, xprev.shape, 0) == xprev.shape[0] - 1,
                                 xprev, 0.0), axis=0, keepdims=True)
        last = last * (i > 0).astype(F32)
        xs = jnp.where(row == 0, last, pltpu.roll(xv, 1, 0))
        gr, gi, xr, xi = gv[:, :half], gv[:, half:], xs[:, :half], xs[:, half:]
        dre = jnp.sum(gr * xr + gi * xi, axis=0, keepdims=True)
        dim = jnp.sum(gi * xr - gr * xi, axis=0, keepdims=True)
        return (jnp.concatenate([dre, dim], axis=1),)

    rows = [(g, 0, 2 * half, 0), (states, 0, 2 * half, 0), (states, 0, 2 * half, -1)]
    return _rowwise(fn, rows, [], [], [(1, 2 * half)], name, tm_cap=128)[0]


def _block_diag(t):
    g, a, b = t.shape
    eye = jnp.eye(g, dtype=t.dtype)
    return (t[:, :, None, :] * eye[:, None, :, None]).reshape(g * a, g * b)


def _ssm_prep(a_re, a_im, b_re, b_im, c_re, c_im, log_dt):
    dt = jnp.exp(log_dt)[:, None]
    er = jnp.exp(a_re * dt)
    lr, li = er * jnp.cos(a_im * dt), er * jnp.sin(a_im * dt)
    nr, ni = lr - 1.0, li
    den = a_re * a_re + a_im * a_im
    qr, qi = (nr * a_re + ni * a_im) / den, (ni * a_re - nr * a_im) / den
    bbr = qr[..., None] * b_re - qi[..., None] * b_im
    bbi = qr[..., None] * b_im + qi[..., None] * b_re
    bmat = jnp.concatenate([_block_diag(bbr.transpose(0, 2, 1)), _block_diag(bbi.transpose(0, 2, 1))], axis=1)
    cmat = jnp.concatenate([_block_diag(c_re.transpose(0, 2, 1)), -_block_diag(c_im.transpose(0, 2, 1))], axis=0)
    return lr.reshape(1, SSM_LANES), li.reshape(1, SSM_LANES), bmat, cmat


def _ssm_act(yc, u, d):
    return jax.nn.gelu(yc + d * u)


def _glu(ys, z):
    return ys * jax.nn.sigmoid(z)


def _merge(ya, yc, ys, gl, b):
    gates = jax.nn.sigmoid(gl + b)
    return gates[:, :D_MODEL] * ya + gates[:, D_MODEL:2 * D_MODEL] * yc + gates[:, 2 * D_MODEL:] * ys


def _swiglu(gu):
    return jax.nn.silu(gu[:, :FFN_HIDDEN]) * gu[:, FFN_HIDDEN:]


def _loss_fn(x, g, t):
    e = _rms(x, g) - t
    per_tok = jnp.mean(e * e, axis=-1, keepdims=True)
    return 0.5 * jnp.sum(per_tok, axis=0, keepdims=True)


def _vjp_rowwise(f, n_row, cot_dtype=F32):
    def fn(ctx, *vals):
        prim = vals[:n_row] + vals[n_row + 1:]
        _, vjp = jax.vjp(f, *prim)
        return vjp(vals[n_row].astype(cot_dtype))
    return fn


def _layer_fwd(i, x, w, tabs):
    nm = lambda s: "l%d_%s" % (i, s)
    sv = {"x": x}
    h = _rms_fwd(x, w["norm_mix"][i:i + 1], nm("rms_mix"))
    proj = _mm(h, w["w_in"][i], "nn", nm("mm_in"))
    qkv = _split_fwd(proj, tabs, nm("rope"))
    q, k, v = (_to_heads(qkv[:, :C_K], N_Q_HEADS), _to_heads(qkv[:, C_K:C_V], N_KV_HEADS),
               _to_heads(qkv[:, C_V:C_CB], N_KV_HEADS))
    sinks = w["attn_sinks"][i].reshape(N_KV_HEADS, Q_GROUP)
    att = _from_heads(_att_fwd(q, k, v, sinks, nm("att")))
    conv = _conv_fwd(proj, w["conv_w"][i], nm("conv"))
    lr, li, bmat, cmat = w["ssm"][i]
    u = proj[:, C_U:C_G]
    bu = _mm(u, bmat, "nn", nm("mm_bu"), tn_cap=1024)
    states = _scan(bu, lr, li, False, nm("scan"))
    yc = _mm(states, cmat, "nn", nm("mm_c"))
    d = w["ssm_d"][i:i + 1]
    ys = _rowwise(lambda ctx, a, b, c: (_ssm_act(a, b, c),), [(yc, 0, SSM_WIDTH, 0), (u, 0, SSM_WIDTH, 0)], [d],
                  [(SSM_WIDTH, F32)], [], nm("ssm_act"))[0]
    z = _mm(ys, w["w_ssm_glu"][i], "nn", nm("mm_glu"))
    sg = _rowwise(lambda ctx, a, b: (_glu(a, b),), [(ys, 0, SSM_WIDTH, 0), (z, 0, SSM_WIDTH, 0)], [],
                  [(SSM_WIDTH, BF16)], [], nm("glu"))[0]
    ya = _mm(att, w["w_attn_o"][i], "nn", nm("mm_ao"))
    yv = _mm(conv, w["w_conv_o"][i], "nn", nm("mm_co"))
    ym = _mm(sg, w["w_ssm_o"][i], "nn", nm("mm_so"))
    bg = w["b_gate"][i:i + 1]
    merged = _rowwise(lambda ctx, a, b, c, gl, bb: (_merge(a, b, c, gl, bb),),
                      [(ya, 0, D_MODEL, 0), (yv, 0, D_MODEL, 0), (ym, 0, D_MODEL, 0), (proj, C_G, GATE_WIDTH, 0)],
                      [bg], [(D_MODEL, BF16)], [], nm("merge"))[0]
    x1 = _mm(merged, w["w_mix_o"][i], "nn", nm("mm_mix"), add=x)
    h2 = _rms_fwd(x1, w["norm_ffn"][i:i + 1], nm("rms_ffn"))
    gu = _mm(h2, w["w_ffn_in"][i], "nn", nm("mm_ffn_in"))
    act = _rowwise(lambda ctx, a: (_swiglu(a),), [(gu, 0, 2 * FFN_HIDDEN, 0)], [], [(FFN_HIDDEN, BF16)], [],
                   nm("swiglu"))[0]
    x2 = _mm(act, w["w_ffn_out"][i], "nn", nm("mm_ffn_out"), add=x1)
    sv.update(h=h, proj=proj, q=q, k=k, v=v, att=att, conv=conv, u=u, states=states, yc=yc, ys=ys, z=z, sg=sg,
              ya=ya, yv=yv, ym=ym, merged=merged, x1=x1, h2=h2, gu=gu, act=act)
    return x2, sv


def _layer_bwd(i, dx2, sv, w, tabs):
    nm = lambda s: "l%d_b_%s" % (i, s)
    g = {}
    dact = _mm(dx2, w["w_ffn_out"][i], "nt", nm("mm_dact"))
    g["w_ffn_out"] = _mm(sv["act"], dx2, "tn", nm("mm_gw_ffn_out"))
    dgu = _rowwise(_vjp_rowwise(_swiglu, 1), [(sv["gu"], 0, 2 * FFN_HIDDEN, 0), (dact, 0, FFN_HIDDEN, 0)], [],
                   [(2 * FFN_HIDDEN, BF16)], [], nm("swiglu"))[0]
    dh2 = _mm(dgu, w["w_ffn_in"][i], "nt", nm("mm_dh2"))
    g["w_ffn_in"] = _mm(sv["h2"], dgu, "tn", nm("mm_gw_ffn_in"))
    dx1, g["norm_ffn"] = _rms_bwd(sv["x1"], w["norm_ffn"][i:i + 1], dh2, dx2, nm("rms_ffn"))
    dmerged = _mm(dx1, w["w_mix_o"][i], "nt", nm("mm_dmerged"))
    g["w_mix_o"] = _mm(sv["merged"], dx1, "tn", nm("mm_gw_mix"))
    proj = sv["proj"]
    bg = w["b_gate"][i:i + 1]
    dya, dyv, dym, dgl, g["b_gate"] = _rowwise(
        _vjp_rowwise(_merge, 4),
        [(sv["ya"], 0, D_MODEL, 0), (sv["yv"], 0, D_MODEL, 0), (sv["ym"], 0, D_MODEL, 0),
         (proj, C_G, GATE_WIDTH, 0), (dmerged, 0, D_MODEL, 0)], [bg],
        [(D_MODEL, BF16), (D_MODEL, BF16), (D_MODEL, BF16), (GATE_WIDTH, BF16)], [(1, GATE_WIDTH)], nm("merge"))
    dsg = _mm(dym, w["w_ssm_o"][i], "nt", nm("mm_dsg"))
    g["w_ssm_o"] = _mm(sv["sg"], dym, "tn", nm("mm_gw_so"))
    dys0, dz = _rowwise(_vjp_rowwise(_glu, 2), [(sv["ys"], 0, SSM_WIDTH, 0), (sv["z"], 0, SSM_WIDTH, 0),
                                                 (dsg, 0, SSM_WIDTH, 0)], [],
                        [(SSM_WIDTH, F32), (SSM_WIDTH, BF16)], [], nm("glu"))
    dys = _mm(dz, w["w_ssm_glu"][i], "nt", nm("mm_dys"), add=dys0)
    g["w_ssm_glu"] = _mm(sv["ys"], dz, "tn", nm("mm_gw_glu"))
    d = w["ssm_d"][i:i + 1]
    dyc, du0, g["ssm_d"] = _rowwise(
        _vjp_rowwise(_ssm_act, 2), [(sv["yc"], 0, SSM_WIDTH, 0), (sv["u"], 0, SSM_WIDTH, 0), (dys, 0, SSM_WIDTH, 0)],
        [d], [(SSM_WIDTH, F32), (SSM_WIDTH, F32)], [(1, SSM_WIDTH)], nm("ssm_act"))
    lr, li, bmat, cmat = w["ssm"][i]
    dstates = _mm(dyc, cmat, "nt", nm("mm_dstates"), tn_cap=1024)
    g_cmat = _mm(sv["states"], dyc, "tn", nm("mm_gc"))
    gs = _scan(dstates, lr, -li, True, nm("scan"))
    g_lam = _dlam(gs, sv["states"], nm("dlam"))
    du = _mm(gs, bmat, "nt", nm("mm_du"), out_dtype=BF16, add=du0)
    g_bmat = _mm(sv["u"], gs, "tn", nm("mm_gb"), tn_cap=1024)
    g["ssm"] = (g_lam[:, :SSM_LANES], g_lam[:, SSM_LANES:], g_bmat, g_cmat)
    dconv = _mm(dyv, w["w_conv_o"][i], "nt", nm("mm_dconv"))
    g["w_conv_o"] = _mm(sv["conv"], dyv, "tn", nm("mm_gw_co"))
    dcb, dcc, dcx, dw0, dw1, dw2 = _conv_bwd(proj, w["conv_w"][i], dconv, nm("conv"))
    g["conv_w"] = jnp.concatenate([dw0, dw1, dw2], axis=0)
    datt = _mm(dya, w["w_attn_o"][i], "nt", nm("mm_datt"))
    g["w_attn_o"] = _mm(sv["att"], dya, "tn", nm("mm_gw_ao"))
    sinks = w["attn_sinks"][i].reshape(N_KV_HEADS, Q_GROUP)
    dq, dkc, dkp, dvc, dvp, dsk = _att_bwd(sv["q"], sv["k"], sv["v"], sinks, _to_heads(datt, N_Q_HEADS), nm("att"))
    g["attn_sinks"] = dsk[:, :Q_GROUP, 0].reshape(N_Q_HEADS)
    dqkv = _split_bwd(_from_heads(dq), _from_heads(dkc), _from_heads(dkp), _from_heads(dvc), _from_heads(dvp),
                      tabs, nm("rope"))
    dproj = jnp.concatenate([dqkv, dcb, dcc, dcx, du, dgl], axis=1)
    dh = _mm(dproj, w["w_in"][i], "nt", nm("mm_dh"))
    g["w_in"] = _mm(sv["h"], dproj, "tn", nm("mm_gw_in"))
    dx, g["norm_mix"] = _rms_bwd(sv["x"], w["norm_mix"][i:i + 1], dh, dx1, nm("rms_mix"))
    return dx, g


def _local_step(x, target, w):
    length = x.shape[0]
    tabs = _rope_tables(length)
    ssm_names = ("ssm_a_re", "ssm_a_im", "ssm_b_re", "ssm_b_im", "ssm_c_re", "ssm_c_im", "ssm_log_dt")
    w = dict(w)
    preps = [jax.vjp(_ssm_prep, *[w[n][i] for n in ssm_names]) for i in range(DEPTH)]
    w["ssm"] = [p[0] for p in preps]
    saved = []
    for i in range(DEPTH):
        x, sv = _layer_fwd(i, x, w, tabs)
        saved.append(sv)

    def loss_fn(ctx, xv, tv, gv):
        val, vjp = jax.vjp(_loss_fn, xv, gv, tv)
        dx, dg, _ = vjp(jnp.ones((1, 1), F32))
        return dx, dg, val + jnp.zeros((1, LANE), F32)

    gfin = w["norm_final"].reshape(1, D_MODEL)
    dx, g_final, loss = _rowwise(loss_fn, [(x, 0, D_MODEL, 0), (target, 0, D_MODEL, 0)], [gfin],
                                 [(D_MODEL, F32)], [(1, D_MODEL), (1, LANE)], "loss")
    layer_grads = [None] * DEPTH
    for i in reversed(range(DEPTH)):
        dx, layer_grads[i] = _layer_bwd(i, dx, saved[i], w, tabs)
    grads = {}
    for n in layer_grads[0]:
        if n != "ssm":
            grads[n] = jnp.stack([lg[n] for lg in layer_grads])
    ssm_g = [preps[i][1](layer_grads[i]["ssm"]) for i in range(DEPTH)]
    for j, n in enumerate(ssm_names):
        grads[n] = jnp.stack([sg[j] for sg in ssm_g])
    grads["norm_final"] = g_final.reshape(D_MODEL)
    for n in ("norm_mix", "norm_ffn", "b_gate", "ssm_d"):
        grads[n] = grads[n].reshape(grads[n].shape[0], -1)
    return loss, dx, grads


COLS = 1024
ANY_SPEC = pl.BlockSpec(memory_space=pl.ANY)


def _place():
    return lax.axis_index("x"), lax.axis_index("y"), lax.axis_index("c")


def _other_chips(x, y):
    return [(1 - x, y), (x, 1 - y), (1 - x, 1 - y)]


def _remote(src, dst, send_sems, recv_sems, k, to):
    return pltpu.make_async_remote_copy(src_ref=src, dst_ref=dst, send_sem=send_sems.at[k], recv_sem=recv_sems.at[k],
                                        device_id=to, device_id_type=MESH_ID)


def _comm_call(body, name, out_shape, n_sems, args):
    return pl.pallas_call(
        body, name=name, out_shape=out_shape, in_specs=[ANY_SPEC] * len(args),
        out_specs=jax.tree.map(lambda _: ANY_SPEC, out_shape),
        scratch_shapes=[pltpu.SemaphoreType.DMA((n_sems,)), pltpu.SemaphoreType.DMA((n_sems,)),
                        pltpu.SemaphoreType.DMA],
    )(*args)


def _gather_weights(flat, name):
    rows = flat.shape[0]
    half = rows // 2

    def body(src_ref, out_ref, send_sems, recv_sems, local_sem):
        x, y, c = _place()
        me, sibling = (x, y, c), (x, y, 1 - c)
        chips = _other_chips(x, y)

        def blk(chip, hc):
            return out_ref.at[2 * chip[0] + chip[1], pl.ds(hc * half, half), :]

        mine = pltpu.make_async_copy(src_ref, out_ref.at[2 * x + y], local_sem)
        mine.start()
        first = [_remote(src_ref.at[pl.ds(c * half, half), :], blk((x, y), c), send_sems, recv_sems, j, (*chip, c))
                 for j, chip in enumerate(chips)]
        for cp in first:
            cp.start()
        passed = [_remote(blk(chip, c), blk(chip, c), send_sems, recv_sems, 3 + j, sibling)
                  for j, chip in enumerate(chips)]
        for j, chip in enumerate(chips):
            _remote(blk(chip, c), blk(chip, c), send_sems, recv_sems, j, me).wait_recv()
            passed[j].start()
        for j, chip in enumerate(chips):
            _remote(blk(chip, 1 - c), blk(chip, 1 - c), send_sems, recv_sems, 3 + j, me).wait_recv()
        for cp in first + passed:
            cp.wait_send()
        mine.wait()

    return _comm_call(body, name, jax.ShapeDtypeStruct((N_CHIPS,) + flat.shape, flat.dtype), 6, [flat])


def _swap_halves(gbuf, name):
    nsh, rows, _ = gbuf.shape
    half = rows // 2

    def body(g_ref, out_ref, send_sems, recv_sems, local_sem):
        x, y, c = _place()
        cps = [_remote(g_ref.at[s, pl.ds((1 - c) * half, half), :], out_ref.at[s], send_sems, recv_sems, s,
                       (x, y, 1 - c)) for s in range(nsh)]
        for cp in cps:
            cp.start()
        for cp in cps:
            cp.wait()

    return _comm_call(body, name, jax.ShapeDtypeStruct((nsh, half, COLS), gbuf.dtype), nsh, [gbuf])


def _exchange_shards(part, name):
    nsh, rows, _ = part.shape

    def body(p_ref, out_ref, send_sems, recv_sems, local_sem):
        x, y, c = _place()
        cps = [_remote(p_ref.at[2 * chip[0] + chip[1]], out_ref.at[j], send_sems, recv_sems, j, (*chip, c))
               for j, chip in enumerate(_other_chips(x, y))]
        for cp in cps:
            cp.start()
        for cp in cps:
            cp.wait()

    return _comm_call(body, name, jax.ShapeDtypeStruct((nsh - 1, rows, COLS), part.dtype), nsh - 1, [part])


def _join_halves(red, name):
    half = red.shape[0]

    def body(r_ref, out_ref, send_sems, recv_sems, local_sem):
        x, y, c = _place()
        mine = pltpu.make_async_copy(r_ref, out_ref.at[pl.ds(c * half, half), :], local_sem)
        mine.start()
        cp = _remote(r_ref, out_ref.at[pl.ds(c * half, half), :], send_sems, recv_sems, 0, (x, y, 1 - c))
        cp.start()
        _remote(r_ref, out_ref.at[pl.ds((1 - c) * half, half), :], send_sems, recv_sems, 0, (x, y, c)).wait_recv()
        cp.wait_send()
        mine.wait()

    return _comm_call(body, name, jax.ShapeDtypeStruct((2 * half, COLS), red.dtype), 1, [red])


def _gather_all(buf, name):
    def body(src_ref, out_ref, send_sems, recv_sems, local_sem):
        x, y, c = _place()
        flips = [(fx, fy, fc) for fx in (0, 1) for fy in (0, 1) for fc in (0, 1)][1:]

        def peer(f):
            return tuple(1 - v if fl else v for v, fl in zip((x, y, c), f))

        def slot(p):
            return out_ref.at[4 * p[0] + 2 * p[1] + p[2]]

        mine = pltpu.make_async_copy(src_ref, slot((x, y, c)), local_sem)
        mine.start()
        cps = [_remote(src_ref, slot((x, y, c)), send_sems, recv_sems, k, peer(f)) for k, f in enumerate(flips)]
        for cp in cps:
            cp.start()
        for k, f in enumerate(flips):
            _remote(src_ref, slot(peer(f)), send_sems, recv_sems, k, (x, y, c)).wait_recv()
        for cp in cps:
            cp.wait_send()
        mine.wait()

    return _comm_call(body, name, jax.ShapeDtypeStruct((N_DEV,) + buf.shape, buf.dtype), N_DEV - 1, [buf])


def _sum_windows(parts, nsh, rows, name):
    tr = _tile(rows, 512, SUBLANE)
    nblk = rows // tr
    wins = jnp.stack([jnp.asarray(wd, jnp.int32) for _, wd in parts])

    def body(win_ref, *refs):
        acc = refs[0][...]
        for r in refs[1:-1]:
            acc = acc + r[...]
        refs[-1][...] = acc

    in_specs = [pl.BlockSpec((1, tr, COLS), lambda s, i, win, k=k: (s, win[k] * nblk + i, 0))
                for k in range(len(parts))]
    return pl.pallas_call(
        body, name=name, out_shape=jax.ShapeDtypeStruct((nsh, rows, COLS), F32),
        grid_spec=pltpu.PrefetchScalarGridSpec(
            num_scalar_prefetch=1, grid=(nsh, nblk), in_specs=in_specs,
            out_specs=pl.BlockSpec((1, tr, COLS), lambda s, i, win: (s, i, 0))),
        compiler_params=_params("arbitrary", "arbitrary"),
    )(wins, *[arr for arr, _ in parts])


def _reduce_scatter(gbuf):
    x, y, c = _place()
    nsh, rows, _ = gbuf.shape
    half = rows // 2
    theirs = _swap_halves(gbuf, "rs_swap_halves")
    pair = _sum_windows([(gbuf, c), (theirs, 0)], nsh, half, "rs_sum_pair")
    others = _exchange_shards(pair, "rs_exchange")
    red = _sum_windows([(pair.reshape(1, nsh * half, COLS), 2 * x + y)] +
                       [(others.reshape(1, (nsh - 1) * half, COLS), j) for j in range(nsh - 1)], 1, half,
                       "rs_sum_chips")
    return _join_halves(red[0], "rs_join")


def _all_reduce_small(buf):
    rows = buf.shape[0]
    everyone = _gather_all(buf, "small_gather").reshape(1, N_DEV * rows, COLS)
    return _sum_windows([(everyone, k) for k in range(N_DEV)], 1, rows, "small_sum")[0]


def _adamw(wt, g, m, v, name):
    n = wt.size
    cols = COLS if n % COLS == 0 else n
    r = n // cols
    tr = _tile(r, 512, SUBLANE)

    def body(w_ref, g_ref, m_ref, v_ref, d_ref, nm_ref, nv_ref):
        gv = g_ref[...]
        mn = ADAM_B1 * m_ref[...] + (1.0 - ADAM_B1) * gv
        vn = ADAM_B2 * v_ref[...] + (1.0 - ADAM_B2) * jnp.square(gv)
        m_hat = mn / (1.0 - ADAM_B1 ** ADAM_STEP)
        v_hat = vn / (1.0 - ADAM_B2 ** ADAM_STEP)
        d_ref[...] = -ADAM_LR * (m_hat / (jnp.sqrt(v_hat) + ADAM_EPS) + ADAM_WD * w_ref[...])
        nm_ref[...] = mn
        nv_ref[...] = vn

    spec = pl.BlockSpec((tr, cols), lambda i: (i, 0))
    shp = jax.ShapeDtypeStruct((r, cols), F32)
    res = pl.pallas_call(
        body, name=name, grid=(r // tr,), in_specs=[spec] * 4, out_specs=[spec] * 3, out_shape=[shp] * 3,
        compiler_params=_params("parallel"),
    )(*[t.reshape(r, cols) for t in (wt, g, m, v)])
    return [t.reshape(wt.shape) for t in res]


def _pack_local(ws):
    return jnp.concatenate([ws[n].reshape(-1) for n in BIG]).reshape(-1, COLS)


def _unpack_local(flat, shapes):
    flat, out, off = flat.reshape(-1), {}, 0
    for n in BIG:
        size = math.prod(shapes[n])
        out[n] = flat[off:off + size].reshape(shapes[n])
        off += size
    return out


def _unpack_gathered(gath, shapes):
    flat, out, off = gath.reshape(N_CHIPS, -1), {}, 0
    for n in BIG:
        depth, a, b = shapes[n]
        size = depth * a * b
        piece = flat[:, off:off + size].reshape(N_CHIPS, depth, a, b)
        off += size
        if BIG_AXIS[n] == 2:
            out[n] = piece.transpose(1, 2, 0, 3).reshape(depth, a, N_CHIPS * b)
        else:
            out[n] = piece.transpose(1, 0, 2, 3).reshape(depth, N_CHIPS * a, b)
    return out


def _pack_grads(grads, shapes):
    pieces = []
    for n in BIG:
        depth, a, b = shapes[n]
        g = grads[n]
        if BIG_AXIS[n] == 2:
            p = g.reshape(depth, a, N_CHIPS, b).transpose(2, 0, 1, 3)
        else:
            p = g.reshape(depth, N_CHIPS, a, b).transpose(1, 0, 2, 3)
        pieces.append(p.reshape(N_CHIPS, -1))
    return jnp.concatenate(pieces, axis=1).reshape(N_CHIPS, -1, COLS)


def _pack_small(ts, rows):
    flat = jnp.concatenate([t.reshape(-1) for t in ts])
    return jnp.pad(flat, (0, rows * COLS - flat.shape[0])).reshape(rows, COLS)


def _small_rows(n_elems):
    return -(-n_elems // (SUBLANE * COLS)) * SUBLANE


def kernel(x, norm_mix, w_in, b_gate, attn_sinks, w_attn_o, conv_w, w_conv_o, ssm_a_re, ssm_a_im, ssm_b_re, ssm_b_im, ssm_c_re, ssm_c_im, ssm_d, ssm_log_dt, w_ssm_glu, w_ssm_o, w_mix_o, norm_ffn, w_ffn_in, w_ffn_out, norm_final, loss_target, m_norm_mix, m_w_in, m_b_gate, m_attn_sinks, m_w_attn_o, m_conv_w, m_w_conv_o, m_ssm_a_re, m_ssm_a_im, m_ssm_b_re, m_ssm_b_im, m_ssm_c_re, m_ssm_c_im, m_ssm_d, m_ssm_log_dt, m_w_ssm_glu, m_w_ssm_o, m_w_mix_o, m_norm_ffn, m_w_ffn_in, m_w_ffn_out, m_norm_final, v_norm_mix, v_w_in, v_b_gate, v_attn_sinks, v_w_attn_o, v_conv_w, v_w_conv_o, v_ssm_a_re, v_ssm_a_im, v_ssm_b_re, v_ssm_b_im, v_ssm_c_re, v_ssm_c_im, v_ssm_d, v_ssm_log_dt, v_w_ssm_glu, v_w_ssm_o, v_w_mix_o, v_norm_ffn, v_w_ffn_in, v_w_ffn_out, v_norm_final):
    a = dict(zip(ARG_NAMES, (
        x, norm_mix, w_in, b_gate, attn_sinks, w_attn_o, conv_w, w_conv_o, ssm_a_re, ssm_a_im, ssm_b_re, ssm_b_im,
        ssm_c_re, ssm_c_im, ssm_d, ssm_log_dt, w_ssm_glu, w_ssm_o, w_mix_o, norm_ffn, w_ffn_in, w_ffn_out, norm_final,
        loss_target, m_norm_mix, m_w_in, m_b_gate, m_attn_sinks, m_w_attn_o, m_conv_w, m_w_conv_o, m_ssm_a_re,
        m_ssm_a_im, m_ssm_b_re, m_ssm_b_im, m_ssm_c_re, m_ssm_c_im, m_ssm_d, m_ssm_log_dt, m_w_ssm_glu, m_w_ssm_o,
        m_w_mix_o, m_norm_ffn, m_w_ffn_in, m_w_ffn_out, m_norm_final, v_norm_mix, v_w_in, v_b_gate, v_attn_sinks,
        v_w_attn_o, v_conv_w, v_w_conv_o, v_ssm_a_re, v_ssm_a_im, v_ssm_b_re, v_ssm_b_im, v_ssm_c_re, v_ssm_c_im,
        v_ssm_d, v_ssm_log_dt, v_w_ssm_glu, v_w_ssm_o, v_w_mix_o, v_norm_ffn, v_w_ffn_in, v_w_ffn_out, v_norm_final)))
    px, py, _ = _place()
    chip = 2 * px + py
    shapes = {n: a[n].shape for n in BIG}

    gath = _gather_weights(_pack_local({n: a[n].astype(BF16) for n in BIG}), "gather_weights")
    w = _unpack_gathered(gath, shapes)
    for n in SMALL:
        w[n] = a[n]
    cw_local = a["conv_w"]
    cw_rows = _small_rows(cw_local.size)
    cw_all = _gather_all(_pack_small([cw_local], cw_rows), "gather_conv_w").reshape(N_DEV, -1)
    cw = jnp.stack([cw_all[2 * s, :cw_local.size].reshape(cw_local.shape) for s in range(N_CHIPS)])
    w["conv_w"] = cw.transpose(1, 2, 0, 3).reshape(cw_local.shape[0], CONV_K, -1)

    loss, dx, grads = _local_step(a["x"][0], a["loss_target"][0], w)

    red = _unpack_local(_reduce_scatter(_pack_grads(grads, shapes)), shapes)
    small_names = SMALL + ("conv_w",)
    n_small = sum(grads[n].size for n in small_names)
    small = _all_reduce_small(_pack_small([grads[n] for n in small_names], _small_rows(n_small))).reshape(-1)
    off = 0
    for n in small_names:
        red[n] = small[off:off + grads[n].size].reshape(grads[n].shape)
        off += grads[n].size
    lane = cw_local.shape[2]
    red["conv_w"] = lax.dynamic_slice_in_dim(red["conv_w"], chip * lane, lane, axis=2)

    loss_all = lax.psum(loss[0, 0], ("x", "y", "c"))
    deltas, new_m, new_v = [], [], []
    for n in WEIGHTS:
        d, mn, vn = _adamw(a[n], red[n], a["m_" + n], a["v_" + n], "adamw_" + n)
        deltas.append(d)
        new_m.append(mn)
        new_v.append(vn)
    return (loss_all, dx[None], *[red[n] for n in WEIGHTS], *deltas, *new_m, *new_v)
```

```python
import functools
import math

import jax
import jax.numpy as jnp
import numpy as np
from jax import lax
from jax.experimental import pallas as pl
from jax.experimental.pallas import tpu as pltpu

F32 = jnp.float32
BF16 = jnp.bfloat16

D_MODEL = 1024
DEPTH = 4
N_Q_HEADS = 8
N_KV_HEADS = 2
HEAD_DIM = 64
Q_GROUP = N_Q_HEADS // N_KV_HEADS
WINDOW = 128
BLOCK = 128
ROPE_THETA = 500000.0
ROT_DIM = HEAD_DIM // 4
ATTN_WIDTH = N_Q_HEADS * HEAD_DIM
KV_WIDTH = N_KV_HEADS * HEAD_DIM
NEG_INF = -1e30
CONV_WIDTH = 512
CONV_K = 3
SSM_WIDTH = 512
SSM_GROUP = 16
SSM_GROUPS = 32
SSM_STATE = 64
SSM_LANES = SSM_GROUPS * SSM_STATE
GATE_WIDTH = 3 * D_MODEL
FFN_HIDDEN = 2816
NORM_EPS = 1e-6
IN_COLS = 5888
C_Q, C_K, C_V, C_CB, C_CC, C_CX, C_U, C_G = 0, 512, 640, 768, 1280, 1792, 2304, 2816

ADAM_LR = 0.001
ADAM_B1 = 0.9
ADAM_B2 = 0.999
ADAM_EPS = 1e-08
ADAM_WD = 0.01
ADAM_STEP = 10

N_CHIPS = 4
N_DEV = 8
MESH_ID = pl.DeviceIdType.MESH

VMEM_LIMIT_BYTES = 48 * 1024 * 1024
LANE = 128
SUBLANE = 8
SCAN_ROWS = 8
SCAN_CHUNK = 128

BIG = ("w_in", "w_attn_o", "w_conv_o", "w_ssm_glu", "w_ssm_o", "w_mix_o", "w_ffn_in", "w_ffn_out")
BIG_AXIS = {"w_in": 2, "w_attn_o": 2, "w_conv_o": 2, "w_ssm_glu": 1, "w_ssm_o": 2, "w_mix_o": 1,
            "w_ffn_in": 2, "w_ffn_out": 1}
SMALL = ("norm_mix", "b_gate", "attn_sinks", "ssm_a_re", "ssm_a_im", "ssm_b_re", "ssm_b_im",
         "ssm_c_re", "ssm_c_im", "ssm_d", "ssm_log_dt", "norm_ffn", "norm_final")
WEIGHTS = ("norm_mix", "w_in", "b_gate", "attn_sinks", "w_attn_o", "conv_w", "w_conv_o", "ssm_a_re",
           "ssm_a_im", "ssm_b_re", "ssm_b_im", "ssm_c_re", "ssm_c_im", "ssm_d", "ssm_log_dt",
           "w_ssm_glu", "w_ssm_o", "w_mix_o", "norm_ffn", "w_ffn_in", "w_ffn_out", "norm_final")
ARG_NAMES = ("x",) + WEIGHTS + ("loss_target",) + tuple("m_" + n for n in WEIGHTS) + tuple(
    "v_" + n for n in WEIGHTS)


def _params(*sem):
    return pltpu.CompilerParams(dimension_semantics=sem if sem else None,
                                vmem_limit_bytes=VMEM_LIMIT_BYTES)


def _tile(dim, cap, align):
    t = min(cap, dim) // align * align
    while t >= align:
        if dim % t == 0:
            return t
        t -= align
    return dim


_DOT_DIMS = {"nn": (((1,), (0,)), ((), ())), "nt": (((1,), (1,)), ((), ())), "tn": (((0,), (0,)), ((), ()))}


def _mm(a, b, mode, name, out_dtype=F32, add=None, tm_cap=512, tn_cap=3072, tk_cap=1024):
    if mode == "nn":
        (m, k), (k2, n) = a.shape, b.shape
    elif mode == "nt":
        (m, k), (n, k2) = a.shape, b.shape
    else:
        (k, m), (k2, n) = a.shape, b.shape
    assert k == k2, (name, a.shape, b.shape)
    tm, tn, tk = _tile(m, tm_cap, LANE), _tile(n, tn_cap, LANE), _tile(k, tk_cap, LANE)
    nk = k // tk
    dims = _DOT_DIMS[mode]

    def body(a_ref, b_ref, *rest):
        if add is None:
            o_ref, acc = rest
        else:
            add_ref, o_ref, acc = rest
        kk = pl.program_id(2)

        @pl.when(kk == 0)
        def _():
            acc[...] = jnp.zeros_like(acc)

        acc[...] += lax.dot_general(a_ref[...].astype(BF16), b_ref[...].astype(BF16), dims,
                                    preferred_element_type=F32)

        @pl.when(kk == nk - 1)
        def _():
            r = acc[...]
            if add is not None:
                r = r + add_ref[...]
            o_ref[...] = r.astype(out_dtype)

    if mode == "tn":
        a_spec = pl.BlockSpec((tk, tm), lambda i, j, kk: (kk, i))
    else:
        a_spec = pl.BlockSpec((tm, tk), lambda i, j, kk: (i, kk))
    if mode == "nt":
        b_spec = pl.BlockSpec((tn, tk), lambda i, j, kk: (j, kk))
    else:
        b_spec = pl.BlockSpec((tk, tn), lambda i, j, kk: (kk, j))
    o_spec = pl.BlockSpec((tm, tn), lambda i, j, kk: (i, j))
    in_specs, args = [a_spec, b_spec], [a, b]
    if add is not None:
        in_specs.append(o_spec)
        args.append(add)
    return pl.pallas_call(
        body, name=name, grid=(m // tm, n // tn, nk), in_specs=in_specs, out_specs=o_spec,
        out_shape=jax.ShapeDtypeStruct((m, n), out_dtype),
        scratch_shapes=[pltpu.VMEM((tm, tn), F32)],
        compiler_params=_params("parallel", "parallel", "arbitrary"),
    )(*args)


def _rowwise(fn, rows, pars, outs, accs, name, tm_cap=256):
    length = rows[0][0].shape[0]
    tm = _tile(length, tm_cap, LANE)
    n = length // tm
    in_specs, args, counts = [], [], []
    for arr, c0, cw, shift in rows:
        bw = math.gcd(c0, cw) if c0 else cw
        assert bw % LANE == 0 or (c0 == 0 and cw == arr.shape[1]), (name, c0, cw)
        cnt = cw // bw
        counts.append(cnt)
        for j in range(cnt):
            in_specs.append(pl.BlockSpec(
                (tm, bw), lambda i, j=j, c0=c0, bw=bw, shift=shift: (jnp.clip(i + shift, 0, n - 1), c0 // bw + j)))
            args.append(arr)
    for p in pars:
        in_specs.append(pl.BlockSpec(p.shape, lambda i: (0, 0)))
        args.append(p)
    out_shape = [jax.ShapeDtypeStruct((length, w), dt) for w, dt in outs]
    out_specs = [pl.BlockSpec((tm, w), lambda i: (i, 0)) for w, _ in outs]
    out_shape += [jax.ShapeDtypeStruct((r, w), F32) for r, w in accs]
    out_specs += [pl.BlockSpec((r, w), lambda i: (0, 0)) for r, w in accs]
    n_in, n_out = len(args), len(outs)

    def body(*refs):
        i = pl.program_id(0)
        vals, p = [], 0
        for cnt in counts:
            blocks = [refs[p + j][...] for j in range(cnt)]
            p += cnt
            vals.append(blocks[0] if cnt == 1 else jnp.concatenate(blocks, axis=1))
        for _ in pars:
            vals.append(refs[p][...])
            p += 1
        res = fn((i, n), *vals)
        out_refs = refs[n_in:n_in + n_out]
        acc_refs = refs[n_in + n_out:]
        for r, v in zip(out_refs, res[:n_out]):
            r[...] = v.astype(r.dtype)
        if acc_refs:
            @pl.when(i == 0)
            def _():
                for r in acc_refs:
                    r[...] = jnp.zeros_like(r)
            for r, v in zip(acc_refs, res[n_out:]):
                r[...] += v

    res = pl.pallas_call(
        body, name=name, grid=(n,), in_specs=in_specs, out_specs=out_specs, out_shape=out_shape,
        compiler_params=_params("arbitrary"),
    )(*args)
    return res


def _rms(x, g):
    return x * lax.rsqrt(jnp.mean(x * x, axis=-1, keepdims=True) + NORM_EPS) * g


def _rms_fwd(x, g, name):
    return _rowwise(lambda ctx, xv, gv: (_rms(xv, gv),), [(x, 0, D_MODEL, 0)], [g],
                    [(D_MODEL, BF16)], [], name)[0]


def _rms_bwd(x, g, dh, dres, name):
    def fn(ctx, xv, dhv, drv, gv):
        _, vjp = jax.vjp(_rms, xv, gv)
        dx, dg = vjp(dhv)
        return dx + drv, dg
    return _rowwise(fn, [(x, 0, D_MODEL, 0), (dh, 0, D_MODEL, 0), (dres, 0, D_MODEL, 0)], [g],
                    [(D_MODEL, F32)], [(1, D_MODEL)], name)


def _rope_tables(length):
    pos = jnp.arange(length, dtype=F32)
    inv_freq = ROPE_THETA ** (-jnp.arange(0, ROT_DIM, 2, dtype=F32) / ROT_DIM)
    ang = pos[:, None] * inv_freq[None, :]
    cos, sin = jnp.cos(ang), jnp.sin(ang)
    half = ROT_DIM // 2
    ones = jnp.ones((length, HEAD_DIM - ROT_DIM), F32)
    zeros = jnp.zeros_like(ones)
    zh = jnp.zeros((length, half), F32)
    c64 = jnp.concatenate([cos, cos, ones], axis=1)
    s1 = jnp.concatenate([-sin, zh, zeros], axis=1)
    s2 = jnp.concatenate([zh, sin, zeros], axis=1)
    tile2 = lambda t: jnp.concatenate([t, t], axis=1)
    return tile2(c64), tile2(s1), tile2(s2)


def _lane_chunks(t):
    return [t[:, j * LANE:(j + 1) * LANE] for j in range(t.shape[1] // LANE)]


def _rope(t, c, s1, s2, n_rot):
    half = ROT_DIM // 2
    out = []
    for j, ch in enumerate(_lane_chunks(t)):
        if j < n_rot:
            ch = ch * c + pltpu.roll(ch, LANE - half, 1) * s1 + pltpu.roll(ch, half, 1) * s2
        out.append(ch)
    return jnp.concatenate(out, axis=1)


def _unrope(d, c, s1, s2, n_rot):
    half = ROT_DIM // 2
    out = []
    for j, ch in enumerate(_lane_chunks(d)):
        if j < n_rot:
            ch = ch * c + pltpu.roll(ch * s1, half, 1) + pltpu.roll(ch * s2, LANE - half, 1)
        out.append(ch)
    return jnp.concatenate(out, axis=1)


N_ROT_CHUNKS = (ATTN_WIDTH + KV_WIDTH) // LANE
QKV_WIDTH = ATTN_WIDTH + 2 * KV_WIDTH


def _split_fwd(proj, tabs, name):
    def fn(ctx, t, c, s1, s2):
        return (_rope(t, c, s1, s2, N_ROT_CHUNKS),)
    rows = [(proj, 0, QKV_WIDTH, 0)] + [(t, 0, LANE, 0) for t in tabs]
    return _rowwise(fn, rows, [], [(QKV_WIDTH, BF16)], [], name, tm_cap=BLOCK)[0]


def _split_bwd(dq, dkc, dkp, dvc, dvp, tabs, name):
    def fn(ctx, dqv, dkcv, dkpv, dvcv, dvpv, c, s1, s2):
        i, n = ctx
        keep = (i < n - 1).astype(F32)
        d = jnp.concatenate([dqv, dkcv + keep * dkpv, dvcv + keep * dvpv], axis=1)
        return (_unrope(d, c, s1, s2, N_ROT_CHUNKS),)
    rows = [(dq, 0, ATTN_WIDTH, 0), (dkc, 0, KV_WIDTH, 0), (dkp, 0, KV_WIDTH, 1), (dvc, 0, KV_WIDTH, 0),
            (dvp, 0, KV_WIDTH, 1)] + [(t, 0, LANE, 0) for t in tabs]
    return _rowwise(fn, rows, [], [(QKV_WIDTH, BF16)], [], name, tm_cap=BLOCK)[0]


def _att_scores(q_ref, kp_ref, kc_ref, sink_ref):
    n = pl.program_id(1)
    j = pl.program_id(0)
    rows = Q_GROUP * BLOCK
    qs = q_ref[...].reshape(rows, HEAD_DIM)
    kb = jnp.concatenate([kp_ref[0], kc_ref[0]], axis=0)
    s = lax.dot_general(qs, kb, _DOT_DIMS["nt"], preferred_element_type=F32) * (HEAD_DIM ** -0.5)
    r = lax.broadcasted_iota(jnp.int32, (rows, 2 * BLOCK), 0)
    kj = lax.broadcasted_iota(jnp.int32, (rows, 2 * BLOCK), 1)
    delta = (r % BLOCK) + BLOCK - kj
    ok = (delta >= 0) & (delta < WINDOW) & ((kj >= BLOCK) | (n > 0))
    s = jnp.where(ok, s, NEG_INF)
    rh = lax.broadcasted_iota(jnp.int32, (rows, 1), 0) // BLOCK
    sinks = sink_ref[...]
    lane = lax.broadcasted_iota(jnp.int32, sinks.shape, 1)
    srow = lax.broadcasted_iota(jnp.int32, sinks.shape, 0)
    sink = jnp.zeros((rows, 1), F32)
    for g in range(Q_GROUP):
        val = jnp.sum(jnp.where((lane == g) & (srow == j), sinks, 0.0), keepdims=True)
        sink = jnp.where(rh == g, val, sink)
    m = jnp.maximum(jnp.max(s, axis=-1, keepdims=True), sink)
    p = jnp.exp(s - m)
    psink = jnp.exp(sink - m)
    denom = jnp.sum(p, axis=-1, keepdims=True) + psink
    return qs, kb, p / denom, psink / denom, rh


def _att_specs(length):
    nb = length // BLOCK
    q_spec = pl.BlockSpec((Q_GROUP, BLOCK, HEAD_DIM), lambda j, n: (j, n, 0))
    prev = pl.BlockSpec((1, BLOCK, HEAD_DIM), lambda j, n: (j, jnp.maximum(n - 1, 0), 0))
    cur = pl.BlockSpec((1, BLOCK, HEAD_DIM), lambda j, n: (j, n, 0))
    sink_spec = pl.BlockSpec((N_KV_HEADS, Q_GROUP), lambda j, n: (0, 0))
    return nb, q_spec, prev, cur, sink_spec


def _att_fwd(q, k, v, sinks, name):
    length = q.shape[1]
    nb, q_spec, prev, cur, sink_spec = _att_specs(length)

    def body(q_ref, kp_ref, kc_ref, vp_ref, vc_ref, sink_ref, o_ref):
        _, _, p, _, _ = _att_scores(q_ref, kp_ref, kc_ref, sink_ref)
        vb = jnp.concatenate([vp_ref[0], vc_ref[0]], axis=0)
        o = jnp.dot(p.astype(BF16), vb, preferred_element_type=F32)
        o_ref[...] = o.reshape(Q_GROUP, BLOCK, HEAD_DIM).astype(o_ref.dtype)

    return pl.pallas_call(
        body, name=name, grid=(N_KV_HEADS, nb),
        in_specs=[q_spec, prev, cur, prev, cur, sink_spec], out_specs=q_spec,
        out_shape=jax.ShapeDtypeStruct((N_Q_HEADS, length, HEAD_DIM), BF16),
        compiler_params=_params("arbitrary", "arbitrary"),
    )(q, k, k, v, v, sinks)


def _att_bwd(q, k, v, sinks, do, name):
    length = q.shape[1]
    nb, q_spec, prev, cur, sink_spec = _att_specs(length)
    ds_spec = pl.BlockSpec((1, SUBLANE, LANE), lambda j, n: (j, 0, 0))

    def body(q_ref, kp_ref, kc_ref, vp_ref, vc_ref, sink_ref, do_ref, dq_ref, dkc_ref, dkp_ref, dvc_ref,
             dvp_ref, dsink_ref):
        n = pl.program_id(1)
        qs, kb, p, psink, rh = _att_scores(q_ref, kp_ref, kc_ref, sink_ref)
        vb = jnp.concatenate([vp_ref[0], vc_ref[0]], axis=0)
        dob = do_ref[...].reshape(Q_GROUP * BLOCK, HEAD_DIM).astype(BF16)
        dv = lax.dot_general(p.astype(BF16), dob, _DOT_DIMS["tn"], preferred_element_type=F32)
        dp = lax.dot_general(dob, vb, _DOT_DIMS["nt"], preferred_element_type=F32)
        dsum = jnp.sum(p * dp, axis=-1, keepdims=True)
        ds = (p * (dp - dsum) * (HEAD_DIM ** -0.5)).astype(BF16)
        dq = jnp.dot(ds, kb, preferred_element_type=F32)
        dk = lax.dot_general(ds, qs, _DOT_DIMS["tn"], preferred_element_type=F32)
        dq_ref[...] = dq.reshape(Q_GROUP, BLOCK, HEAD_DIM)
        dkp_ref[0] = dk[:BLOCK]
        dkc_ref[0] = dk[BLOCK:]
        dvp_ref[0] = dv[:BLOCK]
        dvc_ref[0] = dv[BLOCK:]
        dsr = -psink * dsum
        row = lax.broadcasted_iota(jnp.int32, (SUBLANE, LANE), 0)
        upd = jnp.zeros((SUBLANE, LANE), F32)
        for g in range(Q_GROUP):
            val = jnp.sum(jnp.where(rh == g, dsr, 0.0), keepdims=True)
            upd = jnp.where(row == g, val, upd)

        @pl.when(n == 0)
        def _():
            dsink_ref[...] = jnp.zeros_like(dsink_ref)

        dsink_ref[0] += upd

    kv_shape = jax.ShapeDtypeStruct((N_KV_HEADS, length, HEAD_DIM), F32)
    return pl.pallas_call(
        body, name=name, grid=(N_KV_HEADS, nb),
        in_specs=[q_spec, prev, cur, prev, cur, sink_spec, q_spec],
        out_specs=[q_spec, cur, cur, cur, cur, ds_spec],
        out_shape=[jax.ShapeDtypeStruct((N_Q_HEADS, length, HEAD_DIM), F32), kv_shape, kv_shape, kv_shape,
                   kv_shape, jax.ShapeDtypeStruct((N_KV_HEADS, SUBLANE, LANE), F32)],
        compiler_params=_params("arbitrary", "arbitrary"),
    )(q, k, k, v, v, sinks, do)


def _to_heads(t, heads):
    return t.reshape(t.shape[0], heads, HEAD_DIM).transpose(1, 0, 2)


def _from_heads(t):
    return t.transpose(1, 0, 2).reshape(t.shape[1], t.shape[0] * HEAD_DIM)


def _shift_down(z, s):
    t = lax.broadcasted_iota(jnp.int32, z.shape, 0)
    return jnp.where(t >= s, pltpu.roll(z, s, 0), 0.0)


def _shift_up(z, s):
    t = lax.broadcasted_iota(jnp.int32, z.shape, 0)
    return jnp.where(t < z.shape[0] - s, pltpu.roll(z, z.shape[0] - s, 0), 0.0)


def _conv_specs(length):
    col = lambda c0: pl.BlockSpec((length, LANE), lambda j, c0=c0: (0, c0 // LANE + j))
    w_spec = pl.BlockSpec((CONV_K, LANE), lambda j: (0, j))
    o_spec = pl.BlockSpec((length, LANE), lambda j: (0, j))
    return col, w_spec, o_spec


def _conv_fwd(proj, w, name):
    length = proj.shape[0]
    col, w_spec, o_spec = _conv_specs(length)

    def body(cb_ref, cc_ref, cx_ref, w_ref, o_ref):
        z = cc_ref[...] * cx_ref[...]
        s = w_ref[0:1, :] * _shift_down(z, 2) + w_ref[1:2, :] * _shift_down(z, 1) + w_ref[2:3, :] * z
        o_ref[...] = (cb_ref[...] * s).astype(o_ref.dtype)

    return pl.pallas_call(
        body, name=name, grid=(CONV_WIDTH // LANE,),
        in_specs=[col(C_CB), col(C_CC), col(C_CX), w_spec], out_specs=o_spec,
        out_shape=jax.ShapeDtypeStruct((length, CONV_WIDTH), BF16),
        compiler_params=_params("arbitrary"),
    )(proj, proj, proj, w)


def _conv_bwd(proj, w, dy, name):
    length = proj.shape[0]
    col, w_spec, o_spec = _conv_specs(length)
    dw_spec = pl.BlockSpec((1, LANE), lambda j: (0, j))

    def body(cb_ref, cc_ref, cx_ref, w_ref, dy_ref, dcb_ref, dcc_ref, dcx_ref, dw0_ref, dw1_ref, dw2_ref):
        cc, cx, dyv = cc_ref[...], cx_ref[...], dy_ref[...]
        z = cc * cx
        w0, w1, w2 = w_ref[0:1, :], w_ref[1:2, :], w_ref[2:3, :]
        z1, z2 = _shift_down(z, 1), _shift_down(z, 2)
        s = w0 * z2 + w1 * z1 + w2 * z
        dcb_ref[...] = (dyv * s).astype(dcb_ref.dtype)
        ds = dyv * cb_ref[...]
        dw0_ref[...] = jnp.sum(ds * z2, axis=0, keepdims=True)
        dw1_ref[...] = jnp.sum(ds * z1, axis=0, keepdims=True)
        dw2_ref[...] = jnp.sum(ds * z, axis=0, keepdims=True)
        dz = w2 * ds + w1 * _shift_up(ds, 1) + w0 * _shift_up(ds, 2)
        dcc_ref[...] = (dz * cx).astype(dcc_ref.dtype)
        dcx_ref[...] = (dz * cc).astype(dcx_ref.dtype)

    act = jax.ShapeDtypeStruct((length, CONV_WIDTH), BF16)
    dws = jax.ShapeDtypeStruct((1, CONV_WIDTH), F32)
    return pl.pallas_call(
        body, name=name, grid=(CONV_WIDTH // LANE,),
        in_specs=[col(C_CB), col(C_CC), col(C_CX), w_spec, o_spec],
        out_specs=[o_spec, o_spec, o_spec, dw_spec, dw_spec, dw_spec],
        out_shape=[act, act, act, dws, dws, dws],
        compiler_params=_params("arbitrary"),
    )(proj, proj, proj, w, dy)


def _cmul(ar, ai, br, bi):
    return ar * br - ai * bi, ar * bi + ai * br


def _scan_tables(lr, li, reverse):
    pr, pi = [lr], [li]
    for _ in range(SCAN_ROWS - 1):
        nr, ni = _cmul(pr[-1], pi[-1], lr, li)
        pr.append(nr)
        pi.append(ni)
    row = jnp.arange(SCAN_ROWS)[:, None]
    mr, mi = [], []
    for s in (1, 2, 4):
        live = (row + s < SCAN_ROWS) if reverse else (row >= s)
        mr.append(jnp.where(live, pr[s - 1], 0.0))
        mi.append(jnp.where(live, pi[s - 1], 0.0))
    order = range(SCAN_ROWS - 1, -1, -1) if reverse else range(SCAN_ROWS)
    carry_r = jnp.concatenate([pr[d] for d in order], axis=0)
    carry_i = jnp.concatenate([pi[d] for d in order], axis=0)
    return jnp.stack(mr), jnp.stack(mi), carry_r, carry_i


def _scan(b, lr, li, reverse, name):
    length = b.shape[0]
    mr, mi, cr, ci = _scan_tables(lr, li, reverse)
    nchunk = length // SCAN_CHUNK
    nblk = SCAN_CHUNK // SCAN_ROWS
    half = SSM_LANES

    def body(b_ref, mr_ref, mi_ref, cr_ref, ci_ref, o_ref, carry):
        @pl.when(pl.program_id(0) == 0)
        def _():
            carry[...] = jnp.zeros_like(carry)

        c_r, c_i = carry[0:1, :], carry[1:2, :]
        blocks = range(nblk - 1, -1, -1) if reverse else range(nblk)
        for blk in blocks:
            r0 = blk * SCAN_ROWS
            xr = b_ref[r0:r0 + SCAN_ROWS, :half]
            xi = b_ref[r0:r0 + SCAN_ROWS, half:]
            for kk, s in enumerate((1, 2, 4)):
                sh = SCAN_ROWS - s if reverse else s
                rr, ri = pltpu.roll(xr, sh, 0), pltpu.roll(xi, sh, 0)
                ar, ai = _cmul(mr_ref[kk], mi_ref[kk], rr, ri)
                xr, xi = xr + ar, xi + ai
            ar, ai = _cmul(cr_ref[...], ci_ref[...], c_r, c_i)
            xr, xi = xr + ar, xi + ai
            o_ref[r0:r0 + SCAN_ROWS, :half] = xr
            o_ref[r0:r0 + SCAN_ROWS, half:] = xi
            edge = r0 if reverse else r0 + SCAN_ROWS - 1
            c_r = o_ref[edge:edge + 1, :half]
            c_i = o_ref[edge:edge + 1, half:]
        carry[0:1, :] = c_r
        carry[1:2, :] = c_i

    chunk = (lambda i: (nchunk - 1 - i, 0)) if reverse else (lambda i: (i, 0))
    blk_spec = pl.BlockSpec((SCAN_CHUNK, 2 * half), chunk)
    m_spec = pl.BlockSpec((3, SCAN_ROWS, half), lambda i: (0, 0, 0))
    c_spec = pl.BlockSpec((SCAN_ROWS, half), lambda i: (0, 0))
    return pl.pallas_call(
        body, name=name, grid=(nchunk,), in_specs=[blk_spec, m_spec, m_spec, c_spec, c_spec],
        out_specs=blk_spec, out_shape=jax.ShapeDtypeStruct(b.shape, F32),
        scratch_shapes=[pltpu.VMEM((SUBLANE, half), F32)],
        compiler_params=_params("arbitrary"),
    )(b, mr, mi, cr, ci)


def _dlam(g, states, name):
    half = SSM_LANES

    def fn(ctx, gv, xv, xprev):
        i, _ = ctx
        row = lax.broadcasted_iota(jnp.int32, xv.shape, 0)
        last = jnp.sum(jnp.where(lax.broadcasted_iota(jnp.int32, xprev.shape, 0) == xprev.shape[0] - 1,
                                 xprev, 0.0), axis=0, keepdims=True)
        last = last * (i > 0).astype(F32)
        xs = jnp.where(row == 0, last, pltpu.roll(xv, 1, 0))
        gr, gi, xr, xi = gv[:, :half], gv[:, half:], xs[:, :half], xs[:, half:]
        dre = jnp.sum(gr * xr + gi * xi, axis=0, keepdims=True)
        dim = jnp.sum(gi * xr - gr * xi, axis=0, keepdims=True)
        return (jnp.concatenate([dre, dim], axis=1),)

    rows = [(g, 0, 2 * half, 0), (states, 0, 2 * half, 0), (states, 0, 2 * half, -1)]
    return _rowwise(fn, rows, [], [], [(1, 2 * half)], name, tm_cap=128)[0]


def _block_diag(t):
    g, a, b = t.shape
    eye = jnp.eye(g, dtype=t.dtype)
    return (t[:, :, None, :] * eye[:, None, :, None]).reshape(g * a, g * b)


def _ssm_prep(a_re, a_im, b_re, b_im, c_re, c_im, log_dt):
    dt = jnp.exp(log_dt)[:, None]
    er = jnp.exp(a_re * dt)
    lr, li = er * jnp.cos(a_im * dt), er * jnp.sin(a_im * dt)
    nr, ni = lr - 1.0, li
    den = a_re * a_re + a_im * a_im
    qr, qi = (nr * a_re + ni * a_im) / den, (ni * a_re - nr * a_im) / den
    bbr = qr[..., None] * b_re - qi[..., None] * b_im
    bbi = qr[..., None] * b_im + qi[..., None] * b_re
    bmat = jnp.concatenate([_block_diag(bbr.transpose(0, 2, 1)), _block_diag(bbi.transpose(0, 2, 1))], axis=1)
    cmat = jnp.concatenate([_block_diag(c_re.transpose(0, 2, 1)), -_block_diag(c_im.transpose(0, 2, 1))], axis=0)
    return lr.reshape(1, SSM_LANES), li.reshape(1, SSM_LANES), bmat, cmat


def _ssm_act(yc, u, d):
    return jax.nn.gelu(yc + d * u)


def _glu(ys, z):
    return ys * jax.nn.sigmoid(z)


def _merge(ya, yc, ys, gl, b):
    gates = jax.nn.sigmoid(gl + b)
    return gates[:, :D_MODEL] * ya + gates[:, D_MODEL:2 * D_MODEL] * yc + gates[:, 2 * D_MODEL:] * ys


def _swiglu(gu):
    return jax.nn.silu(gu[:, :FFN_HIDDEN]) * gu[:, FFN_HIDDEN:]


def _loss_fn(x, g, t):
    e = _rms(x, g) - t
    per_tok = jnp.mean(e * e, axis=-1, keepdims=True)
    return 0.5 * jnp.sum(per_tok, axis=0, keepdims=True)


def _vjp_rowwise(f, n_row, cot_dtype=F32):
    def fn(ctx, *vals):
        prim = vals[:n_row] + vals[n_row + 1:]
        _, vjp = jax.vjp(f, *prim)
        return vjp(vals[n_row].astype(cot_dtype))
    return fn


def _layer_fwd(i, x, w, tabs):
    nm = lambda s: "l%d_%s" % (i, s)
    sv = {"x": x}
    h = _rms_fwd(x, w["norm_mix"][i:i + 1], nm("rms_mix"))
    proj = _mm(h, w["w_in"][i], "nn", nm("mm_in"))
    qkv = _split_fwd(proj, tabs, nm("rope"))
    q, k, v = (_to_heads(qkv[:, :C_K], N_Q_HEADS), _to_heads(qkv[:, C_K:C_V], N_KV_HEADS),
               _to_heads(qkv[:, C_V:C_CB], N_KV_HEADS))
    sinks = w["attn_sinks"][i].reshape(N_KV_HEADS, Q_GROUP)
    att = _from_heads(_att_fwd(q, k, v, sinks, nm("att")))
    conv = _conv_fwd(proj, w["conv_w"][i], nm("conv"))
    lr, li, bmat, cmat = w["ssm"][i]
    u = proj[:, C_U:C_G]
    bu = _mm(u, bmat, "nn", nm("mm_bu"), tn_cap=1024)
    states = _scan(bu, lr, li, False, nm("scan"))
    yc = _mm(states, cmat, "nn", nm("mm_c"))
    d = w["ssm_d"][i:i + 1]
    ys = _rowwise(lambda ctx, a, b, c: (_ssm_act(a, b, c),), [(yc, 0, SSM_WIDTH, 0), (u, 0, SSM_WIDTH, 0)], [d],
                  [(SSM_WIDTH, F32)], [], nm("ssm_act"))[0]
    z = _mm(ys, w["w_ssm_glu"][i], "nn", nm("mm_glu"))
    sg = _rowwise(lambda ctx, a, b: (_glu(a, b),), [(ys, 0, SSM_WIDTH, 0), (z, 0, SSM_WIDTH, 0)], [],
                  [(SSM_WIDTH, BF16)], [], nm("glu"))[0]
    ya = _mm(att, w["w_attn_o"][i], "nn", nm("mm_ao"))
    yv = _mm(conv, w["w_conv_o"][i], "nn", nm("mm_co"))
    ym = _mm(sg, w["w_ssm_o"][i], "nn", nm("mm_so"))
    bg = w["b_gate"][i:i + 1]
    merged = _rowwise(lambda ctx, a, b, c, gl, bb: (_merge(a, b, c, gl, bb),),
                      [(ya, 0, D_MODEL, 0), (yv, 0, D_MODEL, 0), (ym, 0, D_MODEL, 0), (proj, C_G, GATE_WIDTH, 0)],
                      [bg], [(D_MODEL, BF16)], [], nm("merge"))[0]
    x1 = _mm(merged, w["w_mix_o"][i], "nn", nm("mm_mix"), add=x)
    h2 = _rms_fwd(x1, w["norm_ffn"][i:i + 1], nm("rms_ffn"))
    gu = _mm(h2, w["w_ffn_in"][i], "nn", nm("mm_ffn_in"))
    act = _rowwise(lambda ctx, a: (_swiglu(a),), [(gu, 0, 2 * FFN_HIDDEN, 0)], [], [(FFN_HIDDEN, BF16)], [],
                   nm("swiglu"))[0]
    x2 = _mm(act, w["w_ffn_out"][i], "nn", nm("mm_ffn_out"), add=x1)
    sv.update(h=h, proj=proj, q=q, k=k, v=v, att=att, conv=conv, u=u, states=states, yc=yc, ys=ys, z=z, sg=sg,
              ya=ya, yv=yv, ym=ym, merged=merged, x1=x1, h2=h2, gu=gu, act=act)
    return x2, sv


def _layer_bwd(i, dx2, sv, w, tabs):
    nm = lambda s: "l%d_b_%s" % (i, s)
    g = {}
    dact = _mm(dx2, w["w_ffn_out"][i], "nt", nm("mm_dact"))
    g["w_ffn_out"] = _mm(sv["act"], dx2, "tn", nm("mm_gw_ffn_out"))
    dgu = _rowwise(_vjp_rowwise(_swiglu, 1), [(sv["gu"], 0, 2 * FFN_HIDDEN, 0), (dact, 0, FFN_HIDDEN, 0)], [],
                   [(2 * FFN_HIDDEN, BF16)], [], nm("swiglu"))[0]
    dh2 = _mm(dgu, w["w_ffn_in"][i], "nt", nm("mm_dh2"), tk_cap=3072)
    g["w_ffn_in"] = _mm(sv["h2"], dgu, "tn", nm("mm_gw_ffn_in"))
    dx1, g["norm_ffn"] = _rms_bwd(sv["x1"], w["norm_ffn"][i:i + 1], dh2, dx2, nm("rms_ffn"))
    dmerged = _mm(dx1, w["w_mix_o"][i], "nt", nm("mm_dmerged"))
    g["w_mix_o"] = _mm(sv["merged"], dx1, "tn", nm("mm_gw_mix"))
    proj = sv["proj"]
    bg = w["b_gate"][i:i + 1]
    dya, dyv, dym, dgl, g["b_gate"] = _rowwise(
        _vjp_rowwise(_merge, 4),
        [(sv["ya"], 0, D_MODEL, 0), (sv["yv"], 0, D_MODEL, 0), (sv["ym"], 0, D_MODEL, 0),
         (proj, C_G, GATE_WIDTH, 0), (dmerged, 0, D_MODEL, 0)], [bg],
        [(D_MODEL, BF16), (D_MODEL, BF16), (D_MODEL, BF16), (GATE_WIDTH, BF16)], [(1, GATE_WIDTH)], nm("merge"))
    dsg = _mm(dym, w["w_ssm_o"][i], "nt", nm("mm_dsg"))
    g["w_ssm_o"] = _mm(sv["sg"], dym, "tn", nm("mm_gw_so"))
    dys0, dz = _rowwise(_vjp_rowwise(_glu, 2), [(sv["ys"], 0, SSM_WIDTH, 0), (sv["z"], 0, SSM_WIDTH, 0),
                                                 (dsg, 0, SSM_WIDTH, 0)], [],
                        [(SSM_WIDTH, F32), (SSM_WIDTH, BF16)], [], nm("glu"))
    dys = _mm(dz, w["w_ssm_glu"][i], "nt", nm("mm_dys"), add=dys0)
    g["w_ssm_glu"] = _mm(sv["ys"], dz, "tn", nm("mm_gw_glu"))
    d = w["ssm_d"][i:i + 1]
    dyc, du0, g["ssm_d"] = _rowwise(
        _vjp_rowwise(_ssm_act, 2), [(sv["yc"], 0, SSM_WIDTH, 0), (sv["u"], 0, SSM_WIDTH, 0), (dys, 0, SSM_WIDTH, 0)],
        [d], [(SSM_WIDTH, F32), (SSM_WIDTH, F32)], [(1, SSM_WIDTH)], nm("ssm_act"))
    lr, li, bmat, cmat = w["ssm"][i]
    dstates = _mm(dyc, cmat, "nt", nm("mm_dstates"), tn_cap=1024)
    g_cmat = _mm(sv["states"], dyc, "tn", nm("mm_gc"))
    gs = _scan(dstates, lr, -li, True, nm("scan"))
    g_lam = _dlam(gs, sv["states"], nm("dlam"))
    du = _mm(gs, bmat, "nt", nm("mm_du"), out_dtype=BF16, add=du0)
    g_bmat = _mm(sv["u"], gs, "tn", nm("mm_gb"), tn_cap=1024)
    g["ssm"] = (g_lam[:, :SSM_LANES], g_lam[:, SSM_LANES:], g_bmat, g_cmat)
    dconv = _mm(dyv, w["w_conv_o"][i], "nt", nm("mm_dconv"))
    g["w_conv_o"] = _mm(sv["conv"], dyv, "tn", nm("mm_gw_co"))
    dcb, dcc, dcx, dw0, dw1, dw2 = _conv_bwd(proj, w["conv_w"][i], dconv, nm("conv"))
    g["conv_w"] = jnp.concatenate([dw0, dw1, dw2], axis=0)
    datt = _mm(dya, w["w_attn_o"][i], "nt", nm("mm_datt"))
    g["w_attn_o"] = _mm(sv["att"], dya, "tn", nm("mm_gw_ao"))
    sinks = w["attn_sinks"][i].reshape(N_KV_HEADS, Q_GROUP)
    dq, dkc, dkp, dvc, dvp, dsk = _att_bwd(sv["q"], sv["k"], sv["v"], sinks, _to_heads(datt, N_Q_HEADS), nm("att"))
    g["attn_sinks"] = dsk[:, :Q_GROUP, 0].reshape(N_Q_HEADS)
    dqkv = _split_bwd(_from_heads(dq), _from_heads(dkc), _from_heads(dkp), _from_heads(dvc), _from_heads(dvp),
                      tabs, nm("rope"))
    dproj = jnp.concatenate([dqkv, dcb, dcc, dcx, du, dgl], axis=1)
    dh = _mm(dproj, w["w_in"][i], "nt", nm("mm_dh"), tk_cap=3072)
    g["w_in"] = _mm(sv["h"], dproj, "tn", nm("mm_gw_in"))
    dx, g["norm_mix"] = _rms_bwd(sv["x"], w["norm_mix"][i:i + 1], dh, dx1, nm("rms_mix"))
    return dx, g


def _local_step(x, target, w):
    length = x.shape[0]
    tabs = _rope_tables(length)
    ssm_names = ("ssm_a_re", "ssm_a_im", "ssm_b_re", "ssm_b_im", "ssm_c_re", "ssm_c_im", "ssm_log_dt")
    w = dict(w)
    preps = [jax.vjp(_ssm_prep, *[w[n][i] for n in ssm_names]) for i in range(DEPTH)]
    w["ssm"] = [p[0] for p in preps]
    saved = []
    for i in range(DEPTH):
        x, sv = _layer_fwd(i, x, w, tabs)
        saved.append(sv)

    def loss_fn(ctx, xv, tv, gv):
        val, vjp = jax.vjp(_loss_fn, xv, gv, tv)
        dx, dg, _ = vjp(jnp.ones((1, 1), F32))
        return dx, dg, val + jnp.zeros((1, LANE), F32)

    gfin = w["norm_final"].reshape(1, D_MODEL)
    dx, g_final, loss = _rowwise(loss_fn, [(x, 0, D_MODEL, 0), (target, 0, D_MODEL, 0)], [gfin],
                                 [(D_MODEL, F32)], [(1, D_MODEL), (1, LANE)], "loss")
    layer_grads = [None] * DEPTH
    for i in reversed(range(DEPTH)):
        dx, layer_grads[i] = _layer_bwd(i, dx, saved[i], w, tabs)
    grads = {}
    for n in layer_grads[0]:
        if n != "ssm":
            grads[n] = jnp.stack([lg[n] for lg in layer_grads])
    ssm_g = [preps[i][1](layer_grads[i]["ssm"]) for i in range(DEPTH)]
    for j, n in enumerate(ssm_names):
        grads[n] = jnp.stack([sg[j] for sg in ssm_g])
    grads["norm_final"] = g_final.reshape(D_MODEL)
    for n in ("norm_mix", "norm_ffn", "b_gate", "ssm_d"):
        grads[n] = grads[n].reshape(grads[n].shape[0], -1)
    return loss, dx, grads


COLS = 1024
ANY_SPEC = pl.BlockSpec(memory_space=pl.ANY)


def _place():
    return lax.axis_index("x"), lax.axis_index("y"), lax.axis_index("c")


def _other_chips(x, y):
    return [(1 - x, y), (x, 1 - y), (1 - x, 1 - y)]


def _remote(src, dst, send_sems, recv_sems, k, to):
    return pltpu.make_async_remote_copy(src_ref=src, dst_ref=dst, send_sem=send_sems.at[k], recv_sem=recv_sems.at[k],
                                        device_id=to, device_id_type=MESH_ID)


def _comm_call(body, name, out_shape, n_sems, n_local, args):
    return pl.pallas_call(
        body, name=name, out_shape=out_shape, in_specs=[ANY_SPEC] * len(args),
        out_specs=[ANY_SPEC] * len(out_shape),
        scratch_shapes=[pltpu.SemaphoreType.DMA((n_sems,)), pltpu.SemaphoreType.DMA((n_sems,)),
                        pltpu.SemaphoreType.DMA((n_local,))],
    )(*args)


def _gather_weights(shards, name):
    nt = len(shards)
    hd = shards[0].shape[0] // 2

    def body(*refs):
        srcs, outs = refs[:nt], refs[nt:2 * nt]
        send_sems, recv_sems, _ = refs[2 * nt:]
        x, y, c = _place()
        me, sibling = (x, y, c), (x, y, 1 - c)
        chips = _other_chips(x, y)

        def blk(t, chip, hc):
            return outs[t].at[2 * chip[0] + chip[1], pl.ds(hc * hd, hd)]

        first = [_remote(srcs[t].at[pl.ds(c * hd, hd)], blk(t, (x, y), c), send_sems, recv_sems, 6 * t + j, (*chip, c))
                 for t in range(nt) for j, chip in enumerate(chips)]
        for cp in first:
            cp.start()
        passed = []
        for j, chip in enumerate(chips):
            for t in range(nt):
                _remote(blk(t, chip, c), blk(t, chip, c), send_sems, recv_sems, 6 * t + j, me).wait_recv()
                passed.append(_remote(blk(t, chip, c), blk(t, chip, c), send_sems, recv_sems, 6 * t + 3 + j, sibling))
                passed[-1].start()
        for j, chip in enumerate(chips):
            for t in range(nt):
                _remote(blk(t, chip, 1 - c), blk(t, chip, 1 - c), send_sems, recv_sems, 6 * t + 3 + j, me).wait_recv()
        for cp in first + passed:
            cp.wait_send()

    out_shape = [jax.ShapeDtypeStruct((N_CHIPS,) + s.shape, s.dtype) for s in shards]
    return _comm_call(body, name, out_shape, 6 * nt, 1, shards)


def _swap_halves(gs, name):
    nt = len(gs)
    hd = gs[0].shape[1] // 2

    def body(*refs):
        srcs, outs = refs[:nt], refs[nt:2 * nt]
        send_sems, recv_sems, _ = refs[2 * nt:]
        x, y, c = _place()
        cps = [_remote(srcs[t].at[s, pl.ds((1 - c) * hd, hd)], outs[t].at[s], send_sems, recv_sems, N_CHIPS * t + s,
                       (x, y, 1 - c)) for t in range(nt) for s in range(N_CHIPS)]
        for cp in cps:
            cp.start()
        for cp in cps:
            cp.wait()

    out_shape = [jax.ShapeDtypeStruct((N_CHIPS, hd) + g.shape[2:], g.dtype) for g in gs]
    return _comm_call(body, name, out_shape, N_CHIPS * nt, 1, gs)


def _exchange_shards(parts, name):
    nt = len(parts)

    def body(*refs):
        srcs, outs = refs[:nt], refs[nt:2 * nt]
        send_sems, recv_sems, _ = refs[2 * nt:]
        x, y, c = _place()
        cps = [_remote(srcs[t].at[2 * chip[0] + chip[1]], outs[t].at[j], send_sems, recv_sems, 3 * t + j, (*chip, c))
               for t in range(nt) for j, chip in enumerate(_other_chips(x, y))]
        for cp in cps:
            cp.start()
        for cp in cps:
            cp.wait()

    out_shape = [jax.ShapeDtypeStruct((N_CHIPS - 1,) + p.shape[1:], p.dtype) for p in parts]
    return _comm_call(body, name, out_shape, 3 * nt, 1, parts)


def _join_halves(reds, name):
    nt = len(reds)
    hd = reds[0].shape[1]

    def body(*refs):
        srcs, outs = refs[:nt], refs[nt:2 * nt]
        send_sems, recv_sems, _ = refs[2 * nt:]
        x, y, c = _place()
        cps = [_remote(srcs[t].at[0], outs[t].at[pl.ds(c * hd, hd)], send_sems, recv_sems, t, (x, y, 1 - c))
               for t in range(nt)]
        for cp in cps:
            cp.start()
        for t in range(nt):
            _remote(srcs[t].at[0], outs[t].at[pl.ds((1 - c) * hd, hd)], send_sems, recv_sems, t, (x, y, c)).wait_recv()
        for cp in cps:
            cp.wait_send()

    out_shape = [jax.ShapeDtypeStruct((2 * hd,) + r.shape[2:], r.dtype) for r in reds]
    return _comm_call(body, name, out_shape, nt, 1, reds)


def _gather_all(buf, name):
    def body(src_ref, out_ref, send_sems, recv_sems, local_sems):
        local_sem = local_sems.at[0]
        x, y, c = _place()
        flips = [(fx, fy, fc) for fx in (0, 1) for fy in (0, 1) for fc in (0, 1)][1:]

        def peer(f):
            return tuple(1 - v if fl else v for v, fl in zip((x, y, c), f))

        def slot(p):
            return out_ref.at[4 * p[0] + 2 * p[1] + p[2]]

        mine = pltpu.make_async_copy(src_ref, slot((x, y, c)), local_sem)
        mine.start()
        cps = [_remote(src_ref, slot((x, y, c)), send_sems, recv_sems, k, peer(f)) for k, f in enumerate(flips)]
        for cp in cps:
            cp.start()
        for k, f in enumerate(flips):
            _remote(src_ref, slot(peer(f)), send_sems, recv_sems, k, (x, y, c)).wait_recv()
        for cp in cps:
            cp.wait_send()
        mine.wait()

    return _comm_call(body, name, [jax.ShapeDtypeStruct((N_DEV,) + buf.shape, buf.dtype)], N_DEV - 1, 1, [buf])[0]


SUM_BLOCK_ELEMS = 256 * 1024


def _sum_windows(parts, lead, name, out_dtypes=(F32,)):
    a, b = parts[0][0].shape[2:]
    ta = _tile(a, max(SUBLANE, SUM_BLOCK_ELEMS // b), 2 * SUBLANE)
    offs = jnp.stack([jnp.stack([jnp.asarray(o, jnp.int32) for o in off]) for _, off in parts])
    n_in = len(parts)

    def body(off_ref, *refs):
        acc = refs[0][...].astype(F32)
        for r in refs[1:n_in]:
            acc = acc + r[...].astype(F32)
        for r in refs[n_in:]:
            r[...] = acc.astype(r.dtype)

    in_specs = [pl.BlockSpec((1, 1, ta, b), lambda p, q, i, off, k=k: (off[k, 0] + p, off[k, 1] + q, i, 0))
                for k in range(n_in)]
    o_spec = pl.BlockSpec((1, 1, ta, b), lambda p, q, i, off: (p, q, i, 0))
    return pl.pallas_call(
        body, name=name, out_shape=[jax.ShapeDtypeStruct(tuple(lead) + (a, b), dt) for dt in out_dtypes],
        grid_spec=pltpu.PrefetchScalarGridSpec(
            num_scalar_prefetch=1, grid=tuple(lead) + (a // ta,), in_specs=in_specs,
            out_specs=[o_spec] * len(out_dtypes)),
        compiler_params=_params("arbitrary", "arbitrary", "arbitrary"),
    )(offs, *[arr for arr, _ in parts])


def _reduce_scatter(gs, names):
    x, y, c = _place()
    hd = gs[0].shape[1] // 2
    theirs = _swap_halves(gs, "rs_swap_halves")
    pairs = [_sum_windows([(g, (0, c * hd)), (t, (0, 0))], (N_CHIPS, hd), "rs_sum_pair_" + n, (F32, BF16))
             for g, t, n in zip(gs, theirs, names)]
    others = _exchange_shards([p[1] for p in pairs], "rs_exchange")
    reds = [_sum_windows([(p[0], (2 * x + y, 0))] + [(o, (j, 0)) for j in range(N_CHIPS - 1)], (1, hd),
                         "rs_sum_chips_" + n)[0] for p, o, n in zip(pairs, others, names)]
    joined = _join_halves(reds, "rs_join")
    return [lax.dynamic_update_slice_in_dim(j, r[0], c * hd, axis=0) for j, r in zip(joined, reds)]


def _all_reduce_small(buf):
    everyone = _gather_all(buf, "small_gather")[:, None]
    return _sum_windows([(everyone, (k, 0)) for k in range(N_DEV)], (1, 1), "small_sum")[0][0, 0]


def _adamw(wt, g, m, v, name):
    n = wt.size
    cols = COLS if n % COLS == 0 else n
    r = n // cols
    tr = _tile(r, 512, SUBLANE)

    def body(w_ref, g_ref, m_ref, v_ref, d_ref, nm_ref, nv_ref):
        gv = g_ref[...]
        mn = ADAM_B1 * m_ref[...] + (1.0 - ADAM_B1) * gv
        vn = ADAM_B2 * v_ref[...] + (1.0 - ADAM_B2) * jnp.square(gv)
        m_hat = mn / (1.0 - ADAM_B1 ** ADAM_STEP)
        v_hat = vn / (1.0 - ADAM_B2 ** ADAM_STEP)
        d_ref[...] = -ADAM_LR * (m_hat / (jnp.sqrt(v_hat) + ADAM_EPS) + ADAM_WD * w_ref[...])
        nm_ref[...] = mn
        nv_ref[...] = vn

    spec = pl.BlockSpec((tr, cols), lambda i: (i, 0))
    shp = jax.ShapeDtypeStruct((r, cols), F32)
    res = pl.pallas_call(
        body, name=name, grid=(r // tr,), in_specs=[spec] * 4, out_specs=[spec] * 3, out_shape=[shp] * 3,
        compiler_params=_params("parallel"),
    )(*[t.reshape(r, cols) for t in (wt, g, m, v)])
    return [t.reshape(wt.shape) for t in res]


def _join_shards(n, piece):
    _, depth, a, b = piece.shape
    if BIG_AXIS[n] == 2:
        return piece.transpose(1, 2, 0, 3).reshape(depth, a, N_CHIPS * b)
    return piece.transpose(1, 0, 2, 3).reshape(depth, N_CHIPS * a, b)


def _split_shards(n, g):
    depth, a, b = g.shape
    if BIG_AXIS[n] == 2:
        return g.reshape(depth, a, N_CHIPS, b // N_CHIPS).transpose(2, 0, 1, 3)
    return g.reshape(depth, N_CHIPS, a // N_CHIPS, b).transpose(1, 0, 2, 3)


def _pack_small(ts, rows):
    flat = jnp.concatenate([t.reshape(-1) for t in ts])
    return jnp.pad(flat, (0, rows * COLS - flat.shape[0])).reshape(rows, COLS)


def _small_rows(n_elems):
    return -(-n_elems // (SUBLANE * COLS)) * SUBLANE


def kernel(x, norm_mix, w_in, b_gate, attn_sinks, w_attn_o, conv_w, w_conv_o, ssm_a_re, ssm_a_im, ssm_b_re, ssm_b_im, ssm_c_re, ssm_c_im, ssm_d, ssm_log_dt, w_ssm_glu, w_ssm_o, w_mix_o, norm_ffn, w_ffn_in, w_ffn_out, norm_final, loss_target, m_norm_mix, m_w_in, m_b_gate, m_attn_sinks, m_w_attn_o, m_conv_w, m_w_conv_o, m_ssm_a_re, m_ssm_a_im, m_ssm_b_re, m_ssm_b_im, m_ssm_c_re, m_ssm_c_im, m_ssm_d, m_ssm_log_dt, m_w_ssm_glu, m_w_ssm_o, m_w_mix_o, m_norm_ffn, m_w_ffn_in, m_w_ffn_out, m_norm_final, v_norm_mix, v_w_in, v_b_gate, v_attn_sinks, v_w_attn_o, v_conv_w, v_w_conv_o, v_ssm_a_re, v_ssm_a_im, v_ssm_b_re, v_ssm_b_im, v_ssm_c_re, v_ssm_c_im, v_ssm_d, v_ssm_log_dt, v_w_ssm_glu, v_w_ssm_o, v_w_mix_o, v_norm_ffn, v_w_ffn_in, v_w_ffn_out, v_norm_final):
    a = dict(zip(ARG_NAMES, (
        x, norm_mix, w_in, b_gate, attn_sinks, w_attn_o, conv_w, w_conv_o, ssm_a_re, ssm_a_im, ssm_b_re, ssm_b_im,
        ssm_c_re, ssm_c_im, ssm_d, ssm_log_dt, w_ssm_glu, w_ssm_o, w_mix_o, norm_ffn, w_ffn_in, w_ffn_out, norm_final,
        loss_target, m_norm_mix, m_w_in, m_b_gate, m_attn_sinks, m_w_attn_o, m_conv_w, m_w_conv_o, m_ssm_a_re,
        m_ssm_a_im, m_ssm_b_re, m_ssm_b_im, m_ssm_c_re, m_ssm_c_im, m_ssm_d, m_ssm_log_dt, m_w_ssm_glu, m_w_ssm_o,
        m_w_mix_o, m_norm_ffn, m_w_ffn_in, m_w_ffn_out, m_norm_final, v_norm_mix, v_w_in, v_b_gate, v_attn_sinks,
        v_w_attn_o, v_conv_w, v_w_conv_o, v_ssm_a_re, v_ssm_a_im, v_ssm_b_re, v_ssm_b_im, v_ssm_c_re, v_ssm_c_im,
        v_ssm_d, v_ssm_log_dt, v_w_ssm_glu, v_w_ssm_o, v_w_mix_o, v_norm_ffn, v_w_ffn_in, v_w_ffn_out, v_norm_final)))
    px, py, _ = _place()
    chip = 2 * px + py

    own = [a[n].astype(BF16) for n in BIG]
    gath = _gather_weights(own, "gather_weights")
    w = {n: _join_shards(n, lax.dynamic_update_slice_in_dim(p, o[None], chip, axis=0))
         for n, p, o in zip(BIG, gath, own)}
    for n in SMALL:
        w[n] = a[n]
    cw_local = a["conv_w"]
    cw_rows = _small_rows(cw_local.size)
    cw_all = _gather_all(_pack_small([cw_local], cw_rows), "gather_conv_w").reshape(N_DEV, -1)
    cw = jnp.stack([cw_all[2 * s, :cw_local.size].reshape(cw_local.shape) for s in range(N_CHIPS)])
    w["conv_w"] = cw.transpose(1, 2, 0, 3).reshape(cw_local.shape[0], CONV_K, -1)

    loss, dx, grads = _local_step(a["x"][0], a["loss_target"][0], w)

    red = dict(zip(BIG, _reduce_scatter([_split_shards(n, grads[n]) for n in BIG], BIG)))
    small_names = SMALL + ("conv_w",)
    n_small = sum(grads[n].size for n in small_names)
    small = _all_reduce_small(_pack_small([grads[n] for n in small_names], _small_rows(n_small))).reshape(-1)
    off = 0
    for n in small_names:
        red[n] = small[off:off + grads[n].size].reshape(grads[n].shape)
        off += grads[n].size
    lane = cw_local.shape[2]
    red["conv_w"] = lax.dynamic_slice_in_dim(red["conv_w"], chip * lane, lane, axis=2)

    loss_all = lax.psum(loss[0, 0], ("x", "y", "c"))
    deltas, new_m, new_v = [], [], []
    for n in WEIGHTS:
        d, mn, vn = _adamw(a[n], red[n], a["m_" + n], a["v_" + n], "adamw_" + n)
        deltas.append(d)
        new_m.append(mn)
        new_v.append(vn)
    return (loss_all, dx[None], *[red[n] for n in WEIGHTS], *deltas, *new_m, *new_v)
```

```python
import functools
import math

import jax
import jax.numpy as jnp
import numpy as np
from jax import lax
from jax.experimental import pallas as pl
from jax.experimental.pallas import tpu as pltpu

F32 = jnp.float32
BF16 = jnp.bfloat16

D_MODEL = 1024
DEPTH = 4
N_Q_HEADS = 8
N_KV_HEADS = 2
HEAD_DIM = 64
Q_GROUP = N_Q_HEADS // N_KV_HEADS
WINDOW = 128
BLOCK = 128
ROPE_THETA = 500000.0
ROT_DIM = HEAD_DIM // 4
ATTN_WIDTH = N_Q_HEADS * HEAD_DIM
KV_WIDTH = N_KV_HEADS * HEAD_DIM
NEG_INF = -1e30
CONV_WIDTH = 512
CONV_K = 3
SSM_WIDTH = 512
SSM_GROUP = 16
SSM_GROUPS = 32
SSM_STATE = 64
SSM_LANES = SSM_GROUPS * SSM_STATE
GATE_WIDTH = 3 * D_MODEL
FFN_HIDDEN = 2816
NORM_EPS = 1e-6
IN_COLS = 5888
C_Q, C_K, C_V, C_CB, C_CC, C_CX, C_U, C_G = 0, 512, 640, 768, 1280, 1792, 2304, 2816

ADAM_LR = 0.001
ADAM_B1 = 0.9
ADAM_B2 = 0.999
ADAM_EPS = 1e-08
ADAM_WD = 0.01
ADAM_STEP = 10

N_CHIPS = 4
N_DEV = 8
MESH_ID = pl.DeviceIdType.MESH

VMEM_LIMIT_BYTES = 48 * 1024 * 1024
LANE = 128
SUBLANE = 8
SCAN_ROWS = 8
SCAN_CHUNK = 128

BIG = ("w_in", "w_attn_o", "w_conv_o", "w_ssm_glu", "w_ssm_o", "w_mix_o", "w_ffn_in", "w_ffn_out")
BIG_AXIS = {"w_in": 2, "w_attn_o": 2, "w_conv_o": 2, "w_ssm_glu": 1, "w_ssm_o": 2, "w_mix_o": 1,
            "w_ffn_in": 2, "w_ffn_out": 1, "conv_w": 2}
SMALL = ("norm_mix", "b_gate", "attn_sinks", "ssm_a_re", "ssm_a_im", "ssm_b_re", "ssm_b_im",
         "ssm_c_re", "ssm_c_im", "ssm_d", "ssm_log_dt", "norm_ffn", "norm_final")
WEIGHTS = ("norm_mix", "w_in", "b_gate", "attn_sinks", "w_attn_o", "conv_w", "w_conv_o", "ssm_a_re",
           "ssm_a_im", "ssm_b_re", "ssm_b_im", "ssm_c_re", "ssm_c_im", "ssm_d", "ssm_log_dt",
           "w_ssm_glu", "w_ssm_o", "w_mix_o", "norm_ffn", "w_ffn_in", "w_ffn_out", "norm_final")
ARG_NAMES = ("x",) + WEIGHTS + ("loss_target",) + tuple("m_" + n for n in WEIGHTS) + tuple(
    "v_" + n for n in WEIGHTS)


def _params(*sem):
    return pltpu.CompilerParams(dimension_semantics=sem if sem else None,
                                vmem_limit_bytes=VMEM_LIMIT_BYTES)


def _tile(dim, cap, align):
    t = min(cap, dim) // align * align
    while t >= align:
        if dim % t == 0:
            return t
        t -= align
    return dim


_DOT_DIMS = {"nn": (((1,), (0,)), ((), ())), "nt": (((1,), (1,)), ((), ())), "tn": (((0,), (0,)), ((), ()))}


def _mm(a, b, mode, name, out_dtype=F32, add=None, tm_cap=512, tn_cap=3072, tk_cap=1024):
    if mode == "nn":
        (m, k), (k2, n) = a.shape, b.shape
    elif mode == "nt":
        (m, k), (n, k2) = a.shape, b.shape
    else:
        (k, m), (k2, n) = a.shape, b.shape
    assert k == k2, (name, a.shape, b.shape)
    tm, tn, tk = _tile(m, tm_cap, LANE), _tile(n, tn_cap, LANE), _tile(k, tk_cap, LANE)
    nk = k // tk
    dims = _DOT_DIMS[mode]

    def body(a_ref, b_ref, *rest):
        if add is None:
            o_ref, acc = rest
        else:
            add_ref, o_ref, acc = rest
        kk = pl.program_id(2)

        @pl.when(kk == 0)
        def _():
            acc[...] = jnp.zeros_like(acc)

        acc[...] += lax.dot_general(a_ref[...].astype(BF16), b_ref[...].astype(BF16), dims,
                                    preferred_element_type=F32)

        @pl.when(kk == nk - 1)
        def _():
            r = acc[...]
            if add is not None:
                r = r + add_ref[...]
            o_ref[...] = r.astype(out_dtype)

    if mode == "tn":
        a_spec = pl.BlockSpec((tk, tm), lambda i, j, kk: (kk, i))
    else:
        a_spec = pl.BlockSpec((tm, tk), lambda i, j, kk: (i, kk))
    if mode == "nt":
        b_spec = pl.BlockSpec((tn, tk), lambda i, j, kk: (j, kk))
    else:
        b_spec = pl.BlockSpec((tk, tn), lambda i, j, kk: (kk, j))
    o_spec = pl.BlockSpec((tm, tn), lambda i, j, kk: (i, j))
    in_specs, args = [a_spec, b_spec], [a, b]
    if add is not None:
        in_specs.append(o_spec)
        args.append(add)
    return pl.pallas_call(
        body, name=name, grid=(m // tm, n // tn, nk), in_specs=in_specs, out_specs=o_spec,
        out_shape=jax.ShapeDtypeStruct((m, n), out_dtype),
        scratch_shapes=[pltpu.VMEM((tm, tn), F32)],
        compiler_params=_params("parallel", "parallel", "arbitrary"),
    )(*args)


def _rowwise(fn, rows, pars, outs, accs, name, tm_cap=256):
    length = rows[0][0].shape[0]
    tm = _tile(length, tm_cap, LANE)
    n = length // tm
    in_specs, args, counts = [], [], []
    for arr, c0, cw, shift in rows:
        bw = math.gcd(c0, cw) if c0 else cw
        assert bw % LANE == 0 or (c0 == 0 and cw == arr.shape[1]), (name, c0, cw)
        cnt = cw // bw
        counts.append(cnt)
        for j in range(cnt):
            in_specs.append(pl.BlockSpec(
                (tm, bw), lambda i, j=j, c0=c0, bw=bw, shift=shift: (jnp.clip(i + shift, 0, n - 1), c0 // bw + j)))
            args.append(arr)
    for p in pars:
        in_specs.append(pl.BlockSpec(p.shape, lambda i: (0, 0)))
        args.append(p)
    out_shape = [jax.ShapeDtypeStruct((length, w), dt) for w, dt in outs]
    out_specs = [pl.BlockSpec((tm, w), lambda i: (i, 0)) for w, _ in outs]
    out_shape += [jax.ShapeDtypeStruct((r, w), F32) for r, w in accs]
    out_specs += [pl.BlockSpec((r, w), lambda i: (0, 0)) for r, w in accs]
    n_in, n_out = len(args), len(outs)

    def body(*refs):
        i = pl.program_id(0)
        vals, p = [], 0
        for cnt in counts:
            blocks = [refs[p + j][...] for j in range(cnt)]
            p += cnt
            vals.append(blocks[0] if cnt == 1 else jnp.concatenate(blocks, axis=1))
        for _ in pars:
            vals.append(refs[p][...])
            p += 1
        res = fn((i, n), *vals)
        out_refs = refs[n_in:n_in + n_out]
        acc_refs = refs[n_in + n_out:]
        for r, v in zip(out_refs, res[:n_out]):
            r[...] = v.astype(r.dtype)
        if acc_refs:
            @pl.when(i == 0)
            def _():
                for r in acc_refs:
                    r[...] = jnp.zeros_like(r)
            for r, v in zip(acc_refs, res[n_out:]):
                r[...] += v

    res = pl.pallas_call(
        body, name=name, grid=(n,), in_specs=in_specs, out_specs=out_specs, out_shape=out_shape,
        compiler_params=_params("arbitrary"),
    )(*args)
    return res


def _rms(x, g):
    return x * lax.rsqrt(jnp.mean(x * x, axis=-1, keepdims=True) + NORM_EPS) * g


def _rms_fwd(x, g, name):
    return _rowwise(lambda ctx, xv, gv: (_rms(xv, gv),), [(x, 0, D_MODEL, 0)], [g],
                    [(D_MODEL, BF16)], [], name)[0]


def _rms_bwd(x, g, dh, dres, name):
    def fn(ctx, xv, dhv, drv, gv):
        _, vjp = jax.vjp(_rms, xv, gv)
        dx, dg = vjp(dhv)
        return dx + drv, dg
    return _rowwise(fn, [(x, 0, D_MODEL, 0), (dh, 0, D_MODEL, 0), (dres, 0, D_MODEL, 0)], [g],
                    [(D_MODEL, F32)], [(1, D_MODEL)], name)


def _rope_tables(length):
    pos = jnp.arange(length, dtype=F32)
    inv_freq = ROPE_THETA ** (-jnp.arange(0, ROT_DIM, 2, dtype=F32) / ROT_DIM)
    ang = pos[:, None] * inv_freq[None, :]
    cos, sin = jnp.cos(ang), jnp.sin(ang)
    half = ROT_DIM // 2
    ones = jnp.ones((length, HEAD_DIM - ROT_DIM), F32)
    zeros = jnp.zeros_like(ones)
    zh = jnp.zeros((length, half), F32)
    c64 = jnp.concatenate([cos, cos, ones], axis=1)
    s1 = jnp.concatenate([-sin, zh, zeros], axis=1)
    s2 = jnp.concatenate([zh, sin, zeros], axis=1)
    tile2 = lambda t: jnp.concatenate([t, t], axis=1)
    return tile2(c64), tile2(s1), tile2(s2)


def _lane_chunks(t):
    return [t[:, j * LANE:(j + 1) * LANE] for j in range(t.shape[1] // LANE)]


def _rope(t, c, s1, s2, n_rot):
    half = ROT_DIM // 2
    out = []
    for j, ch in enumerate(_lane_chunks(t)):
        if j < n_rot:
            ch = ch * c + pltpu.roll(ch, LANE - half, 1) * s1 + pltpu.roll(ch, half, 1) * s2
        out.append(ch)
    return jnp.concatenate(out, axis=1)


def _unrope(d, c, s1, s2, n_rot):
    half = ROT_DIM // 2
    out = []
    for j, ch in enumerate(_lane_chunks(d)):
        if j < n_rot:
            ch = ch * c + pltpu.roll(ch * s1, half, 1) + pltpu.roll(ch * s2, LANE - half, 1)
        out.append(ch)
    return jnp.concatenate(out, axis=1)


N_ROT_CHUNKS = (ATTN_WIDTH + KV_WIDTH) // LANE
QKV_WIDTH = ATTN_WIDTH + 2 * KV_WIDTH


def _split_fwd(proj, tabs, name):
    def fn(ctx, t, c, s1, s2):
        return (_rope(t, c, s1, s2, N_ROT_CHUNKS),)
    rows = [(proj, 0, QKV_WIDTH, 0)] + [(t, 0, LANE, 0) for t in tabs]
    return _rowwise(fn, rows, [], [(QKV_WIDTH, BF16)], [], name, tm_cap=BLOCK)[0]


def _split_bwd(dq, dkc, dkp, dvc, dvp, tabs, name):
    def fn(ctx, dqv, dkcv, dkpv, dvcv, dvpv, c, s1, s2):
        i, n = ctx
        keep = (i < n - 1).astype(F32)
        d = jnp.concatenate([dqv, dkcv + keep * dkpv, dvcv + keep * dvpv], axis=1)
        return (_unrope(d, c, s1, s2, N_ROT_CHUNKS),)
    rows = [(dq, 0, ATTN_WIDTH, 0), (dkc, 0, KV_WIDTH, 0), (dkp, 0, KV_WIDTH, 1), (dvc, 0, KV_WIDTH, 0),
            (dvp, 0, KV_WIDTH, 1)] + [(t, 0, LANE, 0) for t in tabs]
    return _rowwise(fn, rows, [], [(QKV_WIDTH, BF16)], [], name, tm_cap=BLOCK)[0]


def _att_scores(q_ref, kp_ref, kc_ref, sink_ref):
    n = pl.program_id(1)
    j = pl.program_id(0)
    rows = Q_GROUP * BLOCK
    qs = q_ref[...].reshape(rows, HEAD_DIM)
    kb = jnp.concatenate([kp_ref[0], kc_ref[0]], axis=0)
    s = lax.dot_general(qs, kb, _DOT_DIMS["nt"], preferred_element_type=F32) * (HEAD_DIM ** -0.5)
    r = lax.broadcasted_iota(jnp.int32, (rows, 2 * BLOCK), 0)
    kj = lax.broadcasted_iota(jnp.int32, (rows, 2 * BLOCK), 1)
    delta = (r % BLOCK) + BLOCK - kj
    ok = (delta >= 0) & (delta < WINDOW) & ((kj >= BLOCK) | (n > 0))
    s = jnp.where(ok, s, NEG_INF)
    rh = lax.broadcasted_iota(jnp.int32, (rows, 1), 0) // BLOCK
    sinks = sink_ref[...]
    lane = lax.broadcasted_iota(jnp.int32, sinks.shape, 1)
    srow = lax.broadcasted_iota(jnp.int32, sinks.shape, 0)
    sink = jnp.zeros((rows, 1), F32)
    for g in range(Q_GROUP):
        val = jnp.sum(jnp.where((lane == g) & (srow == j), sinks, 0.0), keepdims=True)
        sink = jnp.where(rh == g, val, sink)
    m = jnp.maximum(jnp.max(s, axis=-1, keepdims=True), sink)
    p = jnp.exp(s - m)
    psink = jnp.exp(sink - m)
    denom = jnp.sum(p, axis=-1, keepdims=True) + psink
    return qs, kb, p / denom, psink / denom, rh


def _att_specs(length):
    nb = length // BLOCK
    q_spec = pl.BlockSpec((Q_GROUP, BLOCK, HEAD_DIM), lambda j, n: (j, n, 0))
    prev = pl.BlockSpec((1, BLOCK, HEAD_DIM), lambda j, n: (j, jnp.maximum(n - 1, 0), 0))
    cur = pl.BlockSpec((1, BLOCK, HEAD_DIM), lambda j, n: (j, n, 0))
    sink_spec = pl.BlockSpec((N_KV_HEADS, Q_GROUP), lambda j, n: (0, 0))
    return nb, q_spec, prev, cur, sink_spec


def _att_fwd(q, k, v, sinks, name):
    length = q.shape[1]
    nb, q_spec, prev, cur, sink_spec = _att_specs(length)

    def body(q_ref, kp_ref, kc_ref, vp_ref, vc_ref, sink_ref, o_ref):
        _, _, p, _, _ = _att_scores(q_ref, kp_ref, kc_ref, sink_ref)
        vb = jnp.concatenate([vp_ref[0], vc_ref[0]], axis=0)
        o = jnp.dot(p.astype(BF16), vb, preferred_element_type=F32)
        o_ref[...] = o.reshape(Q_GROUP, BLOCK, HEAD_DIM).astype(o_ref.dtype)

    return pl.pallas_call(
        body, name=name, grid=(N_KV_HEADS, nb),
        in_specs=[q_spec, prev, cur, prev, cur, sink_spec], out_specs=q_spec,
        out_shape=jax.ShapeDtypeStruct((N_Q_HEADS, length, HEAD_DIM), BF16),
        compiler_params=_params("arbitrary", "arbitrary"),
    )(q, k, k, v, v, sinks)


def _att_bwd(q, k, v, sinks, do, name):
    length = q.shape[1]
    nb, q_spec, prev, cur, sink_spec = _att_specs(length)
    ds_spec = pl.BlockSpec((1, SUBLANE, LANE), lambda j, n: (j, 0, 0))

    def body(q_ref, kp_ref, kc_ref, vp_ref, vc_ref, sink_ref, do_ref, dq_ref, dkc_ref, dkp_ref, dvc_ref,
             dvp_ref, dsink_ref):
        n = pl.program_id(1)
        qs, kb, p, psink, rh = _att_scores(q_ref, kp_ref, kc_ref, sink_ref)
        vb = jnp.concatenate([vp_ref[0], vc_ref[0]], axis=0)
        dob = do_ref[...].reshape(Q_GROUP * BLOCK, HEAD_DIM).astype(BF16)
        dv = lax.dot_general(p.astype(BF16), dob, _DOT_DIMS["tn"], preferred_element_type=F32)
        dp = lax.dot_general(dob, vb, _DOT_DIMS["nt"], preferred_element_type=F32)
        dsum = jnp.sum(p * dp, axis=-1, keepdims=True)
        ds = (p * (dp - dsum) * (HEAD_DIM ** -0.5)).astype(BF16)
        dq = jnp.dot(ds, kb, preferred_element_type=F32)
        dk = lax.dot_general(ds, qs, _DOT_DIMS["tn"], preferred_element_type=F32)
        dq_ref[...] = dq.reshape(Q_GROUP, BLOCK, HEAD_DIM)
        dkp_ref[0] = dk[:BLOCK]
        dkc_ref[0] = dk[BLOCK:]
        dvp_ref[0] = dv[:BLOCK]
        dvc_ref[0] = dv[BLOCK:]
        dsr = -psink * dsum
        row = lax.broadcasted_iota(jnp.int32, (SUBLANE, LANE), 0)
        upd = jnp.zeros((SUBLANE, LANE), F32)
        for g in range(Q_GROUP):
            val = jnp.sum(jnp.where(rh == g, dsr, 0.0), keepdims=True)
            upd = jnp.where(row == g, val, upd)

        @pl.when(n == 0)
        def _():
            dsink_ref[...] = jnp.zeros_like(dsink_ref)

        dsink_ref[0] += upd

    kv_shape = jax.ShapeDtypeStruct((N_KV_HEADS, length, HEAD_DIM), F32)
    return pl.pallas_call(
        body, name=name, grid=(N_KV_HEADS, nb),
        in_specs=[q_spec, prev, cur, prev, cur, sink_spec, q_spec],
        out_specs=[q_spec, cur, cur, cur, cur, ds_spec],
        out_shape=[jax.ShapeDtypeStruct((N_Q_HEADS, length, HEAD_DIM), F32), kv_shape, kv_shape, kv_shape,
                   kv_shape, jax.ShapeDtypeStruct((N_KV_HEADS, SUBLANE, LANE), F32)],
        compiler_params=_params("arbitrary", "arbitrary"),
    )(q, k, k, v, v, sinks, do)


def _to_heads(t, heads):
    return t.reshape(t.shape[0], heads, HEAD_DIM).transpose(1, 0, 2)


def _from_heads(t):
    return t.transpose(1, 0, 2).reshape(t.shape[1], t.shape[0] * HEAD_DIM)


def _shift_down(z, s):
    t = lax.broadcasted_iota(jnp.int32, z.shape, 0)
    return jnp.where(t >= s, pltpu.roll(z, s, 0), 0.0)


def _shift_up(z, s):
    t = lax.broadcasted_iota(jnp.int32, z.shape, 0)
    return jnp.where(t < z.shape[0] - s, pltpu.roll(z, z.shape[0] - s, 0), 0.0)


def _conv_specs(length):
    col = lambda c0: pl.BlockSpec((length, LANE), lambda j, c0=c0: (0, c0 // LANE + j))
    w_spec = pl.BlockSpec((CONV_K, LANE), lambda j: (0, j))
    o_spec = pl.BlockSpec((length, LANE), lambda j: (0, j))
    return col, w_spec, o_spec


def _conv_fwd(proj, w, name):
    length = proj.shape[0]
    col, w_spec, o_spec = _conv_specs(length)

    def body(cb_ref, cc_ref, cx_ref, w_ref, o_ref):
        z = cc_ref[...] * cx_ref[...]
        s = w_ref[0:1, :] * _shift_down(z, 2) + w_ref[1:2, :] * _shift_down(z, 1) + w_ref[2:3, :] * z
        o_ref[...] = (cb_ref[...] * s).astype(o_ref.dtype)

    return pl.pallas_call(
        body, name=name, grid=(CONV_WIDTH // LANE,),
        in_specs=[col(C_CB), col(C_CC), col(C_CX), w_spec], out_specs=o_spec,
        out_shape=jax.ShapeDtypeStruct((length, CONV_WIDTH), BF16),
        compiler_params=_params("arbitrary"),
    )(proj, proj, proj, w)


def _conv_bwd(proj, w, dy, name):
    length = proj.shape[0]
    col, w_spec, o_spec = _conv_specs(length)
    dw_spec = pl.BlockSpec((1, LANE), lambda j: (0, j))

    def body(cb_ref, cc_ref, cx_ref, w_ref, dy_ref, dcb_ref, dcc_ref, dcx_ref, dw0_ref, dw1_ref, dw2_ref):
        cc, cx, dyv = cc_ref[...], cx_ref[...], dy_ref[...]
        z = cc * cx
        w0, w1, w2 = w_ref[0:1, :], w_ref[1:2, :], w_ref[2:3, :]
        z1, z2 = _shift_down(z, 1), _shift_down(z, 2)
        s = w0 * z2 + w1 * z1 + w2 * z
        dcb_ref[...] = (dyv * s).astype(dcb_ref.dtype)
        ds = dyv * cb_ref[...]
        dw0_ref[...] = jnp.sum(ds * z2, axis=0, keepdims=True)
        dw1_ref[...] = jnp.sum(ds * z1, axis=0, keepdims=True)
        dw2_ref[...] = jnp.sum(ds * z, axis=0, keepdims=True)
        dz = w2 * ds + w1 * _shift_up(ds, 1) + w0 * _shift_up(ds, 2)
        dcc_ref[...] = (dz * cx).astype(dcc_ref.dtype)
        dcx_ref[...] = (dz * cc).astype(dcx_ref.dtype)

    act = jax.ShapeDtypeStruct((length, CONV_WIDTH), BF16)
    dws = jax.ShapeDtypeStruct((1, CONV_WIDTH), F32)
    return pl.pallas_call(
        body, name=name, grid=(CONV_WIDTH // LANE,),
        in_specs=[col(C_CB), col(C_CC), col(C_CX), w_spec, o_spec],
        out_specs=[o_spec, o_spec, o_spec, dw_spec, dw_spec, dw_spec],
        out_shape=[act, act, act, dws, dws, dws],
        compiler_params=_params("arbitrary"),
    )(proj, proj, proj, w, dy)


def _cmul(ar, ai, br, bi):
    return ar * br - ai * bi, ar * bi + ai * br


def _scan_tables(lr, li, reverse):
    pr, pi = [lr], [li]
    for _ in range(SCAN_ROWS - 1):
        nr, ni = _cmul(pr[-1], pi[-1], lr, li)
        pr.append(nr)
        pi.append(ni)
    row = jnp.arange(SCAN_ROWS)[:, None]
    mr, mi = [], []
    for s in (1, 2, 4):
        live = (row + s < SCAN_ROWS) if reverse else (row >= s)
        mr.append(jnp.where(live, pr[s - 1], 0.0))
        mi.append(jnp.where(live, pi[s - 1], 0.0))
    order = range(SCAN_ROWS - 1, -1, -1) if reverse else range(SCAN_ROWS)
    carry_r = jnp.concatenate([pr[d] for d in order], axis=0)
    carry_i = jnp.concatenate([pi[d] for d in order], axis=0)
    return jnp.stack(mr), jnp.stack(mi), carry_r, carry_i


def _scan(b, lr, li, reverse, name):
    length = b.shape[0]
    mr, mi, cr, ci = _scan_tables(lr, li, reverse)
    nchunk = length // SCAN_CHUNK
    nblk = SCAN_CHUNK // SCAN_ROWS
    half = SSM_LANES

    def body(b_ref, mr_ref, mi_ref, cr_ref, ci_ref, o_ref, carry):
        @pl.when(pl.program_id(0) == 0)
        def _():
            carry[...] = jnp.zeros_like(carry)

        c_r, c_i = carry[0:1, :], carry[1:2, :]
        blocks = range(nblk - 1, -1, -1) if reverse else range(nblk)
        for blk in blocks:
            r0 = blk * SCAN_ROWS
            xr = b_ref[r0:r0 + SCAN_ROWS, :half]
            xi = b_ref[r0:r0 + SCAN_ROWS, half:]
            for kk, s in enumerate((1, 2, 4)):
                sh = SCAN_ROWS - s if reverse else s
                rr, ri = pltpu.roll(xr, sh, 0), pltpu.roll(xi, sh, 0)
                ar, ai = _cmul(mr_ref[kk], mi_ref[kk], rr, ri)
                xr, xi = xr + ar, xi + ai
            ar, ai = _cmul(cr_ref[...], ci_ref[...], c_r, c_i)
            xr, xi = xr + ar, xi + ai
            o_ref[r0:r0 + SCAN_ROWS, :half] = xr
            o_ref[r0:r0 + SCAN_ROWS, half:] = xi
            edge = r0 if reverse else r0 + SCAN_ROWS - 1
            c_r = o_ref[edge:edge + 1, :half]
            c_i = o_ref[edge:edge + 1, half:]
        carry[0:1, :] = c_r
        carry[1:2, :] = c_i

    chunk = (lambda i: (nchunk - 1 - i, 0)) if reverse else (lambda i: (i, 0))
    blk_spec = pl.BlockSpec((SCAN_CHUNK, 2 * half), chunk)
    m_spec = pl.BlockSpec((3, SCAN_ROWS, half), lambda i: (0, 0, 0))
    c_spec = pl.BlockSpec((SCAN_ROWS, half), lambda i: (0, 0))
    return pl.pallas_call(
        body, name=name, grid=(nchunk,), in_specs=[blk_spec, m_spec, m_spec, c_spec, c_spec],
        out_specs=blk_spec, out_shape=jax.ShapeDtypeStruct(b.shape, F32),
        scratch_shapes=[pltpu.VMEM((SUBLANE, half), F32)],
        compiler_params=_params("arbitrary"),
    )(b, mr, mi, cr, ci)


def _dlam(g, states, name):
    half = SSM_LANES

    def fn(ctx, gv, xv, xprev):
        i, _ = ctx
        row = lax.broadcasted_iota(jnp.int32, xv.shape, 0)
        last = jnp.sum(jnp.where(lax.broadcasted_iota(jnp.int32, xprev.shape, 0) == xprev.shape[0] - 1,
                                 xprev, 0.0), axis=0, keepdims=True)
        last = last * (i > 0).astype(F32)
        xs = jnp.where(row == 0, last, pltpu.roll(xv, 1, 0))
        gr, gi, xr, xi = gv[:, :half], gv[:, half:], xs[:, :half], xs[:, half:]
        dre = jnp.sum(gr * xr + gi * xi, axis=0, keepdims=True)
        dim = jnp.sum(gi * xr - gr * xi, axis=0, keepdims=True)
        return (jnp.concatenate([dre, dim], axis=1),)

    rows = [(g, 0, 2 * half, 0), (states, 0, 2 * half, 0), (states, 0, 2 * half, -1)]
    return _rowwise(fn, rows, [], [], [(1, 2 * half)], name, tm_cap=128)[0]


def _block_diag(t):
    g, a, b = t.shape
    eye = jnp.eye(g, dtype=t.dtype)
    return (t[:, :, None, :] * eye[:, None, :, None]).reshape(g * a, g * b)


def _ssm_prep(a_re, a_im, b_re, b_im, c_re, c_im, log_dt):
    dt = jnp.exp(log_dt)[:, None]
    er = jnp.exp(a_re * dt)
    lr, li = er * jnp.cos(a_im * dt), er * jnp.sin(a_im * dt)
    nr, ni = lr - 1.0, li
    den = a_re * a_re + a_im * a_im
    qr, qi = (nr * a_re + ni * a_im) / den, (ni * a_re - nr * a_im) / den
    bbr = qr[..., None] * b_re - qi[..., None] * b_im
    bbi = qr[..., None] * b_im + qi[..., None] * b_re
    bmat = jnp.concatenate([_block_diag(bbr.transpose(0, 2, 1)), _block_diag(bbi.transpose(0, 2, 1))], axis=1)
    cmat = jnp.concatenate([_block_diag(c_re.transpose(0, 2, 1)), -_block_diag(c_im.transpose(0, 2, 1))], axis=0)
    return lr.reshape(1, SSM_LANES), li.reshape(1, SSM_LANES), bmat, cmat


def _ssm_act(yc, u, d):
    return jax.nn.gelu(yc + d * u)


def _glu(ys, z):
    return ys * jax.nn.sigmoid(z)


def _merge(ya, yc, ys, gl, b):
    gates = jax.nn.sigmoid(gl + b)
    return gates[:, :D_MODEL] * ya + gates[:, D_MODEL:2 * D_MODEL] * yc + gates[:, 2 * D_MODEL:] * ys


def _swiglu(gu):
    return jax.nn.silu(gu[:, :FFN_HIDDEN]) * gu[:, FFN_HIDDEN:]


def _loss_fn(x, g, t):
    e = _rms(x, g) - t
    per_tok = jnp.mean(e * e, axis=-1, keepdims=True)
    return 0.5 * jnp.sum(per_tok, axis=0, keepdims=True)


def _vjp_rowwise(f, n_row, cot_dtype=F32):
    def fn(ctx, *vals):
        prim = vals[:n_row] + vals[n_row + 1:]
        _, vjp = jax.vjp(f, *prim)
        return vjp(vals[n_row].astype(cot_dtype))
    return fn


def _layer_fwd(i, x, w, tabs):
    nm = lambda s: "l%d_%s" % (i, s)
    sv = {"x": x}
    h = _rms_fwd(x, w["norm_mix"][i:i + 1], nm("rms_mix"))
    proj = _mm(h, w["w_in"][i], "nn", nm("mm_in"))
    qkv = _split_fwd(proj, tabs, nm("rope"))
    q, k, v = (_to_heads(qkv[:, :C_K], N_Q_HEADS), _to_heads(qkv[:, C_K:C_V], N_KV_HEADS),
               _to_heads(qkv[:, C_V:C_CB], N_KV_HEADS))
    sinks = w["attn_sinks"][i].reshape(N_KV_HEADS, Q_GROUP)
    att = _from_heads(_att_fwd(q, k, v, sinks, nm("att")))
    conv = _conv_fwd(proj, w["conv_w"][i], nm("conv"))
    lr, li, bmat, cmat = w["ssm"][i]
    u = proj[:, C_U:C_G]
    bu = _mm(u, bmat, "nn", nm("mm_bu"), tn_cap=1024)
    states = _scan(bu, lr, li, False, nm("scan"))
    yc = _mm(states, cmat, "nn", nm("mm_c"))
    d = w["ssm_d"][i:i + 1]
    ys = _rowwise(lambda ctx, a, b, c: (_ssm_act(a, b, c),), [(yc, 0, SSM_WIDTH, 0), (u, 0, SSM_WIDTH, 0)], [d],
                  [(SSM_WIDTH, F32)], [], nm("ssm_act"))[0]
    z = _mm(ys, w["w_ssm_glu"][i], "nn", nm("mm_glu"))
    sg = _rowwise(lambda ctx, a, b: (_glu(a, b),), [(ys, 0, SSM_WIDTH, 0), (z, 0, SSM_WIDTH, 0)], [],
                  [(SSM_WIDTH, BF16)], [], nm("glu"))[0]
    ya = _mm(att, w["w_attn_o"][i], "nn", nm("mm_ao"))
    yv = _mm(conv, w["w_conv_o"][i], "nn", nm("mm_co"))
    ym = _mm(sg, w["w_ssm_o"][i], "nn", nm("mm_so"))
    bg = w["b_gate"][i:i + 1]
    merged = _rowwise(lambda ctx, a, b, c, gl, bb: (_merge(a, b, c, gl, bb),),
                      [(ya, 0, D_MODEL, 0), (yv, 0, D_MODEL, 0), (ym, 0, D_MODEL, 0), (proj, C_G, GATE_WIDTH, 0)],
                      [bg], [(D_MODEL, BF16)], [], nm("merge"))[0]
    x1 = _mm(merged, w["w_mix_o"][i], "nn", nm("mm_mix"), add=x)
    h2 = _rms_fwd(x1, w["norm_ffn"][i:i + 1], nm("rms_ffn"))
    gu = _mm(h2, w["w_ffn_in"][i], "nn", nm("mm_ffn_in"))
    act = _rowwise(lambda ctx, a: (_swiglu(a),), [(gu, 0, 2 * FFN_HIDDEN, 0)], [], [(FFN_HIDDEN, BF16)], [],
                   nm("swiglu"))[0]
    x2 = _mm(act, w["w_ffn_out"][i], "nn", nm("mm_ffn_out"), add=x1)
    sv.update(h=h, proj=proj, q=q, k=k, v=v, att=att, conv=conv, u=u, states=states, yc=yc, ys=ys, z=z, sg=sg,
              ya=ya, yv=yv, ym=ym, merged=merged, x1=x1, h2=h2, gu=gu, act=act)
    return x2, sv


def _layer_bwd(i, dx2, sv, w, tabs):
    nm = lambda s: "l%d_b_%s" % (i, s)
    g = {}
    dact = _mm(dx2, w["w_ffn_out"][i], "nt", nm("mm_dact"))
    g["w_ffn_out"] = _mm(sv["act"], dx2, "tn", nm("mm_gw_ffn_out"))
    dgu = _rowwise(_vjp_rowwise(_swiglu, 1), [(sv["gu"], 0, 2 * FFN_HIDDEN, 0), (dact, 0, FFN_HIDDEN, 0)], [],
                   [(2 * FFN_HIDDEN, BF16)], [], nm("swiglu"))[0]
    dh2 = _mm(dgu, w["w_ffn_in"][i], "nt", nm("mm_dh2"), tk_cap=3072)
    g["w_ffn_in"] = _mm(sv["h2"], dgu, "tn", nm("mm_gw_ffn_in"))
    dx1, g["norm_ffn"] = _rms_bwd(sv["x1"], w["norm_ffn"][i:i + 1], dh2, dx2, nm("rms_ffn"))
    dmerged = _mm(dx1, w["w_mix_o"][i], "nt", nm("mm_dmerged"))
    g["w_mix_o"] = _mm(sv["merged"], dx1, "tn", nm("mm_gw_mix"))
    proj = sv["proj"]
    bg = w["b_gate"][i:i + 1]
    dya, dyv, dym, dgl, g["b_gate"] = _rowwise(
        _vjp_rowwise(_merge, 4),
        [(sv["ya"], 0, D_MODEL, 0), (sv["yv"], 0, D_MODEL, 0), (sv["ym"], 0, D_MODEL, 0),
         (proj, C_G, GATE_WIDTH, 0), (dmerged, 0, D_MODEL, 0)], [bg],
        [(D_MODEL, BF16), (D_MODEL, BF16), (D_MODEL, BF16), (GATE_WIDTH, BF16)], [(1, GATE_WIDTH)], nm("merge"))
    dsg = _mm(dym, w["w_ssm_o"][i], "nt", nm("mm_dsg"))
    g["w_ssm_o"] = _mm(sv["sg"], dym, "tn", nm("mm_gw_so"))
    dys0, dz = _rowwise(_vjp_rowwise(_glu, 2), [(sv["ys"], 0, SSM_WIDTH, 0), (sv["z"], 0, SSM_WIDTH, 0),
                                                 (dsg, 0, SSM_WIDTH, 0)], [],
                        [(SSM_WIDTH, F32), (SSM_WIDTH, BF16)], [], nm("glu"))
    dys = _mm(dz, w["w_ssm_glu"][i], "nt", nm("mm_dys"), add=dys0)
    g["w_ssm_glu"] = _mm(sv["ys"], dz, "tn", nm("mm_gw_glu"))
    d = w["ssm_d"][i:i + 1]
    dyc, du0, g["ssm_d"] = _rowwise(
        _vjp_rowwise(_ssm_act, 2), [(sv["yc"], 0, SSM_WIDTH, 0), (sv["u"], 0, SSM_WIDTH, 0), (dys, 0, SSM_WIDTH, 0)],
        [d], [(SSM_WIDTH, F32), (SSM_WIDTH, F32)], [(1, SSM_WIDTH)], nm("ssm_act"))
    lr, li, bmat, cmat = w["ssm"][i]
    dstates = _mm(dyc, cmat, "nt", nm("mm_dstates"), tn_cap=1024)
    g_cmat = _mm(sv["states"], dyc, "tn", nm("mm_gc"))
    gs = _scan(dstates, lr, -li, True, nm("scan"))
    g_lam = _dlam(gs, sv["states"], nm("dlam"))
    du = _mm(gs, bmat, "nt", nm("mm_du"), out_dtype=BF16, add=du0)
    g_bmat = _mm(sv["u"], gs, "tn", nm("mm_gb"), tn_cap=1024)
    g["ssm"] = (g_lam[:, :SSM_LANES], g_lam[:, SSM_LANES:], g_bmat, g_cmat)
    dconv = _mm(dyv, w["w_conv_o"][i], "nt", nm("mm_dconv"))
    g["w_conv_o"] = _mm(sv["conv"], dyv, "tn", nm("mm_gw_co"))
    dcb, dcc, dcx, dw0, dw1, dw2 = _conv_bwd(proj, w["conv_w"][i], dconv, nm("conv"))
    g["conv_w"] = jnp.concatenate([dw0, dw1, dw2], axis=0)
    datt = _mm(dya, w["w_attn_o"][i], "nt", nm("mm_datt"))
    g["w_attn_o"] = _mm(sv["att"], dya, "tn", nm("mm_gw_ao"))
    sinks = w["attn_sinks"][i].reshape(N_KV_HEADS, Q_GROUP)
    dq, dkc, dkp, dvc, dvp, dsk = _att_bwd(sv["q"], sv["k"], sv["v"], sinks, _to_heads(datt, N_Q_HEADS), nm("att"))
    g["attn_sinks"] = dsk[:, :Q_GROUP, 0].reshape(N_Q_HEADS)
    dqkv = _split_bwd(_from_heads(dq), _from_heads(dkc), _from_heads(dkp), _from_heads(dvc), _from_heads(dvp),
                      tabs, nm("rope"))
    dproj = jnp.concatenate([dqkv, dcb, dcc, dcx, du, dgl], axis=1)
    dh = _mm(dproj, w["w_in"][i], "nt", nm("mm_dh"), tk_cap=3072)
    g["w_in"] = _mm(sv["h"], dproj, "tn", nm("mm_gw_in"))
    dx, g["norm_mix"] = _rms_bwd(sv["x"], w["norm_mix"][i:i + 1], dh, dx1, nm("rms_mix"))
    return dx, g


def _local_step(x, target, w):
    length = x.shape[0]
    tabs = _rope_tables(length)
    ssm_names = ("ssm_a_re", "ssm_a_im", "ssm_b_re", "ssm_b_im", "ssm_c_re", "ssm_c_im", "ssm_log_dt")
    w = dict(w)
    preps = [jax.vjp(_ssm_prep, *[w[n][i] for n in ssm_names]) for i in range(DEPTH)]
    w["ssm"] = [p[0] for p in preps]
    saved = []
    for i in range(DEPTH):
        x, sv = _layer_fwd(i, x, w, tabs)
        saved.append(sv)

    def loss_fn(ctx, xv, tv, gv):
        val, vjp = jax.vjp(_loss_fn, xv, gv, tv)
        dx, dg, _ = vjp(jnp.ones((1, 1), F32))
        return dx, dg, val + jnp.zeros((1, LANE), F32)

    gfin = w["norm_final"].reshape(1, D_MODEL)
    dx, g_final, loss = _rowwise(loss_fn, [(x, 0, D_MODEL, 0), (target, 0, D_MODEL, 0)], [gfin],
                                 [(D_MODEL, F32)], [(1, D_MODEL), (1, LANE)], "loss")
    layer_grads = [None] * DEPTH
    for i in reversed(range(DEPTH)):
        dx, layer_grads[i] = _layer_bwd(i, dx, saved[i], w, tabs)
    grads = {}
    for n in layer_grads[0]:
        if n != "ssm":
            grads[n] = jnp.stack([lg[n] for lg in layer_grads])
    ssm_g = [preps[i][1](layer_grads[i]["ssm"]) for i in range(DEPTH)]
    for j, n in enumerate(ssm_names):
        grads[n] = jnp.stack([sg[j] for sg in ssm_g])
    grads["norm_final"] = g_final.reshape(D_MODEL)
    for n in ("norm_mix", "norm_ffn", "b_gate", "ssm_d"):
        grads[n] = grads[n].reshape(grads[n].shape[0], -1)
    return loss, dx, grads


COLS = 1024
ANY_SPEC = pl.BlockSpec(memory_space=pl.ANY)


def _place():
    return lax.axis_index("x"), lax.axis_index("y"), lax.axis_index("c")


def _other_chips(x, y):
    return [(1 - x, y), (x, 1 - y), (1 - x, 1 - y)]


def _remote(src, dst, send_sems, recv_sems, k, to):
    return pltpu.make_async_remote_copy(src_ref=src, dst_ref=dst, send_sem=send_sems.at[k], recv_sem=recv_sems.at[k],
                                        device_id=to, device_id_type=MESH_ID)


def _comm_call(body, name, out_shape, n_sems, n_local, args):
    return pl.pallas_call(
        body, name=name, out_shape=out_shape, in_specs=[ANY_SPEC] * len(args),
        out_specs=[ANY_SPEC] * len(out_shape),
        scratch_shapes=[pltpu.SemaphoreType.DMA((n_sems,)), pltpu.SemaphoreType.DMA((n_sems,)),
                        pltpu.SemaphoreType.DMA((n_local,))],
    )(*args)


def _gather_weights(shards, name):
    nt = len(shards)
    hds = [s.shape[0] // 2 for s in shards]

    def body(*refs):
        srcs, outs = refs[:nt], refs[nt:2 * nt]
        send_sems, recv_sems, _ = refs[2 * nt:]
        x, y, c = _place()
        me, sibling = (x, y, c), (x, y, 1 - c)
        chips = _other_chips(x, y)

        def blk(t, chip, hc):
            return outs[t].at[2 * chip[0] + chip[1], pl.ds(hc * hds[t], hds[t])]

        first = [_remote(srcs[t].at[pl.ds(c * hds[t], hds[t])], blk(t, (x, y), c), send_sems, recv_sems, 6 * t + j,
                         (*chip, c)) for t in range(nt) for j, chip in enumerate(chips)]
        for cp in first:
            cp.start()
        passed = []
        for j, chip in enumerate(chips):
            for t in range(nt):
                _remote(blk(t, chip, c), blk(t, chip, c), send_sems, recv_sems, 6 * t + j, me).wait_recv()
                passed.append(_remote(blk(t, chip, c), blk(t, chip, c), send_sems, recv_sems, 6 * t + 3 + j, sibling))
                passed[-1].start()
        for j, chip in enumerate(chips):
            for t in range(nt):
                _remote(blk(t, chip, 1 - c), blk(t, chip, 1 - c), send_sems, recv_sems, 6 * t + 3 + j, me).wait_recv()
        for cp in first + passed:
            cp.wait_send()

    out_shape = [jax.ShapeDtypeStruct((N_CHIPS,) + s.shape, s.dtype) for s in shards]
    return _comm_call(body, name, out_shape, 6 * nt, 1, shards)


def _swap_halves(gs, name):
    nt = len(gs)
    hds = [g.shape[1] // 2 for g in gs]

    def body(*refs):
        srcs, outs = refs[:nt], refs[nt:2 * nt]
        send_sems, recv_sems, _ = refs[2 * nt:]
        x, y, c = _place()
        cps = [_remote(srcs[t].at[s, pl.ds((1 - c) * hds[t], hds[t])], outs[t].at[s], send_sems, recv_sems,
                       N_CHIPS * t + s, (x, y, 1 - c)) for t in range(nt) for s in range(N_CHIPS)]
        for cp in cps:
            cp.start()
        for cp in cps:
            cp.wait()

    out_shape = [jax.ShapeDtypeStruct((N_CHIPS, g.shape[1] // 2) + g.shape[2:], g.dtype) for g in gs]
    return _comm_call(body, name, out_shape, N_CHIPS * nt, 1, gs)


def _exchange_shards(parts, name):
    nt = len(parts)

    def body(*refs):
        srcs, outs = refs[:nt], refs[nt:2 * nt]
        send_sems, recv_sems, _ = refs[2 * nt:]
        x, y, c = _place()
        cps = [_remote(srcs[t].at[2 * chip[0] + chip[1]], outs[t].at[j], send_sems, recv_sems, 3 * t + j, (*chip, c))
               for t in range(nt) for j, chip in enumerate(_other_chips(x, y))]
        for cp in cps:
            cp.start()
        for cp in cps:
            cp.wait()

    out_shape = [jax.ShapeDtypeStruct((N_CHIPS - 1,) + p.shape[1:], p.dtype) for p in parts]
    return _comm_call(body, name, out_shape, 3 * nt, 1, parts)


def _join_halves(reds, name):
    nt = len(reds)
    hds = [r.shape[1] for r in reds]

    def body(*refs):
        srcs, outs = refs[:nt], refs[nt:2 * nt]
        send_sems, recv_sems, _ = refs[2 * nt:]
        x, y, c = _place()
        cps = [_remote(srcs[t].at[0], outs[t].at[pl.ds(c * hds[t], hds[t])], send_sems, recv_sems, t, (x, y, 1 - c))
               for t in range(nt)]
        for cp in cps:
            cp.start()
        for t in range(nt):
            _remote(srcs[t].at[0], outs[t].at[pl.ds((1 - c) * hds[t], hds[t])], send_sems, recv_sems, t,
                    (x, y, c)).wait_recv()
        for cp in cps:
            cp.wait_send()

    out_shape = [jax.ShapeDtypeStruct((2 * r.shape[1],) + r.shape[2:], r.dtype) for r in reds]
    return _comm_call(body, name, out_shape, nt, 1, reds)


SUM_BLOCK_ELEMS = 256 * 1024


def _sum_windows(parts, lead, name, out_dtypes=(F32,)):
    a, b = parts[0][0].shape[2:]
    ta = _tile(a, max(SUBLANE, SUM_BLOCK_ELEMS // b), 2 * SUBLANE)
    offs = jnp.stack([jnp.stack([jnp.asarray(o, jnp.int32) for o in off]) for _, off in parts])
    n_in = len(parts)

    def body(off_ref, *refs):
        acc = refs[0][...].astype(F32)
        for r in refs[1:n_in]:
            acc = acc + r[...].astype(F32)
        for r in refs[n_in:]:
            r[...] = acc.astype(r.dtype)

    in_specs = [pl.BlockSpec((1, 1, ta, b), lambda p, q, i, off, k=k: (off[k, 0] + p, off[k, 1] + q, i, 0))
                for k in range(n_in)]
    o_spec = pl.BlockSpec((1, 1, ta, b), lambda p, q, i, off: (p, q, i, 0))
    return pl.pallas_call(
        body, name=name, out_shape=[jax.ShapeDtypeStruct(tuple(lead) + (a, b), dt) for dt in out_dtypes],
        grid_spec=pltpu.PrefetchScalarGridSpec(
            num_scalar_prefetch=1, grid=tuple(lead) + (a // ta,), in_specs=in_specs,
            out_specs=[o_spec] * len(out_dtypes)),
        compiler_params=_params("arbitrary", "arbitrary", "arbitrary"),
    )(offs, *[arr for arr, _ in parts])


def _reduce_scatter(gs, names, wire):
    x, y, c = _place()
    theirs = _swap_halves(gs, "rs_swap_halves")
    pairs = [_sum_windows([(g, (0, c * (g.shape[1] // 2))), (t, (0, 0))], (N_CHIPS, g.shape[1] // 2),
                          "rs_sum_pair_" + n, (F32, wd)) for g, t, n, wd in zip(gs, theirs, names, wire)]
    others = _exchange_shards([p[1] for p in pairs], "rs_exchange")
    reds = [_sum_windows([(p[0], (2 * x + y, 0))] + [(o, (j, 0)) for j in range(N_CHIPS - 1)], (1, p[0].shape[1]),
                         "rs_sum_chips_" + n)[0] for p, o, n in zip(pairs, others, names)]
    joined = _join_halves(reds, "rs_join")
    return [lax.dynamic_update_slice_in_dim(j, r[0], c * r.shape[1], axis=0) for j, r in zip(joined, reds)]


def _adamw(wt, g, m, v, name):
    cols = wt.shape[-1]
    r = wt.size // cols
    tr = _tile(r, max(SUBLANE, SUM_BLOCK_ELEMS // 2 // max(cols, LANE)), SUBLANE)

    def body(w_ref, g_ref, m_ref, v_ref, d_ref, nm_ref, nv_ref):
        gv = g_ref[...]
        mn = ADAM_B1 * m_ref[...] + (1.0 - ADAM_B1) * gv
        vn = ADAM_B2 * v_ref[...] + (1.0 - ADAM_B2) * jnp.square(gv)
        m_hat = mn / (1.0 - ADAM_B1 ** ADAM_STEP)
        v_hat = vn / (1.0 - ADAM_B2 ** ADAM_STEP)
        d_ref[...] = -ADAM_LR * (m_hat / (jnp.sqrt(v_hat) + ADAM_EPS) + ADAM_WD * w_ref[...])
        nm_ref[...] = mn
        nv_ref[...] = vn

    spec = pl.BlockSpec((tr, cols), lambda i: (i, 0))
    shp = jax.ShapeDtypeStruct((r, cols), F32)
    res = pl.pallas_call(
        body, name=name, grid=(r // tr,), in_specs=[spec] * 4, out_specs=[spec] * 3, out_shape=[shp] * 3,
        compiler_params=_params("parallel"),
    )(*[t.reshape(r, cols) for t in (wt, g, m, v)])
    return [t.reshape(wt.shape) for t in res]


def _join_shards(n, piece):
    _, depth, a, b = piece.shape
    if BIG_AXIS[n] == 2:
        return piece.transpose(1, 2, 0, 3).reshape(depth, a, N_CHIPS * b)
    return piece.transpose(1, 0, 2, 3).reshape(depth, N_CHIPS * a, b)


def _split_shards(n, g):
    depth, a, b = g.shape
    if BIG_AXIS[n] == 2:
        return g.reshape(depth, a, N_CHIPS, b // N_CHIPS).transpose(2, 0, 1, 3)
    return g.reshape(depth, N_CHIPS, a // N_CHIPS, b).transpose(1, 0, 2, 3)


SMALL_PART_ROWS = 2 * SUBLANE


def _rows_of(t):
    return -(-t.size // COLS)


def _pack_small(ts):
    rows = [jnp.pad(t.reshape(-1), (0, _rows_of(t) * COLS - t.size)).reshape(-1, COLS) for t in ts]
    total = sum(r.shape[0] for r in rows)
    part = -(-total // (N_DEV * SMALL_PART_ROWS)) * SMALL_PART_ROWS
    rows.append(jnp.zeros((N_DEV * part - total, COLS), F32))
    return jnp.concatenate(rows, axis=0).reshape(N_CHIPS, 2, part, COLS)


def _unpack_small(buf, like):
    buf, out, r0 = buf.reshape(-1, COLS), [], 0
    for t in like:
        out.append(buf[r0:r0 + _rows_of(t)].reshape(-1)[:t.size].reshape(t.shape))
        r0 += _rows_of(t)
    return out


def kernel(x, norm_mix, w_in, b_gate, attn_sinks, w_attn_o, conv_w, w_conv_o, ssm_a_re, ssm_a_im, ssm_b_re, ssm_b_im, ssm_c_re, ssm_c_im, ssm_d, ssm_log_dt, w_ssm_glu, w_ssm_o, w_mix_o, norm_ffn, w_ffn_in, w_ffn_out, norm_final, loss_target, m_norm_mix, m_w_in, m_b_gate, m_attn_sinks, m_w_attn_o, m_conv_w, m_w_conv_o, m_ssm_a_re, m_ssm_a_im, m_ssm_b_re, m_ssm_b_im, m_ssm_c_re, m_ssm_c_im, m_ssm_d, m_ssm_log_dt, m_w_ssm_glu, m_w_ssm_o, m_w_mix_o, m_norm_ffn, m_w_ffn_in, m_w_ffn_out, m_norm_final, v_norm_mix, v_w_in, v_b_gate, v_attn_sinks, v_w_attn_o, v_conv_w, v_w_conv_o, v_ssm_a_re, v_ssm_a_im, v_ssm_b_re, v_ssm_b_im, v_ssm_c_re, v_ssm_c_im, v_ssm_d, v_ssm_log_dt, v_w_ssm_glu, v_w_ssm_o, v_w_mix_o, v_norm_ffn, v_w_ffn_in, v_w_ffn_out, v_norm_final):
    a = dict(zip(ARG_NAMES, (
        x, norm_mix, w_in, b_gate, attn_sinks, w_attn_o, conv_w, w_conv_o, ssm_a_re, ssm_a_im, ssm_b_re, ssm_b_im,
        ssm_c_re, ssm_c_im, ssm_d, ssm_log_dt, w_ssm_glu, w_ssm_o, w_mix_o, norm_ffn, w_ffn_in, w_ffn_out, norm_final,
        loss_target, m_norm_mix, m_w_in, m_b_gate, m_attn_sinks, m_w_attn_o, m_conv_w, m_w_conv_o, m_ssm_a_re,
        m_ssm_a_im, m_ssm_b_re, m_ssm_b_im, m_ssm_c_re, m_ssm_c_im, m_ssm_d, m_ssm_log_dt, m_w_ssm_glu, m_w_ssm_o,
        m_w_mix_o, m_norm_ffn, m_w_ffn_in, m_w_ffn_out, m_norm_final, v_norm_mix, v_w_in, v_b_gate, v_attn_sinks,
        v_w_attn_o, v_conv_w, v_w_conv_o, v_ssm_a_re, v_ssm_a_im, v_ssm_b_re, v_ssm_b_im, v_ssm_c_re, v_ssm_c_im,
        v_ssm_d, v_ssm_log_dt, v_w_ssm_glu, v_w_ssm_o, v_w_mix_o, v_norm_ffn, v_w_ffn_in, v_w_ffn_out, v_norm_final)))
    px, py, _ = _place()
    chip = 2 * px + py

    gathered = BIG + ("conv_w",)
    own = [a[n].astype(BF16) for n in BIG] + [a["conv_w"]]
    gath = _gather_weights(own, "gather_weights")
    w = {n: _join_shards(n, lax.dynamic_update_slice_in_dim(p, o[None], chip, axis=0))
         for n, p, o in zip(gathered, gath, own)}
    for n in SMALL:
        w[n] = a[n]

    loss, dx, grads = _local_step(a["x"][0], a["loss_target"][0], w)

    small_names = SMALL + ("conv_w",)
    small = _pack_small([grads[n] for n in small_names])
    rs = _reduce_scatter([_split_shards(n, grads[n]) for n in BIG] + [small], BIG + ("small",),
                         (BF16,) * len(BIG) + (F32,))
    red = dict(zip(BIG, rs))
    small_all = _gather_weights([rs[-1]], "gather_small")[0]
    small_all = lax.dynamic_update_slice_in_dim(small_all, rs[-1][None], chip, axis=0)
    red.update(zip(small_names, _unpack_small(small_all, [grads[n] for n in small_names])))
    lane = a["conv_w"].shape[2]
    red["conv_w"] = lax.dynamic_slice_in_dim(red["conv_w"], chip * lane, lane, axis=2)

    loss_all = lax.psum(loss[0, 0], ("x", "y", "c"))
    deltas, new_m, new_v = [], [], []
    for n in WEIGHTS:
        d, mn, vn = _adamw(a[n], red[n], a["m_" + n], a["v_" + n], "adamw_" + n)
        deltas.append(d)
        new_m.append(mn)
        new_v.append(vn)
    return (loss_all, dx[None], *[red[n] for n in WEIGHTS], *deltas, *new_m, *new_v)
```

```python
import functools
import math

import jax
import jax.numpy as jnp
import numpy as np
from jax import lax
from jax.experimental import pallas as pl
from jax.experimental.pallas import tpu as pltpu

F32 = jnp.float32
BF16 = jnp.bfloat16

D_MODEL = 1024
DEPTH = 4
N_Q_HEADS = 8
N_KV_HEADS = 2
HEAD_DIM = 64
Q_GROUP = N_Q_HEADS // N_KV_HEADS
WINDOW = 128
BLOCK = 128
ROPE_THETA = 500000.0
ROT_DIM = HEAD_DIM // 4
ATTN_WIDTH = N_Q_HEADS * HEAD_DIM
KV_WIDTH = N_KV_HEADS * HEAD_DIM
NEG_INF = -1e30
CONV_WIDTH = 512
CONV_K = 3
SSM_WIDTH = 512
SSM_GROUP = 16
SSM_GROUPS = 32
SSM_STATE = 64
SSM_LANES = SSM_GROUPS * SSM_STATE
GATE_WIDTH = 3 * D_MODEL
FFN_HIDDEN = 2816
NORM_EPS = 1e-6
IN_COLS = 5888
C_Q, C_K, C_V, C_CB, C_CC, C_CX, C_U, C_G = 0, 512, 640, 768, 1280, 1792, 2304, 2816

ADAM_LR = 0.001
ADAM_B1 = 0.9
ADAM_B2 = 0.999
ADAM_EPS = 1e-08
ADAM_WD = 0.01
ADAM_STEP = 10

N_CHIPS = 4
N_DEV = 8
MESH_ID = pl.DeviceIdType.MESH

VMEM_LIMIT_BYTES = 48 * 1024 * 1024
LANE = 128
SUBLANE = 8
SCAN_ROWS = 8
SCAN_CHUNK = 128

BIG = ("w_in", "w_attn_o", "w_conv_o", "w_ssm_glu", "w_ssm_o", "w_mix_o", "w_ffn_in", "w_ffn_out")
BIG_AXIS = {"w_in": 2, "w_attn_o": 2, "w_conv_o": 2, "w_ssm_glu": 1, "w_ssm_o": 2, "w_mix_o": 1,
            "w_ffn_in": 2, "w_ffn_out": 1, "conv_w": 2}
SMALL = ("norm_mix", "b_gate", "attn_sinks", "ssm_a_re", "ssm_a_im", "ssm_b_re", "ssm_b_im",
         "ssm_c_re", "ssm_c_im", "ssm_d", "ssm_log_dt", "norm_ffn", "norm_final")
WEIGHTS = ("norm_mix", "w_in", "b_gate", "attn_sinks", "w_attn_o", "conv_w", "w_conv_o", "ssm_a_re",
           "ssm_a_im", "ssm_b_re", "ssm_b_im", "ssm_c_re", "ssm_c_im", "ssm_d", "ssm_log_dt",
           "w_ssm_glu", "w_ssm_o", "w_mix_o", "norm_ffn", "w_ffn_in", "w_ffn_out", "norm_final")
ARG_NAMES = ("x",) + WEIGHTS + ("loss_target",) + tuple("m_" + n for n in WEIGHTS) + tuple(
    "v_" + n for n in WEIGHTS)


def _params(*sem):
    return pltpu.CompilerParams(dimension_semantics=sem if sem else None,
                                vmem_limit_bytes=VMEM_LIMIT_BYTES)


def _tile(dim, cap, align):
    t = min(cap, dim) // align * align
    while t >= align:
        if dim % t == 0:
            return t
        t -= align
    return dim


_DOT_DIMS = {"nn": (((1,), (0,)), ((), ())), "nt": (((1,), (1,)), ((), ())), "tn": (((0,), (0,)), ((), ()))}


def _mm(a, b, mode, name, out_dtype=F32, add=None, tm_cap=512, tn_cap=3072, tk_cap=1024, b_layer=None, into=None):
    bshape = b.shape if b_layer is None else b.shape[1:]
    if mode == "nn":
        (m, k), (k2, n) = a.shape, bshape
    elif mode == "nt":
        (m, k), (n, k2) = a.shape, bshape
    else:
        (k, m), (k2, n) = a.shape, bshape
    assert k == k2, (name, a.shape, b.shape)
    tm, tn, tk = _tile(m, tm_cap, LANE), _tile(n, tn_cap, LANE), _tile(k, tk_cap, LANE)
    if into is not None:
        buf, layer, axis = into
        _, _, ra, cb = buf.shape
        if axis == 1:
            tm = m
        else:
            tn = _tile(cb, tn_cap, LANE)
            assert cb % tn == 0 and tn % LANE == 0, (name, cb, tn)
    nk = k // tk
    dims = _DOT_DIMS[mode]

    def body(a_ref, b_ref, *rest):
        rest = list(rest)
        add_ref = rest.pop(0) if add is not None else None
        if into is not None:
            rest.pop(0)
        o_ref, acc = rest
        kk = pl.program_id(2)

        @pl.when(kk == 0)
        def _():
            acc[...] = jnp.zeros_like(acc)

        acc[...] += lax.dot_general(a_ref[...].astype(BF16), b_ref[...].astype(BF16), dims,
                                    preferred_element_type=F32)

        @pl.when(kk == nk - 1)
        def _():
            r = acc[...]
            if add is not None:
                r = r + add_ref[...]
            o_ref[...] = r.astype(o_ref.dtype).reshape(o_ref.shape)

    if mode == "tn":
        a_spec = pl.BlockSpec((tk, tm), lambda i, j, kk: (kk, i))
    else:
        a_spec = pl.BlockSpec((tm, tk), lambda i, j, kk: (i, kk))
    lead = () if b_layer is None else (None,)
    at = (lambda *ix: ix) if b_layer is None else (lambda *ix: (b_layer,) + ix)
    if mode == "nt":
        b_spec = pl.BlockSpec(lead + (tn, tk), lambda i, j, kk: at(j, kk))
    else:
        b_spec = pl.BlockSpec(lead + (tk, tn), lambda i, j, kk: at(kk, j))
    o_spec = pl.BlockSpec((tm, tn), lambda i, j, kk: (i, j))
    in_specs, args = [a_spec, b_spec], [a, b]
    if add is not None:
        in_specs.append(o_spec)
        args.append(add)
    out_shape, aliases = jax.ShapeDtypeStruct((m, n), out_dtype), {}
    if into is not None:
        in_specs.append(pl.BlockSpec(memory_space=pl.ANY))
        aliases = {len(args): 0}
        args.append(buf)
        out_shape = jax.ShapeDtypeStruct(buf.shape, buf.dtype)
        if axis == 1:
            o_spec = pl.BlockSpec((N_CHIPS, None, ra, tn), lambda i, j, kk: (0, layer, 0, j))
        else:
            per = cb // tn
            o_spec = pl.BlockSpec((None, None, tm, tn), lambda i, j, kk: (j // per, layer, i, j % per))
    return pl.pallas_call(
        body, name=name, grid=(m // tm, n // tn, nk), in_specs=in_specs, out_specs=o_spec,
        out_shape=out_shape, input_output_aliases=aliases,
        scratch_shapes=[pltpu.VMEM((tm, tn), F32)],
        compiler_params=_params("parallel", "parallel", "arbitrary"),
    )(*args)


def _rowwise(fn, rows, pars, outs, accs, name, tm_cap=256):
    length = rows[0][0].shape[0]
    tm = _tile(length, tm_cap, LANE)
    n = length // tm
    in_specs, args, counts = [], [], []
    for arr, c0, cw, shift in rows:
        bw = math.gcd(c0, cw) if c0 else cw
        assert bw % LANE == 0 or (c0 == 0 and cw == arr.shape[1]), (name, c0, cw)
        cnt = cw // bw
        counts.append(cnt)
        for j in range(cnt):
            in_specs.append(pl.BlockSpec(
                (tm, bw), lambda i, j=j, c0=c0, bw=bw, shift=shift: (jnp.clip(i + shift, 0, n - 1), c0 // bw + j)))
            args.append(arr)
    for p in pars:
        in_specs.append(pl.BlockSpec(p.shape, lambda i: (0, 0)))
        args.append(p)
    out_shape = [jax.ShapeDtypeStruct((length, w), dt) for w, dt in outs]
    out_specs = [pl.BlockSpec((tm, w), lambda i: (i, 0)) for w, _ in outs]
    out_shape += [jax.ShapeDtypeStruct((r, w), F32) for r, w in accs]
    out_specs += [pl.BlockSpec((r, w), lambda i: (0, 0)) for r, w in accs]
    n_in, n_out = len(args), len(outs)

    def body(*refs):
        i = pl.program_id(0)
        vals, p = [], 0
        for cnt in counts:
            blocks = [refs[p + j][...] for j in range(cnt)]
            p += cnt
            vals.append(blocks[0] if cnt == 1 else jnp.concatenate(blocks, axis=1))
        for _ in pars:
            vals.append(refs[p][...])
            p += 1
        res = fn((i, n), *vals)
        out_refs = refs[n_in:n_in + n_out]
        acc_refs = refs[n_in + n_out:]
        for r, v in zip(out_refs, res[:n_out]):
            r[...] = v.astype(r.dtype)
        if acc_refs:
            @pl.when(i == 0)
            def _():
                for r in acc_refs:
                    r[...] = jnp.zeros_like(r)
            for r, v in zip(acc_refs, res[n_out:]):
                r[...] += v

    res = pl.pallas_call(
        body, name=name, grid=(n,), in_specs=in_specs, out_specs=out_specs, out_shape=out_shape,
        compiler_params=_params("arbitrary"),
    )(*args)
    return res


def _rms(x, g):
    return x * lax.rsqrt(jnp.mean(x * x, axis=-1, keepdims=True) + NORM_EPS) * g


def _rms_fwd(x, g, name):
    return _rowwise(lambda ctx, xv, gv: (_rms(xv, gv),), [(x, 0, D_MODEL, 0)], [g],
                    [(D_MODEL, BF16)], [], name)[0]


def _rms_bwd(x, g, dh, dres, name):
    def fn(ctx, xv, dhv, drv, gv):
        _, vjp = jax.vjp(_rms, xv, gv)
        dx, dg = vjp(dhv)
        return dx + drv, dg
    return _rowwise(fn, [(x, 0, D_MODEL, 0), (dh, 0, D_MODEL, 0), (dres, 0, D_MODEL, 0)], [g],
                    [(D_MODEL, F32)], [(1, D_MODEL)], name)


def _rope_tables(length):
    pos = jnp.arange(length, dtype=F32)
    inv_freq = ROPE_THETA ** (-jnp.arange(0, ROT_DIM, 2, dtype=F32) / ROT_DIM)
    ang = pos[:, None] * inv_freq[None, :]
    cos, sin = jnp.cos(ang), jnp.sin(ang)
    half = ROT_DIM // 2
    ones = jnp.ones((length, HEAD_DIM - ROT_DIM), F32)
    zeros = jnp.zeros_like(ones)
    zh = jnp.zeros((length, half), F32)
    c64 = jnp.concatenate([cos, cos, ones], axis=1)
    s1 = jnp.concatenate([-sin, zh, zeros], axis=1)
    s2 = jnp.concatenate([zh, sin, zeros], axis=1)
    tile2 = lambda t: jnp.concatenate([t, t], axis=1)
    return tile2(c64), tile2(s1), tile2(s2)


def _lane_chunks(t):
    return [t[:, j * LANE:(j + 1) * LANE] for j in range(t.shape[1] // LANE)]


def _rope(t, c, s1, s2, n_rot):
    half = ROT_DIM // 2
    out = []
    for j, ch in enumerate(_lane_chunks(t)):
        if j < n_rot:
            ch = ch * c + pltpu.roll(ch, LANE - half, 1) * s1 + pltpu.roll(ch, half, 1) * s2
        out.append(ch)
    return jnp.concatenate(out, axis=1)


def _unrope(d, c, s1, s2, n_rot):
    half = ROT_DIM // 2
    out = []
    for j, ch in enumerate(_lane_chunks(d)):
        if j < n_rot:
            ch = ch * c + pltpu.roll(ch * s1, half, 1) + pltpu.roll(ch * s2, LANE - half, 1)
        out.append(ch)
    return jnp.concatenate(out, axis=1)


N_ROT_CHUNKS = (ATTN_WIDTH + KV_WIDTH) // LANE
QKV_WIDTH = ATTN_WIDTH + 2 * KV_WIDTH


def _split_fwd(proj, tabs, name):
    def fn(ctx, t, c, s1, s2):
        return (_rope(t, c, s1, s2, N_ROT_CHUNKS),)
    rows = [(proj, 0, QKV_WIDTH, 0)] + [(t, 0, LANE, 0) for t in tabs]
    return _rowwise(fn, rows, [], [(QKV_WIDTH, BF16)], [], name, tm_cap=BLOCK)[0]


def _split_bwd(dq, dkc, dkp, dvc, dvp, tabs, name):
    def fn(ctx, dqv, dkcv, dkpv, dvcv, dvpv, c, s1, s2):
        i, n = ctx
        keep = (i < n - 1).astype(F32)
        d = jnp.concatenate([dqv, dkcv + keep * dkpv, dvcv + keep * dvpv], axis=1)
        return (_unrope(d, c, s1, s2, N_ROT_CHUNKS),)
    rows = [(dq, 0, ATTN_WIDTH, 0), (dkc, 0, KV_WIDTH, 0), (dkp, 0, KV_WIDTH, 1), (dvc, 0, KV_WIDTH, 0),
            (dvp, 0, KV_WIDTH, 1)] + [(t, 0, LANE, 0) for t in tabs]
    return _rowwise(fn, rows, [], [(QKV_WIDTH, BF16)], [], name, tm_cap=BLOCK)[0]


def _att_scores(q_ref, kp_ref, kc_ref, sink_ref):
    n = pl.program_id(1)
    j = pl.program_id(0)
    rows = Q_GROUP * BLOCK
    qs = q_ref[...].reshape(rows, HEAD_DIM)
    kb = jnp.concatenate([kp_ref[0], kc_ref[0]], axis=0)
    s = lax.dot_general(qs, kb, _DOT_DIMS["nt"], preferred_element_type=F32) * (HEAD_DIM ** -0.5)
    r = lax.broadcasted_iota(jnp.int32, (rows, 2 * BLOCK), 0)
    kj = lax.broadcasted_iota(jnp.int32, (rows, 2 * BLOCK), 1)
    delta = (r % BLOCK) + BLOCK - kj
    ok = (delta >= 0) & (delta < WINDOW) & ((kj >= BLOCK) | (n > 0))
    s = jnp.where(ok, s, NEG_INF)
    rh = lax.broadcasted_iota(jnp.int32, (rows, 1), 0) // BLOCK
    sinks = sink_ref[...]
    lane = lax.broadcasted_iota(jnp.int32, sinks.shape, 1)
    srow = lax.broadcasted_iota(jnp.int32, sinks.shape, 0)
    sink = jnp.zeros((rows, 1), F32)
    for g in range(Q_GROUP):
        val = jnp.sum(jnp.where((lane == g) & (srow == j), sinks, 0.0), keepdims=True)
        sink = jnp.where(rh == g, val, sink)
    m = jnp.maximum(jnp.max(s, axis=-1, keepdims=True), sink)
    p = jnp.exp(s - m)
    psink = jnp.exp(sink - m)
    denom = jnp.sum(p, axis=-1, keepdims=True) + psink
    return qs, kb, p / denom, psink / denom, rh


def _att_specs(length):
    nb = length // BLOCK
    q_spec = pl.BlockSpec((Q_GROUP, BLOCK, HEAD_DIM), lambda j, n: (j, n, 0))
    prev = pl.BlockSpec((1, BLOCK, HEAD_DIM), lambda j, n: (j, jnp.maximum(n - 1, 0), 0))
    cur = pl.BlockSpec((1, BLOCK, HEAD_DIM), lambda j, n: (j, n, 0))
    sink_spec = pl.BlockSpec((N_KV_HEADS, Q_GROUP), lambda j, n: (0, 0))
    return nb, q_spec, prev, cur, sink_spec


def _att_fwd(q, k, v, sinks, name):
    length = q.shape[1]
    nb, q_spec, prev, cur, sink_spec = _att_specs(length)

    def body(q_ref, kp_ref, kc_ref, vp_ref, vc_ref, sink_ref, o_ref):
        _, _, p, _, _ = _att_scores(q_ref, kp_ref, kc_ref, sink_ref)
        vb = jnp.concatenate([vp_ref[0], vc_ref[0]], axis=0)
        o = jnp.dot(p.astype(BF16), vb, preferred_element_type=F32)
        o_ref[...] = o.reshape(Q_GROUP, BLOCK, HEAD_DIM).astype(o_ref.dtype)

    return pl.pallas_call(
        body, name=name, grid=(N_KV_HEADS, nb),
        in_specs=[q_spec, prev, cur, prev, cur, sink_spec], out_specs=q_spec,
        out_shape=jax.ShapeDtypeStruct((N_Q_HEADS, length, HEAD_DIM), BF16),
        compiler_params=_params("arbitrary", "arbitrary"),
    )(q, k, k, v, v, sinks)


def _att_bwd(q, k, v, sinks, do, name):
    length = q.shape[1]
    nb, q_spec, prev, cur, sink_spec = _att_specs(length)
    ds_spec = pl.BlockSpec((1, SUBLANE, LANE), lambda j, n: (j, 0, 0))

    def body(q_ref, kp_ref, kc_ref, vp_ref, vc_ref, sink_ref, do_ref, dq_ref, dkc_ref, dkp_ref, dvc_ref,
             dvp_ref, dsink_ref):
        n = pl.program_id(1)
        qs, kb, p, psink, rh = _att_scores(q_ref, kp_ref, kc_ref, sink_ref)
        vb = jnp.concatenate([vp_ref[0], vc_ref[0]], axis=0)
        dob = do_ref[...].reshape(Q_GROUP * BLOCK, HEAD_DIM).astype(BF16)
        dv = lax.dot_general(p.astype(BF16), dob, _DOT_DIMS["tn"], preferred_element_type=F32)
        dp = lax.dot_general(dob, vb, _DOT_DIMS["nt"], preferred_element_type=F32)
        dsum = jnp.sum(p * dp, axis=-1, keepdims=True)
        ds = (p * (dp - dsum) * (HEAD_DIM ** -0.5)).astype(BF16)
        dq = jnp.dot(ds, kb, preferred_element_type=F32)
        dk = lax.dot_general(ds, qs, _DOT_DIMS["tn"], preferred_element_type=F32)
        dq_ref[...] = dq.reshape(Q_GROUP, BLOCK, HEAD_DIM)
        dkp_ref[0] = dk[:BLOCK]
        dkc_ref[0] = dk[BLOCK:]
        dvp_ref[0] = dv[:BLOCK]
        dvc_ref[0] = dv[BLOCK:]
        dsr = -psink * dsum
        row = lax.broadcasted_iota(jnp.int32, (SUBLANE, LANE), 0)
        upd = jnp.zeros((SUBLANE, LANE), F32)
        for g in range(Q_GROUP):
            val = jnp.sum(jnp.where(rh == g, dsr, 0.0), keepdims=True)
            upd = jnp.where(row == g, val, upd)

        @pl.when(n == 0)
        def _():
            dsink_ref[...] = jnp.zeros_like(dsink_ref)

        dsink_ref[0] += upd

    kv_shape = jax.ShapeDtypeStruct((N_KV_HEADS, length, HEAD_DIM), F32)
    return pl.pallas_call(
        body, name=name, grid=(N_KV_HEADS, nb),
        in_specs=[q_spec, prev, cur, prev, cur, sink_spec, q_spec],
        out_specs=[q_spec, cur, cur, cur, cur, ds_spec],
        out_shape=[jax.ShapeDtypeStruct((N_Q_HEADS, length, HEAD_DIM), F32), kv_shape, kv_shape, kv_shape,
                   kv_shape, jax.ShapeDtypeStruct((N_KV_HEADS, SUBLANE, LANE), F32)],
        compiler_params=_params("arbitrary", "arbitrary"),
    )(q, k, k, v, v, sinks, do)


def _to_heads(t, heads):
    return t.reshape(t.shape[0], heads, HEAD_DIM).transpose(1, 0, 2)


def _from_heads(t):
    return t.transpose(1, 0, 2).reshape(t.shape[1], t.shape[0] * HEAD_DIM)


def _shift_down(z, s):
    t = lax.broadcasted_iota(jnp.int32, z.shape, 0)
    return jnp.where(t >= s, pltpu.roll(z, s, 0), 0.0)


def _shift_up(z, s):
    t = lax.broadcasted_iota(jnp.int32, z.shape, 0)
    return jnp.where(t < z.shape[0] - s, pltpu.roll(z, z.shape[0] - s, 0), 0.0)


def _conv_specs(length):
    col = lambda c0: pl.BlockSpec((length, LANE), lambda j, c0=c0: (0, c0 // LANE + j))
    w_spec = pl.BlockSpec((CONV_K, LANE), lambda j: (0, j))
    o_spec = pl.BlockSpec((length, LANE), lambda j: (0, j))
    return col, w_spec, o_spec


def _conv_fwd(proj, w, name):
    length = proj.shape[0]
    col, w_spec, o_spec = _conv_specs(length)

    def body(cb_ref, cc_ref, cx_ref, w_ref, o_ref):
        z = cc_ref[...] * cx_ref[...]
        s = w_ref[0:1, :] * _shift_down(z, 2) + w_ref[1:2, :] * _shift_down(z, 1) + w_ref[2:3, :] * z
        o_ref[...] = (cb_ref[...] * s).astype(o_ref.dtype)

    return pl.pallas_call(
        body, name=name, grid=(CONV_WIDTH // LANE,),
        in_specs=[col(C_CB), col(C_CC), col(C_CX), w_spec], out_specs=o_spec,
        out_shape=jax.ShapeDtypeStruct((length, CONV_WIDTH), BF16),
        compiler_params=_params("arbitrary"),
    )(proj, proj, proj, w)


def _conv_bwd(proj, w, dy, name):
    length = proj.shape[0]
    col, w_spec, o_spec = _conv_specs(length)
    dw_spec = pl.BlockSpec((1, LANE), lambda j: (0, j))

    def body(cb_ref, cc_ref, cx_ref, w_ref, dy_ref, dcb_ref, dcc_ref, dcx_ref, dw0_ref, dw1_ref, dw2_ref):
        cc, cx, dyv = cc_ref[...], cx_ref[...], dy_ref[...]
        z = cc * cx
        w0, w1, w2 = w_ref[0:1, :], w_ref[1:2, :], w_ref[2:3, :]
        z1, z2 = _shift_down(z, 1), _shift_down(z, 2)
        s = w0 * z2 + w1 * z1 + w2 * z
        dcb_ref[...] = (dyv * s).astype(dcb_ref.dtype)
        ds = dyv * cb_ref[...]
        dw0_ref[...] = jnp.sum(ds * z2, axis=0, keepdims=True)
        dw1_ref[...] = jnp.sum(ds * z1, axis=0, keepdims=True)
        dw2_ref[...] = jnp.sum(ds * z, axis=0, keepdims=True)
        dz = w2 * ds + w1 * _shift_up(ds, 1) + w0 * _shift_up(ds, 2)
        dcc_ref[...] = (dz * cx).astype(dcc_ref.dtype)
        dcx_ref[...] = (dz * cc).astype(dcx_ref.dtype)

    act = jax.ShapeDtypeStruct((length, CONV_WIDTH), BF16)
    dws = jax.ShapeDtypeStruct((1, CONV_WIDTH), F32)
    return pl.pallas_call(
        body, name=name, grid=(CONV_WIDTH // LANE,),
        in_specs=[col(C_CB), col(C_CC), col(C_CX), w_spec, o_spec],
        out_specs=[o_spec, o_spec, o_spec, dw_spec, dw_spec, dw_spec],
        out_shape=[act, act, act, dws, dws, dws],
        compiler_params=_params("arbitrary"),
    )(proj, proj, proj, w, dy)


def _cmul(ar, ai, br, bi):
    return ar * br - ai * bi, ar * bi + ai * br


def _scan_tables(lr, li, reverse):
    pr, pi = [lr], [li]
    for _ in range(SCAN_ROWS - 1):
        nr, ni = _cmul(pr[-1], pi[-1], lr, li)
        pr.append(nr)
        pi.append(ni)
    row = jnp.arange(SCAN_ROWS)[:, None]
    mr, mi = [], []
    for s in (1, 2, 4):
        live = (row + s < SCAN_ROWS) if reverse else (row >= s)
        mr.append(jnp.where(live, pr[s - 1], 0.0))
        mi.append(jnp.where(live, pi[s - 1], 0.0))
    order = range(SCAN_ROWS - 1, -1, -1) if reverse else range(SCAN_ROWS)
    carry_r = jnp.concatenate([pr[d] for d in order], axis=0)
    carry_i = jnp.concatenate([pi[d] for d in order], axis=0)
    return jnp.stack(mr), jnp.stack(mi), carry_r, carry_i


def _scan(b, lr, li, reverse, name):
    length = b.shape[0]
    mr, mi, cr, ci = _scan_tables(lr, li, reverse)
    nchunk = length // SCAN_CHUNK
    nblk = SCAN_CHUNK // SCAN_ROWS
    half = SSM_LANES

    def body(b_ref, mr_ref, mi_ref, cr_ref, ci_ref, o_ref, carry):
        @pl.when(pl.program_id(0) == 0)
        def _():
            carry[...] = jnp.zeros_like(carry)

        c_r, c_i = carry[0:1, :], carry[1:2, :]
        blocks = range(nblk - 1, -1, -1) if reverse else range(nblk)
        for blk in blocks:
            r0 = blk * SCAN_ROWS
            xr = b_ref[r0:r0 + SCAN_ROWS, :half]
            xi = b_ref[r0:r0 + SCAN_ROWS, half:]
            for kk, s in enumerate((1, 2, 4)):
                sh = SCAN_ROWS - s if reverse else s
                rr, ri = pltpu.roll(xr, sh, 0), pltpu.roll(xi, sh, 0)
                ar, ai = _cmul(mr_ref[kk], mi_ref[kk], rr, ri)
                xr, xi = xr + ar, xi + ai
            ar, ai = _cmul(cr_ref[...], ci_ref[...], c_r, c_i)
            xr, xi = xr + ar, xi + ai
            o_ref[r0:r0 + SCAN_ROWS, :half] = xr
            o_ref[r0:r0 + SCAN_ROWS, half:] = xi
            edge = r0 if reverse else r0 + SCAN_ROWS - 1
            c_r = o_ref[edge:edge + 1, :half]
            c_i = o_ref[edge:edge + 1, half:]
        carry[0:1, :] = c_r
        carry[1:2, :] = c_i

    chunk = (lambda i: (nchunk - 1 - i, 0)) if reverse else (lambda i: (i, 0))
    blk_spec = pl.BlockSpec((SCAN_CHUNK, 2 * half), chunk)
    m_spec = pl.BlockSpec((3, SCAN_ROWS, half), lambda i: (0, 0, 0))
    c_spec = pl.BlockSpec((SCAN_ROWS, half), lambda i: (0, 0))
    return pl.pallas_call(
        body, name=name, grid=(nchunk,), in_specs=[blk_spec, m_spec, m_spec, c_spec, c_spec],
        out_specs=blk_spec, out_shape=jax.ShapeDtypeStruct(b.shape, F32),
        scratch_shapes=[pltpu.VMEM((SUBLANE, half), F32)],
        compiler_params=_params("arbitrary"),
    )(b, mr, mi, cr, ci)


def _dlam(g, states, name):
    half = SSM_LANES

    def fn(ctx, gv, xv, xprev):
        i, _ = ctx
        row = lax.broadcasted_iota(jnp.int32, xv.shape, 0)
        last = jnp.sum(jnp.where(lax.broadcasted_iota(jnp.int32, xprev.shape, 0) == xprev.shape[0] - 1,
                                 xprev, 0.0), axis=0, keepdims=True)
        last = last * (i > 0).astype(F32)
        xs = jnp.where(row == 0, last, pltpu.roll(xv, 1, 0))
        gr, gi, xr, xi = gv[:, :half], gv[:, half:], xs[:, :half], xs[:, half:]
        dre = jnp.sum(gr * xr + gi * xi, axis=0, keepdims=True)
        dim = jnp.sum(gi * xr - gr * xi, axis=0, keepdims=True)
        return (jnp.concatenate([dre, dim], axis=1),)

    rows = [(g, 0, 2 * half, 0), (states, 0, 2 * half, 0), (states, 0, 2 * half, -1)]
    return _rowwise(fn, rows, [], [], [(1, 2 * half)], name, tm_cap=128)[0]


def _block_diag(t):
    g, a, b = t.shape
    eye = jnp.eye(g, dtype=t.dtype)
    return (t[:, :, None, :] * eye[:, None, :, None]).reshape(g * a, g * b)


def _ssm_prep(a_re, a_im, b_re, b_im, c_re, c_im, log_dt):
    dt = jnp.exp(log_dt)[:, None]
    er = jnp.exp(a_re * dt)
    lr, li = er * jnp.cos(a_im * dt), er * jnp.sin(a_im * dt)
    nr, ni = lr - 1.0, li
    den = a_re * a_re + a_im * a_im
    qr, qi = (nr * a_re + ni * a_im) / den, (ni * a_re - nr * a_im) / den
    bbr = qr[..., None] * b_re - qi[..., None] * b_im
    bbi = qr[..., None] * b_im + qi[..., None] * b_re
    bmat = jnp.concatenate([_block_diag(bbr.transpose(0, 2, 1)), _block_diag(bbi.transpose(0, 2, 1))], axis=1)
    cmat = jnp.concatenate([_block_diag(c_re.transpose(0, 2, 1)), -_block_diag(c_im.transpose(0, 2, 1))], axis=0)
    return lr.reshape(1, SSM_LANES), li.reshape(1, SSM_LANES), bmat, cmat


def _ssm_act(yc, u, d):
    return jax.nn.gelu(yc + d * u)


def _glu(ys, z):
    return ys * jax.nn.sigmoid(z)


def _merge(ya, yc, ys, gl, b):
    gates = jax.nn.sigmoid(gl + b)
    return gates[:, :D_MODEL] * ya + gates[:, D_MODEL:2 * D_MODEL] * yc + gates[:, 2 * D_MODEL:] * ys


def _swiglu(gu):
    return jax.nn.silu(gu[:, :FFN_HIDDEN]) * gu[:, FFN_HIDDEN:]


def _loss_fn(x, g, t):
    e = _rms(x, g) - t
    per_tok = jnp.mean(e * e, axis=-1, keepdims=True)
    return 0.5 * jnp.sum(per_tok, axis=0, keepdims=True)


def _vjp_rowwise(f, n_row, cot_dtype=F32):
    def fn(ctx, *vals):
        prim = vals[:n_row] + vals[n_row + 1:]
        _, vjp = jax.vjp(f, *prim)
        return vjp(vals[n_row].astype(cot_dtype))
    return fn


def _layer_fwd(i, x, w, tabs):
    nm = lambda s: "l%d_%s" % (i, s)
    sv = {"x": x}
    h = _rms_fwd(x, w["norm_mix"][i:i + 1], nm("rms_mix"))
    proj = _mm(h, w["w_in"], "nn", nm("mm_in"), b_layer=i)
    qkv = _split_fwd(proj, tabs, nm("rope"))
    q, k, v = (_to_heads(qkv[:, :C_K], N_Q_HEADS), _to_heads(qkv[:, C_K:C_V], N_KV_HEADS),
               _to_heads(qkv[:, C_V:C_CB], N_KV_HEADS))
    sinks = w["attn_sinks"][i].reshape(N_KV_HEADS, Q_GROUP)
    att = _from_heads(_att_fwd(q, k, v, sinks, nm("att")))
    conv = _conv_fwd(proj, w["conv_w"][i], nm("conv"))
    lr, li, bmat, cmat = w["ssm"][i]
    u = proj[:, C_U:C_G]
    bu = _mm(u, bmat, "nn", nm("mm_bu"), tn_cap=1024)
    states = _scan(bu, lr, li, False, nm("scan"))
    yc = _mm(states, cmat, "nn", nm("mm_c"))
    d = w["ssm_d"][i:i + 1]
    ys = _rowwise(lambda ctx, a, b, c: (_ssm_act(a, b, c),), [(yc, 0, SSM_WIDTH, 0), (u, 0, SSM_WIDTH, 0)], [d],
                  [(SSM_WIDTH, F32)], [], nm("ssm_act"))[0]
    z = _mm(ys, w["w_ssm_glu"], "nn", nm("mm_glu"), b_layer=i)
    sg = _rowwise(lambda ctx, a, b: (_glu(a, b),), [(ys, 0, SSM_WIDTH, 0), (z, 0, SSM_WIDTH, 0)], [],
                  [(SSM_WIDTH, BF16)], [], nm("glu"))[0]
    ya = _mm(att, w["w_attn_o"], "nn", nm("mm_ao"), b_layer=i)
    yv = _mm(conv, w["w_conv_o"], "nn", nm("mm_co"), b_layer=i)
    ym = _mm(sg, w["w_ssm_o"], "nn", nm("mm_so"), b_layer=i)
    bg = w["b_gate"][i:i + 1]
    merged = _rowwise(lambda ctx, a, b, c, gl, bb: (_merge(a, b, c, gl, bb),),
                      [(ya, 0, D_MODEL, 0), (yv, 0, D_MODEL, 0), (ym, 0, D_MODEL, 0), (proj, C_G, GATE_WIDTH, 0)],
                      [bg], [(D_MODEL, BF16)], [], nm("merge"))[0]
    x1 = _mm(merged, w["w_mix_o"], "nn", nm("mm_mix"), add=x, b_layer=i)
    h2 = _rms_fwd(x1, w["norm_ffn"][i:i + 1], nm("rms_ffn"))
    gu = _mm(h2, w["w_ffn_in"], "nn", nm("mm_ffn_in"), b_layer=i)
    act = _rowwise(lambda ctx, a: (_swiglu(a),), [(gu, 0, 2 * FFN_HIDDEN, 0)], [], [(FFN_HIDDEN, BF16)], [],
                   nm("swiglu"))[0]
    x2 = _mm(act, w["w_ffn_out"], "nn", nm("mm_ffn_out"), add=x1, b_layer=i)
    sv.update(h=h, proj=proj, q=q, k=k, v=v, att=att, conv=conv, u=u, states=states, yc=yc, ys=ys, z=z, sg=sg,
              ya=ya, yv=yv, ym=ym, merged=merged, x1=x1, h2=h2, gu=gu, act=act)
    return x2, sv


def _layer_bwd(i, dx2, sv, w, tabs, gb):
    nm = lambda s: "l%d_b_%s" % (i, s)
    g = {}

    def wgrad(n, lhs, rhs, label, **kw):
        if n in gb:
            gb[n] = _mm(lhs, rhs, "tn", nm(label), into=(gb[n], i, BIG_AXIS[n]), **kw)
        else:
            g[n] = _mm(lhs, rhs, "tn", nm(label), **kw)

    dact = _mm(dx2, w["w_ffn_out"], "nt", nm("mm_dact"), b_layer=i)
    wgrad("w_ffn_out", sv["act"], dx2, "mm_gw_ffn_out", tn_cap=512)
    dgu = _rowwise(_vjp_rowwise(_swiglu, 1), [(sv["gu"], 0, 2 * FFN_HIDDEN, 0), (dact, 0, FFN_HIDDEN, 0)], [],
                   [(2 * FFN_HIDDEN, BF16)], [], nm("swiglu"))[0]
    dh2 = _mm(dgu, w["w_ffn_in"], "nt", nm("mm_dh2"), tk_cap=3072, b_layer=i)
    wgrad("w_ffn_in", sv["h2"], dgu, "mm_gw_ffn_in")
    dx1, g["norm_ffn"] = _rms_bwd(sv["x1"], w["norm_ffn"][i:i + 1], dh2, dx2, nm("rms_ffn"))
    dmerged = _mm(dx1, w["w_mix_o"], "nt", nm("mm_dmerged"), b_layer=i)
    wgrad("w_mix_o", sv["merged"], dx1, "mm_gw_mix", tn_cap=512)
    proj = sv["proj"]
    bg = w["b_gate"][i:i + 1]
    dya, dyv, dym, dgl, g["b_gate"] = _rowwise(
        _vjp_rowwise(_merge, 4),
        [(sv["ya"], 0, D_MODEL, 0), (sv["yv"], 0, D_MODEL, 0), (sv["ym"], 0, D_MODEL, 0),
         (proj, C_G, GATE_WIDTH, 0), (dmerged, 0, D_MODEL, 0)], [bg],
        [(D_MODEL, BF16), (D_MODEL, BF16), (D_MODEL, BF16), (GATE_WIDTH, BF16)], [(1, GATE_WIDTH)], nm("merge"))
    dsg = _mm(dym, w["w_ssm_o"], "nt", nm("mm_dsg"), b_layer=i)
    wgrad("w_ssm_o", sv["sg"], dym, "mm_gw_so")
    dys0, dz = _rowwise(_vjp_rowwise(_glu, 2), [(sv["ys"], 0, SSM_WIDTH, 0), (sv["z"], 0, SSM_WIDTH, 0),
                                                 (dsg, 0, SSM_WIDTH, 0)], [],
                        [(SSM_WIDTH, F32), (SSM_WIDTH, BF16)], [], nm("glu"))
    dys = _mm(dz, w["w_ssm_glu"], "nt", nm("mm_dys"), add=dys0, b_layer=i)
    wgrad("w_ssm_glu", sv["ys"], dz, "mm_gw_glu")
    d = w["ssm_d"][i:i + 1]
    dyc, du0, g["ssm_d"] = _rowwise(
        _vjp_rowwise(_ssm_act, 2), [(sv["yc"], 0, SSM_WIDTH, 0), (sv["u"], 0, SSM_WIDTH, 0), (dys, 0, SSM_WIDTH, 0)],
        [d], [(SSM_WIDTH, F32), (SSM_WIDTH, F32)], [(1, SSM_WIDTH)], nm("ssm_act"))
    lr, li, bmat, cmat = w["ssm"][i]
    dstates = _mm(dyc, cmat, "nt", nm("mm_dstates"), tn_cap=1024)
    g_cmat = _mm(sv["states"], dyc, "tn", nm("mm_gc"))
    gs = _scan(dstates, lr, -li, True, nm("scan"))
    g_lam = _dlam(gs, sv["states"], nm("dlam"))
    du = _mm(gs, bmat, "nt", nm("mm_du"), out_dtype=BF16, add=du0)
    g_bmat = _mm(sv["u"], gs, "tn", nm("mm_gb"), tn_cap=1024)
    g["ssm"] = (g_lam[:, :SSM_LANES], g_lam[:, SSM_LANES:], g_bmat, g_cmat)
    dconv = _mm(dyv, w["w_conv_o"], "nt", nm("mm_dconv"), b_layer=i)
    wgrad("w_conv_o", sv["conv"], dyv, "mm_gw_co")
    dcb, dcc, dcx, dw0, dw1, dw2 = _conv_bwd(proj, w["conv_w"][i], dconv, nm("conv"))
    g["conv_w"] = jnp.concatenate([dw0, dw1, dw2], axis=0)
    datt = _mm(dya, w["w_attn_o"], "nt", nm("mm_datt"), b_layer=i)
    wgrad("w_attn_o", sv["att"], dya, "mm_gw_ao")
    sinks = w["attn_sinks"][i].reshape(N_KV_HEADS, Q_GROUP)
    dq, dkc, dkp, dvc, dvp, dsk = _att_bwd(sv["q"], sv["k"], sv["v"], sinks, _to_heads(datt, N_Q_HEADS), nm("att"))
    g["attn_sinks"] = dsk[:, :Q_GROUP, 0].reshape(N_Q_HEADS)
    dqkv = _split_bwd(_from_heads(dq), _from_heads(dkc), _from_heads(dkp), _from_heads(dvc), _from_heads(dvp),
                      tabs, nm("rope"))
    dproj = jnp.concatenate([dqkv, dcb, dcc, dcx, du, dgl], axis=1)
    dh = _mm(dproj, w["w_in"], "nt", nm("mm_dh"), tk_cap=3072, b_layer=i)
    wgrad("w_in", sv["h"], dproj, "mm_gw_in")
    dx, g["norm_mix"] = _rms_bwd(sv["x"], w["norm_mix"][i:i + 1], dh, dx1, nm("rms_mix"))
    return dx, g


def _local_step(x, target, w):
    length = x.shape[0]
    tabs = _rope_tables(length)
    ssm_names = ("ssm_a_re", "ssm_a_im", "ssm_b_re", "ssm_b_im", "ssm_c_re", "ssm_c_im", "ssm_log_dt")
    w = dict(w)
    preps = [jax.vjp(_ssm_prep, *[w[n][i] for n in ssm_names]) for i in range(DEPTH)]
    w["ssm"] = [p[0] for p in preps]
    saved = []
    for i in range(DEPTH):
        x, sv = _layer_fwd(i, x, w, tabs)
        saved.append(sv)

    def loss_fn(ctx, xv, tv, gv):
        val, vjp = jax.vjp(_loss_fn, xv, gv, tv)
        dx, dg, _ = vjp(jnp.ones((1, 1), F32))
        return dx, dg, val + jnp.zeros((1, LANE), F32)

    gfin = w["norm_final"].reshape(1, D_MODEL)
    dx, g_final, loss = _rowwise(loss_fn, [(x, 0, D_MODEL, 0), (target, 0, D_MODEL, 0)], [gfin],
                                 [(D_MODEL, F32)], [(1, D_MODEL), (1, LANE)], "loss")
    gb = {}
    for n in BIG:
        depth, ra, cb = w[n].shape
        if n != "w_in":
            gb[n] = lax.empty((N_CHIPS, depth) + ((ra // N_CHIPS, cb) if BIG_AXIS[n] == 1 else (ra, cb // N_CHIPS)), F32)
    layer_grads = [None] * DEPTH
    for i in reversed(range(DEPTH)):
        dx, layer_grads[i] = _layer_bwd(i, dx, saved[i], w, tabs, gb)
    grads = dict(gb)
    for n in layer_grads[0]:
        if n != "ssm":
            grads[n] = jnp.stack([lg[n] for lg in layer_grads])
    ssm_g = [preps[i][1](layer_grads[i]["ssm"]) for i in range(DEPTH)]
    for j, n in enumerate(ssm_names):
        grads[n] = jnp.stack([sg[j] for sg in ssm_g])
    grads["norm_final"] = g_final.reshape(D_MODEL)
    for n in ("norm_mix", "norm_ffn", "b_gate", "ssm_d"):
        grads[n] = grads[n].reshape(grads[n].shape[0], -1)
    return loss, dx, grads


COLS = 1024
ANY_SPEC = pl.BlockSpec(memory_space=pl.ANY)


def _place():
    return lax.axis_index("x"), lax.axis_index("y"), lax.axis_index("c")


def _other_chips(x, y):
    return [(1 - x, y), (x, 1 - y), (1 - x, 1 - y)]


def _remote(src, dst, send_sems, recv_sems, k, to):
    return pltpu.make_async_remote_copy(src_ref=src, dst_ref=dst, send_sem=send_sems.at[k], recv_sem=recv_sems.at[k],
                                        device_id=to, device_id_type=MESH_ID)


def _comm_call(body, name, out_shape, n_sems, n_local, args):
    return pl.pallas_call(
        body, name=name, out_shape=out_shape, in_specs=[ANY_SPEC] * len(args),
        out_specs=[ANY_SPEC] * len(out_shape),
        scratch_shapes=[pltpu.SemaphoreType.DMA((n_sems,)), pltpu.SemaphoreType.DMA((n_sems,)),
                        pltpu.SemaphoreType.DMA((n_local,))],
    )(*args)


def _gather_weights(shards, name):
    nt = len(shards)
    hds = [s.shape[0] // 2 for s in shards]

    def body(*refs):
        srcs, outs = refs[:nt], refs[nt:2 * nt]
        send_sems, recv_sems, _ = refs[2 * nt:]
        x, y, c = _place()
        me, sibling = (x, y, c), (x, y, 1 - c)
        chips = _other_chips(x, y)

        def blk(t, chip, hc):
            return outs[t].at[2 * chip[0] + chip[1], pl.ds(hc * hds[t], hds[t])]

        first = [_remote(srcs[t].at[pl.ds(c * hds[t], hds[t])], blk(t, (x, y), c), send_sems, recv_sems, 6 * t + j,
                         (*chip, c)) for t in range(nt) for j, chip in enumerate(chips)]
        for cp in first:
            cp.start()
        passed = []
        for j, chip in enumerate(chips):
            for t in range(nt):
                _remote(blk(t, chip, c), blk(t, chip, c), send_sems, recv_sems, 6 * t + j, me).wait_recv()
                passed.append(_remote(blk(t, chip, c), blk(t, chip, c), send_sems, recv_sems, 6 * t + 3 + j, sibling))
                passed[-1].start()
        for j, chip in enumerate(chips):
            for t in range(nt):
                _remote(blk(t, chip, 1 - c), blk(t, chip, 1 - c), send_sems, recv_sems, 6 * t + 3 + j, me).wait_recv()
        for cp in first + passed:
            cp.wait_send()

    out_shape = [jax.ShapeDtypeStruct((N_CHIPS,) + s.shape, s.dtype) for s in shards]
    return _comm_call(body, name, out_shape, 6 * nt, 1, shards)


def _swap_halves(gs, name):
    nt = len(gs)
    hds = [g.shape[1] // 2 for g in gs]

    def body(*refs):
        srcs, outs = refs[:nt], refs[nt:2 * nt]
        send_sems, recv_sems, _ = refs[2 * nt:]
        x, y, c = _place()
        cps = [_remote(srcs[t].at[s, pl.ds((1 - c) * hds[t], hds[t])], outs[t].at[s], send_sems, recv_sems,
                       N_CHIPS * t + s, (x, y, 1 - c)) for t in range(nt) for s in range(N_CHIPS)]
        for cp in cps:
            cp.start()
        for cp in cps:
            cp.wait()

    out_shape = [jax.ShapeDtypeStruct((N_CHIPS, g.shape[1] // 2) + g.shape[2:], g.dtype) for g in gs]
    return _comm_call(body, name, out_shape, N_CHIPS * nt, 1, gs)


def _exchange_shards(parts, name):
    nt = len(parts)

    def body(*refs):
        srcs, outs = refs[:nt], refs[nt:2 * nt]
        send_sems, recv_sems, _ = refs[2 * nt:]
        x, y, c = _place()
        cps = [_remote(srcs[t].at[2 * chip[0] + chip[1]], outs[t].at[j], send_sems, recv_sems, 3 * t + j, (*chip, c))
               for t in range(nt) for j, chip in enumerate(_other_chips(x, y))]
        for cp in cps:
            cp.start()
        for cp in cps:
            cp.wait()

    out_shape = [jax.ShapeDtypeStruct((N_CHIPS - 1,) + p.shape[1:], p.dtype) for p in parts]
    return _comm_call(body, name, out_shape, 3 * nt, 1, parts)


def _join_halves(reds, name):
    nt = len(reds)
    hds = [r.shape[1] for r in reds]

    def body(*refs):
        srcs, outs = refs[:nt], refs[nt:2 * nt]
        send_sems, recv_sems, _ = refs[2 * nt:]
        x, y, c = _place()
        cps = [_remote(srcs[t].at[0], outs[t].at[pl.ds(c * hds[t], hds[t])], send_sems, recv_sems, t, (x, y, 1 - c))
               for t in range(nt)]
        for cp in cps:
            cp.start()
        for t in range(nt):
            _remote(srcs[t].at[0], outs[t].at[pl.ds((1 - c) * hds[t], hds[t])], send_sems, recv_sems, t,
                    (x, y, c)).wait_recv()
        for cp in cps:
            cp.wait_send()

    out_shape = [jax.ShapeDtypeStruct((2 * r.shape[1],) + r.shape[2:], r.dtype) for r in reds]
    return _comm_call(body, name, out_shape, nt, 1, reds)


SUM_BLOCK_ELEMS = 256 * 1024


def _sum_windows(parts, lead, name, out_dtypes=(F32,)):
    a, b = parts[0][0].shape[2:]
    ta = _tile(a, max(SUBLANE, SUM_BLOCK_ELEMS // b), 2 * SUBLANE)
    offs = jnp.stack([jnp.stack([jnp.asarray(o, jnp.int32) for o in off]) for _, off in parts])
    n_in = len(parts)

    def body(off_ref, *refs):
        acc = refs[0][...].astype(F32)
        for r in refs[1:n_in]:
            acc = acc + r[...].astype(F32)
        for r in refs[n_in:]:
            r[...] = acc.astype(r.dtype)

    in_specs = [pl.BlockSpec((1, 1, ta, b), lambda p, q, i, off, k=k: (off[k, 0] + p, off[k, 1] + q, i, 0))
                for k in range(n_in)]
    o_spec = pl.BlockSpec((1, 1, ta, b), lambda p, q, i, off: (p, q, i, 0))
    return pl.pallas_call(
        body, name=name, out_shape=[jax.ShapeDtypeStruct(tuple(lead) + (a, b), dt) for dt in out_dtypes],
        grid_spec=pltpu.PrefetchScalarGridSpec(
            num_scalar_prefetch=1, grid=tuple(lead) + (a // ta,), in_specs=in_specs,
            out_specs=[o_spec] * len(out_dtypes)),
        compiler_params=_params("arbitrary", "arbitrary", "arbitrary"),
    )(offs, *[arr for arr, _ in parts])


def _reduce_scatter(gs, names, wire):
    x, y, c = _place()
    theirs = _swap_halves(gs, "rs_swap_halves")
    pairs = [_sum_windows([(g, (0, c * (g.shape[1] // 2))), (t, (0, 0))], (N_CHIPS, g.shape[1] // 2),
                          "rs_sum_pair_" + n, (F32, wd)) for g, t, n, wd in zip(gs, theirs, names, wire)]
    others = _exchange_shards([p[1] for p in pairs], "rs_exchange")
    reds = [_sum_windows([(p[0], (2 * x + y, 0))] + [(o, (j, 0)) for j in range(N_CHIPS - 1)], (1, p[0].shape[1]),
                         "rs_sum_chips_" + n)[0] for p, o, n in zip(pairs, others, names)]
    joined = _join_halves(reds, "rs_join")
    return [lax.dynamic_update_slice_in_dim(j, r[0], c * r.shape[1], axis=0) for j, r in zip(joined, reds)]


def _adamw(wt, g, m, v, name):
    cols = wt.shape[-1]
    r = wt.size // cols
    tr = _tile(r, max(SUBLANE, SUM_BLOCK_ELEMS // 2 // max(cols, LANE)), SUBLANE)

    def body(w_ref, g_ref, m_ref, v_ref, d_ref, nm_ref, nv_ref):
        gv = g_ref[...]
        mn = ADAM_B1 * m_ref[...] + (1.0 - ADAM_B1) * gv
        vn = ADAM_B2 * v_ref[...] + (1.0 - ADAM_B2) * jnp.square(gv)
        m_hat = mn / (1.0 - ADAM_B1 ** ADAM_STEP)
        v_hat = vn / (1.0 - ADAM_B2 ** ADAM_STEP)
        d_ref[...] = -ADAM_LR * (m_hat / (jnp.sqrt(v_hat) + ADAM_EPS) + ADAM_WD * w_ref[...])
        nm_ref[...] = mn
        nv_ref[...] = vn

    spec = pl.BlockSpec((tr, cols), lambda i: (i, 0))
    shp = jax.ShapeDtypeStruct((r, cols), F32)
    res = pl.pallas_call(
        body, name=name, grid=(r // tr,), in_specs=[spec] * 4, out_specs=[spec] * 3, out_shape=[shp] * 3,
        compiler_params=_params("parallel"),
    )(*[t.reshape(r, cols) for t in (wt, g, m, v)])
    return [t.reshape(wt.shape) for t in res]


def _join_shards(n, piece):
    _, depth, a, b = piece.shape
    if BIG_AXIS[n] == 2:
        return piece.transpose(1, 2, 0, 3).reshape(depth, a, N_CHIPS * b)
    return piece.transpose(1, 0, 2, 3).reshape(depth, N_CHIPS * a, b)


def _split_shards(n, g):
    depth, a, b = g.shape
    if BIG_AXIS[n] == 2:
        return g.reshape(depth, a, N_CHIPS, b // N_CHIPS).transpose(2, 0, 1, 3)
    return g.reshape(depth, N_CHIPS, a // N_CHIPS, b).transpose(1, 0, 2, 3)


SMALL_PART_ROWS = 2 * SUBLANE


def _rows_of(t):
    return -(-t.size // COLS)


def _pack_small(ts):
    rows = [jnp.pad(t.reshape(-1), (0, _rows_of(t) * COLS - t.size)).reshape(-1, COLS) for t in ts]
    total = sum(r.shape[0] for r in rows)
    part = -(-total // (N_DEV * SMALL_PART_ROWS)) * SMALL_PART_ROWS
    rows.append(jnp.zeros((N_DEV * part - total, COLS), F32))
    return jnp.concatenate(rows, axis=0).reshape(N_CHIPS, 2, part, COLS)


def _unpack_small(buf, like):
    buf, out, r0 = buf.reshape(-1, COLS), [], 0
    for t in like:
        out.append(buf[r0:r0 + _rows_of(t)].reshape(-1)[:t.size].reshape(t.shape))
        r0 += _rows_of(t)
    return out


def kernel(x, norm_mix, w_in, b_gate, attn_sinks, w_attn_o, conv_w, w_conv_o, ssm_a_re, ssm_a_im, ssm_b_re, ssm_b_im, ssm_c_re, ssm_c_im, ssm_d, ssm_log_dt, w_ssm_glu, w_ssm_o, w_mix_o, norm_ffn, w_ffn_in, w_ffn_out, norm_final, loss_target, m_norm_mix, m_w_in, m_b_gate, m_attn_sinks, m_w_attn_o, m_conv_w, m_w_conv_o, m_ssm_a_re, m_ssm_a_im, m_ssm_b_re, m_ssm_b_im, m_ssm_c_re, m_ssm_c_im, m_ssm_d, m_ssm_log_dt, m_w_ssm_glu, m_w_ssm_o, m_w_mix_o, m_norm_ffn, m_w_ffn_in, m_w_ffn_out, m_norm_final, v_norm_mix, v_w_in, v_b_gate, v_attn_sinks, v_w_attn_o, v_conv_w, v_w_conv_o, v_ssm_a_re, v_ssm_a_im, v_ssm_b_re, v_ssm_b_im, v_ssm_c_re, v_ssm_c_im, v_ssm_d, v_ssm_log_dt, v_w_ssm_glu, v_w_ssm_o, v_w_mix_o, v_norm_ffn, v_w_ffn_in, v_w_ffn_out, v_norm_final):
    a = dict(zip(ARG_NAMES, (
        x, norm_mix, w_in, b_gate, attn_sinks, w_attn_o, conv_w, w_conv_o, ssm_a_re, ssm_a_im, ssm_b_re, ssm_b_im,
        ssm_c_re, ssm_c_im, ssm_d, ssm_log_dt, w_ssm_glu, w_ssm_o, w_mix_o, norm_ffn, w_ffn_in, w_ffn_out, norm_final,
        loss_target, m_norm_mix, m_w_in, m_b_gate, m_attn_sinks, m_w_attn_o, m_conv_w, m_w_conv_o, m_ssm_a_re,
        m_ssm_a_im, m_ssm_b_re, m_ssm_b_im, m_ssm_c_re, m_ssm_c_im, m_ssm_d, m_ssm_log_dt, m_w_ssm_glu, m_w_ssm_o,
        m_w_mix_o, m_norm_ffn, m_w_ffn_in, m_w_ffn_out, m_norm_final, v_norm_mix, v_w_in, v_b_gate, v_attn_sinks,
        v_w_attn_o, v_conv_w, v_w_conv_o, v_ssm_a_re, v_ssm_a_im, v_ssm_b_re, v_ssm_b_im, v_ssm_c_re, v_ssm_c_im,
        v_ssm_d, v_ssm_log_dt, v_w_ssm_glu, v_w_ssm_o, v_w_mix_o, v_norm_ffn, v_w_ffn_in, v_w_ffn_out, v_norm_final)))
    px, py, _ = _place()
    chip = 2 * px + py

    gathered = BIG + ("conv_w",)
    own = [a[n].astype(BF16) for n in BIG] + [a["conv_w"]]
    gath = _gather_weights(own, "gather_weights")
    w = {n: _join_shards(n, lax.dynamic_update_slice_in_dim(p, o[None], chip, axis=0))
         for n, p, o in zip(gathered, gath, own)}
    for n in SMALL:
        w[n] = a[n]

    loss, dx, grads = _local_step(a["x"][0], a["loss_target"][0], w)

    small_names = SMALL + ("conv_w",)
    small = _pack_small([grads[n] for n in small_names])
    rs = _reduce_scatter([_split_shards(n, grads[n]) if n == "w_in" else grads[n] for n in BIG] + [small],
                         BIG + ("small",),
                         (BF16,) * len(BIG) + (F32,))
    red = dict(zip(BIG, rs))
    small_all = _gather_weights([rs[-1]], "gather_small")[0]
    small_all = lax.dynamic_update_slice_in_dim(small_all, rs[-1][None], chip, axis=0)
    red.update(zip(small_names, _unpack_small(small_all, [grads[n] for n in small_names])))
    lane = a["conv_w"].shape[2]
    red["conv_w"] = lax.dynamic_slice_in_dim(red["conv_w"], chip * lane, lane, axis=2)

    loss_all = lax.psum(loss[0, 0], ("x", "y", "c"))
    deltas, new_m, new_v = [], [], []
    for n in WEIGHTS:
        d, mn, vn = _adamw(a[n], red[n], a["m_" + n], a["v_" + n], "adamw_" + n)
        deltas.append(d)
        new_m.append(mn)
        new_v.append(vn)
    return (loss_all, dx[None], *[red[n] for n in WEIGHTS], *deltas, *new_m, *new_v)
```

```python
import functools
import math

import jax
import jax.numpy as jnp
import numpy as np
from jax import lax
from jax.experimental import pallas as pl
from jax.experimental.pallas import tpu as pltpu

F32 = jnp.float32
BF16 = jnp.bfloat16

D_MODEL = 1024
DEPTH = 4
N_Q_HEADS = 8
N_KV_HEADS = 2
HEAD_DIM = 64
Q_GROUP = N_Q_HEADS // N_KV_HEADS
WINDOW = 128
BLOCK = 128
ROPE_THETA = 500000.0
ROT_DIM = HEAD_DIM // 4
ATTN_WIDTH = N_Q_HEADS * HEAD_DIM
KV_WIDTH = N_KV_HEADS * HEAD_DIM
NEG_INF = -1e30
CONV_WIDTH = 512
CONV_K = 3
SSM_WIDTH = 512
SSM_GROUP = 16
SSM_GROUPS = 32
SSM_STATE = 64
SSM_LANES = SSM_GROUPS * SSM_STATE
GATE_WIDTH = 3 * D_MODEL
FFN_HIDDEN = 2816
NORM_EPS = 1e-6
IN_COLS = 5888
C_Q, C_K, C_V, C_CB, C_CC, C_CX, C_U, C_G = 0, 512, 640, 768, 1280, 1792, 2304, 2816

ADAM_LR = 0.001
ADAM_B1 = 0.9
ADAM_B2 = 0.999
ADAM_EPS = 1e-08
ADAM_WD = 0.01
ADAM_STEP = 10

N_CHIPS = 4
N_DEV = 8
MESH_ID = pl.DeviceIdType.MESH

VMEM_LIMIT_BYTES = 48 * 1024 * 1024
LANE = 128
SUBLANE = 8
SCAN_ROWS = 8
SCAN_CHUNK = 128

BIG = ("w_in", "w_attn_o", "w_conv_o", "w_ssm_glu", "w_ssm_o", "w_mix_o", "w_ffn_in", "w_ffn_out")
BIG_AXIS = {"w_in": 2, "w_attn_o": 2, "w_conv_o": 2, "w_ssm_glu": 1, "w_ssm_o": 2, "w_mix_o": 1,
            "w_ffn_in": 2, "w_ffn_out": 1, "conv_w": 2}
SMALL = ("norm_mix", "b_gate", "attn_sinks", "ssm_a_re", "ssm_a_im", "ssm_b_re", "ssm_b_im",
         "ssm_c_re", "ssm_c_im", "ssm_d", "ssm_log_dt", "norm_ffn", "norm_final")
WEIGHTS = ("norm_mix", "w_in", "b_gate", "attn_sinks", "w_attn_o", "conv_w", "w_conv_o", "ssm_a_re",
           "ssm_a_im", "ssm_b_re", "ssm_b_im", "ssm_c_re", "ssm_c_im", "ssm_d", "ssm_log_dt",
           "w_ssm_glu", "w_ssm_o", "w_mix_o", "norm_ffn", "w_ffn_in", "w_ffn_out", "norm_final")
ARG_NAMES = ("x",) + WEIGHTS + ("loss_target",) + tuple("m_" + n for n in WEIGHTS) + tuple(
    "v_" + n for n in WEIGHTS)


def _params(*sem):
    return pltpu.CompilerParams(dimension_semantics=sem if sem else None,
                                vmem_limit_bytes=VMEM_LIMIT_BYTES)


def _tile(dim, cap, align):
    t = min(cap, dim) // align * align
    while t >= align:
        if dim % t == 0:
            return t
        t -= align
    return dim


_DOT_DIMS = {"nn": (((1,), (0,)), ((), ())), "nt": (((1,), (1,)), ((), ())), "tn": (((0,), (0,)), ((), ()))}


def _mm(a, b, mode, name, out_dtype=F32, add=None, tm_cap=512, tn_cap=3072, tk_cap=1024, b_layer=None, into=None):
    bshape = b.shape if b_layer is None else b.shape[1:]
    if mode == "nn":
        (m, k), (k2, n) = a.shape, bshape
    elif mode == "nt":
        (m, k), (n, k2) = a.shape, bshape
    else:
        (k, m), (k2, n) = a.shape, bshape
    assert k == k2, (name, a.shape, b.shape)
    tm, tn, tk = _tile(m, tm_cap, LANE), _tile(n, tn_cap, LANE), _tile(k, tk_cap, LANE)
    if into is not None:
        buf, layer, axis = into
        _, _, ra, cb = buf.shape
        if axis == 1:
            tm = m
        else:
            tn = _tile(cb, tn_cap, LANE)
            assert cb % tn == 0 and tn % LANE == 0, (name, cb, tn)
    nk = k // tk
    dims = _DOT_DIMS[mode]

    def body(a_ref, b_ref, *rest):
        rest = list(rest)
        add_ref = rest.pop(0) if add is not None else None
        if into is not None:
            rest.pop(0)
        o_ref, acc = rest
        kk = pl.program_id(2)

        @pl.when(kk == 0)
        def _():
            acc[...] = jnp.zeros_like(acc)

        acc[...] += lax.dot_general(a_ref[...].astype(BF16), b_ref[...].astype(BF16), dims,
                                    preferred_element_type=F32)

        @pl.when(kk == nk - 1)
        def _():
            r = acc[...]
            if add is not None:
                r = r + add_ref[...]
            o_ref[...] = r.astype(o_ref.dtype).reshape(o_ref.shape)

    if mode == "tn":
        a_spec = pl.BlockSpec((tk, tm), lambda i, j, kk: (kk, i))
    else:
        a_spec = pl.BlockSpec((tm, tk), lambda i, j, kk: (i, kk))
    lead = () if b_layer is None else (None,)
    at = (lambda *ix: ix) if b_layer is None else (lambda *ix: (b_layer,) + ix)
    if mode == "nt":
        b_spec = pl.BlockSpec(lead + (tn, tk), lambda i, j, kk: at(j, kk))
    else:
        b_spec = pl.BlockSpec(lead + (tk, tn), lambda i, j, kk: at(kk, j))
    o_spec = pl.BlockSpec((tm, tn), lambda i, j, kk: (i, j))
    in_specs, args = [a_spec, b_spec], [a, b]
    if add is not None:
        in_specs.append(o_spec)
        args.append(add)
    out_shape, aliases = jax.ShapeDtypeStruct((m, n), out_dtype), {}
    if into is not None:
        in_specs.append(pl.BlockSpec(memory_space=pl.ANY))
        aliases = {len(args): 0}
        args.append(buf)
        out_shape = jax.ShapeDtypeStruct(buf.shape, buf.dtype)
        if axis == 1:
            o_spec = pl.BlockSpec((N_CHIPS, None, ra, tn), lambda i, j, kk: (0, layer, 0, j))
        else:
            per = cb // tn
            o_spec = pl.BlockSpec((None, None, tm, tn), lambda i, j, kk: (j // per, layer, i, j % per))
    return pl.pallas_call(
        body, name=name, grid=(m // tm, n // tn, nk), in_specs=in_specs, out_specs=o_spec,
        out_shape=out_shape, input_output_aliases=aliases,
        scratch_shapes=[pltpu.VMEM((tm, tn), F32)],
        compiler_params=_params("parallel", "parallel", "arbitrary"),
    )(*args)


def _rowwise(fn, rows, pars, outs, accs, name, tm_cap=256):
    length = rows[0][0].shape[0]
    tm = _tile(length, tm_cap, LANE)
    n = length // tm
    in_specs, args, counts = [], [], []
    for arr, c0, cw, shift in rows:
        bw = math.gcd(c0, cw) if c0 else cw
        assert bw % LANE == 0 or (c0 == 0 and cw == arr.shape[1]), (name, c0, cw)
        cnt = cw // bw
        counts.append(cnt)
        for j in range(cnt):
            in_specs.append(pl.BlockSpec(
                (tm, bw), lambda i, j=j, c0=c0, bw=bw, shift=shift: (jnp.clip(i + shift, 0, n - 1), c0 // bw + j)))
            args.append(arr)
    for p in pars:
        in_specs.append(pl.BlockSpec(p.shape, lambda i: (0, 0)))
        args.append(p)
    out_shape = [jax.ShapeDtypeStruct((length, w), dt) for w, dt in outs]
    out_specs = [pl.BlockSpec((tm, w), lambda i: (i, 0)) for w, _ in outs]
    out_shape += [jax.ShapeDtypeStruct((r, w), F32) for r, w in accs]
    out_specs += [pl.BlockSpec((r, w), lambda i: (0, 0)) for r, w in accs]
    n_in, n_out = len(args), len(outs)

    def body(*refs):
        i = pl.program_id(0)
        vals, p = [], 0
        for cnt in counts:
            blocks = [refs[p + j][...] for j in range(cnt)]
            p += cnt
            vals.append(blocks[0] if cnt == 1 else jnp.concatenate(blocks, axis=1))
        for _ in pars:
            vals.append(refs[p][...])
            p += 1
        res = fn((i, n), *vals)
        out_refs = refs[n_in:n_in + n_out]
        acc_refs = refs[n_in + n_out:]
        for r, v in zip(out_refs, res[:n_out]):
            r[...] = v.astype(r.dtype)
        if acc_refs:
            @pl.when(i == 0)
            def _():
                for r in acc_refs:
                    r[...] = jnp.zeros_like(r)
            for r, v in zip(acc_refs, res[n_out:]):
                r[...] += v

    res = pl.pallas_call(
        body, name=name, grid=(n,), in_specs=in_specs, out_specs=out_specs, out_shape=out_shape,
        compiler_params=_params("arbitrary"),
    )(*args)
    return res


def _rms(x, g):
    return x * lax.rsqrt(jnp.mean(x * x, axis=-1, keepdims=True) + NORM_EPS) * g


def _rms_fwd(x, g, name):
    return _rowwise(lambda ctx, xv, gv: (_rms(xv, gv),), [(x, 0, D_MODEL, 0)], [g],
                    [(D_MODEL, BF16)], [], name)[0]


def _rms_bwd(x, g, dh, dres, name):
    def fn(ctx, xv, dhv, drv, gv):
        _, vjp = jax.vjp(_rms, xv, gv)
        dx, dg = vjp(dhv)
        return dx + drv, dg
    return _rowwise(fn, [(x, 0, D_MODEL, 0), (dh, 0, D_MODEL, 0), (dres, 0, D_MODEL, 0)], [g],
                    [(D_MODEL, F32)], [(1, D_MODEL)], name)


def _rope_tables(length):
    pos = jnp.arange(length, dtype=F32)
    inv_freq = ROPE_THETA ** (-jnp.arange(0, ROT_DIM, 2, dtype=F32) / ROT_DIM)
    ang = pos[:, None] * inv_freq[None, :]
    cos, sin = jnp.cos(ang), jnp.sin(ang)
    half = ROT_DIM // 2
    ones = jnp.ones((length, HEAD_DIM - ROT_DIM), F32)
    zeros = jnp.zeros_like(ones)
    zh = jnp.zeros((length, half), F32)
    c64 = jnp.concatenate([cos, cos, ones], axis=1)
    s1 = jnp.concatenate([-sin, zh, zeros], axis=1)
    s2 = jnp.concatenate([zh, sin, zeros], axis=1)
    tile2 = lambda t: jnp.concatenate([t, t], axis=1)
    return tile2(c64), tile2(s1), tile2(s2)


def _lane_chunks(t):
    return [t[:, j * LANE:(j + 1) * LANE] for j in range(t.shape[1] // LANE)]


def _rope(t, c, s1, s2, n_rot):
    half = ROT_DIM // 2
    out = []
    for j, ch in enumerate(_lane_chunks(t)):
        if j < n_rot:
            ch = ch * c + pltpu.roll(ch, LANE - half, 1) * s1 + pltpu.roll(ch, half, 1) * s2
        out.append(ch)
    return jnp.concatenate(out, axis=1)


def _unrope(d, c, s1, s2, n_rot):
    half = ROT_DIM // 2
    out = []
    for j, ch in enumerate(_lane_chunks(d)):
        if j < n_rot:
            ch = ch * c + pltpu.roll(ch * s1, half, 1) + pltpu.roll(ch * s2, LANE - half, 1)
        out.append(ch)
    return jnp.concatenate(out, axis=1)


N_ROT_CHUNKS = (ATTN_WIDTH + KV_WIDTH) // LANE
QKV_WIDTH = ATTN_WIDTH + 2 * KV_WIDTH


def _split_fwd(proj, tabs, name):
    def fn(ctx, t, c, s1, s2):
        return (_rope(t, c, s1, s2, N_ROT_CHUNKS),)
    rows = [(proj, 0, QKV_WIDTH, 0)] + [(t, 0, LANE, 0) for t in tabs]
    return _rowwise(fn, rows, [], [(QKV_WIDTH, BF16)], [], name, tm_cap=BLOCK)[0]


def _split_bwd(dq, dkc, dkp, dvc, dvp, tabs, name):
    def fn(ctx, dqv, dkcv, dkpv, dvcv, dvpv, c, s1, s2):
        i, n = ctx
        keep = (i < n - 1).astype(F32)
        d = jnp.concatenate([dqv, dkcv + keep * dkpv, dvcv + keep * dvpv], axis=1)
        return (_unrope(d, c, s1, s2, N_ROT_CHUNKS),)
    rows = [(dq, 0, ATTN_WIDTH, 0), (dkc, 0, KV_WIDTH, 0), (dkp, 0, KV_WIDTH, 1), (dvc, 0, KV_WIDTH, 0),
            (dvp, 0, KV_WIDTH, 1)] + [(t, 0, LANE, 0) for t in tabs]
    return _rowwise(fn, rows, [], [(QKV_WIDTH, BF16)], [], name, tm_cap=BLOCK)[0]


def _att_scores(q_ref, kp_ref, kc_ref, sink_ref):
    n = pl.program_id(1)
    j = pl.program_id(0)
    rows = Q_GROUP * BLOCK
    qs = q_ref[...].reshape(rows, HEAD_DIM)
    kb = jnp.concatenate([kp_ref[0], kc_ref[0]], axis=0)
    s = lax.dot_general(qs, kb, _DOT_DIMS["nt"], preferred_element_type=F32) * (HEAD_DIM ** -0.5)
    r = lax.broadcasted_iota(jnp.int32, (rows, 2 * BLOCK), 0)
    kj = lax.broadcasted_iota(jnp.int32, (rows, 2 * BLOCK), 1)
    delta = (r % BLOCK) + BLOCK - kj
    ok = (delta >= 0) & (delta < WINDOW) & ((kj >= BLOCK) | (n > 0))
    s = jnp.where(ok, s, NEG_INF)
    rh = lax.broadcasted_iota(jnp.int32, (rows, 1), 0) // BLOCK
    sinks = sink_ref[...]
    lane = lax.broadcasted_iota(jnp.int32, sinks.shape, 1)
    srow = lax.broadcasted_iota(jnp.int32, sinks.shape, 0)
    sink = jnp.zeros((rows, 1), F32)
    for g in range(Q_GROUP):
        val = jnp.sum(jnp.where((lane == g) & (srow == j), sinks, 0.0), keepdims=True)
        sink = jnp.where(rh == g, val, sink)
    m = jnp.maximum(jnp.max(s, axis=-1, keepdims=True), sink)
    p = jnp.exp(s - m)
    psink = jnp.exp(sink - m)
    denom = jnp.sum(p, axis=-1, keepdims=True) + psink
    return qs, kb, p / denom, psink / denom, rh


def _att_specs(length):
    nb = length // BLOCK
    q_spec = pl.BlockSpec((Q_GROUP, BLOCK, HEAD_DIM), lambda j, n: (j, n, 0))
    prev = pl.BlockSpec((1, BLOCK, HEAD_DIM), lambda j, n: (j, jnp.maximum(n - 1, 0), 0))
    cur = pl.BlockSpec((1, BLOCK, HEAD_DIM), lambda j, n: (j, n, 0))
    sink_spec = pl.BlockSpec((N_KV_HEADS, Q_GROUP), lambda j, n: (0, 0))
    return nb, q_spec, prev, cur, sink_spec


def _att_fwd(q, k, v, sinks, name):
    length = q.shape[1]
    nb, q_spec, prev, cur, sink_spec = _att_specs(length)

    def body(q_ref, kp_ref, kc_ref, vp_ref, vc_ref, sink_ref, o_ref):
        _, _, p, _, _ = _att_scores(q_ref, kp_ref, kc_ref, sink_ref)
        vb = jnp.concatenate([vp_ref[0], vc_ref[0]], axis=0)
        o = jnp.dot(p.astype(BF16), vb, preferred_element_type=F32)
        o_ref[...] = o.reshape(Q_GROUP, BLOCK, HEAD_DIM).astype(o_ref.dtype)

    return pl.pallas_call(
        body, name=name, grid=(N_KV_HEADS, nb),
        in_specs=[q_spec, prev, cur, prev, cur, sink_spec], out_specs=q_spec,
        out_shape=jax.ShapeDtypeStruct((N_Q_HEADS, length, HEAD_DIM), BF16),
        compiler_params=_params("arbitrary", "arbitrary"),
    )(q, k, k, v, v, sinks)


def _att_bwd(q, k, v, sinks, do, name):
    length = q.shape[1]
    nb, q_spec, prev, cur, sink_spec = _att_specs(length)
    ds_spec = pl.BlockSpec((1, SUBLANE, LANE), lambda j, n: (j, 0, 0))

    def body(q_ref, kp_ref, kc_ref, vp_ref, vc_ref, sink_ref, do_ref, dq_ref, dkc_ref, dkp_ref, dvc_ref,
             dvp_ref, dsink_ref):
        n = pl.program_id(1)
        qs, kb, p, psink, rh = _att_scores(q_ref, kp_ref, kc_ref, sink_ref)
        vb = jnp.concatenate([vp_ref[0], vc_ref[0]], axis=0)
        dob = do_ref[...].reshape(Q_GROUP * BLOCK, HEAD_DIM).astype(BF16)
        dv = lax.dot_general(p.astype(BF16), dob, _DOT_DIMS["tn"], preferred_element_type=F32)
        dp = lax.dot_general(dob, vb, _DOT_DIMS["nt"], preferred_element_type=F32)
        dsum = jnp.sum(p * dp, axis=-1, keepdims=True)
        ds = (p * (dp - dsum) * (HEAD_DIM ** -0.5)).astype(BF16)
        dq = jnp.dot(ds, kb, preferred_element_type=F32)
        dk = lax.dot_general(ds, qs, _DOT_DIMS["tn"], preferred_element_type=F32)
        dq_ref[...] = dq.reshape(Q_GROUP, BLOCK, HEAD_DIM)
        dkp_ref[0] = dk[:BLOCK]
        dkc_ref[0] = dk[BLOCK:]
        dvp_ref[0] = dv[:BLOCK]
        dvc_ref[0] = dv[BLOCK:]
        dsr = -psink * dsum
        row = lax.broadcasted_iota(jnp.int32, (SUBLANE, LANE), 0)
        upd = jnp.zeros((SUBLANE, LANE), F32)
        for g in range(Q_GROUP):
            val = jnp.sum(jnp.where(rh == g, dsr, 0.0), keepdims=True)
            upd = jnp.where(row == g, val, upd)

        @pl.when(n == 0)
        def _():
            dsink_ref[...] = jnp.zeros_like(dsink_ref)

        dsink_ref[0] += upd

    kv_shape = jax.ShapeDtypeStruct((N_KV_HEADS, length, HEAD_DIM), F32)
    return pl.pallas_call(
        body, name=name, grid=(N_KV_HEADS, nb),
        in_specs=[q_spec, prev, cur, prev, cur, sink_spec, q_spec],
        out_specs=[q_spec, cur, cur, cur, cur, ds_spec],
        out_shape=[jax.ShapeDtypeStruct((N_Q_HEADS, length, HEAD_DIM), F32), kv_shape, kv_shape, kv_shape,
                   kv_shape, jax.ShapeDtypeStruct((N_KV_HEADS, SUBLANE, LANE), F32)],
        compiler_params=_params("arbitrary", "arbitrary"),
    )(q, k, k, v, v, sinks, do)


def _to_heads(t, heads):
    return t.reshape(t.shape[0], heads, HEAD_DIM).transpose(1, 0, 2)


def _from_heads(t):
    return t.transpose(1, 0, 2).reshape(t.shape[1], t.shape[0] * HEAD_DIM)


def _shift_down(z, s):
    t = lax.broadcasted_iota(jnp.int32, z.shape, 0)
    return jnp.where(t >= s, pltpu.roll(z, s, 0), 0.0)


def _shift_up(z, s):
    t = lax.broadcasted_iota(jnp.int32, z.shape, 0)
    return jnp.where(t < z.shape[0] - s, pltpu.roll(z, z.shape[0] - s, 0), 0.0)


def _conv_specs(length):
    col = lambda c0: pl.BlockSpec((length, LANE), lambda j, c0=c0: (0, c0 // LANE + j))
    w_spec = pl.BlockSpec((CONV_K, LANE), lambda j: (0, j))
    o_spec = pl.BlockSpec((length, LANE), lambda j: (0, j))
    return col, w_spec, o_spec


def _conv_fwd(proj, w, name):
    length = proj.shape[0]
    col, w_spec, o_spec = _conv_specs(length)

    def body(cb_ref, cc_ref, cx_ref, w_ref, o_ref):
        z = cc_ref[...] * cx_ref[...]
        s = w_ref[0:1, :] * _shift_down(z, 2) + w_ref[1:2, :] * _shift_down(z, 1) + w_ref[2:3, :] * z
        o_ref[...] = (cb_ref[...] * s).astype(o_ref.dtype)

    return pl.pallas_call(
        body, name=name, grid=(CONV_WIDTH // LANE,),
        in_specs=[col(C_CB), col(C_CC), col(C_CX), w_spec], out_specs=o_spec,
        out_shape=jax.ShapeDtypeStruct((length, CONV_WIDTH), BF16),
        compiler_params=_params("arbitrary"),
    )(proj, proj, proj, w)


def _conv_bwd(proj, w, dy, name):
    length = proj.shape[0]
    col, w_spec, o_spec = _conv_specs(length)
    dw_spec = pl.BlockSpec((1, LANE), lambda j: (0, j))

    def body(cb_ref, cc_ref, cx_ref, w_ref, dy_ref, dcb_ref, dcc_ref, dcx_ref, dw0_ref, dw1_ref, dw2_ref):
        cc, cx, dyv = cc_ref[...], cx_ref[...], dy_ref[...]
        z = cc * cx
        w0, w1, w2 = w_ref[0:1, :], w_ref[1:2, :], w_ref[2:3, :]
        z1, z2 = _shift_down(z, 1), _shift_down(z, 2)
        s = w0 * z2 + w1 * z1 + w2 * z
        dcb_ref[...] = (dyv * s).astype(dcb_ref.dtype)
        ds = dyv * cb_ref[...]
        dw0_ref[...] = jnp.sum(ds * z2, axis=0, keepdims=True)
        dw1_ref[...] = jnp.sum(ds * z1, axis=0, keepdims=True)
        dw2_ref[...] = jnp.sum(ds * z, axis=0, keepdims=True)
        dz = w2 * ds + w1 * _shift_up(ds, 1) + w0 * _shift_up(ds, 2)
        dcc_ref[...] = (dz * cx).astype(dcc_ref.dtype)
        dcx_ref[...] = (dz * cc).astype(dcx_ref.dtype)

    act = jax.ShapeDtypeStruct((length, CONV_WIDTH), BF16)
    dws = jax.ShapeDtypeStruct((1, CONV_WIDTH), F32)
    return pl.pallas_call(
        body, name=name, grid=(CONV_WIDTH // LANE,),
        in_specs=[col(C_CB), col(C_CC), col(C_CX), w_spec, o_spec],
        out_specs=[o_spec, o_spec, o_spec, dw_spec, dw_spec, dw_spec],
        out_shape=[act, act, act, dws, dws, dws],
        compiler_params=_params("arbitrary"),
    )(proj, proj, proj, w, dy)


def _cmul(ar, ai, br, bi):
    return ar * br - ai * bi, ar * bi + ai * br


def _scan_tables(lr, li, reverse):
    pr, pi = [lr], [li]
    for _ in range(SCAN_ROWS - 1):
        nr, ni = _cmul(pr[-1], pi[-1], lr, li)
        pr.append(nr)
        pi.append(ni)
    row = jnp.arange(SCAN_ROWS)[:, None]
    mr, mi = [], []
    for s in (1, 2, 4):
        live = (row + s < SCAN_ROWS) if reverse else (row >= s)
        mr.append(jnp.where(live, pr[s - 1], 0.0))
        mi.append(jnp.where(live, pi[s - 1], 0.0))
    order = range(SCAN_ROWS - 1, -1, -1) if reverse else range(SCAN_ROWS)
    carry_r = jnp.concatenate([pr[d] for d in order], axis=0)
    carry_i = jnp.concatenate([pi[d] for d in order], axis=0)
    return jnp.stack(mr), jnp.stack(mi), carry_r, carry_i


N_SSM_CHUNKS = 4
CHUNK_STATES = SSM_LANES // N_SSM_CHUNKS
CHUNK_CHANNELS = SSM_WIDTH // N_SSM_CHUNKS


def _scan(b, lr, li, reverse, name, states=None):
    length = b.shape[0]
    mr, mi, cr, ci = _scan_tables(lr, li, reverse)
    nchunk = length // SCAN_CHUNK
    nblk = SCAN_CHUNK // SCAN_ROWS
    cw = CHUNK_STATES
    with_dlam = states is not None

    def body(*refs):
        if with_dlam:
            b_ref, mr_ref, mi_ref, cr_ref, ci_ref, s_ref, sp_ref, o_ref, dl_ref, carry = refs
        else:
            b_ref, mr_ref, mi_ref, cr_ref, ci_ref, o_ref, carry = refs
        step = pl.program_id(0)

        @pl.when(step == 0)
        def _():
            carry[...] = jnp.zeros_like(carry)
            if with_dlam:
                dl_ref[...] = jnp.zeros_like(dl_ref)

        blocks = range(nblk - 1, -1, -1) if reverse else range(nblk)
        for j in range(N_SSM_CHUNKS):
            re, im = slice(2 * cw * j, 2 * cw * j + cw), slice(2 * cw * j + cw, 2 * cw * (j + 1))
            tl = slice(cw * j, cw * (j + 1))
            c_r, c_i = carry[0:1, re], carry[0:1, im]
            acc_r = acc_i = jnp.zeros((SCAN_ROWS, cw), F32)
            for blk in blocks:
                r0 = blk * SCAN_ROWS
                xr = b_ref[r0:r0 + SCAN_ROWS, re]
                xi = b_ref[r0:r0 + SCAN_ROWS, im]
                for kk, s in enumerate((1, 2, 4)):
                    sh = SCAN_ROWS - s if reverse else s
                    rr, ri = pltpu.roll(xr, sh, 0), pltpu.roll(xi, sh, 0)
                    ar, ai = _cmul(mr_ref[kk, :, tl], mi_ref[kk, :, tl], rr, ri)
                    xr, xi = xr + ar, xi + ai
                ar, ai = _cmul(cr_ref[:, tl], ci_ref[:, tl], c_r, c_i)
                xr, xi = xr + ar, xi + ai
                o_ref[r0:r0 + SCAN_ROWS, re] = xr
                o_ref[r0:r0 + SCAN_ROWS, im] = xi
                edge = r0 if reverse else r0 + SCAN_ROWS - 1
                c_r = o_ref[edge:edge + 1, re]
                c_i = o_ref[edge:edge + 1, im]
                if with_dlam:
                    if r0 > 0:
                        pr, pi = s_ref[r0 - 1:r0 + SCAN_ROWS - 1, re], s_ref[r0 - 1:r0 + SCAN_ROWS - 1, im]
                    else:
                        live = (step < nchunk - 1).astype(F32)
                        row = lax.broadcasted_iota(jnp.int32, (SCAN_ROWS, cw), 0)
                        pr = jnp.where(row == 0, sp_ref[SCAN_ROWS - 1:SCAN_ROWS, re] * live,
                                       pltpu.roll(s_ref[0:SCAN_ROWS, re], 1, 0))
                        pi = jnp.where(row == 0, sp_ref[SCAN_ROWS - 1:SCAN_ROWS, im] * live,
                                       pltpu.roll(s_ref[0:SCAN_ROWS, im], 1, 0))
                    acc_r = acc_r + xr * pr + xi * pi
                    acc_i = acc_i + xi * pr - xr * pi
            carry[0:1, re] = c_r
            carry[0:1, im] = c_i
            if with_dlam:
                dl_ref[:, re] += acc_r
                dl_ref[:, im] += acc_i

    width = 2 * SSM_LANES
    chunk = (lambda i: (nchunk - 1 - i, 0)) if reverse else (lambda i: (i, 0))
    blk_spec = pl.BlockSpec((SCAN_CHUNK, width), chunk)
    m_spec = pl.BlockSpec((3, SCAN_ROWS, SSM_LANES), lambda i: (0, 0, 0))
    c_spec = pl.BlockSpec((SCAN_ROWS, SSM_LANES), lambda i: (0, 0))
    in_specs, args = [blk_spec, m_spec, m_spec, c_spec, c_spec], [b, mr, mi, cr, ci]
    out_specs, out_shape = blk_spec, jax.ShapeDtypeStruct(b.shape, F32)
    if with_dlam:
        assert reverse
        per = SCAN_CHUNK // SCAN_ROWS
        before = pl.BlockSpec((SCAN_ROWS, width), lambda i: (jnp.maximum((nchunk - 1 - i) * per - 1, 0), 0))
        in_specs += [blk_spec, before]
        args += [states, states]
        out_specs = [blk_spec, pl.BlockSpec((SCAN_ROWS, width), lambda i: (0, 0))]
        out_shape = [out_shape, jax.ShapeDtypeStruct((SCAN_ROWS, width), F32)]
    return pl.pallas_call(
        body, name=name, grid=(nchunk,), in_specs=in_specs, out_specs=out_specs, out_shape=out_shape,
        scratch_shapes=[pltpu.VMEM((SUBLANE, width), F32)],
        compiler_params=_params("arbitrary"),
    )(*args)


def _mm_bd(a, b, mode, name, out_dtype=F32, add=None):
    nc = N_SSM_CHUNKS
    if mode == "tn":
        k, wa, wb = a.shape[0], a.shape[1] // nc, b.shape[1] // nc
        tk = _tile(k, 1024, LANE)

        def body(a_ref, b_ref, o_ref):
            @pl.when(pl.program_id(1) == 0)
            def _():
                o_ref[...] = jnp.zeros_like(o_ref)

            o_ref[...] += lax.dot_general(a_ref[...].astype(BF16), b_ref[...].astype(BF16), _DOT_DIMS["tn"],
                                          preferred_element_type=F32)

        return pl.pallas_call(
            body, name=name, grid=(nc, k // tk),
            in_specs=[pl.BlockSpec((tk, wa), lambda j, kk: (kk, j)), pl.BlockSpec((tk, wb), lambda j, kk: (kk, j))],
            out_specs=pl.BlockSpec((None, wa, wb), lambda j, kk: (j, 0, 0)),
            out_shape=jax.ShapeDtypeStruct((nc, wa, wb), F32),
            compiler_params=_params("parallel", "arbitrary"),
        )(a, b)
    m, wa = a.shape[0], a.shape[1] // nc
    wo = b.shape[2] if mode == "nn" else b.shape[1]
    tm = _tile(m, 512, LANE)

    def body(a_ref, b_ref, *rest):
        r = lax.dot_general(a_ref[...].astype(BF16), b_ref[...].astype(BF16), _DOT_DIMS[mode],
                            preferred_element_type=F32)
        if add is not None:
            r = r + rest[0][...]
        rest[-1][...] = r.astype(out_dtype)

    o_spec = pl.BlockSpec((tm, wo), lambda i, j: (i, j))
    in_specs = [pl.BlockSpec((tm, wa), lambda i, j: (i, j)), pl.BlockSpec((None,) + b.shape[1:], lambda i, j: (j, 0, 0))]
    args = [a, b]
    if add is not None:
        in_specs.append(o_spec)
        args.append(add)
    return pl.pallas_call(
        body, name=name, grid=(m // tm, nc), in_specs=in_specs, out_specs=o_spec,
        out_shape=jax.ShapeDtypeStruct((m, nc * wo), out_dtype),
        compiler_params=_params("parallel", "parallel"),
    )(*args)


def _block_diag(t):
    g, a, b = t.shape
    per = g // N_SSM_CHUNKS
    eye = jnp.eye(per, dtype=t.dtype)
    t = t.reshape(N_SSM_CHUNKS, per, a, b)
    return (t[:, :, :, None, :] * eye[None, :, None, :, None]).reshape(N_SSM_CHUNKS, per * a, per * b)


def _ssm_prep(a_re, a_im, b_re, b_im, c_re, c_im, log_dt):
    dt = jnp.exp(log_dt)[:, None]
    er = jnp.exp(a_re * dt)
    lr, li = er * jnp.cos(a_im * dt), er * jnp.sin(a_im * dt)
    nr, ni = lr - 1.0, li
    den = a_re * a_re + a_im * a_im
    qr, qi = (nr * a_re + ni * a_im) / den, (ni * a_re - nr * a_im) / den
    bbr = qr[..., None] * b_re - qi[..., None] * b_im
    bbi = qr[..., None] * b_im + qi[..., None] * b_re
    bmat = jnp.concatenate([_block_diag(bbr.transpose(0, 2, 1)), _block_diag(bbi.transpose(0, 2, 1))], axis=2)
    cmat = jnp.concatenate([_block_diag(c_re.transpose(0, 2, 1)), -_block_diag(c_im.transpose(0, 2, 1))], axis=1)
    return lr.reshape(1, SSM_LANES), li.reshape(1, SSM_LANES), bmat, cmat


def _ssm_act(yc, u, d):
    return jax.nn.gelu(yc + d * u)


def _glu(ys, z):
    return ys * jax.nn.sigmoid(z)


def _merge(ya, yc, ys, gl, b):
    gates = jax.nn.sigmoid(gl + b)
    return gates[:, :D_MODEL] * ya + gates[:, D_MODEL:2 * D_MODEL] * yc + gates[:, 2 * D_MODEL:] * ys


def _swiglu(gu):
    return jax.nn.silu(gu[:, :FFN_HIDDEN]) * gu[:, FFN_HIDDEN:]


def _loss_fn(x, g, t):
    e = _rms(x, g) - t
    per_tok = jnp.mean(e * e, axis=-1, keepdims=True)
    return 0.5 * jnp.sum(per_tok, axis=0, keepdims=True)


def _vjp_rowwise(f, n_row, cot_dtype=F32):
    def fn(ctx, *vals):
        prim = vals[:n_row] + vals[n_row + 1:]
        _, vjp = jax.vjp(f, *prim)
        return vjp(vals[n_row].astype(cot_dtype))
    return fn


def _layer_fwd(i, x, w, tabs):
    nm = lambda s: "l%d_%s" % (i, s)
    sv = {"x": x}
    h = _rms_fwd(x, w["norm_mix"][i:i + 1], nm("rms_mix"))
    proj = _mm(h, w["w_in"], "nn", nm("mm_in"), b_layer=i)
    qkv = _split_fwd(proj, tabs, nm("rope"))
    q, k, v = (_to_heads(qkv[:, :C_K], N_Q_HEADS), _to_heads(qkv[:, C_K:C_V], N_KV_HEADS),
               _to_heads(qkv[:, C_V:C_CB], N_KV_HEADS))
    sinks = w["attn_sinks"][i].reshape(N_KV_HEADS, Q_GROUP)
    att = _from_heads(_att_fwd(q, k, v, sinks, nm("att")))
    conv = _conv_fwd(proj, w["conv_w"][i], nm("conv"))
    lr, li, bmat, cmat = w["ssm"][i]
    u = proj[:, C_U:C_G]
    bu = _mm_bd(u, bmat, "nn", nm("mm_bu"))
    states = _scan(bu, lr, li, False, nm("scan"))
    yc = _mm_bd(states, cmat, "nn", nm("mm_c"))
    d = w["ssm_d"][i:i + 1]
    ys = _rowwise(lambda ctx, a, b, c: (_ssm_act(a, b, c),), [(yc, 0, SSM_WIDTH, 0), (u, 0, SSM_WIDTH, 0)], [d],
                  [(SSM_WIDTH, F32)], [], nm("ssm_act"))[0]
    z = _mm(ys, w["w_ssm_glu"], "nn", nm("mm_glu"), b_layer=i)
    sg = _rowwise(lambda ctx, a, b: (_glu(a, b),), [(ys, 0, SSM_WIDTH, 0), (z, 0, SSM_WIDTH, 0)], [],
                  [(SSM_WIDTH, BF16)], [], nm("glu"))[0]
    ya = _mm(att, w["w_attn_o"], "nn", nm("mm_ao"), b_layer=i)
    yv = _mm(conv, w["w_conv_o"], "nn", nm("mm_co"), b_layer=i)
    ym = _mm(sg, w["w_ssm_o"], "nn", nm("mm_so"), b_layer=i)
    bg = w["b_gate"][i:i + 1]
    merged = _rowwise(lambda ctx, a, b, c, gl, bb: (_merge(a, b, c, gl, bb),),
                      [(ya, 0, D_MODEL, 0), (yv, 0, D_MODEL, 0), (ym, 0, D_MODEL, 0), (proj, C_G, GATE_WIDTH, 0)],
                      [bg], [(D_MODEL, BF16)], [], nm("merge"))[0]
    x1 = _mm(merged, w["w_mix_o"], "nn", nm("mm_mix"), add=x, b_layer=i)
    h2 = _rms_fwd(x1, w["norm_ffn"][i:i + 1], nm("rms_ffn"))
    gu = _mm(h2, w["w_ffn_in"], "nn", nm("mm_ffn_in"), b_layer=i)
    act = _rowwise(lambda ctx, a: (_swiglu(a),), [(gu, 0, 2 * FFN_HIDDEN, 0)], [], [(FFN_HIDDEN, BF16)], [],
                   nm("swiglu"))[0]
    x2 = _mm(act, w["w_ffn_out"], "nn", nm("mm_ffn_out"), add=x1, b_layer=i)
    sv.update(h=h, proj=proj, q=q, k=k, v=v, att=att, conv=conv, u=u, states=states, yc=yc, ys=ys, z=z, sg=sg,
              ya=ya, yv=yv, ym=ym, merged=merged, x1=x1, h2=h2, gu=gu, act=act)
    return x2, sv


def _layer_bwd(i, dx2, sv, w, tabs, gb):
    nm = lambda s: "l%d_b_%s" % (i, s)
    g = {}

    def wgrad(n, lhs, rhs, label, **kw):
        if n in gb:
            gb[n] = _mm(lhs, rhs, "tn", nm(label), into=(gb[n], i, BIG_AXIS[n]), **kw)
        else:
            g[n] = _mm(lhs, rhs, "tn", nm(label), **kw)

    dact = _mm(dx2, w["w_ffn_out"], "nt", nm("mm_dact"), b_layer=i)
    wgrad("w_ffn_out", sv["act"], dx2, "mm_gw_ffn_out", tn_cap=512)
    dgu = _rowwise(_vjp_rowwise(_swiglu, 1), [(sv["gu"], 0, 2 * FFN_HIDDEN, 0), (dact, 0, FFN_HIDDEN, 0)], [],
                   [(2 * FFN_HIDDEN, BF16)], [], nm("swiglu"))[0]
    dh2 = _mm(dgu, w["w_ffn_in"], "nt", nm("mm_dh2"), tk_cap=3072, b_layer=i)
    wgrad("w_ffn_in", sv["h2"], dgu, "mm_gw_ffn_in")
    dx1, g["norm_ffn"] = _rms_bwd(sv["x1"], w["norm_ffn"][i:i + 1], dh2, dx2, nm("rms_ffn"))
    dmerged = _mm(dx1, w["w_mix_o"], "nt", nm("mm_dmerged"), b_layer=i)
    wgrad("w_mix_o", sv["merged"], dx1, "mm_gw_mix", tn_cap=512)
    proj = sv["proj"]
    bg = w["b_gate"][i:i + 1]
    dya, dyv, dym, dgl, g["b_gate"] = _rowwise(
        _vjp_rowwise(_merge, 4),
        [(sv["ya"], 0, D_MODEL, 0), (sv["yv"], 0, D_MODEL, 0), (sv["ym"], 0, D_MODEL, 0),
         (proj, C_G, GATE_WIDTH, 0), (dmerged, 0, D_MODEL, 0)], [bg],
        [(D_MODEL, BF16), (D_MODEL, BF16), (D_MODEL, BF16), (GATE_WIDTH, BF16)], [(1, GATE_WIDTH)], nm("merge"))
    dsg = _mm(dym, w["w_ssm_o"], "nt", nm("mm_dsg"), b_layer=i)
    wgrad("w_ssm_o", sv["sg"], dym, "mm_gw_so")
    dys0, dz = _rowwise(_vjp_rowwise(_glu, 2), [(sv["ys"], 0, SSM_WIDTH, 0), (sv["z"], 0, SSM_WIDTH, 0),
                                                 (dsg, 0, SSM_WIDTH, 0)], [],
                        [(SSM_WIDTH, F32), (SSM_WIDTH, BF16)], [], nm("glu"))
    dys = _mm(dz, w["w_ssm_glu"], "nt", nm("mm_dys"), add=dys0, b_layer=i)
    wgrad("w_ssm_glu", sv["ys"], dz, "mm_gw_glu")
    d = w["ssm_d"][i:i + 1]
    dyc, du0, g["ssm_d"] = _rowwise(
        _vjp_rowwise(_ssm_act, 2), [(sv["yc"], 0, SSM_WIDTH, 0), (sv["u"], 0, SSM_WIDTH, 0), (dys, 0, SSM_WIDTH, 0)],
        [d], [(SSM_WIDTH, F32), (SSM_WIDTH, F32)], [(1, SSM_WIDTH)], nm("ssm_act"))
    lr, li, bmat, cmat = w["ssm"][i]
    dstates = _mm_bd(dyc, cmat, "nt", nm("mm_dstates"))
    g_cmat = _mm_bd(sv["states"], dyc, "tn", nm("mm_gc"))
    gs, dl = _scan(dstates, lr, -li, True, nm("scan"), states=sv["states"])
    g_lam = jnp.sum(dl, axis=0).reshape(N_SSM_CHUNKS, 2, CHUNK_STATES)
    du = _mm_bd(gs, bmat, "nt", nm("mm_du"), out_dtype=BF16, add=du0)
    g_bmat = _mm_bd(sv["u"], gs, "tn", nm("mm_gb"))
    g["ssm"] = (g_lam[:, 0].reshape(1, SSM_LANES), g_lam[:, 1].reshape(1, SSM_LANES), g_bmat, g_cmat)
    dconv = _mm(dyv, w["w_conv_o"], "nt", nm("mm_dconv"), b_layer=i)
    wgrad("w_conv_o", sv["conv"], dyv, "mm_gw_co")
    dcb, dcc, dcx, dw0, dw1, dw2 = _conv_bwd(proj, w["conv_w"][i], dconv, nm("conv"))
    g["conv_w"] = jnp.concatenate([dw0, dw1, dw2], axis=0)
    datt = _mm(dya, w["w_attn_o"], "nt", nm("mm_datt"), b_layer=i)
    wgrad("w_attn_o", sv["att"], dya, "mm_gw_ao")
    sinks = w["attn_sinks"][i].reshape(N_KV_HEADS, Q_GROUP)
    dq, dkc, dkp, dvc, dvp, dsk = _att_bwd(sv["q"], sv["k"], sv["v"], sinks, _to_heads(datt, N_Q_HEADS), nm("att"))
    g["attn_sinks"] = dsk[:, :Q_GROUP, 0].reshape(N_Q_HEADS)
    dqkv = _split_bwd(_from_heads(dq), _from_heads(dkc), _from_heads(dkp), _from_heads(dvc), _from_heads(dvp),
                      tabs, nm("rope"))
    dproj = jnp.concatenate([dqkv, dcb, dcc, dcx, du, dgl], axis=1)
    dh = _mm(dproj, w["w_in"], "nt", nm("mm_dh"), tk_cap=3072, b_layer=i)
    wgrad("w_in", sv["h"], dproj, "mm_gw_in")
    dx, g["norm_mix"] = _rms_bwd(sv["x"], w["norm_mix"][i:i + 1], dh, dx1, nm("rms_mix"))
    return dx, g


def _local_step(x, target, w):
    length = x.shape[0]
    tabs = _rope_tables(length)
    ssm_names = ("ssm_a_re", "ssm_a_im", "ssm_b_re", "ssm_b_im", "ssm_c_re", "ssm_c_im", "ssm_log_dt")
    w = dict(w)
    preps = [jax.vjp(_ssm_prep, *[w[n][i] for n in ssm_names]) for i in range(DEPTH)]
    w["ssm"] = [p[0] for p in preps]
    saved = []
    for i in range(DEPTH):
        x, sv = _layer_fwd(i, x, w, tabs)
        saved.append(sv)

    def loss_fn(ctx, xv, tv, gv):
        val, vjp = jax.vjp(_loss_fn, xv, gv, tv)
        dx, dg, _ = vjp(jnp.ones((1, 1), F32))
        return dx, dg, val + jnp.zeros((1, LANE), F32)

    gfin = w["norm_final"].reshape(1, D_MODEL)
    dx, g_final, loss = _rowwise(loss_fn, [(x, 0, D_MODEL, 0), (target, 0, D_MODEL, 0)], [gfin],
                                 [(D_MODEL, F32)], [(1, D_MODEL), (1, LANE)], "loss")
    gb = {}
    for n in BIG:
        depth, ra, cb = w[n].shape
        if n != "w_in":
            gb[n] = lax.empty((N_CHIPS, depth) + ((ra // N_CHIPS, cb) if BIG_AXIS[n] == 1 else (ra, cb // N_CHIPS)), F32)
    layer_grads = [None] * DEPTH
    for i in reversed(range(DEPTH)):
        dx, layer_grads[i] = _layer_bwd(i, dx, saved[i], w, tabs, gb)
    grads = dict(gb)
    for n in layer_grads[0]:
        if n != "ssm":
            grads[n] = jnp.stack([lg[n] for lg in layer_grads])
    ssm_g = [preps[i][1](layer_grads[i]["ssm"]) for i in range(DEPTH)]
    for j, n in enumerate(ssm_names):
        grads[n] = jnp.stack([sg[j] for sg in ssm_g])
    grads["norm_final"] = g_final.reshape(D_MODEL)
    for n in ("norm_mix", "norm_ffn", "b_gate", "ssm_d"):
        grads[n] = grads[n].reshape(grads[n].shape[0], -1)
    return loss, dx, grads


COLS = 1024
ANY_SPEC = pl.BlockSpec(memory_space=pl.ANY)


def _place():
    return lax.axis_index("x"), lax.axis_index("y"), lax.axis_index("c")


def _other_chips(x, y):
    return [(1 - x, y), (x, 1 - y), (1 - x, 1 - y)]


def _remote(src, dst, send_sems, recv_sems, k, to):
    return pltpu.make_async_remote_copy(src_ref=src, dst_ref=dst, send_sem=send_sems.at[k], recv_sem=recv_sems.at[k],
                                        device_id=to, device_id_type=MESH_ID)


def _comm_call(body, name, out_shape, n_sems, n_local, args):
    return pl.pallas_call(
        body, name=name, out_shape=out_shape, in_specs=[ANY_SPEC] * len(args),
        out_specs=[ANY_SPEC] * len(out_shape),
        scratch_shapes=[pltpu.SemaphoreType.DMA((n_sems,)), pltpu.SemaphoreType.DMA((n_sems,)),
                        pltpu.SemaphoreType.DMA((n_local,))],
    )(*args)


def _gather_weights(shards, name):
    nt = len(shards)
    hds = [s.shape[0] // 2 for s in shards]

    def body(*refs):
        srcs, outs = refs[:nt], refs[nt:2 * nt]
        send_sems, recv_sems, _ = refs[2 * nt:]
        x, y, c = _place()
        me, sibling = (x, y, c), (x, y, 1 - c)
        chips = _other_chips(x, y)

        def blk(t, chip, hc):
            return outs[t].at[2 * chip[0] + chip[1], pl.ds(hc * hds[t], hds[t])]

        first = [_remote(srcs[t].at[pl.ds(c * hds[t], hds[t])], blk(t, (x, y), c), send_sems, recv_sems, 6 * t + j,
                         (*chip, c)) for t in range(nt) for j, chip in enumerate(chips)]
        for cp in first:
            cp.start()
        passed = []
        for j, chip in enumerate(chips):
            for t in range(nt):
                _remote(blk(t, chip, c), blk(t, chip, c), send_sems, recv_sems, 6 * t + j, me).wait_recv()
                passed.append(_remote(blk(t, chip, c), blk(t, chip, c), send_sems, recv_sems, 6 * t + 3 + j, sibling))
                passed[-1].start()
        for j, chip in enumerate(chips):
            for t in range(nt):
                _remote(blk(t, chip, 1 - c), blk(t, chip, 1 - c), send_sems, recv_sems, 6 * t + 3 + j, me).wait_recv()
        for cp in first + passed:
            cp.wait_send()

    out_shape = [jax.ShapeDtypeStruct((N_CHIPS,) + s.shape, s.dtype) for s in shards]
    return _comm_call(body, name, out_shape, 6 * nt, 1, shards)


def _swap_halves(gs, name):
    nt = len(gs)
    hds = [g.shape[1] // 2 for g in gs]

    def body(*refs):
        srcs, outs = refs[:nt], refs[nt:2 * nt]
        send_sems, recv_sems, _ = refs[2 * nt:]
        x, y, c = _place()
        cps = [_remote(srcs[t].at[s, pl.ds((1 - c) * hds[t], hds[t])], outs[t].at[s], send_sems, recv_sems,
                       N_CHIPS * t + s, (x, y, 1 - c)) for t in range(nt) for s in range(N_CHIPS)]
        for cp in cps:
            cp.start()
        for cp in cps:
            cp.wait()

    out_shape = [jax.ShapeDtypeStruct((N_CHIPS, g.shape[1] // 2) + g.shape[2:], g.dtype) for g in gs]
    return _comm_call(body, name, out_shape, N_CHIPS * nt, 1, gs)


def _exchange_shards(parts, name):
    nt = len(parts)

    def body(*refs):
        srcs, outs = refs[:nt], refs[nt:2 * nt]
        send_sems, recv_sems, _ = refs[2 * nt:]
        x, y, c = _place()
        cps = [_remote(srcs[t].at[2 * chip[0] + chip[1]], outs[t].at[j], send_sems, recv_sems, 3 * t + j, (*chip, c))
               for t in range(nt) for j, chip in enumerate(_other_chips(x, y))]
        for cp in cps:
            cp.start()
        for cp in cps:
            cp.wait()

    out_shape = [jax.ShapeDtypeStruct((N_CHIPS - 1,) + p.shape[1:], p.dtype) for p in parts]
    return _comm_call(body, name, out_shape, 3 * nt, 1, parts)


def _join_halves(reds, name):
    nt = len(reds)
    hds = [r.shape[1] for r in reds]

    def body(*refs):
        srcs, outs = refs[:nt], refs[nt:2 * nt]
        send_sems, recv_sems, _ = refs[2 * nt:]
        x, y, c = _place()
        cps = [_remote(srcs[t].at[0], outs[t].at[pl.ds(c * hds[t], hds[t])], send_sems, recv_sems, t, (x, y, 1 - c))
               for t in range(nt)]
        for cp in cps:
            cp.start()
        for t in range(nt):
            _remote(srcs[t].at[0], outs[t].at[pl.ds((1 - c) * hds[t], hds[t])], send_sems, recv_sems, t,
                    (x, y, c)).wait_recv()
        for cp in cps:
            cp.wait_send()

    out_shape = [jax.ShapeDtypeStruct((2 * r.shape[1],) + r.shape[2:], r.dtype) for r in reds]
    return _comm_call(body, name, out_shape, nt, 1, reds)


SUM_BLOCK_ELEMS = 256 * 1024


def _sum_windows(parts, lead, name, out_dtypes=(F32,)):
    a, b = parts[0][0].shape[2:]
    ta = _tile(a, max(SUBLANE, SUM_BLOCK_ELEMS // b), 2 * SUBLANE)
    offs = jnp.stack([jnp.stack([jnp.asarray(o, jnp.int32) for o in off]) for _, off in parts])
    n_in = len(parts)

    def body(off_ref, *refs):
        acc = refs[0][...].astype(F32)
        for r in refs[1:n_in]:
            acc = acc + r[...].astype(F32)
        for r in refs[n_in:]:
            r[...] = acc.astype(r.dtype)

    in_specs = [pl.BlockSpec((1, 1, ta, b), lambda p, q, i, off, k=k: (off[k, 0] + p, off[k, 1] + q, i, 0))
                for k in range(n_in)]
    o_spec = pl.BlockSpec((1, 1, ta, b), lambda p, q, i, off: (p, q, i, 0))
    return pl.pallas_call(
        body, name=name, out_shape=[jax.ShapeDtypeStruct(tuple(lead) + (a, b), dt) for dt in out_dtypes],
        grid_spec=pltpu.PrefetchScalarGridSpec(
            num_scalar_prefetch=1, grid=tuple(lead) + (a // ta,), in_specs=in_specs,
            out_specs=[o_spec] * len(out_dtypes)),
        compiler_params=_params("arbitrary", "arbitrary", "arbitrary"),
    )(offs, *[arr for arr, _ in parts])


def _reduce_scatter(gs, names, wire):
    x, y, c = _place()
    theirs = _swap_halves(gs, "rs_swap_halves")
    pairs = [_sum_windows([(g, (0, c * (g.shape[1] // 2))), (t, (0, 0))], (N_CHIPS, g.shape[1] // 2),
                          "rs_sum_pair_" + n, (F32, wd)) for g, t, n, wd in zip(gs, theirs, names, wire)]
    others = _exchange_shards([p[1] for p in pairs], "rs_exchange")
    reds = [_sum_windows([(p[0], (2 * x + y, 0))] + [(o, (j, 0)) for j in range(N_CHIPS - 1)], (1, p[0].shape[1]),
                         "rs_sum_chips_" + n)[0] for p, o, n in zip(pairs, others, names)]
    joined = _join_halves(reds, "rs_join")
    return [lax.dynamic_update_slice_in_dim(j, r[0], c * r.shape[1], axis=0) for j, r in zip(joined, reds)]


def _adamw(wt, g, m, v, name):
    cols = wt.shape[-1]
    r = wt.size // cols
    tr = _tile(r, max(SUBLANE, SUM_BLOCK_ELEMS // 2 // max(cols, LANE)), SUBLANE)

    def body(w_ref, g_ref, m_ref, v_ref, d_ref, nm_ref, nv_ref):
        gv = g_ref[...]
        mn = ADAM_B1 * m_ref[...] + (1.0 - ADAM_B1) * gv
        vn = ADAM_B2 * v_ref[...] + (1.0 - ADAM_B2) * jnp.square(gv)
        m_hat = mn / (1.0 - ADAM_B1 ** ADAM_STEP)
        v_hat = vn / (1.0 - ADAM_B2 ** ADAM_STEP)
        d_ref[...] = -ADAM_LR * (m_hat / (jnp.sqrt(v_hat) + ADAM_EPS) + ADAM_WD * w_ref[...])
        nm_ref[...] = mn
        nv_ref[...] = vn

    spec = pl.BlockSpec((tr, cols), lambda i: (i, 0))
    shp = jax.ShapeDtypeStruct((r, cols), F32)
    res = pl.pallas_call(
        body, name=name, grid=(r // tr,), in_specs=[spec] * 4, out_specs=[spec] * 3, out_shape=[shp] * 3,
        compiler_params=_params("parallel"),
    )(*[t.reshape(r, cols) for t in (wt, g, m, v)])
    return [t.reshape(wt.shape) for t in res]


def _join_shards(n, piece):
    _, depth, a, b = piece.shape
    if BIG_AXIS[n] == 2:
        return piece.transpose(1, 2, 0, 3).reshape(depth, a, N_CHIPS * b)
    return piece.transpose(1, 0, 2, 3).reshape(depth, N_CHIPS * a, b)


def _split_shards(n, g):
    depth, a, b = g.shape
    if BIG_AXIS[n] == 2:
        return g.reshape(depth, a, N_CHIPS, b // N_CHIPS).transpose(2, 0, 1, 3)
    return g.reshape(depth, N_CHIPS, a // N_CHIPS, b).transpose(1, 0, 2, 3)


SMALL_PART_ROWS = 2 * SUBLANE


def _rows_of(t):
    return -(-t.size // COLS)


def _pack_small(ts):
    rows = [jnp.pad(t.reshape(-1), (0, _rows_of(t) * COLS - t.size)).reshape(-1, COLS) for t in ts]
    total = sum(r.shape[0] for r in rows)
    part = -(-total // (N_DEV * SMALL_PART_ROWS)) * SMALL_PART_ROWS
    rows.append(jnp.zeros((N_DEV * part - total, COLS), F32))
    return jnp.concatenate(rows, axis=0).reshape(N_CHIPS, 2, part, COLS)


def _unpack_small(buf, like):
    buf, out, r0 = buf.reshape(-1, COLS), [], 0
    for t in like:
        out.append(buf[r0:r0 + _rows_of(t)].reshape(-1)[:t.size].reshape(t.shape))
        r0 += _rows_of(t)
    return out


def kernel(x, norm_mix, w_in, b_gate, attn_sinks, w_attn_o, conv_w, w_conv_o, ssm_a_re, ssm_a_im, ssm_b_re, ssm_b_im, ssm_c_re, ssm_c_im, ssm_d, ssm_log_dt, w_ssm_glu, w_ssm_o, w_mix_o, norm_ffn, w_ffn_in, w_ffn_out, norm_final, loss_target, m_norm_mix, m_w_in, m_b_gate, m_attn_sinks, m_w_attn_o, m_conv_w, m_w_conv_o, m_ssm_a_re, m_ssm_a_im, m_ssm_b_re, m_ssm_b_im, m_ssm_c_re, m_ssm_c_im, m_ssm_d, m_ssm_log_dt, m_w_ssm_glu, m_w_ssm_o, m_w_mix_o, m_norm_ffn, m_w_ffn_in, m_w_ffn_out, m_norm_final, v_norm_mix, v_w_in, v_b_gate, v_attn_sinks, v_w_attn_o, v_conv_w, v_w_conv_o, v_ssm_a_re, v_ssm_a_im, v_ssm_b_re, v_ssm_b_im, v_ssm_c_re, v_ssm_c_im, v_ssm_d, v_ssm_log_dt, v_w_ssm_glu, v_w_ssm_o, v_w_mix_o, v_norm_ffn, v_w_ffn_in, v_w_ffn_out, v_norm_final):
    a = dict(zip(ARG_NAMES, (
        x, norm_mix, w_in, b_gate, attn_sinks, w_attn_o, conv_w, w_conv_o, ssm_a_re, ssm_a_im, ssm_b_re, ssm_b_im,
        ssm_c_re, ssm_c_im, ssm_d, ssm_log_dt, w_ssm_glu, w_ssm_o, w_mix_o, norm_ffn, w_ffn_in, w_ffn_out, norm_final,
        loss_target, m_norm_mix, m_w_in, m_b_gate, m_attn_sinks, m_w_attn_o, m_conv_w, m_w_conv_o, m_ssm_a_re,
        m_ssm_a_im, m_ssm_b_re, m_ssm_b_im, m_ssm_c_re, m_ssm_c_im, m_ssm_d, m_ssm_log_dt, m_w_ssm_glu, m_w_ssm_o,
        m_w_mix_o, m_norm_ffn, m_w_ffn_in, m_w_ffn_out, m_norm_final, v_norm_mix, v_w_in, v_b_gate, v_attn_sinks,
        v_w_attn_o, v_conv_w, v_w_conv_o, v_ssm_a_re, v_ssm_a_im, v_ssm_b_re, v_ssm_b_im, v_ssm_c_re, v_ssm_c_im,
        v_ssm_d, v_ssm_log_dt, v_w_ssm_glu, v_w_ssm_o, v_w_mix_o, v_norm_ffn, v_w_ffn_in, v_w_ffn_out, v_norm_final)))
    px, py, _ = _place()
    chip = 2 * px + py

    gathered = BIG + ("conv_w",)
    own = [a[n].astype(BF16) for n in BIG] + [a["conv_w"]]
    gath = _gather_weights(own, "gather_weights")
    w = {n: _join_shards(n, lax.dynamic_update_slice_in_dim(p, o[None], chip, axis=0))
         for n, p, o in zip(gathered, gath, own)}
    for n in SMALL:
        w[n] = a[n]

    loss, dx, grads = _local_step(a["x"][0], a["loss_target"][0], w)

    small_names = SMALL + ("conv_w",)
    small = _pack_small([grads[n] for n in small_names])
    rs = _reduce_scatter([_split_shards(n, grads[n]) if n == "w_in" else grads[n] for n in BIG] + [small],
                         BIG + ("small",),
                         (BF16,) * len(BIG) + (F32,))
    red = dict(zip(BIG, rs))
    small_all = _gather_weights([rs[-1]], "gather_small")[0]
    small_all = lax.dynamic_update_slice_in_dim(small_all, rs[-1][None], chip, axis=0)
    red.update(zip(small_names, _unpack_small(small_all, [grads[n] for n in small_names])))
    lane = a["conv_w"].shape[2]
    red["conv_w"] = lax.dynamic_slice_in_dim(red["conv_w"], chip * lane, lane, axis=2)

    loss_all = lax.psum(loss[0, 0], ("x", "y", "c"))
    deltas, new_m, new_v = [], [], []
    for n in WEIGHTS:
        d, mn, vn = _adamw(a[n], red[n], a["m_" + n], a["v_" + n], "adamw_" + n)
        deltas.append(d)
        new_m.append(mn)
        new_v.append(vn)
    return (loss_all, dx[None], *[red[n] for n in WEIGHTS], *deltas, *new_m, *new_v)
```

```python
import functools
import math

import jax
import jax.numpy as jnp
import numpy as np
from jax import lax
from jax.experimental import pallas as pl
from jax.experimental.pallas import tpu as pltpu

F32 = jnp.float32
BF16 = jnp.bfloat16

D_MODEL = 1024
DEPTH = 4
N_Q_HEADS = 8
N_KV_HEADS = 2
HEAD_DIM = 64
Q_GROUP = N_Q_HEADS // N_KV_HEADS
WINDOW = 128
BLOCK = 128
ROPE_THETA = 500000.0
ROT_DIM = HEAD_DIM // 4
ATTN_WIDTH = N_Q_HEADS * HEAD_DIM
KV_WIDTH = N_KV_HEADS * HEAD_DIM
NEG_INF = -1e30
CONV_WIDTH = 512
CONV_K = 3
SSM_WIDTH = 512
SSM_GROUP = 16
SSM_GROUPS = 32
SSM_STATE = 64
SSM_LANES = SSM_GROUPS * SSM_STATE
GATE_WIDTH = 3 * D_MODEL
FFN_HIDDEN = 2816
NORM_EPS = 1e-6
IN_COLS = 5888
C_Q, C_K, C_V, C_CB, C_CC, C_CX, C_U, C_G = 0, 512, 640, 768, 1280, 1792, 2304, 2816

ADAM_LR = 0.001
ADAM_B1 = 0.9
ADAM_B2 = 0.999
ADAM_EPS = 1e-08
ADAM_WD = 0.01
ADAM_STEP = 10

N_CHIPS = 4
N_DEV = 8
MESH_ID = pl.DeviceIdType.MESH

VMEM_LIMIT_BYTES = 48 * 1024 * 1024
LANE = 128
SUBLANE = 8
SCAN_ROWS = 8
SCAN_CHUNK = 128

BIG = ("w_in", "w_attn_o", "w_conv_o", "w_ssm_glu", "w_ssm_o", "w_mix_o", "w_ffn_in", "w_ffn_out")
BIG_AXIS = {"w_in": 2, "w_attn_o": 2, "w_conv_o": 2, "w_ssm_glu": 1, "w_ssm_o": 2, "w_mix_o": 1,
            "w_ffn_in": 2, "w_ffn_out": 1, "conv_w": 2}
SMALL = ("norm_mix", "b_gate", "attn_sinks", "ssm_a_re", "ssm_a_im", "ssm_b_re", "ssm_b_im",
         "ssm_c_re", "ssm_c_im", "ssm_d", "ssm_log_dt", "norm_ffn", "norm_final")
WEIGHTS = ("norm_mix", "w_in", "b_gate", "attn_sinks", "w_attn_o", "conv_w", "w_conv_o", "ssm_a_re",
           "ssm_a_im", "ssm_b_re", "ssm_b_im", "ssm_c_re", "ssm_c_im", "ssm_d", "ssm_log_dt",
           "w_ssm_glu", "w_ssm_o", "w_mix_o", "norm_ffn", "w_ffn_in", "w_ffn_out", "norm_final")
ARG_NAMES = ("x",) + WEIGHTS + ("loss_target",) + tuple("m_" + n for n in WEIGHTS) + tuple(
    "v_" + n for n in WEIGHTS)


def _params(*sem):
    return pltpu.CompilerParams(dimension_semantics=sem if sem else None,
                                vmem_limit_bytes=VMEM_LIMIT_BYTES)


def _tile(dim, cap, align):
    t = min(cap, dim) // align * align
    while t >= align:
        if dim % t == 0:
            return t
        t -= align
    return dim


_DOT_DIMS = {"nn": (((1,), (0,)), ((), ())), "nt": (((1,), (1,)), ((), ())), "tn": (((0,), (0,)), ((), ()))}


def _mm(a, b, mode, name, out_dtype=F32, add=None, tm_cap=512, tn_cap=3072, tk_cap=1024, b_layer=None, into=None):
    bshape = b.shape if b_layer is None else b.shape[1:]
    if mode == "nn":
        (m, k), (k2, n) = a.shape, bshape
    elif mode == "nt":
        (m, k), (n, k2) = a.shape, bshape
    else:
        (k, m), (k2, n) = a.shape, bshape
    assert k == k2, (name, a.shape, b.shape)
    tm, tn, tk = _tile(m, tm_cap, LANE), _tile(n, tn_cap, LANE), _tile(k, tk_cap, LANE)
    if into is not None:
        buf, layer, axis = into
        _, _, ra, cb = buf.shape
        if axis == 1:
            tm = m
        else:
            tn = _tile(cb, tn_cap, LANE)
            assert cb % tn == 0 and tn % LANE == 0, (name, cb, tn)
    nk = k // tk
    dims = _DOT_DIMS[mode]

    def body(a_ref, b_ref, *rest):
        rest = list(rest)
        add_ref = rest.pop(0) if add is not None else None
        if into is not None:
            rest.pop(0)
        o_ref, acc = rest
        kk = pl.program_id(2)

        @pl.when(kk == 0)
        def _():
            acc[...] = jnp.zeros_like(acc)

        acc[...] += lax.dot_general(a_ref[...].astype(BF16), b_ref[...].astype(BF16), dims,
                                    preferred_element_type=F32)

        @pl.when(kk == nk - 1)
        def _():
            r = acc[...]
            if add is not None:
                r = r + add_ref[...]
            o_ref[...] = r.astype(o_ref.dtype).reshape(o_ref.shape)

    if mode == "tn":
        a_spec = pl.BlockSpec((tk, tm), lambda i, j, kk: (kk, i))
    else:
        a_spec = pl.BlockSpec((tm, tk), lambda i, j, kk: (i, kk))
    lead = () if b_layer is None else (None,)
    at = (lambda *ix: ix) if b_layer is None else (lambda *ix: (b_layer,) + ix)
    if mode == "nt":
        b_spec = pl.BlockSpec(lead + (tn, tk), lambda i, j, kk: at(j, kk))
    else:
        b_spec = pl.BlockSpec(lead + (tk, tn), lambda i, j, kk: at(kk, j))
    o_spec = pl.BlockSpec((tm, tn), lambda i, j, kk: (i, j))
    in_specs, args = [a_spec, b_spec], [a, b]
    if add is not None:
        in_specs.append(o_spec)
        args.append(add)
    out_shape, aliases = jax.ShapeDtypeStruct((m, n), out_dtype), {}
    if into is not None:
        in_specs.append(pl.BlockSpec(memory_space=pl.ANY))
        aliases = {len(args): 0}
        args.append(buf)
        out_shape = jax.ShapeDtypeStruct(buf.shape, buf.dtype)
        if axis == 1:
            o_spec = pl.BlockSpec((N_CHIPS, None, ra, tn), lambda i, j, kk: (0, layer, 0, j))
        else:
            per = cb // tn
            o_spec = pl.BlockSpec((None, None, tm, tn), lambda i, j, kk: (j // per, layer, i, j % per))
    return pl.pallas_call(
        body, name=name, grid=(m // tm, n // tn, nk), in_specs=in_specs, out_specs=o_spec,
        out_shape=out_shape, input_output_aliases=aliases,
        scratch_shapes=[pltpu.VMEM((tm, tn), F32)],
        compiler_params=_params("parallel", "parallel", "arbitrary"),
    )(*args)


def _rowwise(fn, rows, pars, outs, accs, name, tm_cap=256):
    length = rows[0][0].shape[0]
    tm = _tile(length, tm_cap, LANE)
    n = length // tm
    in_specs, args, counts = [], [], []
    for arr, c0, cw, shift in rows:
        bw = math.gcd(c0, cw) if c0 else cw
        assert bw % LANE == 0 or (c0 == 0 and cw == arr.shape[1]), (name, c0, cw)
        cnt = cw // bw
        counts.append(cnt)
        for j in range(cnt):
            in_specs.append(pl.BlockSpec(
                (tm, bw), lambda i, j=j, c0=c0, bw=bw, shift=shift: (jnp.clip(i + shift, 0, n - 1), c0 // bw + j)))
            args.append(arr)
    for p in pars:
        in_specs.append(pl.BlockSpec(p.shape, lambda i: (0, 0)))
        args.append(p)
    out_shape = [jax.ShapeDtypeStruct((length, w), dt) for w, dt in outs]
    out_specs = [pl.BlockSpec((tm, w), lambda i: (i, 0)) for w, _ in outs]
    out_shape += [jax.ShapeDtypeStruct((r, w), F32) for r, w in accs]
    out_specs += [pl.BlockSpec((r, w), lambda i: (0, 0)) for r, w in accs]
    n_in, n_out = len(args), len(outs)

    def body(*refs):
        i = pl.program_id(0)
        vals, p = [], 0
        for cnt in counts:
            blocks = [refs[p + j][...] for j in range(cnt)]
            p += cnt
            vals.append(blocks[0] if cnt == 1 else jnp.concatenate(blocks, axis=1))
        for _ in pars:
            vals.append(refs[p][...])
            p += 1
        res = fn((i, n), *vals)
        out_refs = refs[n_in:n_in + n_out]
        acc_refs = refs[n_in + n_out:]
        for r, v in zip(out_refs, res[:n_out]):
            r[...] = v.astype(r.dtype)
        if acc_refs:
            @pl.when(i == 0)
            def _():
                for r in acc_refs:
                    r[...] = jnp.zeros_like(r)
            for r, v in zip(acc_refs, res[n_out:]):
                r[...] += v

    res = pl.pallas_call(
        body, name=name, grid=(n,), in_specs=in_specs, out_specs=out_specs, out_shape=out_shape,
        compiler_params=_params("arbitrary"),
    )(*args)
    return res


def _rms(x, g):
    return x * lax.rsqrt(jnp.mean(x * x, axis=-1, keepdims=True) + NORM_EPS) * g


def _rms_fwd(x, g, name):
    return _rowwise(lambda ctx, xv, gv: (_rms(xv, gv),), [(x, 0, D_MODEL, 0)], [g],
                    [(D_MODEL, BF16)], [], name)[0]


def _rms_bwd(x, g, dh, dres, name):
    def fn(ctx, xv, dhv, drv, gv):
        _, vjp = jax.vjp(_rms, xv, gv)
        dx, dg = vjp(dhv)
        return dx + drv, dg
    return _rowwise(fn, [(x, 0, D_MODEL, 0), (dh, 0, D_MODEL, 0), (dres, 0, D_MODEL, 0)], [g],
                    [(D_MODEL, F32)], [(1, D_MODEL)], name)


def _rope_tables(length):
    pos = jnp.arange(length, dtype=F32)
    inv_freq = ROPE_THETA ** (-jnp.arange(0, ROT_DIM, 2, dtype=F32) / ROT_DIM)
    ang = pos[:, None] * inv_freq[None, :]
    cos, sin = jnp.cos(ang), jnp.sin(ang)
    half = ROT_DIM // 2
    ones = jnp.ones((length, HEAD_DIM - ROT_DIM), F32)
    zeros = jnp.zeros_like(ones)
    zh = jnp.zeros((length, half), F32)
    c64 = jnp.concatenate([cos, cos, ones], axis=1)
    s1 = jnp.concatenate([-sin, zh, zeros], axis=1)
    s2 = jnp.concatenate([zh, sin, zeros], axis=1)
    tile2 = lambda t: jnp.concatenate([t, t], axis=1)
    return tile2(c64), tile2(s1), tile2(s2)


def _lane_chunks(t):
    return [t[:, j * LANE:(j + 1) * LANE] for j in range(t.shape[1] // LANE)]


def _rope(t, c, s1, s2, n_rot):
    half = ROT_DIM // 2
    out = []
    for j, ch in enumerate(_lane_chunks(t)):
        if j < n_rot:
            ch = ch * c + pltpu.roll(ch, LANE - half, 1) * s1 + pltpu.roll(ch, half, 1) * s2
        out.append(ch)
    return jnp.concatenate(out, axis=1)


def _unrope(d, c, s1, s2, n_rot):
    half = ROT_DIM // 2
    out = []
    for j, ch in enumerate(_lane_chunks(d)):
        if j < n_rot:
            ch = ch * c + pltpu.roll(ch * s1, half, 1) + pltpu.roll(ch * s2, LANE - half, 1)
        out.append(ch)
    return jnp.concatenate(out, axis=1)


N_ROT_CHUNKS = (ATTN_WIDTH + KV_WIDTH) // LANE
QKV_WIDTH = ATTN_WIDTH + 2 * KV_WIDTH


def _split_fwd(proj, tabs, name):
    def fn(ctx, t, c, s1, s2):
        return (_rope(t, c, s1, s2, N_ROT_CHUNKS),)
    rows = [(proj, 0, QKV_WIDTH, 0)] + [(t, 0, LANE, 0) for t in tabs]
    return _rowwise(fn, rows, [], [(QKV_WIDTH, BF16)], [], name, tm_cap=BLOCK)[0]


def _split_bwd(dq, dkc, dkp, dvc, dvp, tabs, name):
    def fn(ctx, dqv, dkcv, dkpv, dvcv, dvpv, c, s1, s2):
        i, n = ctx
        keep = (i < n - 1).astype(F32)
        d = jnp.concatenate([dqv, dkcv + keep * dkpv, dvcv + keep * dvpv], axis=1)
        return (_unrope(d, c, s1, s2, N_ROT_CHUNKS),)
    rows = [(dq, 0, ATTN_WIDTH, 0), (dkc, 0, KV_WIDTH, 0), (dkp, 0, KV_WIDTH, 1), (dvc, 0, KV_WIDTH, 0),
            (dvp, 0, KV_WIDTH, 1)] + [(t, 0, LANE, 0) for t in tabs]
    return _rowwise(fn, rows, [], [(QKV_WIDTH, BF16)], [], name, tm_cap=BLOCK)[0]


def _att_scores(q_ref, kp_ref, kc_ref, sink_ref):
    n = pl.program_id(1)
    j = pl.program_id(0)
    rows = Q_GROUP * BLOCK
    qs = q_ref[...].reshape(rows, HEAD_DIM)
    kb = jnp.concatenate([kp_ref[0], kc_ref[0]], axis=0)
    s = lax.dot_general(qs, kb, _DOT_DIMS["nt"], preferred_element_type=F32) * (HEAD_DIM ** -0.5)
    r = lax.broadcasted_iota(jnp.int32, (rows, 2 * BLOCK), 0)
    kj = lax.broadcasted_iota(jnp.int32, (rows, 2 * BLOCK), 1)
    delta = (r % BLOCK) + BLOCK - kj
    ok = (delta >= 0) & (delta < WINDOW) & ((kj >= BLOCK) | (n > 0))
    s = jnp.where(ok, s, NEG_INF)
    rh = lax.broadcasted_iota(jnp.int32, (rows, 1), 0) // BLOCK
    sinks = sink_ref[...]
    lane = lax.broadcasted_iota(jnp.int32, sinks.shape, 1)
    srow = lax.broadcasted_iota(jnp.int32, sinks.shape, 0)
    sink = jnp.zeros((rows, 1), F32)
    for g in range(Q_GROUP):
        val = jnp.sum(jnp.where((lane == g) & (srow == j), sinks, 0.0), keepdims=True)
        sink = jnp.where(rh == g, val, sink)
    m = jnp.maximum(jnp.max(s, axis=-1, keepdims=True), sink)
    p = jnp.exp(s - m)
    psink = jnp.exp(sink - m)
    denom = jnp.sum(p, axis=-1, keepdims=True) + psink
    return qs, kb, p / denom, psink / denom, rh


def _att_specs(length):
    nb = length // BLOCK
    q_spec = pl.BlockSpec((Q_GROUP, BLOCK, HEAD_DIM), lambda j, n: (j, n, 0))
    prev = pl.BlockSpec((1, BLOCK, HEAD_DIM), lambda j, n: (j, jnp.maximum(n - 1, 0), 0))
    cur = pl.BlockSpec((1, BLOCK, HEAD_DIM), lambda j, n: (j, n, 0))
    sink_spec = pl.BlockSpec((N_KV_HEADS, Q_GROUP), lambda j, n: (0, 0))
    return nb, q_spec, prev, cur, sink_spec


def _att_fwd(q, k, v, sinks, name):
    length = q.shape[1]
    nb, q_spec, prev, cur, sink_spec = _att_specs(length)

    def body(q_ref, kp_ref, kc_ref, vp_ref, vc_ref, sink_ref, o_ref):
        _, _, p, _, _ = _att_scores(q_ref, kp_ref, kc_ref, sink_ref)
        vb = jnp.concatenate([vp_ref[0], vc_ref[0]], axis=0)
        o = jnp.dot(p.astype(BF16), vb, preferred_element_type=F32)
        o_ref[...] = o.reshape(Q_GROUP, BLOCK, HEAD_DIM).astype(o_ref.dtype)

    return pl.pallas_call(
        body, name=name, grid=(N_KV_HEADS, nb),
        in_specs=[q_spec, prev, cur, prev, cur, sink_spec], out_specs=q_spec,
        out_shape=jax.ShapeDtypeStruct((N_Q_HEADS, length, HEAD_DIM), BF16),
        compiler_params=_params("arbitrary", "arbitrary"),
    )(q, k, k, v, v, sinks)


def _att_bwd(q, k, v, sinks, do, name):
    length = q.shape[1]
    nb, q_spec, prev, cur, sink_spec = _att_specs(length)
    ds_spec = pl.BlockSpec((1, SUBLANE, LANE), lambda j, n: (j, 0, 0))

    def body(q_ref, kp_ref, kc_ref, vp_ref, vc_ref, sink_ref, do_ref, dq_ref, dkc_ref, dkp_ref, dvc_ref,
             dvp_ref, dsink_ref):
        n = pl.program_id(1)
        qs, kb, p, psink, rh = _att_scores(q_ref, kp_ref, kc_ref, sink_ref)
        vb = jnp.concatenate([vp_ref[0], vc_ref[0]], axis=0)
        dob = do_ref[...].reshape(Q_GROUP * BLOCK, HEAD_DIM).astype(BF16)
        dv = lax.dot_general(p.astype(BF16), dob, _DOT_DIMS["tn"], preferred_element_type=F32)
        dp = lax.dot_general(dob, vb, _DOT_DIMS["nt"], preferred_element_type=F32)
        dsum = jnp.sum(p * dp, axis=-1, keepdims=True)
        ds = (p * (dp - dsum) * (HEAD_DIM ** -0.5)).astype(BF16)
        dq = jnp.dot(ds, kb, preferred_element_type=F32)
        dk = lax.dot_general(ds, qs, _DOT_DIMS["tn"], preferred_element_type=F32)
        dq_ref[...] = dq.reshape(Q_GROUP, BLOCK, HEAD_DIM)
        dkp_ref[0] = dk[:BLOCK]
        dkc_ref[0] = dk[BLOCK:]
        dvp_ref[0] = dv[:BLOCK]
        dvc_ref[0] = dv[BLOCK:]
        dsr = -psink * dsum
        row = lax.broadcasted_iota(jnp.int32, (SUBLANE, LANE), 0)
        upd = jnp.zeros((SUBLANE, LANE), F32)
        for g in range(Q_GROUP):
            val = jnp.sum(jnp.where(rh == g, dsr, 0.0), keepdims=True)
            upd = jnp.where(row == g, val, upd)

        @pl.when(n == 0)
        def _():
            dsink_ref[...] = jnp.zeros_like(dsink_ref)

        dsink_ref[0] += upd

    kv_shape = jax.ShapeDtypeStruct((N_KV_HEADS, length, HEAD_DIM), F32)
    return pl.pallas_call(
        body, name=name, grid=(N_KV_HEADS, nb),
        in_specs=[q_spec, prev, cur, prev, cur, sink_spec, q_spec],
        out_specs=[q_spec, cur, cur, cur, cur, ds_spec],
        out_shape=[jax.ShapeDtypeStruct((N_Q_HEADS, length, HEAD_DIM), F32), kv_shape, kv_shape, kv_shape,
                   kv_shape, jax.ShapeDtypeStruct((N_KV_HEADS, SUBLANE, LANE), F32)],
        compiler_params=_params("arbitrary", "arbitrary"),
    )(q, k, k, v, v, sinks, do)


def _to_heads(t, heads):
    return t.reshape(t.shape[0], heads, HEAD_DIM).transpose(1, 0, 2)


def _from_heads(t):
    return t.transpose(1, 0, 2).reshape(t.shape[1], t.shape[0] * HEAD_DIM)


def _shift_down(z, s):
    t = lax.broadcasted_iota(jnp.int32, z.shape, 0)
    return jnp.where(t >= s, pltpu.roll(z, s, 0), 0.0)


def _shift_up(z, s):
    t = lax.broadcasted_iota(jnp.int32, z.shape, 0)
    return jnp.where(t < z.shape[0] - s, pltpu.roll(z, z.shape[0] - s, 0), 0.0)


def _conv_specs(length):
    col = lambda c0: pl.BlockSpec((length, LANE), lambda j, c0=c0: (0, c0 // LANE + j))
    w_spec = pl.BlockSpec((CONV_K, LANE), lambda j: (0, j))
    o_spec = pl.BlockSpec((length, LANE), lambda j: (0, j))
    return col, w_spec, o_spec


def _conv_fwd(proj, w, name):
    length = proj.shape[0]
    col, w_spec, o_spec = _conv_specs(length)

    def body(cb_ref, cc_ref, cx_ref, w_ref, o_ref):
        z = cc_ref[...] * cx_ref[...]
        s = w_ref[0:1, :] * _shift_down(z, 2) + w_ref[1:2, :] * _shift_down(z, 1) + w_ref[2:3, :] * z
        o_ref[...] = (cb_ref[...] * s).astype(o_ref.dtype)

    return pl.pallas_call(
        body, name=name, grid=(CONV_WIDTH // LANE,),
        in_specs=[col(C_CB), col(C_CC), col(C_CX), w_spec], out_specs=o_spec,
        out_shape=jax.ShapeDtypeStruct((length, CONV_WIDTH), BF16),
        compiler_params=_params("arbitrary"),
    )(proj, proj, proj, w)


def _conv_bwd(proj, w, dy, name):
    length = proj.shape[0]
    col, w_spec, o_spec = _conv_specs(length)
    dw_spec = pl.BlockSpec((1, LANE), lambda j: (0, j))

    def body(cb_ref, cc_ref, cx_ref, w_ref, dy_ref, dcb_ref, dcc_ref, dcx_ref, dw0_ref, dw1_ref, dw2_ref):
        cc, cx, dyv = cc_ref[...], cx_ref[...], dy_ref[...]
        z = cc * cx
        w0, w1, w2 = w_ref[0:1, :], w_ref[1:2, :], w_ref[2:3, :]
        z1, z2 = _shift_down(z, 1), _shift_down(z, 2)
        s = w0 * z2 + w1 * z1 + w2 * z
        dcb_ref[...] = (dyv * s).astype(dcb_ref.dtype)
        ds = dyv * cb_ref[...]
        dw0_ref[...] = jnp.sum(ds * z2, axis=0, keepdims=True)
        dw1_ref[...] = jnp.sum(ds * z1, axis=0, keepdims=True)
        dw2_ref[...] = jnp.sum(ds * z, axis=0, keepdims=True)
        dz = w2 * ds + w1 * _shift_up(ds, 1) + w0 * _shift_up(ds, 2)
        dcc_ref[...] = (dz * cx).astype(dcc_ref.dtype)
        dcx_ref[...] = (dz * cc).astype(dcx_ref.dtype)

    act = jax.ShapeDtypeStruct((length, CONV_WIDTH), BF16)
    dws = jax.ShapeDtypeStruct((1, CONV_WIDTH), F32)
    return pl.pallas_call(
        body, name=name, grid=(CONV_WIDTH // LANE,),
        in_specs=[col(C_CB), col(C_CC), col(C_CX), w_spec, o_spec],
        out_specs=[o_spec, o_spec, o_spec, dw_spec, dw_spec, dw_spec],
        out_shape=[act, act, act, dws, dws, dws],
        compiler_params=_params("arbitrary"),
    )(proj, proj, proj, w, dy)


def _cmul(ar, ai, br, bi):
    return ar * br - ai * bi, ar * bi + ai * br


def _scan_tables(lr, li, reverse):
    pr, pi = [lr], [li]
    for _ in range(SCAN_ROWS - 1):
        nr, ni = _cmul(pr[-1], pi[-1], lr, li)
        pr.append(nr)
        pi.append(ni)
    row = jnp.arange(SCAN_ROWS)[:, None]
    mr, mi = [], []
    for s in (1, 2, 4):
        live = (row + s < SCAN_ROWS) if reverse else (row >= s)
        mr.append(jnp.where(live, pr[s - 1], 0.0))
        mi.append(jnp.where(live, pi[s - 1], 0.0))
    order = range(SCAN_ROWS - 1, -1, -1) if reverse else range(SCAN_ROWS)
    carry_r = jnp.concatenate([pr[d] for d in order], axis=0)
    carry_i = jnp.concatenate([pi[d] for d in order], axis=0)
    return jnp.stack(mr), jnp.stack(mi), carry_r, carry_i


N_SSM_CHUNKS = 4
CHUNK_STATES = SSM_LANES // N_SSM_CHUNKS
CHUNK_CHANNELS = SSM_WIDTH // N_SSM_CHUNKS


def _scan(b, lr, li, reverse, name, states=None):
    length = b.shape[0]
    mr, mi, cr, ci = _scan_tables(lr, li, reverse)
    nchunk = length // SCAN_CHUNK
    nblk = SCAN_CHUNK // SCAN_ROWS
    cw = CHUNK_STATES
    with_dlam = states is not None

    def body(*refs):
        if with_dlam:
            b_ref, mr_ref, mi_ref, cr_ref, ci_ref, s_ref, sp_ref, o_ref, dl_ref, carry = refs
        else:
            b_ref, mr_ref, mi_ref, cr_ref, ci_ref, o_ref, carry = refs
        step = pl.program_id(0)

        @pl.when(step == 0)
        def _():
            carry[...] = jnp.zeros_like(carry)
            if with_dlam:
                dl_ref[...] = jnp.zeros_like(dl_ref)

        blocks = range(nblk - 1, -1, -1) if reverse else range(nblk)
        for j in range(N_SSM_CHUNKS):
            re, im = slice(2 * cw * j, 2 * cw * j + cw), slice(2 * cw * j + cw, 2 * cw * (j + 1))
            tl = slice(cw * j, cw * (j + 1))
            c_r, c_i = carry[0:1, re], carry[0:1, im]
            acc_r = acc_i = jnp.zeros((SCAN_ROWS, cw), F32)
            for blk in blocks:
                r0 = blk * SCAN_ROWS
                xr = b_ref[r0:r0 + SCAN_ROWS, re]
                xi = b_ref[r0:r0 + SCAN_ROWS, im]
                for kk, s in enumerate((1, 2, 4)):
                    sh = SCAN_ROWS - s if reverse else s
                    rr, ri = pltpu.roll(xr, sh, 0), pltpu.roll(xi, sh, 0)
                    ar, ai = _cmul(mr_ref[kk, :, tl], mi_ref[kk, :, tl], rr, ri)
                    xr, xi = xr + ar, xi + ai
                ar, ai = _cmul(cr_ref[:, tl], ci_ref[:, tl], c_r, c_i)
                xr, xi = xr + ar, xi + ai
                o_ref[r0:r0 + SCAN_ROWS, re] = xr
                o_ref[r0:r0 + SCAN_ROWS, im] = xi
                edge = r0 if reverse else r0 + SCAN_ROWS - 1
                c_r = o_ref[edge:edge + 1, re]
                c_i = o_ref[edge:edge + 1, im]
                if with_dlam:
                    if r0 > 0:
                        pr, pi = s_ref[r0 - 1:r0 + SCAN_ROWS - 1, re], s_ref[r0 - 1:r0 + SCAN_ROWS - 1, im]
                    else:
                        live = (step < nchunk - 1).astype(F32)
                        row = lax.broadcasted_iota(jnp.int32, (SCAN_ROWS, cw), 0)
                        pr = jnp.where(row == 0, sp_ref[SCAN_ROWS - 1:SCAN_ROWS, re] * live,
                                       pltpu.roll(s_ref[0:SCAN_ROWS, re], 1, 0))
                        pi = jnp.where(row == 0, sp_ref[SCAN_ROWS - 1:SCAN_ROWS, im] * live,
                                       pltpu.roll(s_ref[0:SCAN_ROWS, im], 1, 0))
                    acc_r = acc_r + xr * pr + xi * pi
                    acc_i = acc_i + xi * pr - xr * pi
            carry[0:1, re] = c_r
            carry[0:1, im] = c_i
            if with_dlam:
                dl_ref[:, re] += acc_r
                dl_ref[:, im] += acc_i

    width = 2 * SSM_LANES
    chunk = (lambda i: (nchunk - 1 - i, 0)) if reverse else (lambda i: (i, 0))
    blk_spec = pl.BlockSpec((SCAN_CHUNK, width), chunk)
    m_spec = pl.BlockSpec((3, SCAN_ROWS, SSM_LANES), lambda i: (0, 0, 0))
    c_spec = pl.BlockSpec((SCAN_ROWS, SSM_LANES), lambda i: (0, 0))
    in_specs, args = [blk_spec, m_spec, m_spec, c_spec, c_spec], [b, mr, mi, cr, ci]
    out_specs, out_shape = blk_spec, jax.ShapeDtypeStruct(b.shape, F32)
    if with_dlam:
        assert reverse
        per = SCAN_CHUNK // SCAN_ROWS
        before = pl.BlockSpec((SCAN_ROWS, width), lambda i: (jnp.maximum((nchunk - 1 - i) * per - 1, 0), 0))
        in_specs += [blk_spec, before]
        args += [states, states]
        out_specs = [blk_spec, pl.BlockSpec((SCAN_ROWS, width), lambda i: (0, 0))]
        out_shape = [out_shape, jax.ShapeDtypeStruct((SCAN_ROWS, width), F32)]
    return pl.pallas_call(
        body, name=name, grid=(nchunk,), in_specs=in_specs, out_specs=out_specs, out_shape=out_shape,
        scratch_shapes=[pltpu.VMEM((SUBLANE, width), F32)],
        compiler_params=_params("arbitrary"),
    )(*args)


def _mm_bd(a, b, mode, name, out_dtype=F32, add=None):
    nc = N_SSM_CHUNKS
    if mode == "tn":
        k, wa, wb = a.shape[0], a.shape[1] // nc, b.shape[1] // nc
        tk = _tile(k, 1024, LANE)

        def body(a_ref, b_ref, o_ref):
            @pl.when(pl.program_id(1) == 0)
            def _():
                o_ref[...] = jnp.zeros_like(o_ref)

            o_ref[...] += lax.dot_general(a_ref[...].astype(BF16), b_ref[...].astype(BF16), _DOT_DIMS["tn"],
                                          preferred_element_type=F32)

        return pl.pallas_call(
            body, name=name, grid=(nc, k // tk),
            in_specs=[pl.BlockSpec((tk, wa), lambda j, kk: (kk, j)), pl.BlockSpec((tk, wb), lambda j, kk: (kk, j))],
            out_specs=pl.BlockSpec((None, wa, wb), lambda j, kk: (j, 0, 0)),
            out_shape=jax.ShapeDtypeStruct((nc, wa, wb), F32),
            compiler_params=_params("parallel", "arbitrary"),
        )(a, b)
    m, wa = a.shape[0], a.shape[1] // nc
    wo = b.shape[2] if mode == "nn" else b.shape[1]
    tm = _tile(m, 512, LANE)

    def body(a_ref, b_ref, *rest):
        r = lax.dot_general(a_ref[...].astype(BF16), b_ref[...].astype(BF16), _DOT_DIMS[mode],
                            preferred_element_type=F32)
        if add is not None:
            r = r + rest[0][...]
        rest[-1][...] = r.astype(out_dtype)

    o_spec = pl.BlockSpec((tm, wo), lambda i, j: (i, j))
    in_specs = [pl.BlockSpec((tm, wa), lambda i, j: (i, j)), pl.BlockSpec((None,) + b.shape[1:], lambda i, j: (j, 0, 0))]
    args = [a, b]
    if add is not None:
        in_specs.append(o_spec)
        args.append(add)
    return pl.pallas_call(
        body, name=name, grid=(m // tm, nc), in_specs=in_specs, out_specs=o_spec,
        out_shape=jax.ShapeDtypeStruct((m, nc * wo), out_dtype),
        compiler_params=_params("parallel", "parallel"),
    )(*args)


def _block_diag(t):
    g, a, b = t.shape
    per = g // N_SSM_CHUNKS
    eye = jnp.eye(per, dtype=t.dtype)
    t = t.reshape(N_SSM_CHUNKS, per, a, b)
    return (t[:, :, :, None, :] * eye[None, :, None, :, None]).reshape(N_SSM_CHUNKS, per * a, per * b)


def _ssm_prep(a_re, a_im, b_re, b_im, c_re, c_im, log_dt):
    dt = jnp.exp(log_dt)[:, None]
    er = jnp.exp(a_re * dt)
    lr, li = er * jnp.cos(a_im * dt), er * jnp.sin(a_im * dt)
    nr, ni = lr - 1.0, li
    den = a_re * a_re + a_im * a_im
    qr, qi = (nr * a_re + ni * a_im) / den, (ni * a_re - nr * a_im) / den
    bbr = qr[..., None] * b_re - qi[..., None] * b_im
    bbi = qr[..., None] * b_im + qi[..., None] * b_re
    bmat = jnp.concatenate([_block_diag(bbr.transpose(0, 2, 1)), _block_diag(bbi.transpose(0, 2, 1))], axis=2)
    cmat = jnp.concatenate([_block_diag(c_re.transpose(0, 2, 1)), -_block_diag(c_im.transpose(0, 2, 1))], axis=1)
    return lr.reshape(1, SSM_LANES), li.reshape(1, SSM_LANES), bmat, cmat


def _ssm_act(yc, u, d):
    return jax.nn.gelu(yc + d * u)


def _glu(ys, z):
    return ys * jax.nn.sigmoid(z)


def _merge(ya, yc, ys, gl, b):
    gates = jax.nn.sigmoid(gl + b)
    return gates[:, :D_MODEL] * ya + gates[:, D_MODEL:2 * D_MODEL] * yc + gates[:, 2 * D_MODEL:] * ys


def _swiglu(gu):
    return jax.nn.silu(gu[:, :FFN_HIDDEN]) * gu[:, FFN_HIDDEN:]


def _loss_fn(x, g, t):
    e = _rms(x, g) - t
    per_tok = jnp.mean(e * e, axis=-1, keepdims=True)
    return 0.5 * jnp.sum(per_tok, axis=0, keepdims=True)


def _vjp_rowwise(f, n_row, cot_dtype=F32):
    def fn(ctx, *vals):
        prim = vals[:n_row] + vals[n_row + 1:]
        _, vjp = jax.vjp(f, *prim)
        return vjp(vals[n_row].astype(cot_dtype))
    return fn


def _layer_fwd(i, x, w, tabs):
    nm = lambda s: "l%d_%s" % (i, s)
    sv = {"x": x}
    h = _rms_fwd(x, w["norm_mix"][i:i + 1], nm("rms_mix"))
    proj = _mm(h, w["w_in"], "nn", nm("mm_in"), b_layer=i)
    qkv = _split_fwd(proj, tabs, nm("rope"))
    q, k, v = (_to_heads(qkv[:, :C_K], N_Q_HEADS), _to_heads(qkv[:, C_K:C_V], N_KV_HEADS),
               _to_heads(qkv[:, C_V:C_CB], N_KV_HEADS))
    sinks = w["attn_sinks"][i].reshape(N_KV_HEADS, Q_GROUP)
    att = _from_heads(_att_fwd(q, k, v, sinks, nm("att")))
    conv = _conv_fwd(proj, w["conv_w"][i], nm("conv"))
    lr, li, bmat, cmat = w["ssm"][i]
    u = proj[:, C_U:C_G]
    bu = _mm_bd(u, bmat, "nn", nm("mm_bu"))
    states = _scan(bu, lr, li, False, nm("scan"))
    yc = _mm_bd(states, cmat, "nn", nm("mm_c"))
    d = w["ssm_d"][i:i + 1]
    ys = _rowwise(lambda ctx, a, b, c: (_ssm_act(a, b, c),), [(yc, 0, SSM_WIDTH, 0), (u, 0, SSM_WIDTH, 0)], [d],
                  [(SSM_WIDTH, F32)], [], nm("ssm_act"))[0]
    z = _mm(ys, w["w_ssm_glu"], "nn", nm("mm_glu"), b_layer=i)
    sg = _rowwise(lambda ctx, a, b: (_glu(a, b),), [(ys, 0, SSM_WIDTH, 0), (z, 0, SSM_WIDTH, 0)], [],
                  [(SSM_WIDTH, BF16)], [], nm("glu"))[0]
    ya = _mm(att, w["w_attn_o"], "nn", nm("mm_ao"), b_layer=i)
    yv = _mm(conv, w["w_conv_o"], "nn", nm("mm_co"), b_layer=i)
    ym = _mm(sg, w["w_ssm_o"], "nn", nm("mm_so"), b_layer=i)
    bg = w["b_gate"][i:i + 1]
    merged = _rowwise(lambda ctx, a, b, c, gl, bb: (_merge(a, b, c, gl, bb),),
                      [(ya, 0, D_MODEL, 0), (yv, 0, D_MODEL, 0), (ym, 0, D_MODEL, 0), (proj, C_G, GATE_WIDTH, 0)],
                      [bg], [(D_MODEL, BF16)], [], nm("merge"))[0]
    x1 = _mm(merged, w["w_mix_o"], "nn", nm("mm_mix"), add=x, b_layer=i)
    h2 = _rms_fwd(x1, w["norm_ffn"][i:i + 1], nm("rms_ffn"))
    gu = _mm(h2, w["w_ffn_in"], "nn", nm("mm_ffn_in"), b_layer=i)
    act = _rowwise(lambda ctx, a: (_swiglu(a),), [(gu, 0, 2 * FFN_HIDDEN, 0)], [], [(FFN_HIDDEN, BF16)], [],
                   nm("swiglu"))[0]
    x2 = _mm(act, w["w_ffn_out"], "nn", nm("mm_ffn_out"), add=x1, b_layer=i, tk_cap=3072)
    sv.update(h=h, proj=proj, q=q, k=k, v=v, att=att, conv=conv, u=u, states=states, yc=yc, ys=ys, z=z, sg=sg,
              ya=ya, yv=yv, ym=ym, merged=merged, x1=x1, h2=h2, gu=gu, act=act)
    return x2, sv


def _layer_bwd(i, dx2, sv, w, tabs, gb):
    nm = lambda s: "l%d_b_%s" % (i, s)
    g = {}

    def wgrad(n, lhs, rhs, label, **kw):
        if n in gb:
            gb[n] = _mm(lhs, rhs, "tn", nm(label), into=(gb[n], i, BIG_AXIS[n]), **kw)
        else:
            g[n] = _mm(lhs, rhs, "tn", nm(label), **kw)

    dact = _mm(dx2, w["w_ffn_out"], "nt", nm("mm_dact"), b_layer=i)
    wgrad("w_ffn_out", sv["act"], dx2, "mm_gw_ffn_out", tn_cap=512)
    dgu = _rowwise(_vjp_rowwise(_swiglu, 1), [(sv["gu"], 0, 2 * FFN_HIDDEN, 0), (dact, 0, FFN_HIDDEN, 0)], [],
                   [(2 * FFN_HIDDEN, BF16)], [], nm("swiglu"))[0]
    dh2 = _mm(dgu, w["w_ffn_in"], "nt", nm("mm_dh2"), tk_cap=3072, b_layer=i)
    wgrad("w_ffn_in", sv["h2"], dgu, "mm_gw_ffn_in")
    dx1, g["norm_ffn"] = _rms_bwd(sv["x1"], w["norm_ffn"][i:i + 1], dh2, dx2, nm("rms_ffn"))
    dmerged = _mm(dx1, w["w_mix_o"], "nt", nm("mm_dmerged"), b_layer=i)
    wgrad("w_mix_o", sv["merged"], dx1, "mm_gw_mix", tn_cap=512)
    proj = sv["proj"]
    bg = w["b_gate"][i:i + 1]
    dya, dyv, dym, dgl, g["b_gate"] = _rowwise(
        _vjp_rowwise(_merge, 4),
        [(sv["ya"], 0, D_MODEL, 0), (sv["yv"], 0, D_MODEL, 0), (sv["ym"], 0, D_MODEL, 0),
         (proj, C_G, GATE_WIDTH, 0), (dmerged, 0, D_MODEL, 0)], [bg],
        [(D_MODEL, BF16), (D_MODEL, BF16), (D_MODEL, BF16), (GATE_WIDTH, BF16)], [(1, GATE_WIDTH)], nm("merge"))
    dsg = _mm(dym, w["w_ssm_o"], "nt", nm("mm_dsg"), b_layer=i)
    wgrad("w_ssm_o", sv["sg"], dym, "mm_gw_so")
    dys0, dz = _rowwise(_vjp_rowwise(_glu, 2), [(sv["ys"], 0, SSM_WIDTH, 0), (sv["z"], 0, SSM_WIDTH, 0),
                                                 (dsg, 0, SSM_WIDTH, 0)], [],
                        [(SSM_WIDTH, F32), (SSM_WIDTH, BF16)], [], nm("glu"))
    dys = _mm(dz, w["w_ssm_glu"], "nt", nm("mm_dys"), add=dys0, b_layer=i)
    wgrad("w_ssm_glu", sv["ys"], dz, "mm_gw_glu")
    d = w["ssm_d"][i:i + 1]
    dyc, du0, g["ssm_d"] = _rowwise(
        _vjp_rowwise(_ssm_act, 2), [(sv["yc"], 0, SSM_WIDTH, 0), (sv["u"], 0, SSM_WIDTH, 0), (dys, 0, SSM_WIDTH, 0)],
        [d], [(SSM_WIDTH, F32), (SSM_WIDTH, F32)], [(1, SSM_WIDTH)], nm("ssm_act"))
    lr, li, bmat, cmat = w["ssm"][i]
    dstates = _mm_bd(dyc, cmat, "nt", nm("mm_dstates"))
    g_cmat = _mm_bd(sv["states"], dyc, "tn", nm("mm_gc"))
    gs, dl = _scan(dstates, lr, -li, True, nm("scan"), states=sv["states"])
    g_lam = jnp.sum(dl, axis=0).reshape(N_SSM_CHUNKS, 2, CHUNK_STATES)
    du = _mm_bd(gs, bmat, "nt", nm("mm_du"), out_dtype=BF16, add=du0)
    g_bmat = _mm_bd(sv["u"], gs, "tn", nm("mm_gb"))
    g["ssm"] = (g_lam[:, 0].reshape(1, SSM_LANES), g_lam[:, 1].reshape(1, SSM_LANES), g_bmat, g_cmat)
    dconv = _mm(dyv, w["w_conv_o"], "nt", nm("mm_dconv"), b_layer=i)
    wgrad("w_conv_o", sv["conv"], dyv, "mm_gw_co")
    dcb, dcc, dcx, dw0, dw1, dw2 = _conv_bwd(proj, w["conv_w"][i], dconv, nm("conv"))
    g["conv_w"] = jnp.concatenate([dw0, dw1, dw2], axis=0)
    datt = _mm(dya, w["w_attn_o"], "nt", nm("mm_datt"), b_layer=i)
    wgrad("w_attn_o", sv["att"], dya, "mm_gw_ao")
    sinks = w["attn_sinks"][i].reshape(N_KV_HEADS, Q_GROUP)
    dq, dkc, dkp, dvc, dvp, dsk = _att_bwd(sv["q"], sv["k"], sv["v"], sinks, _to_heads(datt, N_Q_HEADS), nm("att"))
    g["attn_sinks"] = dsk[:, :Q_GROUP, 0].reshape(N_Q_HEADS)
    dqkv = _split_bwd(_from_heads(dq), _from_heads(dkc), _from_heads(dkp), _from_heads(dvc), _from_heads(dvp),
                      tabs, nm("rope"))
    dproj = jnp.concatenate([dqkv, dcb, dcc, dcx, du, dgl], axis=1)
    dh = _mm(dproj, w["w_in"], "nt", nm("mm_dh"), tk_cap=3072, b_layer=i)
    wgrad("w_in", sv["h"], dproj, "mm_gw_in")
    dx, g["norm_mix"] = _rms_bwd(sv["x"], w["norm_mix"][i:i + 1], dh, dx1, nm("rms_mix"))
    return dx, g


def _local_step(x, target, w):
    length = x.shape[0]
    tabs = _rope_tables(length)
    ssm_names = ("ssm_a_re", "ssm_a_im", "ssm_b_re", "ssm_b_im", "ssm_c_re", "ssm_c_im", "ssm_log_dt")
    w = dict(w)
    preps = [jax.vjp(_ssm_prep, *[w[n][i] for n in ssm_names]) for i in range(DEPTH)]
    w["ssm"] = [p[0] for p in preps]
    saved = []
    for i in range(DEPTH):
        x, sv = _layer_fwd(i, x, w, tabs)
        saved.append(sv)

    def loss_fn(ctx, xv, tv, gv):
        val, vjp = jax.vjp(_loss_fn, xv, gv, tv)
        dx, dg, _ = vjp(jnp.ones((1, 1), F32))
        return dx, dg, val + jnp.zeros((1, LANE), F32)

    gfin = w["norm_final"].reshape(1, D_MODEL)
    dx, g_final, loss = _rowwise(loss_fn, [(x, 0, D_MODEL, 0), (target, 0, D_MODEL, 0)], [gfin],
                                 [(D_MODEL, F32)], [(1, D_MODEL), (1, LANE)], "loss")
    gb = {}
    for n in BIG:
        depth, ra, cb = w[n].shape
        if n != "w_in":
            gb[n] = lax.empty((N_CHIPS, depth) + ((ra // N_CHIPS, cb) if BIG_AXIS[n] == 1 else (ra, cb // N_CHIPS)), F32)
    layer_grads = [None] * DEPTH
    for i in reversed(range(DEPTH)):
        dx, layer_grads[i] = _layer_bwd(i, dx, saved[i], w, tabs, gb)
    grads = dict(gb)
    for n in layer_grads[0]:
        if n != "ssm":
            grads[n] = jnp.stack([lg[n] for lg in layer_grads])
    ssm_g = [preps[i][1](layer_grads[i]["ssm"]) for i in range(DEPTH)]
    for j, n in enumerate(ssm_names):
        grads[n] = jnp.stack([sg[j] for sg in ssm_g])
    grads["norm_final"] = g_final.reshape(D_MODEL)
    for n in ("norm_mix", "norm_ffn", "b_gate", "ssm_d"):
        grads[n] = grads[n].reshape(grads[n].shape[0], -1)
    return loss, dx, grads


COLS = 1024
ANY_SPEC = pl.BlockSpec(memory_space=pl.ANY)


def _place():
    return lax.axis_index("x"), lax.axis_index("y"), lax.axis_index("c")


def _other_chips(x, y):
    return [(1 - x, y), (x, 1 - y), (1 - x, 1 - y)]


def _remote(src, dst, send_sems, recv_sems, k, to):
    return pltpu.make_async_remote_copy(src_ref=src, dst_ref=dst, send_sem=send_sems.at[k], recv_sem=recv_sems.at[k],
                                        device_id=to, device_id_type=MESH_ID)


def _comm_call(body, name, out_shape, n_sems, n_local, args):
    return pl.pallas_call(
        body, name=name, out_shape=out_shape, in_specs=[ANY_SPEC] * len(args),
        out_specs=[ANY_SPEC] * len(out_shape),
        scratch_shapes=[pltpu.SemaphoreType.DMA((n_sems,)), pltpu.SemaphoreType.DMA((n_sems,)),
                        pltpu.SemaphoreType.DMA((n_local,))],
    )(*args)


def _gather_weights(shards, name):
    nt = len(shards)
    per = 8
    hds = [s.shape[0] // 2 for s in shards]

    def body(*refs):
        srcs, outs = refs[:nt], refs[nt:2 * nt]
        send_sems, recv_sems, _ = refs[2 * nt:]
        x, y, c = _place()
        me, sibling = (x, y, c), (x, y, 1 - c)
        xn, yn, dg = _other_chips(x, y)

        def blk(t, chip, hc, part=None):
            lo, n = hc * hds[t], hds[t]
            if part is not None:
                first_n = (n + 1) // 2
                lo, n = (lo, first_n) if part == 0 else (lo + first_n, n - first_n)
            return outs[t].at[2 * chip[0] + chip[1], pl.ds(lo, n)] if n else None

        def copy(t, k, src, dst, to):
            return _remote(src, dst, send_sems, recv_sems, per * t + k, to)

        def arrived(t, k, ref):
            copy(t, k, ref, ref, me).wait_recv()

        sent = []

        def start(t, k, ref, to):
            if ref is not None:
                sent.append(copy(t, k, ref, ref, to))
                sent[-1].start()

        for t in range(nt):
            own = srcs[t].at[pl.ds(c * hds[t], hds[t])]
            for k, chip in enumerate((xn, yn)):
                sent.append(copy(t, k, own, blk(t, (x, y), c), (*chip, c)))
                sent[-1].start()
        for t in range(nt):
            arrived(t, 0, blk(t, xn, c))
            start(t, 3, blk(t, xn, c, 1), (*yn, c))
            start(t, 4, blk(t, xn, c), sibling)
            arrived(t, 1, blk(t, yn, c))
            start(t, 2, blk(t, yn, c, 0), (*xn, c))
            start(t, 5, blk(t, yn, c), sibling)
        for t in range(nt):
            arrived(t, 2, blk(t, dg, c, 0))
            start(t, 6, blk(t, dg, c, 0), sibling)
            if blk(t, dg, c, 1) is not None:
                arrived(t, 3, blk(t, dg, c, 1))
                start(t, 7, blk(t, dg, c, 1), sibling)
        for t in range(nt):
            arrived(t, 4, blk(t, xn, 1 - c))
            arrived(t, 5, blk(t, yn, 1 - c))
            arrived(t, 6, blk(t, dg, 1 - c, 0))
            if blk(t, dg, 1 - c, 1) is not None:
                arrived(t, 7, blk(t, dg, 1 - c, 1))
        for cp in sent:
            cp.wait_send()

    out_shape = [jax.ShapeDtypeStruct((N_CHIPS,) + s.shape, s.dtype) for s in shards]
    return _comm_call(body, name, out_shape, per * nt, 1, shards)


def _swap_halves(gs, name):
    nt = len(gs)
    hds = [g.shape[1] // 2 for g in gs]

    def body(*refs):
        srcs, outs = refs[:nt], refs[nt:2 * nt]
        send_sems, recv_sems, _ = refs[2 * nt:]
        x, y, c = _place()
        cps = [_remote(srcs[t].at[s, pl.ds((1 - c) * hds[t], hds[t])], outs[t].at[s], send_sems, recv_sems,
                       N_CHIPS * t + s, (x, y, 1 - c)) for t in range(nt) for s in range(N_CHIPS)]
        for cp in cps:
            cp.start()
        for cp in cps:
            cp.wait()

    out_shape = [jax.ShapeDtypeStruct((N_CHIPS, g.shape[1] // 2) + g.shape[2:], g.dtype) for g in gs]
    return _comm_call(body, name, out_shape, N_CHIPS * nt, 1, gs)


def _exchange_shards(parts, name):
    nt = len(parts)

    def body(*refs):
        srcs, outs = refs[:nt], refs[nt:2 * nt]
        send_sems, recv_sems, _ = refs[2 * nt:]
        x, y, c = _place()
        cps = [_remote(srcs[t].at[2 * chip[0] + chip[1]], outs[t].at[j], send_sems, recv_sems, 3 * t + j, (*chip, c))
               for t in range(nt) for j, chip in enumerate(_other_chips(x, y))]
        for cp in cps:
            cp.start()
        for cp in cps:
            cp.wait()

    out_shape = [jax.ShapeDtypeStruct((N_CHIPS - 1,) + p.shape[1:], p.dtype) for p in parts]
    return _comm_call(body, name, out_shape, 3 * nt, 1, parts)


def _join_halves(reds, name):
    nt = len(reds)
    hds = [r.shape[1] for r in reds]

    def body(*refs):
        srcs, outs = refs[:nt], refs[nt:2 * nt]
        send_sems, recv_sems, _ = refs[2 * nt:]
        x, y, c = _place()
        cps = [_remote(srcs[t].at[0], outs[t].at[pl.ds(c * hds[t], hds[t])], send_sems, recv_sems, t, (x, y, 1 - c))
               for t in range(nt)]
        for cp in cps:
            cp.start()
        for t in range(nt):
            _remote(srcs[t].at[0], outs[t].at[pl.ds((1 - c) * hds[t], hds[t])], send_sems, recv_sems, t,
                    (x, y, c)).wait_recv()
        for cp in cps:
            cp.wait_send()

    out_shape = [jax.ShapeDtypeStruct((2 * r.shape[1],) + r.shape[2:], r.dtype) for r in reds]
    return _comm_call(body, name, out_shape, nt, 1, reds)


SUM_BLOCK_ELEMS = 256 * 1024


def _sum_windows(parts, lead, name, out_dtypes=(F32,)):
    a, b = parts[0][0].shape[2:]
    ta = _tile(a, max(SUBLANE, SUM_BLOCK_ELEMS // b), 2 * SUBLANE)
    offs = jnp.stack([jnp.stack([jnp.asarray(o, jnp.int32) for o in off]) for _, off in parts])
    n_in = len(parts)

    def body(off_ref, *refs):
        acc = refs[0][...].astype(F32)
        for r in refs[1:n_in]:
            acc = acc + r[...].astype(F32)
        for r in refs[n_in:]:
            r[...] = acc.astype(r.dtype)

    in_specs = [pl.BlockSpec((1, 1, ta, b), lambda p, q, i, off, k=k: (off[k, 0] + p, off[k, 1] + q, i, 0))
                for k in range(n_in)]
    o_spec = pl.BlockSpec((1, 1, ta, b), lambda p, q, i, off: (p, q, i, 0))
    return pl.pallas_call(
        body, name=name, out_shape=[jax.ShapeDtypeStruct(tuple(lead) + (a, b), dt) for dt in out_dtypes],
        grid_spec=pltpu.PrefetchScalarGridSpec(
            num_scalar_prefetch=1, grid=tuple(lead) + (a // ta,), in_specs=in_specs,
            out_specs=[o_spec] * len(out_dtypes)),
        compiler_params=_params("arbitrary", "arbitrary", "arbitrary"),
    )(offs, *[arr for arr, _ in parts])


def _reduce_scatter(gs, names, wire):
    x, y, c = _place()
    theirs = _swap_halves(gs, "rs_swap_halves")
    pairs = [_sum_windows([(g, (0, c * (g.shape[1] // 2))), (t, (0, 0))], (N_CHIPS, g.shape[1] // 2),
                          "rs_sum_pair_" + n, (F32, wd)) for g, t, n, wd in zip(gs, theirs, names, wire)]
    others = _exchange_shards([p[1] for p in pairs], "rs_exchange")
    reds = [_sum_windows([(p[0], (2 * x + y, 0))] + [(o, (j, 0)) for j in range(N_CHIPS - 1)], (1, p[0].shape[1]),
                         "rs_sum_chips_" + n)[0] for p, o, n in zip(pairs, others, names)]
    joined = _join_halves(reds, "rs_join")
    return [lax.dynamic_update_slice_in_dim(j, r[0], c * r.shape[1], axis=0) for j, r in zip(joined, reds)]


def _adamw(wt, g, m, v, name):
    cols = wt.shape[-1]
    r = wt.size // cols
    tr = _tile(r, max(SUBLANE, 2 * SUM_BLOCK_ELEMS // max(cols, LANE)), SUBLANE)

    def body(w_ref, g_ref, m_ref, v_ref, d_ref, nm_ref, nv_ref):
        gv = g_ref[...]
        mn = ADAM_B1 * m_ref[...] + (1.0 - ADAM_B1) * gv
        vn = ADAM_B2 * v_ref[...] + (1.0 - ADAM_B2) * jnp.square(gv)
        m_hat = mn / (1.0 - ADAM_B1 ** ADAM_STEP)
        v_hat = vn / (1.0 - ADAM_B2 ** ADAM_STEP)
        d_ref[...] = -ADAM_LR * (m_hat / (jnp.sqrt(v_hat) + ADAM_EPS) + ADAM_WD * w_ref[...])
        nm_ref[...] = mn
        nv_ref[...] = vn

    spec = pl.BlockSpec((tr, cols), lambda i: (i, 0))
    shp = jax.ShapeDtypeStruct((r, cols), F32)
    res = pl.pallas_call(
        body, name=name, grid=(r // tr,), in_specs=[spec] * 4, out_specs=[spec] * 3, out_shape=[shp] * 3,
        compiler_params=_params("parallel"),
    )(*[t.reshape(r, cols) for t in (wt, g, m, v)])
    return [t.reshape(wt.shape) for t in res]


def _join_shards(n, piece):
    _, depth, a, b = piece.shape
    if BIG_AXIS[n] == 2:
        return piece.transpose(1, 2, 0, 3).reshape(depth, a, N_CHIPS * b)
    return piece.transpose(1, 0, 2, 3).reshape(depth, N_CHIPS * a, b)


def _split_shards(n, g):
    depth, a, b = g.shape
    if BIG_AXIS[n] == 2:
        return g.reshape(depth, a, N_CHIPS, b // N_CHIPS).transpose(2, 0, 1, 3)
    return g.reshape(depth, N_CHIPS, a // N_CHIPS, b).transpose(1, 0, 2, 3)


SMALL_PART_ROWS = 2 * SUBLANE


def _rows_of(t):
    return -(-t.size // COLS)


def _pack_small(ts):
    rows = [jnp.pad(t.reshape(-1), (0, _rows_of(t) * COLS - t.size)).reshape(-1, COLS) for t in ts]
    total = sum(r.shape[0] for r in rows)
    part = -(-total // (N_DEV * SMALL_PART_ROWS)) * SMALL_PART_ROWS
    rows.append(jnp.zeros((N_DEV * part - total, COLS), F32))
    return jnp.concatenate(rows, axis=0).reshape(N_CHIPS, 2, part, COLS)


def _unpack_small(buf, like):
    buf, out, r0 = buf.reshape(-1, COLS), [], 0
    for t in like:
        out.append(buf[r0:r0 + _rows_of(t)].reshape(-1)[:t.size].reshape(t.shape))
        r0 += _rows_of(t)
    return out


def kernel(x, norm_mix, w_in, b_gate, attn_sinks, w_attn_o, conv_w, w_conv_o, ssm_a_re, ssm_a_im, ssm_b_re, ssm_b_im, ssm_c_re, ssm_c_im, ssm_d, ssm_log_dt, w_ssm_glu, w_ssm_o, w_mix_o, norm_ffn, w_ffn_in, w_ffn_out, norm_final, loss_target, m_norm_mix, m_w_in, m_b_gate, m_attn_sinks, m_w_attn_o, m_conv_w, m_w_conv_o, m_ssm_a_re, m_ssm_a_im, m_ssm_b_re, m_ssm_b_im, m_ssm_c_re, m_ssm_c_im, m_ssm_d, m_ssm_log_dt, m_w_ssm_glu, m_w_ssm_o, m_w_mix_o, m_norm_ffn, m_w_ffn_in, m_w_ffn_out, m_norm_final, v_norm_mix, v_w_in, v_b_gate, v_attn_sinks, v_w_attn_o, v_conv_w, v_w_conv_o, v_ssm_a_re, v_ssm_a_im, v_ssm_b_re, v_ssm_b_im, v_ssm_c_re, v_ssm_c_im, v_ssm_d, v_ssm_log_dt, v_w_ssm_glu, v_w_ssm_o, v_w_mix_o, v_norm_ffn, v_w_ffn_in, v_w_ffn_out, v_norm_final):
    a = dict(zip(ARG_NAMES, (
        x, norm_mix, w_in, b_gate, attn_sinks, w_attn_o, conv_w, w_conv_o, ssm_a_re, ssm_a_im, ssm_b_re, ssm_b_im,
        ssm_c_re, ssm_c_im, ssm_d, ssm_log_dt, w_ssm_glu, w_ssm_o, w_mix_o, norm_ffn, w_ffn_in, w_ffn_out, norm_final,
        loss_target, m_norm_mix, m_w_in, m_b_gate, m_attn_sinks, m_w_attn_o, m_conv_w, m_w_conv_o, m_ssm_a_re,
        m_ssm_a_im, m_ssm_b_re, m_ssm_b_im, m_ssm_c_re, m_ssm_c_im, m_ssm_d, m_ssm_log_dt, m_w_ssm_glu, m_w_ssm_o,
        m_w_mix_o, m_norm_ffn, m_w_ffn_in, m_w_ffn_out, m_norm_final, v_norm_mix, v_w_in, v_b_gate, v_attn_sinks,
        v_w_attn_o, v_conv_w, v_w_conv_o, v_ssm_a_re, v_ssm_a_im, v_ssm_b_re, v_ssm_b_im, v_ssm_c_re, v_ssm_c_im,
        v_ssm_d, v_ssm_log_dt, v_w_ssm_glu, v_w_ssm_o, v_w_mix_o, v_norm_ffn, v_w_ffn_in, v_w_ffn_out, v_norm_final)))
    px, py, _ = _place()
    chip = 2 * px + py

    gathered = BIG + ("conv_w",)
    own = [a[n].astype(BF16) for n in BIG] + [a["conv_w"]]
    gath = _gather_weights(own, "gather_weights")
    w = {n: _join_shards(n, lax.dynamic_update_slice_in_dim(p, o[None], chip, axis=0))
         for n, p, o in zip(gathered, gath, own)}
    for n in SMALL:
        w[n] = a[n]

    loss, dx, grads = _local_step(a["x"][0], a["loss_target"][0], w)

    small_names = SMALL + ("conv_w",)
    small = _pack_small([grads[n] for n in small_names])
    rs = _reduce_scatter([_split_shards(n, grads[n]) if n == "w_in" else grads[n] for n in BIG] + [small],
                         BIG + ("small",),
                         (BF16,) * len(BIG) + (F32,))
    red = dict(zip(BIG, rs))
    small_all = _gather_weights([rs[-1]], "gather_small")[0]
    small_all = lax.dynamic_update_slice_in_dim(small_all, rs[-1][None], chip, axis=0)
    red.update(zip(small_names, _unpack_small(small_all, [grads[n] for n in small_names])))
    lane = a["conv_w"].shape[2]
    red["conv_w"] = lax.dynamic_slice_in_dim(red["conv_w"], chip * lane, lane, axis=2)

    loss_all = lax.psum(loss[0, 0], ("x", "y", "c"))
    deltas, new_m, new_v = [], [], []
    for n in WEIGHTS:
        d, mn, vn = _adamw(a[n], red[n], a["m_" + n], a["v_" + n], "adamw_" + n)
        deltas.append(d)
        new_m.append(mn)
        new_v.append(vn)
    return (loss_all, dx[None], *[red[n] for n in WEIGHTS], *deltas, *new_m, *new_v)
```

```python
import math

import jax
import jax.numpy as jnp
from jax import lax
from jax.experimental import pallas as pl
from jax.experimental.pallas import tpu as pltpu

F32 = jnp.float32
BF16 = jnp.bfloat16

D_MODEL = 1024
DEPTH = 4
N_Q_HEADS = 8
N_KV_HEADS = 2
HEAD_DIM = 64
Q_GROUP = N_Q_HEADS // N_KV_HEADS
WINDOW = 128
BLOCK = 128
ROPE_THETA = 500000.0
ROT_DIM = HEAD_DIM // 4
ATTN_WIDTH = N_Q_HEADS * HEAD_DIM
KV_WIDTH = N_KV_HEADS * HEAD_DIM
NEG_INF = -1e30
CONV_WIDTH = 512
CONV_K = 3
SSM_WIDTH = 512
SSM_GROUP = 16
SSM_GROUPS = 32
SSM_STATE = 64
SSM_LANES = SSM_GROUPS * SSM_STATE
GATE_WIDTH = 3 * D_MODEL
FFN_HIDDEN = 2816
NORM_EPS = 1e-6
IN_COLS = 5888
C_Q, C_K, C_V, C_CB, C_CC, C_CX, C_U, C_G = 0, 512, 640, 768, 1280, 1792, 2304, 2816

ADAM_LR = 0.001
ADAM_B1 = 0.9
ADAM_B2 = 0.999
ADAM_EPS = 1e-08
ADAM_WD = 0.01
ADAM_STEP = 10

N_CHIPS = 4
N_DEV = 8
MESH_ID = pl.DeviceIdType.MESH

VMEM_LIMIT_BYTES = 48 * 1024 * 1024
LANE = 128
SUBLANE = 8
SCAN_ROWS = 8
SCAN_CHUNK = 128

BIG = ("w_in", "w_attn_o", "w_conv_o", "w_ssm_glu", "w_ssm_o", "w_mix_o", "w_ffn_in", "w_ffn_out")
BIG_AXIS = {"w_in": 2, "w_attn_o": 2, "w_conv_o": 2, "w_ssm_glu": 1, "w_ssm_o": 2, "w_mix_o": 1,
            "w_ffn_in": 2, "w_ffn_out": 1, "conv_w": 2}
SMALL = ("norm_mix", "b_gate", "attn_sinks", "ssm_a_re", "ssm_a_im", "ssm_b_re", "ssm_b_im",
         "ssm_c_re", "ssm_c_im", "ssm_d", "ssm_log_dt", "norm_ffn", "norm_final")
WEIGHTS = ("norm_mix", "w_in", "b_gate", "attn_sinks", "w_attn_o", "conv_w", "w_conv_o", "ssm_a_re",
           "ssm_a_im", "ssm_b_re", "ssm_b_im", "ssm_c_re", "ssm_c_im", "ssm_d", "ssm_log_dt",
           "w_ssm_glu", "w_ssm_o", "w_mix_o", "norm_ffn", "w_ffn_in", "w_ffn_out", "norm_final")
ARG_NAMES = ("x",) + WEIGHTS + ("loss_target",) + tuple("m_" + n for n in WEIGHTS) + tuple(
    "v_" + n for n in WEIGHTS)


def _params(*sem):
    return pltpu.CompilerParams(dimension_semantics=sem if sem else None,
                                vmem_limit_bytes=VMEM_LIMIT_BYTES)


def _tile(dim, cap, align):
    t = min(cap, dim) // align * align
    while t >= align:
        if dim % t == 0:
            return t
        t -= align
    return dim


_DOT_DIMS = {"nn": (((1,), (0,)), ((), ())), "nt": (((1,), (1,)), ((), ())), "tn": (((0,), (0,)), ((), ()))}


def _mm(a, b, mode, name, out_dtype=F32, add=None, tm_cap=512, tn_cap=3072, tk_cap=1024, b_layer=None, into=None,
        b_shards=False):
    bshape = b.shape if b_layer is None else b.shape[1:]
    if b_shards:
        assert mode == "tn" and b_layer is None and into is not None and into[2] == 2
        bshape = (b.shape[1], N_CHIPS * b.shape[2])
    if mode == "nn":
        (m, k), (k2, n) = a.shape, bshape
    elif mode == "nt":
        (m, k), (n, k2) = a.shape, bshape
    else:
        (k, m), (k2, n) = a.shape, bshape
    assert k == k2, (name, a.shape, b.shape)
    tm, tn, tk = _tile(m, tm_cap, LANE), _tile(n, tn_cap, LANE), _tile(k, tk_cap, LANE)
    if into is not None:
        buf, layer, axis = into
        _, _, ra, cb = buf.shape
        if axis == 1:
            tm = m
        elif b_shards:
            tn = cb
        else:
            tn = _tile(cb, tn_cap, LANE)
            assert cb % tn == 0 and tn % LANE == 0, (name, cb, tn)
    nk = k // tk
    dims = _DOT_DIMS[mode]

    def body(a_ref, b_ref, *rest):
        rest = list(rest)
        add_ref = rest.pop(0) if add is not None else None
        if into is not None:
            rest.pop(0)
        o_ref, acc = rest
        kk = pl.program_id(2)

        @pl.when(kk == 0)
        def _():
            acc[...] = jnp.zeros_like(acc)

        acc[...] += lax.dot_general(a_ref[...].astype(BF16), b_ref[...].astype(BF16), dims,
                                    preferred_element_type=F32)

        @pl.when(kk == nk - 1)
        def _():
            r = acc[...]
            if add is not None:
                r = r + add_ref[...]
            o_ref[...] = r.astype(o_ref.dtype).reshape(o_ref.shape)

    if mode == "tn":
        a_spec = pl.BlockSpec((tk, tm), lambda i, j, kk: (kk, i))
    else:
        a_spec = pl.BlockSpec((tm, tk), lambda i, j, kk: (i, kk))
    lead = () if b_layer is None else (None,)
    at = (lambda *ix: ix) if b_layer is None else (lambda *ix: (b_layer,) + ix)
    if b_shards:
        b_spec = pl.BlockSpec((None, tk, tn), lambda i, j, kk: (j, kk, 0))
    elif mode == "nt":
        b_spec = pl.BlockSpec(lead + (tn, tk), lambda i, j, kk: at(j, kk))
    else:
        b_spec = pl.BlockSpec(lead + (tk, tn), lambda i, j, kk: at(kk, j))
    o_spec = pl.BlockSpec((tm, tn), lambda i, j, kk: (i, j))
    in_specs, args = [a_spec, b_spec], [a, b]
    if add is not None:
        in_specs.append(o_spec)
        args.append(add)
    out_shape, aliases = jax.ShapeDtypeStruct((m, n), out_dtype), {}
    if into is not None:
        in_specs.append(pl.BlockSpec(memory_space=pl.ANY))
        aliases = {len(args): 0}
        args.append(buf)
        out_shape = jax.ShapeDtypeStruct(buf.shape, buf.dtype)
        if axis == 1:
            o_spec = pl.BlockSpec((N_CHIPS, None, ra, tn), lambda i, j, kk: (0, layer, 0, j))
        else:
            per = cb // tn
            o_spec = pl.BlockSpec((None, None, tm, tn), lambda i, j, kk: (j // per, layer, i, j % per))
    return pl.pallas_call(
        body, name=name, grid=(m // tm, n // tn, nk), in_specs=in_specs, out_specs=o_spec,
        out_shape=out_shape, input_output_aliases=aliases,
        scratch_shapes=[pltpu.VMEM((tm, tn), F32)],
        compiler_params=_params("parallel", "parallel", "arbitrary"),
    )(*args)


def _rowwise(fn, rows, pars, outs, accs, name, tm_cap=256):
    length = rows[0][0].shape[0]
    tm = _tile(length, tm_cap, LANE)
    n = length // tm
    in_specs, args, counts = [], [], []
    for arr, c0, cw, shift in rows:
        bw = math.gcd(c0, cw) if c0 else cw
        assert bw % LANE == 0 or (c0 == 0 and cw == arr.shape[1]), (name, c0, cw)
        cnt = cw // bw
        counts.append(cnt)
        for j in range(cnt):
            in_specs.append(pl.BlockSpec(
                (tm, bw), lambda i, j=j, c0=c0, bw=bw, shift=shift: (jnp.clip(i + shift, 0, n - 1), c0 // bw + j)))
            args.append(arr)
    for p in pars:
        in_specs.append(pl.BlockSpec(p.shape, lambda i: (0, 0)))
        args.append(p)
    out_shape = [jax.ShapeDtypeStruct((length, w), dt) for w, dt in outs]
    out_specs = [pl.BlockSpec((tm, w), lambda i: (i, 0)) for w, _ in outs]
    out_shape += [jax.ShapeDtypeStruct((r, w), F32) for r, w in accs]
    out_specs += [pl.BlockSpec((r, w), lambda i: (0, 0)) for r, w in accs]
    n_in, n_out = len(args), len(outs)

    def body(*refs):
        i = pl.program_id(0)
        vals, p = [], 0
        for cnt in counts:
            blocks = [refs[p + j][...] for j in range(cnt)]
            p += cnt
            vals.append(blocks[0] if cnt == 1 else jnp.concatenate(blocks, axis=1))
        for _ in pars:
            vals.append(refs[p][...])
            p += 1
        res = fn((i, n), *vals)
        out_refs = refs[n_in:n_in + n_out]
        acc_refs = refs[n_in + n_out:]
        for r, v in zip(out_refs, res[:n_out]):
            r[...] = v.astype(r.dtype)
        if acc_refs:
            @pl.when(i == 0)
            def _():
                for r in acc_refs:
                    r[...] = jnp.zeros_like(r)
            for r, v in zip(acc_refs, res[n_out:]):
                r[...] += v

    res = pl.pallas_call(
        body, name=name, grid=(n,), in_specs=in_specs, out_specs=out_specs, out_shape=out_shape,
        compiler_params=_params("arbitrary"),
    )(*args)
    return res


def _rms(x, g):
    return x * lax.rsqrt(jnp.mean(x * x, axis=-1, keepdims=True) + NORM_EPS) * g


def _rms_fwd(x, g, name):
    return _rowwise(lambda ctx, xv, gv: (_rms(xv, gv),), [(x, 0, D_MODEL, 0)], [g],
                    [(D_MODEL, BF16)], [], name)[0]


def _rms_bwd(x, g, dh, dres, name):
    def fn(ctx, xv, dhv, drv, gv):
        _, vjp = jax.vjp(_rms, xv, gv)
        dx, dg = vjp(dhv)
        return dx + drv, dg
    return _rowwise(fn, [(x, 0, D_MODEL, 0), (dh, 0, D_MODEL, 0), (dres, 0, D_MODEL, 0)], [g],
                    [(D_MODEL, F32)], [(1, D_MODEL)], name)


def _rope_tables(length):
    pos = jnp.arange(length, dtype=F32)
    inv_freq = ROPE_THETA ** (-jnp.arange(0, ROT_DIM, 2, dtype=F32) / ROT_DIM)
    ang = pos[:, None] * inv_freq[None, :]
    cos, sin = jnp.cos(ang), jnp.sin(ang)
    half = ROT_DIM // 2
    ones = jnp.ones((length, HEAD_DIM - ROT_DIM), F32)
    zeros = jnp.zeros_like(ones)
    zh = jnp.zeros((length, half), F32)
    c64 = jnp.concatenate([cos, cos, ones], axis=1)
    s1 = jnp.concatenate([-sin, zh, zeros], axis=1)
    s2 = jnp.concatenate([zh, sin, zeros], axis=1)
    tile2 = lambda t: jnp.concatenate([t, t], axis=1)
    return tile2(c64), tile2(s1), tile2(s2)


def _lane_chunks(t):
    return [t[:, j * LANE:(j + 1) * LANE] for j in range(t.shape[1] // LANE)]


def _rope(t, c, s1, s2, n_rot):
    half = ROT_DIM // 2
    out = []
    for j, ch in enumerate(_lane_chunks(t)):
        if j < n_rot:
            ch = ch * c + pltpu.roll(ch, LANE - half, 1) * s1 + pltpu.roll(ch, half, 1) * s2
        out.append(ch)
    return jnp.concatenate(out, axis=1)


def _unrope(d, c, s1, s2, n_rot):
    half = ROT_DIM // 2
    out = []
    for j, ch in enumerate(_lane_chunks(d)):
        if j < n_rot:
            ch = ch * c + pltpu.roll(ch * s1, half, 1) + pltpu.roll(ch * s2, LANE - half, 1)
        out.append(ch)
    return jnp.concatenate(out, axis=1)


N_ROT_CHUNKS = (ATTN_WIDTH + KV_WIDTH) // LANE
QKV_WIDTH = ATTN_WIDTH + 2 * KV_WIDTH


def _split_fwd(proj, tabs, name):
    def fn(ctx, t, c, s1, s2):
        return (_rope(t, c, s1, s2, N_ROT_CHUNKS),)
    rows = [(proj, 0, QKV_WIDTH, 0)] + [(t, 0, LANE, 0) for t in tabs]
    return _rowwise(fn, rows, [], [(QKV_WIDTH, BF16)], [], name, tm_cap=BLOCK)[0]


def _split_bwd(dq, dkc, dkp, dvc, dvp, tabs, name):
    def fn(ctx, dqv, dkcv, dkpv, dvcv, dvpv, c, s1, s2):
        i, n = ctx
        keep = (i < n - 1).astype(F32)
        d = jnp.concatenate([dqv, dkcv + keep * dkpv, dvcv + keep * dvpv], axis=1)
        return (_unrope(d, c, s1, s2, N_ROT_CHUNKS),)
    rows = [(dq, 0, ATTN_WIDTH, 0), (dkc, 0, KV_WIDTH, 0), (dkp, 0, KV_WIDTH, 1), (dvc, 0, KV_WIDTH, 0),
            (dvp, 0, KV_WIDTH, 1)] + [(t, 0, LANE, 0) for t in tabs]
    return _rowwise(fn, rows, [], [(QKV_WIDTH, BF16)], [], name, tm_cap=BLOCK)[0]


def _att_scores(q_ref, kp_ref, kc_ref, sink_ref):
    n = pl.program_id(1)
    j = pl.program_id(0)
    rows = Q_GROUP * BLOCK
    qs = q_ref[...].reshape(rows, HEAD_DIM)
    kb = jnp.concatenate([kp_ref[0], kc_ref[0]], axis=0)
    s = lax.dot_general(qs, kb, _DOT_DIMS["nt"], preferred_element_type=F32) * (HEAD_DIM ** -0.5)
    r = lax.broadcasted_iota(jnp.int32, (rows, 2 * BLOCK), 0)
    kj = lax.broadcasted_iota(jnp.int32, (rows, 2 * BLOCK), 1)
    delta = (r % BLOCK) + BLOCK - kj
    ok = (delta >= 0) & (delta < WINDOW) & ((kj >= BLOCK) | (n > 0))
    s = jnp.where(ok, s, NEG_INF)
    rh = lax.broadcasted_iota(jnp.int32, (rows, 1), 0) // BLOCK
    sinks = sink_ref[...]
    lane = lax.broadcasted_iota(jnp.int32, sinks.shape, 1)
    srow = lax.broadcasted_iota(jnp.int32, sinks.shape, 0)
    sink = jnp.zeros((rows, 1), F32)
    for g in range(Q_GROUP):
        val = jnp.sum(jnp.where((lane == g) & (srow == j), sinks, 0.0), keepdims=True)
        sink = jnp.where(rh == g, val, sink)
    m = jnp.maximum(jnp.max(s, axis=-1, keepdims=True), sink)
    p = jnp.exp(s - m)
    psink = jnp.exp(sink - m)
    denom = jnp.sum(p, axis=-1, keepdims=True) + psink
    return qs, kb, p / denom, psink / denom, rh


def _att_specs(length):
    nb = length // BLOCK
    q_spec = pl.BlockSpec((Q_GROUP, BLOCK, HEAD_DIM), lambda j, n: (j, n, 0))
    prev = pl.BlockSpec((1, BLOCK, HEAD_DIM), lambda j, n: (j, jnp.maximum(n - 1, 0), 0))
    cur = pl.BlockSpec((1, BLOCK, HEAD_DIM), lambda j, n: (j, n, 0))
    sink_spec = pl.BlockSpec((N_KV_HEADS, Q_GROUP), lambda j, n: (0, 0))
    return nb, q_spec, prev, cur, sink_spec


def _att_fwd(q, k, v, sinks, name):
    length = q.shape[1]
    nb, q_spec, prev, cur, sink_spec = _att_specs(length)

    def body(q_ref, kp_ref, kc_ref, vp_ref, vc_ref, sink_ref, o_ref):
        _, _, p, _, _ = _att_scores(q_ref, kp_ref, kc_ref, sink_ref)
        vb = jnp.concatenate([vp_ref[0], vc_ref[0]], axis=0)
        o = jnp.dot(p.astype(BF16), vb, preferred_element_type=F32)
        o_ref[...] = o.reshape(Q_GROUP, BLOCK, HEAD_DIM).astype(o_ref.dtype)

    return pl.pallas_call(
        body, name=name, grid=(N_KV_HEADS, nb),
        in_specs=[q_spec, prev, cur, prev, cur, sink_spec], out_specs=q_spec,
        out_shape=jax.ShapeDtypeStruct((N_Q_HEADS, length, HEAD_DIM), BF16),
        compiler_params=_params("arbitrary", "arbitrary"),
    )(q, k, k, v, v, sinks)


def _att_bwd(q, k, v, sinks, do, name):
    length = q.shape[1]
    nb, q_spec, prev, cur, sink_spec = _att_specs(length)
    ds_spec = pl.BlockSpec((1, SUBLANE, LANE), lambda j, n: (j, 0, 0))

    def body(q_ref, kp_ref, kc_ref, vp_ref, vc_ref, sink_ref, do_ref, dq_ref, dkc_ref, dkp_ref, dvc_ref,
             dvp_ref, dsink_ref):
        n = pl.program_id(1)
        qs, kb, p, psink, rh = _att_scores(q_ref, kp_ref, kc_ref, sink_ref)
        vb = jnp.concatenate([vp_ref[0], vc_ref[0]], axis=0)
        dob = do_ref[...].reshape(Q_GROUP * BLOCK, HEAD_DIM).astype(BF16)
        dv = lax.dot_general(p.astype(BF16), dob, _DOT_DIMS["tn"], preferred_element_type=F32)
        dp = lax.dot_general(dob, vb, _DOT_DIMS["nt"], preferred_element_type=F32)
        dsum = jnp.sum(p * dp, axis=-1, keepdims=True)
        ds = (p * (dp - dsum) * (HEAD_DIM ** -0.5)).astype(BF16)
        dq = jnp.dot(ds, kb, preferred_element_type=F32)
        dk = lax.dot_general(ds, qs, _DOT_DIMS["tn"], preferred_element_type=F32)
        dq_ref[...] = dq.reshape(Q_GROUP, BLOCK, HEAD_DIM)
        dkp_ref[0] = dk[:BLOCK]
        dkc_ref[0] = dk[BLOCK:]
        dvp_ref[0] = dv[:BLOCK]
        dvc_ref[0] = dv[BLOCK:]
        dsr = -psink * dsum
        row = lax.broadcasted_iota(jnp.int32, (SUBLANE, LANE), 0)
        upd = jnp.zeros((SUBLANE, LANE), F32)
        for g in range(Q_GROUP):
            val = jnp.sum(jnp.where(rh == g, dsr, 0.0), keepdims=True)
            upd = jnp.where(row == g, val, upd)

        @pl.when(n == 0)
        def _():
            dsink_ref[...] = jnp.zeros_like(dsink_ref)

        dsink_ref[0] += upd

    kv_shape = jax.ShapeDtypeStruct((N_KV_HEADS, length, HEAD_DIM), F32)
    return pl.pallas_call(
        body, name=name, grid=(N_KV_HEADS, nb),
        in_specs=[q_spec, prev, cur, prev, cur, sink_spec, q_spec],
        out_specs=[q_spec, cur, cur, cur, cur, ds_spec],
        out_shape=[jax.ShapeDtypeStruct((N_Q_HEADS, length, HEAD_DIM), F32), kv_shape, kv_shape, kv_shape,
                   kv_shape, jax.ShapeDtypeStruct((N_KV_HEADS, SUBLANE, LANE), F32)],
        compiler_params=_params("arbitrary", "arbitrary"),
    )(q, k, k, v, v, sinks, do)


def _to_heads(t, heads):
    return t.reshape(t.shape[0], heads, HEAD_DIM).transpose(1, 0, 2)


def _from_heads(t):
    return t.transpose(1, 0, 2).reshape(t.shape[1], t.shape[0] * HEAD_DIM)


def _shift_down(z, s):
    t = lax.broadcasted_iota(jnp.int32, z.shape, 0)
    return jnp.where(t >= s, pltpu.roll(z, s, 0), 0.0)


def _shift_up(z, s):
    t = lax.broadcasted_iota(jnp.int32, z.shape, 0)
    return jnp.where(t < z.shape[0] - s, pltpu.roll(z, z.shape[0] - s, 0), 0.0)


def _conv_specs(length):
    col = lambda c0: pl.BlockSpec((length, LANE), lambda j, c0=c0: (0, c0 // LANE + j))
    w_spec = pl.BlockSpec((CONV_K, LANE), lambda j: (0, j))
    o_spec = pl.BlockSpec((length, LANE), lambda j: (0, j))
    return col, w_spec, o_spec


def _conv_fwd(proj, w, name):
    length = proj.shape[0]
    col, w_spec, o_spec = _conv_specs(length)

    def body(cb_ref, cc_ref, cx_ref, w_ref, o_ref):
        z = cc_ref[...] * cx_ref[...]
        s = w_ref[0:1, :] * _shift_down(z, 2) + w_ref[1:2, :] * _shift_down(z, 1) + w_ref[2:3, :] * z
        o_ref[...] = (cb_ref[...] * s).astype(o_ref.dtype)

    return pl.pallas_call(
        body, name=name, grid=(CONV_WIDTH // LANE,),
        in_specs=[col(C_CB), col(C_CC), col(C_CX), w_spec], out_specs=o_spec,
        out_shape=jax.ShapeDtypeStruct((length, CONV_WIDTH), BF16),
        compiler_params=_params("arbitrary"),
    )(proj, proj, proj, w)


def _conv_bwd(proj, w, dy, name):
    length = proj.shape[0]
    col, w_spec, o_spec = _conv_specs(length)
    dw_spec = pl.BlockSpec((1, LANE), lambda j: (0, j))

    def body(cb_ref, cc_ref, cx_ref, w_ref, dy_ref, dcb_ref, dcc_ref, dcx_ref, dw0_ref, dw1_ref, dw2_ref):
        cc, cx, dyv = cc_ref[...], cx_ref[...], dy_ref[...]
        z = cc * cx
        w0, w1, w2 = w_ref[0:1, :], w_ref[1:2, :], w_ref[2:3, :]
        z1, z2 = _shift_down(z, 1), _shift_down(z, 2)
        s = w0 * z2 + w1 * z1 + w2 * z
        dcb_ref[...] = (dyv * s).astype(dcb_ref.dtype)
        ds = dyv * cb_ref[...]
        dw0_ref[...] = jnp.sum(ds * z2, axis=0, keepdims=True)
        dw1_ref[...] = jnp.sum(ds * z1, axis=0, keepdims=True)
        dw2_ref[...] = jnp.sum(ds * z, axis=0, keepdims=True)
        dz = w2 * ds + w1 * _shift_up(ds, 1) + w0 * _shift_up(ds, 2)
        dcc_ref[...] = (dz * cx).astype(dcc_ref.dtype)
        dcx_ref[...] = (dz * cc).astype(dcx_ref.dtype)

    act = jax.ShapeDtypeStruct((length, CONV_WIDTH), BF16)
    dws = jax.ShapeDtypeStruct((1, CONV_WIDTH), F32)
    return pl.pallas_call(
        body, name=name, grid=(CONV_WIDTH // LANE,),
        in_specs=[col(C_CB), col(C_CC), col(C_CX), w_spec, o_spec],
        out_specs=[o_spec, o_spec, o_spec, dw_spec, dw_spec, dw_spec],
        out_shape=[act, act, act, dws, dws, dws],
        compiler_params=_params("arbitrary"),
    )(proj, proj, proj, w, dy)


def _cmul(ar, ai, br, bi):
    return ar * br - ai * bi, ar * bi + ai * br


def _scan_tables(lr, li, reverse):
    pr, pi = [lr], [li]
    for _ in range(SCAN_ROWS - 1):
        nr, ni = _cmul(pr[-1], pi[-1], lr, li)
        pr.append(nr)
        pi.append(ni)
    row = jnp.arange(SCAN_ROWS)[:, None]
    mr, mi = [], []
    for s in (1, 2, 4):
        live = (row + s < SCAN_ROWS) if reverse else (row >= s)
        mr.append(jnp.where(live, pr[s - 1], 0.0))
        mi.append(jnp.where(live, pi[s - 1], 0.0))
    order = range(SCAN_ROWS - 1, -1, -1) if reverse else range(SCAN_ROWS)
    carry_r = jnp.concatenate([pr[d] for d in order], axis=0)
    carry_i = jnp.concatenate([pi[d] for d in order], axis=0)
    return jnp.stack(mr), jnp.stack(mi), carry_r, carry_i


N_SSM_CHUNKS = 4
CHUNK_STATES = SSM_LANES // N_SSM_CHUNKS
CHUNK_CHANNELS = SSM_WIDTH // N_SSM_CHUNKS


def _scan(b, lr, li, reverse, name, states=None):
    length = b.shape[0]
    mr, mi, cr, ci = _scan_tables(lr, li, reverse)
    nchunk = length // SCAN_CHUNK
    nblk = SCAN_CHUNK // SCAN_ROWS
    cw = CHUNK_STATES
    with_dlam = states is not None

    def body(*refs):
        if with_dlam:
            b_ref, mr_ref, mi_ref, cr_ref, ci_ref, s_ref, sp_ref, o_ref, dl_ref, carry = refs
        else:
            b_ref, mr_ref, mi_ref, cr_ref, ci_ref, o_ref, carry = refs
        step = pl.program_id(0)

        @pl.when(step == 0)
        def _():
            carry[...] = jnp.zeros_like(carry)
            if with_dlam:
                dl_ref[...] = jnp.zeros_like(dl_ref)

        blocks = range(nblk - 1, -1, -1) if reverse else range(nblk)
        for j in range(N_SSM_CHUNKS):
            re, im = slice(2 * cw * j, 2 * cw * j + cw), slice(2 * cw * j + cw, 2 * cw * (j + 1))
            tl = slice(cw * j, cw * (j + 1))
            c_r, c_i = carry[0:1, re], carry[0:1, im]
            acc_r = acc_i = jnp.zeros((SCAN_ROWS, cw), F32)
            for blk in blocks:
                r0 = blk * SCAN_ROWS
                xr = b_ref[r0:r0 + SCAN_ROWS, re]
                xi = b_ref[r0:r0 + SCAN_ROWS, im]
                for kk, s in enumerate((1, 2, 4)):
                    sh = SCAN_ROWS - s if reverse else s
                    rr, ri = pltpu.roll(xr, sh, 0), pltpu.roll(xi, sh, 0)
                    ar, ai = _cmul(mr_ref[kk, :, tl], mi_ref[kk, :, tl], rr, ri)
                    xr, xi = xr + ar, xi + ai
                ar, ai = _cmul(cr_ref[:, tl], ci_ref[:, tl], c_r, c_i)
                xr, xi = xr + ar, xi + ai
                o_ref[r0:r0 + SCAN_ROWS, re] = xr
                o_ref[r0:r0 + SCAN_ROWS, im] = xi
                edge = r0 if reverse else r0 + SCAN_ROWS - 1
                c_r = o_ref[edge:edge + 1, re]
                c_i = o_ref[edge:edge + 1, im]
                if with_dlam:
                    if r0 > 0:
                        pr, pi = s_ref[r0 - 1:r0 + SCAN_ROWS - 1, re], s_ref[r0 - 1:r0 + SCAN_ROWS - 1, im]
                    else:
                        live = (step < nchunk - 1).astype(F32)
                        row = lax.broadcasted_iota(jnp.int32, (SCAN_ROWS, cw), 0)
                        pr = jnp.where(row == 0, sp_ref[SCAN_ROWS - 1:SCAN_ROWS, re] * live,
                                       pltpu.roll(s_ref[0:SCAN_ROWS, re], 1, 0))
                        pi = jnp.where(row == 0, sp_ref[SCAN_ROWS - 1:SCAN_ROWS, im] * live,
                                       pltpu.roll(s_ref[0:SCAN_ROWS, im], 1, 0))
                    acc_r = acc_r + xr * pr + xi * pi
                    acc_i = acc_i + xi * pr - xr * pi
            carry[0:1, re] = c_r
            carry[0:1, im] = c_i
            if with_dlam:
                dl_ref[:, re] += acc_r
                dl_ref[:, im] += acc_i

    width = 2 * SSM_LANES
    chunk = (lambda i: (nchunk - 1 - i, 0)) if reverse else (lambda i: (i, 0))
    blk_spec = pl.BlockSpec((SCAN_CHUNK, width), chunk)
    m_spec = pl.BlockSpec((3, SCAN_ROWS, SSM_LANES), lambda i: (0, 0, 0))
    c_spec = pl.BlockSpec((SCAN_ROWS, SSM_LANES), lambda i: (0, 0))
    in_specs, args = [blk_spec, m_spec, m_spec, c_spec, c_spec], [b, mr, mi, cr, ci]
    out_specs, out_shape = blk_spec, jax.ShapeDtypeStruct(b.shape, F32)
    if with_dlam:
        assert reverse
        per = SCAN_CHUNK // SCAN_ROWS
        before = pl.BlockSpec((SCAN_ROWS, width), lambda i: (jnp.maximum((nchunk - 1 - i) * per - 1, 0), 0))
        in_specs += [blk_spec, before]
        args += [states, states]
        out_specs = [blk_spec, pl.BlockSpec((SCAN_ROWS, width), lambda i: (0, 0))]
        out_shape = [out_shape, jax.ShapeDtypeStruct((SCAN_ROWS, width), F32)]
    return pl.pallas_call(
        body, name=name, grid=(nchunk,), in_specs=in_specs, out_specs=out_specs, out_shape=out_shape,
        scratch_shapes=[pltpu.VMEM((SUBLANE, width), F32)],
        compiler_params=_params("arbitrary"),
    )(*args)


def _mm_bd(a, b, mode, name, out_dtype=F32, add=None):
    nc = N_SSM_CHUNKS
    if mode == "tn":
        k, wa, wb = a.shape[0], a.shape[1] // nc, b.shape[1] // nc
        tk = _tile(k, 1024, LANE)

        def body(a_ref, b_ref, o_ref):
            @pl.when(pl.program_id(1) == 0)
            def _():
                o_ref[...] = jnp.zeros_like(o_ref)

            o_ref[...] += lax.dot_general(a_ref[...].astype(BF16), b_ref[...].astype(BF16), _DOT_DIMS["tn"],
                                          preferred_element_type=F32)

        return pl.pallas_call(
            body, name=name, grid=(nc, k // tk),
            in_specs=[pl.BlockSpec((tk, wa), lambda j, kk: (kk, j)), pl.BlockSpec((tk, wb), lambda j, kk: (kk, j))],
            out_specs=pl.BlockSpec((None, wa, wb), lambda j, kk: (j, 0, 0)),
            out_shape=jax.ShapeDtypeStruct((nc, wa, wb), F32),
            compiler_params=_params("parallel", "arbitrary"),
        )(a, b)
    m, wa = a.shape[0], a.shape[1] // nc
    wo = b.shape[2] if mode == "nn" else b.shape[1]
    tm = _tile(m, 512, LANE)

    def body(a_ref, b_ref, *rest):
        r = lax.dot_general(a_ref[...].astype(BF16), b_ref[...].astype(BF16), _DOT_DIMS[mode],
                            preferred_element_type=F32)
        if add is not None:
            r = r + rest[0][...]
        rest[-1][...] = r.astype(out_dtype)

    o_spec = pl.BlockSpec((tm, wo), lambda i, j: (i, j))
    in_specs = [pl.BlockSpec((tm, wa), lambda i, j: (i, j)), pl.BlockSpec((None,) + b.shape[1:], lambda i, j: (j, 0, 0))]
    args = [a, b]
    if add is not None:
        in_specs.append(o_spec)
        args.append(add)
    return pl.pallas_call(
        body, name=name, grid=(m // tm, nc), in_specs=in_specs, out_specs=o_spec,
        out_shape=jax.ShapeDtypeStruct((m, nc * wo), out_dtype),
        compiler_params=_params("parallel", "parallel"),
    )(*args)


def _block_diag(t):
    g, a, b = t.shape
    per = g // N_SSM_CHUNKS
    eye = jnp.eye(per, dtype=t.dtype)
    t = t.reshape(N_SSM_CHUNKS, per, a, b)
    return (t[:, :, :, None, :] * eye[None, :, None, :, None]).reshape(N_SSM_CHUNKS, per * a, per * b)


def _ssm_prep(a_re, a_im, b_re, b_im, c_re, c_im, log_dt):
    dt = jnp.exp(log_dt)[:, None]
    er = jnp.exp(a_re * dt)
    lr, li = er * jnp.cos(a_im * dt), er * jnp.sin(a_im * dt)
    nr, ni = lr - 1.0, li
    den = a_re * a_re + a_im * a_im
    qr, qi = (nr * a_re + ni * a_im) / den, (ni * a_re - nr * a_im) / den
    bbr = qr[..., None] * b_re - qi[..., None] * b_im
    bbi = qr[..., None] * b_im + qi[..., None] * b_re
    bmat = jnp.concatenate([_block_diag(bbr.transpose(0, 2, 1)), _block_diag(bbi.transpose(0, 2, 1))], axis=2)
    cmat = jnp.concatenate([_block_diag(c_re.transpose(0, 2, 1)), -_block_diag(c_im.transpose(0, 2, 1))], axis=1)
    return lr.reshape(1, SSM_LANES), li.reshape(1, SSM_LANES), bmat, cmat


def _ssm_act(yc, u, d):
    return jax.nn.gelu(yc + d * u)


def _glu(ys, z):
    return ys * jax.nn.sigmoid(z)


def _merge(ya, yc, ys, gl, b):
    gates = jax.nn.sigmoid(gl + b)
    return gates[:, :D_MODEL] * ya + gates[:, D_MODEL:2 * D_MODEL] * yc + gates[:, 2 * D_MODEL:] * ys


def _swiglu(gu):
    return jax.nn.silu(gu[:, :FFN_HIDDEN]) * gu[:, FFN_HIDDEN:]


def _loss_fn(x, g, t):
    e = _rms(x, g) - t
    per_tok = jnp.mean(e * e, axis=-1, keepdims=True)
    return 0.5 * jnp.sum(per_tok, axis=0, keepdims=True)


def _vjp_rowwise(f, n_row, cot_dtype=F32):
    def fn(ctx, *vals):
        prim = vals[:n_row] + vals[n_row + 1:]
        _, vjp = jax.vjp(f, *prim)
        return vjp(vals[n_row].astype(cot_dtype))
    return fn


def _layer_fwd(i, x, w, tabs):
    nm = lambda s: "l%d_%s" % (i, s)
    sv = {"x": x}
    h = _rms_fwd(x, w["norm_mix"][i:i + 1], nm("rms_mix"))
    proj = _mm(h, w["w_in"], "nn", nm("mm_in"), b_layer=i)
    qkv = _split_fwd(proj, tabs, nm("rope"))
    q, k, v = (_to_heads(qkv[:, :C_K], N_Q_HEADS), _to_heads(qkv[:, C_K:C_V], N_KV_HEADS),
               _to_heads(qkv[:, C_V:C_CB], N_KV_HEADS))
    sinks = w["attn_sinks"][i].reshape(N_KV_HEADS, Q_GROUP)
    att = _from_heads(_att_fwd(q, k, v, sinks, nm("att")))
    conv = _conv_fwd(proj, w["conv_w"][i], nm("conv"))
    lr, li, bmat, cmat = w["ssm"][i]
    u = proj[:, C_U:C_G]
    bu = _mm_bd(u, bmat, "nn", nm("mm_bu"))
    states = _scan(bu, lr, li, False, nm("scan"))
    yc = _mm_bd(states, cmat, "nn", nm("mm_c"))
    d = w["ssm_d"][i:i + 1]
    ys = _rowwise(lambda ctx, a, b, c: (_ssm_act(a, b, c),), [(yc, 0, SSM_WIDTH, 0), (u, 0, SSM_WIDTH, 0)], [d],
                  [(SSM_WIDTH, F32)], [], nm("ssm_act"))[0]
    z = _mm(ys, w["w_ssm_glu"], "nn", nm("mm_glu"), b_layer=i)
    sg = _rowwise(lambda ctx, a, b: (_glu(a, b),), [(ys, 0, SSM_WIDTH, 0), (z, 0, SSM_WIDTH, 0)], [],
                  [(SSM_WIDTH, BF16)], [], nm("glu"))[0]
    ya = _mm(att, w["w_attn_o"], "nn", nm("mm_ao"), b_layer=i)
    yv = _mm(conv, w["w_conv_o"], "nn", nm("mm_co"), b_layer=i)
    ym = _mm(sg, w["w_ssm_o"], "nn", nm("mm_so"), b_layer=i)
    bg = w["b_gate"][i:i + 1]
    merged = _rowwise(lambda ctx, a, b, c, gl, bb: (_merge(a, b, c, gl, bb),),
                      [(ya, 0, D_MODEL, 0), (yv, 0, D_MODEL, 0), (ym, 0, D_MODEL, 0), (proj, C_G, GATE_WIDTH, 0)],
                      [bg], [(D_MODEL, BF16)], [], nm("merge"))[0]
    x1 = _mm(merged, w["w_mix_o"], "nn", nm("mm_mix"), add=x, b_layer=i)
    h2 = _rms_fwd(x1, w["norm_ffn"][i:i + 1], nm("rms_ffn"))
    gu = _mm(h2, w["w_ffn_in"], "nn", nm("mm_ffn_in"), b_layer=i)
    act = _rowwise(lambda ctx, a: (_swiglu(a),), [(gu, 0, 2 * FFN_HIDDEN, 0)], [], [(FFN_HIDDEN, BF16)], [],
                   nm("swiglu"))[0]
    x2 = _mm(act, w["w_ffn_out"], "nn", nm("mm_ffn_out"), add=x1, b_layer=i, tk_cap=3072)
    sv.update(h=h, proj=proj, q=q, k=k, v=v, att=att, conv=conv, u=u, states=states, yc=yc, ys=ys, z=z, sg=sg,
              ya=ya, yv=yv, ym=ym, merged=merged, x1=x1, h2=h2, gu=gu, act=act)
    return x2, sv


def _layer_bwd(i, dx2, sv, w, tabs, gb):
    nm = lambda s: "l%d_b_%s" % (i, s)
    g = {}

    def wgrad(n, lhs, rhs, label, **kw):
        if n in gb:
            gb[n] = _mm(lhs, rhs, "tn", nm(label), into=(gb[n], i, BIG_AXIS[n]), **kw)
        else:
            g[n] = _mm(lhs, rhs, "tn", nm(label), **kw)

    dact = _mm(dx2, w["w_ffn_out"], "nt", nm("mm_dact"), b_layer=i)
    wgrad("w_ffn_out", sv["act"], dx2, "mm_gw_ffn_out", tn_cap=512)
    dgu = _rowwise(_vjp_rowwise(_swiglu, 1), [(sv["gu"], 0, 2 * FFN_HIDDEN, 0), (dact, 0, FFN_HIDDEN, 0)], [],
                   [(2 * FFN_HIDDEN, BF16)], [], nm("swiglu"))[0]
    dh2 = _mm(dgu, w["w_ffn_in"], "nt", nm("mm_dh2"), tk_cap=3072, b_layer=i)
    wgrad("w_ffn_in", sv["h2"], dgu, "mm_gw_ffn_in")
    dx1, g["norm_ffn"] = _rms_bwd(sv["x1"], w["norm_ffn"][i:i + 1], dh2, dx2, nm("rms_ffn"))
    dmerged = _mm(dx1, w["w_mix_o"], "nt", nm("mm_dmerged"), b_layer=i)
    wgrad("w_mix_o", sv["merged"], dx1, "mm_gw_mix", tn_cap=512)
    proj = sv["proj"]
    bg = w["b_gate"][i:i + 1]
    dya, dyv, dym, dgl, g["b_gate"] = _rowwise(
        _vjp_rowwise(_merge, 4),
        [(sv["ya"], 0, D_MODEL, 0), (sv["yv"], 0, D_MODEL, 0), (sv["ym"], 0, D_MODEL, 0),
         (proj, C_G, GATE_WIDTH, 0), (dmerged, 0, D_MODEL, 0)], [bg],
        [(D_MODEL, BF16), (D_MODEL, BF16), (D_MODEL, BF16), (GATE_WIDTH, BF16)], [(1, GATE_WIDTH)], nm("merge"))
    dsg = _mm(dym, w["w_ssm_o"], "nt", nm("mm_dsg"), b_layer=i)
    wgrad("w_ssm_o", sv["sg"], dym, "mm_gw_so")
    dys0, dz = _rowwise(_vjp_rowwise(_glu, 2), [(sv["ys"], 0, SSM_WIDTH, 0), (sv["z"], 0, SSM_WIDTH, 0),
                                                 (dsg, 0, SSM_WIDTH, 0)], [],
                        [(SSM_WIDTH, F32), (SSM_WIDTH, BF16)], [], nm("glu"))
    dys = _mm(dz, w["w_ssm_glu"], "nt", nm("mm_dys"), add=dys0, b_layer=i)
    wgrad("w_ssm_glu", sv["ys"], dz, "mm_gw_glu")
    d = w["ssm_d"][i:i + 1]
    dyc, du0, g["ssm_d"] = _rowwise(
        _vjp_rowwise(_ssm_act, 2), [(sv["yc"], 0, SSM_WIDTH, 0), (sv["u"], 0, SSM_WIDTH, 0), (dys, 0, SSM_WIDTH, 0)],
        [d], [(SSM_WIDTH, F32), (SSM_WIDTH, F32)], [(1, SSM_WIDTH)], nm("ssm_act"))
    lr, li, bmat, cmat = w["ssm"][i]
    dstates = _mm_bd(dyc, cmat, "nt", nm("mm_dstates"))
    g_cmat = _mm_bd(sv["states"], dyc, "tn", nm("mm_gc"))
    gs, dl = _scan(dstates, lr, -li, True, nm("scan"), states=sv["states"])
    g_lam = jnp.sum(dl, axis=0).reshape(N_SSM_CHUNKS, 2, CHUNK_STATES)
    du = _mm_bd(gs, bmat, "nt", nm("mm_du"), out_dtype=BF16, add=du0)
    g_bmat = _mm_bd(sv["u"], gs, "tn", nm("mm_gb"))
    g["ssm"] = (g_lam[:, 0].reshape(1, SSM_LANES), g_lam[:, 1].reshape(1, SSM_LANES), g_bmat, g_cmat)
    dconv = _mm(dyv, w["w_conv_o"], "nt", nm("mm_dconv"), b_layer=i)
    wgrad("w_conv_o", sv["conv"], dyv, "mm_gw_co")
    dcb, dcc, dcx, dw0, dw1, dw2 = _conv_bwd(proj, w["conv_w"][i], dconv, nm("conv"))
    g["conv_w"] = jnp.concatenate([dw0, dw1, dw2], axis=0)
    datt = _mm(dya, w["w_attn_o"], "nt", nm("mm_datt"), b_layer=i)
    wgrad("w_attn_o", sv["att"], dya, "mm_gw_ao")
    sinks = w["attn_sinks"][i].reshape(N_KV_HEADS, Q_GROUP)
    dq, dkc, dkp, dvc, dvp, dsk = _att_bwd(sv["q"], sv["k"], sv["v"], sinks, _to_heads(datt, N_Q_HEADS), nm("att"))
    g["attn_sinks"] = dsk[:, :Q_GROUP, 0].reshape(N_Q_HEADS)
    dqkv = _split_bwd(_from_heads(dq), _from_heads(dkc), _from_heads(dkp), _from_heads(dvc), _from_heads(dvp),
                      tabs, nm("rope"))
    dproj = jnp.concatenate([dqkv, dcb, dcc, dcx, du, dgl], axis=1)
    dh = _mm(dproj, w["w_in"], "nt", nm("mm_dh"), tk_cap=3072, b_layer=i)
    dproj_shards = dproj.reshape(dproj.shape[0], N_CHIPS, IN_COLS // N_CHIPS).transpose(1, 0, 2)
    wgrad("w_in", sv["h"], dproj_shards, "mm_gw_in", b_shards=True)
    dx, g["norm_mix"] = _rms_bwd(sv["x"], w["norm_mix"][i:i + 1], dh, dx1, nm("rms_mix"))
    return dx, g


def _local_step(x, target, w):
    length = x.shape[0]
    tabs = _rope_tables(length)
    ssm_names = ("ssm_a_re", "ssm_a_im", "ssm_b_re", "ssm_b_im", "ssm_c_re", "ssm_c_im", "ssm_log_dt")
    w = dict(w)
    preps = [jax.vjp(_ssm_prep, *[w[n][i] for n in ssm_names]) for i in range(DEPTH)]
    w["ssm"] = [p[0] for p in preps]
    saved = []
    for i in range(DEPTH):
        x, sv = _layer_fwd(i, x, w, tabs)
        saved.append(sv)

    def loss_fn(ctx, xv, tv, gv):
        val, vjp = jax.vjp(_loss_fn, xv, gv, tv)
        dx, dg, _ = vjp(jnp.ones((1, 1), F32))
        return dx, dg, val + jnp.zeros((1, LANE), F32)

    gfin = w["norm_final"].reshape(1, D_MODEL)
    dx, g_final, loss = _rowwise(loss_fn, [(x, 0, D_MODEL, 0), (target, 0, D_MODEL, 0)], [gfin],
                                 [(D_MODEL, F32)], [(1, D_MODEL), (1, LANE)], "loss")
    gb = {}
    for n in BIG:
        depth, ra, cb = w[n].shape
        gb[n] = lax.empty((N_CHIPS, depth) + ((ra // N_CHIPS, cb) if BIG_AXIS[n] == 1 else (ra, cb // N_CHIPS)), F32)
    layer_grads = [None] * DEPTH
    for i in reversed(range(DEPTH)):
        dx, layer_grads[i] = _layer_bwd(i, dx, saved[i], w, tabs, gb)
    grads = dict(gb)
    for n in layer_grads[0]:
        if n != "ssm":
            grads[n] = jnp.stack([lg[n] for lg in layer_grads])
    ssm_g = [preps[i][1](layer_grads[i]["ssm"]) for i in range(DEPTH)]
    for j, n in enumerate(ssm_names):
        grads[n] = jnp.stack([sg[j] for sg in ssm_g])
    grads["norm_final"] = g_final.reshape(D_MODEL)
    for n in ("norm_mix", "norm_ffn", "b_gate", "ssm_d"):
        grads[n] = grads[n].reshape(grads[n].shape[0], -1)
    return loss, dx, grads


COLS = 1024
ANY_SPEC = pl.BlockSpec(memory_space=pl.ANY)


def _place():
    return lax.axis_index("x"), lax.axis_index("y"), lax.axis_index("c")


def _other_chips(x, y):
    return [(1 - x, y), (x, 1 - y), (1 - x, 1 - y)]


def _remote(src, dst, send_sems, recv_sems, k, to):
    return pltpu.make_async_remote_copy(src_ref=src, dst_ref=dst, send_sem=send_sems.at[k], recv_sem=recv_sems.at[k],
                                        device_id=to, device_id_type=MESH_ID)


def _comm_call(body, name, out_shape, n_sems, n_local, args):
    return pl.pallas_call(
        body, name=name, out_shape=out_shape, in_specs=[ANY_SPEC] * len(args),
        out_specs=[ANY_SPEC] * len(out_shape),
        scratch_shapes=[pltpu.SemaphoreType.DMA((n_sems,)), pltpu.SemaphoreType.DMA((n_sems,)),
                        pltpu.SemaphoreType.DMA((n_local,))],
    )(*args)


def _gather_weights(shards, name):
    nt = len(shards)
    per = 8
    hds = [s.shape[0] // 2 for s in shards]

    def body(*refs):
        srcs, outs = refs[:nt], refs[nt:2 * nt]
        send_sems, recv_sems, _ = refs[2 * nt:]
        x, y, c = _place()
        me, sibling = (x, y, c), (x, y, 1 - c)
        xn, yn, dg = _other_chips(x, y)

        def blk(t, chip, hc, part=None):
            lo, n = hc * hds[t], hds[t]
            if part is not None:
                first_n = (n + 1) // 2
                lo, n = (lo, first_n) if part == 0 else (lo + first_n, n - first_n)
            return outs[t].at[2 * chip[0] + chip[1], pl.ds(lo, n)] if n else None

        def copy(t, k, src, dst, to):
            return _remote(src, dst, send_sems, recv_sems, per * t + k, to)

        def arrived(t, k, ref):
            copy(t, k, ref, ref, me).wait_recv()

        sent = []

        def start(t, k, ref, to):
            if ref is not None:
                sent.append(copy(t, k, ref, ref, to))
                sent[-1].start()

        for t in range(nt):
            own = srcs[t].at[pl.ds(c * hds[t], hds[t])]
            for k, chip in enumerate((xn, yn)):
                sent.append(copy(t, k, own, blk(t, (x, y), c), (*chip, c)))
                sent[-1].start()
        for t in range(nt):
            arrived(t, 0, blk(t, xn, c))
            start(t, 3, blk(t, xn, c, 1), (*yn, c))
            start(t, 4, blk(t, xn, c), sibling)
            arrived(t, 1, blk(t, yn, c))
            start(t, 2, blk(t, yn, c, 0), (*xn, c))
            start(t, 5, blk(t, yn, c), sibling)
        for t in range(nt):
            arrived(t, 2, blk(t, dg, c, 0))
            start(t, 6, blk(t, dg, c, 0), sibling)
            if blk(t, dg, c, 1) is not None:
                arrived(t, 3, blk(t, dg, c, 1))
                start(t, 7, blk(t, dg, c, 1), sibling)
        for t in range(nt):
            arrived(t, 4, blk(t, xn, 1 - c))
            arrived(t, 5, blk(t, yn, 1 - c))
            arrived(t, 6, blk(t, dg, 1 - c, 0))
            if blk(t, dg, 1 - c, 1) is not None:
                arrived(t, 7, blk(t, dg, 1 - c, 1))
        for cp in sent:
            cp.wait_send()

    out_shape = [jax.ShapeDtypeStruct((N_CHIPS,) + s.shape, s.dtype) for s in shards]
    return _comm_call(body, name, out_shape, per * nt, 1, shards)


def _swap_halves(gs, name):
    nt = len(gs)
    hds = [g.shape[1] // 2 for g in gs]

    def body(*refs):
        srcs, outs = refs[:nt], refs[nt:2 * nt]
        send_sems, recv_sems, _ = refs[2 * nt:]
        x, y, c = _place()
        cps = [_remote(srcs[t].at[s, pl.ds((1 - c) * hds[t], hds[t])], outs[t].at[s], send_sems, recv_sems,
                       N_CHIPS * t + s, (x, y, 1 - c)) for t in range(nt) for s in range(N_CHIPS)]
        for cp in cps:
            cp.start()
        for cp in cps:
            cp.wait()

    out_shape = [jax.ShapeDtypeStruct((N_CHIPS, g.shape[1] // 2) + g.shape[2:], g.dtype) for g in gs]
    return _comm_call(body, name, out_shape, N_CHIPS * nt, 1, gs)


def _exchange_shards(parts, name, swap=()):
    nt, ns = len(parts), len(swap)
    hds = [g.shape[1] // 2 for g in swap]

    def body(*refs):
        srcs, swap_srcs = refs[:nt], refs[nt:nt + ns]
        outs, swap_outs = refs[nt + ns:2 * nt + ns], refs[2 * nt + ns:2 * (nt + ns)]
        send_sems, recv_sems, _ = refs[2 * (nt + ns):]
        x, y, c = _place()
        cps = [_remote(srcs[t].at[2 * chip[0] + chip[1]], outs[t].at[j], send_sems, recv_sems, 3 * t + j, (*chip, c))
               for t in range(nt) for j, chip in enumerate(_other_chips(x, y))]
        cps += [_remote(swap_srcs[t].at[s, pl.ds((1 - c) * hds[t], hds[t])], swap_outs[t].at[s], send_sems, recv_sems,
                        3 * nt + N_CHIPS * t + s, (x, y, 1 - c)) for t in range(ns) for s in range(N_CHIPS)]
        for cp in cps:
            cp.start()
        for cp in cps:
            cp.wait()

    out_shape = [jax.ShapeDtypeStruct((N_CHIPS - 1,) + p.shape[1:], p.dtype) for p in parts]
    out_shape += [jax.ShapeDtypeStruct((N_CHIPS, g.shape[1] // 2) + g.shape[2:], g.dtype) for g in swap]
    res = _comm_call(body, name, out_shape, 3 * nt + N_CHIPS * ns, 1, list(parts) + list(swap))
    return res[:nt], res[nt:]


def _join_halves(reds, name):
    nt = len(reds)
    hds = [r.shape[1] for r in reds]

    def body(*refs):
        srcs, outs = refs[:nt], refs[nt:2 * nt]
        send_sems, recv_sems, _ = refs[2 * nt:]
        x, y, c = _place()
        cps = [_remote(srcs[t].at[0], outs[t].at[pl.ds(c * hds[t], hds[t])], send_sems, recv_sems, t, (x, y, 1 - c))
               for t in range(nt)]
        for cp in cps:
            cp.start()
        for t in range(nt):
            _remote(srcs[t].at[0], outs[t].at[pl.ds((1 - c) * hds[t], hds[t])], send_sems, recv_sems, t,
                    (x, y, c)).wait_recv()
        for cp in cps:
            cp.wait_send()

    out_shape = [jax.ShapeDtypeStruct((2 * r.shape[1],) + r.shape[2:], r.dtype) for r in reds]
    return _comm_call(body, name, out_shape, nt, 1, reds)


SUM_BLOCK_ELEMS = 256 * 1024


def _sum_windows(parts, lead, name, out_dtypes=(F32,)):
    a, b = parts[0][0].shape[2:]
    ta = _tile(a, max(SUBLANE, SUM_BLOCK_ELEMS // b), 2 * SUBLANE)
    offs = jnp.stack([jnp.stack([jnp.asarray(o, jnp.int32) for o in off]) for _, off in parts])
    n_in = len(parts)

    def body(off_ref, *refs):
        acc = refs[0][...].astype(F32)
        for r in refs[1:n_in]:
            acc = acc + r[...].astype(F32)
        for r in refs[n_in:]:
            r[...] = acc.astype(r.dtype)

    in_specs = [pl.BlockSpec((1, 1, ta, b), lambda p, q, i, off, k=k: (off[k, 0] + p, off[k, 1] + q, i, 0))
                for k in range(n_in)]
    o_spec = pl.BlockSpec((1, 1, ta, b), lambda p, q, i, off: (p, q, i, 0))
    return pl.pallas_call(
        body, name=name, out_shape=[jax.ShapeDtypeStruct(tuple(lead) + (a, b), dt) for dt in out_dtypes],
        grid_spec=pltpu.PrefetchScalarGridSpec(
            num_scalar_prefetch=1, grid=tuple(lead) + (a // ta,), in_specs=in_specs,
            out_specs=[o_spec] * len(out_dtypes)),
        compiler_params=_params("arbitrary", "arbitrary", "arbitrary"),
    )(offs, *[arr for arr, _ in parts])


def _reduce_scatter(gs, names, wire, groups):
    x, y, c = _place()
    theirs = dict(zip(groups[0], _swap_halves([gs[i] for i in groups[0]], "rs_swap_halves")))
    pairs, others = {}, {}
    for k, group in enumerate(groups):
        for i in group:
            g = gs[i]
            pairs[i] = _sum_windows([(g, (0, c * (g.shape[1] // 2))), (theirs[i], (0, 0))],
                                    (N_CHIPS, g.shape[1] // 2), "rs_sum_pair_" + names[i], (F32, wire[i]))
        nxt = groups[k + 1] if k + 1 < len(groups) else []
        got, swapped = _exchange_shards([pairs[i][1] for i in group], "rs_exchange_%d" % k, [gs[i] for i in nxt])
        others.update(zip(group, got))
        theirs.update(zip(nxt, swapped))
    pairs, others = [pairs[i] for i in range(len(gs))], [others[i] for i in range(len(gs))]
    reds = [_sum_windows([(p[0], (2 * x + y, 0))] + [(o, (j, 0)) for j in range(N_CHIPS - 1)], (1, p[0].shape[1]),
                         "rs_sum_chips_" + n)[0] for p, o, n in zip(pairs, others, names)]
    joined = _join_halves(reds, "rs_join")
    return [lax.dynamic_update_slice_in_dim(j, r[0], c * r.shape[1], axis=0) for j, r in zip(joined, reds)]


def _adamw(wt, g, m, v, name):
    cols = wt.shape[-1]
    r = wt.size // cols
    tr = _tile(r, max(SUBLANE, 2 * SUM_BLOCK_ELEMS // max(cols, LANE)), SUBLANE)

    def body(w_ref, g_ref, m_ref, v_ref, d_ref, nm_ref, nv_ref):
        gv = g_ref[...]
        mn = ADAM_B1 * m_ref[...] + (1.0 - ADAM_B1) * gv
        vn = ADAM_B2 * v_ref[...] + (1.0 - ADAM_B2) * jnp.square(gv)
        m_hat = mn / (1.0 - ADAM_B1 ** ADAM_STEP)
        v_hat = vn / (1.0 - ADAM_B2 ** ADAM_STEP)
        d_ref[...] = -ADAM_LR * (m_hat / (jnp.sqrt(v_hat) + ADAM_EPS) + ADAM_WD * w_ref[...])
        nm_ref[...] = mn
        nv_ref[...] = vn

    spec = pl.BlockSpec((tr, cols), lambda i: (i, 0))
    shp = jax.ShapeDtypeStruct((r, cols), F32)
    res = pl.pallas_call(
        body, name=name, grid=(r // tr,), in_specs=[spec] * 4, out_specs=[spec] * 3, out_shape=[shp] * 3,
        compiler_params=_params("parallel"),
    )(*[t.reshape(r, cols) for t in (wt, g, m, v)])
    return [t.reshape(wt.shape) for t in res]


def _join_shards(n, piece):
    _, depth, a, b = piece.shape
    if BIG_AXIS[n] == 2:
        return piece.transpose(1, 2, 0, 3).reshape(depth, a, N_CHIPS * b)
    return piece.transpose(1, 0, 2, 3).reshape(depth, N_CHIPS * a, b)


RS_GROUPS = (("w_ffn_out", "w_mix_o", "small"), ("w_ffn_in",), ("w_in", "w_attn_o", "w_conv_o", "w_ssm_glu", "w_ssm_o"))


SMALL_PART_ROWS = 2 * SUBLANE


def _rows_of(t):
    return -(-t.size // COLS)


def _pack_small(ts):
    rows = [jnp.pad(t.reshape(-1), (0, _rows_of(t) * COLS - t.size)).reshape(-1, COLS) for t in ts]
    total = sum(r.shape[0] for r in rows)
    part = -(-total // (N_DEV * SMALL_PART_ROWS)) * SMALL_PART_ROWS
    rows.append(jnp.zeros((N_DEV * part - total, COLS), F32))
    return jnp.concatenate(rows, axis=0).reshape(N_CHIPS, 2, part, COLS)


def _unpack_small(buf, like):
    buf, out, r0 = buf.reshape(-1, COLS), [], 0
    for t in like:
        out.append(buf[r0:r0 + _rows_of(t)].reshape(-1)[:t.size].reshape(t.shape))
        r0 += _rows_of(t)
    return out


def kernel(x, norm_mix, w_in, b_gate, attn_sinks, w_attn_o, conv_w, w_conv_o, ssm_a_re, ssm_a_im, ssm_b_re, ssm_b_im, ssm_c_re, ssm_c_im, ssm_d, ssm_log_dt, w_ssm_glu, w_ssm_o, w_mix_o, norm_ffn, w_ffn_in, w_ffn_out, norm_final, loss_target, m_norm_mix, m_w_in, m_b_gate, m_attn_sinks, m_w_attn_o, m_conv_w, m_w_conv_o, m_ssm_a_re, m_ssm_a_im, m_ssm_b_re, m_ssm_b_im, m_ssm_c_re, m_ssm_c_im, m_ssm_d, m_ssm_log_dt, m_w_ssm_glu, m_w_ssm_o, m_w_mix_o, m_norm_ffn, m_w_ffn_in, m_w_ffn_out, m_norm_final, v_norm_mix, v_w_in, v_b_gate, v_attn_sinks, v_w_attn_o, v_conv_w, v_w_conv_o, v_ssm_a_re, v_ssm_a_im, v_ssm_b_re, v_ssm_b_im, v_ssm_c_re, v_ssm_c_im, v_ssm_d, v_ssm_log_dt, v_w_ssm_glu, v_w_ssm_o, v_w_mix_o, v_norm_ffn, v_w_ffn_in, v_w_ffn_out, v_norm_final):
    a = dict(zip(ARG_NAMES, (
        x, norm_mix, w_in, b_gate, attn_sinks, w_attn_o, conv_w, w_conv_o, ssm_a_re, ssm_a_im, ssm_b_re, ssm_b_im,
        ssm_c_re, ssm_c_im, ssm_d, ssm_log_dt, w_ssm_glu, w_ssm_o, w_mix_o, norm_ffn, w_ffn_in, w_ffn_out, norm_final,
        loss_target, m_norm_mix, m_w_in, m_b_gate, m_attn_sinks, m_w_attn_o, m_conv_w, m_w_conv_o, m_ssm_a_re,
        m_ssm_a_im, m_ssm_b_re, m_ssm_b_im, m_ssm_c_re, m_ssm_c_im, m_ssm_d, m_ssm_log_dt, m_w_ssm_glu, m_w_ssm_o,
        m_w_mix_o, m_norm_ffn, m_w_ffn_in, m_w_ffn_out, m_norm_final, v_norm_mix, v_w_in, v_b_gate, v_attn_sinks,
        v_w_attn_o, v_conv_w, v_w_conv_o, v_ssm_a_re, v_ssm_a_im, v_ssm_b_re, v_ssm_b_im, v_ssm_c_re, v_ssm_c_im,
        v_ssm_d, v_ssm_log_dt, v_w_ssm_glu, v_w_ssm_o, v_w_mix_o, v_norm_ffn, v_w_ffn_in, v_w_ffn_out, v_norm_final)))
    px, py, _ = _place()
    chip = 2 * px + py

    gathered = BIG + ("conv_w",)
    own = [a[n].astype(BF16) for n in BIG] + [a["conv_w"]]
    gath = _gather_weights(own, "gather_weights")
    w = {n: _join_shards(n, lax.dynamic_update_slice_in_dim(p, o[None], chip, axis=0))
         for n, p, o in zip(gathered, gath, own)}
    for n in SMALL:
        w[n] = a[n]

    loss, dx, grads = _local_step(a["x"][0], a["loss_target"][0], w)

    small_names = SMALL + ("conv_w",)
    small = _pack_small([grads[n] for n in small_names])
    rs_names = BIG + ("small",)
    groups = [[rs_names.index(n) for n in grp] for grp in RS_GROUPS]
    rs = _reduce_scatter([grads[n] for n in BIG] + [small], rs_names, (BF16,) * len(BIG) + (F32,), groups)
    red = dict(zip(BIG, rs))
    small_all = _gather_weights([rs[-1]], "gather_small")[0]
    small_all = lax.dynamic_update_slice_in_dim(small_all, rs[-1][None], chip, axis=0)
    red.update(zip(small_names, _unpack_small(small_all, [grads[n] for n in small_names])))
    lane = a["conv_w"].shape[2]
    red["conv_w"] = lax.dynamic_slice_in_dim(red["conv_w"], chip * lane, lane, axis=2)

    loss_all = lax.psum(loss[0, 0], ("x", "y", "c"))
    deltas, new_m, new_v = [], [], []
    for n in WEIGHTS:
        d, mn, vn = _adamw(a[n], red[n], a["m_" + n], a["v_" + n], "adamw_" + n)
        deltas.append(d)
        new_m.append(mn)
        new_v.append(vn)
    return (loss_all, dx[None], *[red[n] for n in WEIGHTS], *deltas, *new_m, *new_v)
```

```python
import math

import jax
import jax.numpy as jnp
from jax import lax
from jax.experimental import pallas as pl
from jax.experimental.pallas import tpu as pltpu

F32 = jnp.float32
BF16 = jnp.bfloat16

D_MODEL = 1024
DEPTH = 4
N_Q_HEADS = 8
N_KV_HEADS = 2
HEAD_DIM = 64
Q_GROUP = N_Q_HEADS // N_KV_HEADS
WINDOW = 128
BLOCK = 128
ROPE_THETA = 500000.0
ROT_DIM = HEAD_DIM // 4
ATTN_WIDTH = N_Q_HEADS * HEAD_DIM
KV_WIDTH = N_KV_HEADS * HEAD_DIM
NEG_INF = -1e30
CONV_WIDTH = 512
CONV_K = 3
SSM_WIDTH = 512
SSM_GROUP = 16
SSM_GROUPS = 32
SSM_STATE = 64
SSM_LANES = SSM_GROUPS * SSM_STATE
GATE_WIDTH = 3 * D_MODEL
FFN_HIDDEN = 2816
NORM_EPS = 1e-6
IN_COLS = 5888
C_Q, C_K, C_V, C_CB, C_CC, C_CX, C_U, C_G = 0, 512, 640, 768, 1280, 1792, 2304, 2816

ADAM_LR = 0.001
ADAM_B1 = 0.9
ADAM_B2 = 0.999
ADAM_EPS = 1e-08
ADAM_WD = 0.01
ADAM_STEP = 10

N_CHIPS = 4
N_DEV = 8
MESH_ID = pl.DeviceIdType.MESH

VMEM_LIMIT_BYTES = 48 * 1024 * 1024
LANE = 128
SUBLANE = 8
SCAN_ROWS = 8
SCAN_CHUNK = 128

BIG = ("w_in", "w_attn_o", "w_conv_o", "w_ssm_glu", "w_ssm_o", "w_mix_o", "w_ffn_in", "w_ffn_out")
BIG_AXIS = {"w_in": 2, "w_attn_o": 2, "w_conv_o": 2, "w_ssm_glu": 1, "w_ssm_o": 2, "w_mix_o": 1,
            "w_ffn_in": 2, "w_ffn_out": 1, "conv_w": 2}
SMALL = ("norm_mix", "b_gate", "attn_sinks", "ssm_a_re", "ssm_a_im", "ssm_b_re", "ssm_b_im",
         "ssm_c_re", "ssm_c_im", "ssm_d", "ssm_log_dt", "norm_ffn", "norm_final")
WEIGHTS = ("norm_mix", "w_in", "b_gate", "attn_sinks", "w_attn_o", "conv_w", "w_conv_o", "ssm_a_re",
           "ssm_a_im", "ssm_b_re", "ssm_b_im", "ssm_c_re", "ssm_c_im", "ssm_d", "ssm_log_dt",
           "w_ssm_glu", "w_ssm_o", "w_mix_o", "norm_ffn", "w_ffn_in", "w_ffn_out", "norm_final")
ARG_NAMES = ("x",) + WEIGHTS + ("loss_target",) + tuple("m_" + n for n in WEIGHTS) + tuple(
    "v_" + n for n in WEIGHTS)


def _params(*sem):
    return pltpu.CompilerParams(dimension_semantics=sem if sem else None,
                                vmem_limit_bytes=VMEM_LIMIT_BYTES)


def _tile(dim, cap, align):
    t = min(cap, dim) // align * align
    while t >= align:
        if dim % t == 0:
            return t
        t -= align
    return dim


_DOT_DIMS = {"nn": (((1,), (0,)), ((), ())), "nt": (((1,), (1,)), ((), ())), "tn": (((0,), (0,)), ((), ()))}


def _mm(a, b, mode, name, out_dtype=F32, add=None, tm_cap=512, tn_cap=3072, tk_cap=1024, b_layer=None, into=None,
        b_shards=False):
    bshape = b.shape if b_layer is None else b.shape[1:]
    if b_shards:
        assert mode == "tn" and b_layer is None and into is not None and into[2] == 2
        bshape = (b.shape[1], N_CHIPS * b.shape[2])
    if mode == "nn":
        (m, k), (k2, n) = a.shape, bshape
    elif mode == "nt":
        (m, k), (n, k2) = a.shape, bshape
    else:
        (k, m), (k2, n) = a.shape, bshape
    assert k == k2, (name, a.shape, b.shape)
    tm, tn, tk = _tile(m, tm_cap, LANE), _tile(n, tn_cap, LANE), _tile(k, tk_cap, LANE)
    if into is not None:
        buf, layer, axis = into
        _, _, ra, cb = buf.shape
        if axis == 1:
            tm = m
        elif b_shards:
            tn = cb
        else:
            tn = _tile(cb, tn_cap, LANE)
            assert cb % tn == 0 and tn % LANE == 0, (name, cb, tn)
    nk = k // tk
    dims = _DOT_DIMS[mode]

    def body(a_ref, b_ref, *rest):
        rest = list(rest)
        add_ref = rest.pop(0) if add is not None else None
        if into is not None:
            rest.pop(0)
        o_ref, acc = rest
        kk = pl.program_id(2)
        part = lax.dot_general(a_ref[...].astype(BF16), b_ref[...].astype(BF16), dims, preferred_element_type=F32)

        def finish(r):
            if add is not None:
                r = r + add_ref[...]
            o_ref[...] = r.astype(o_ref.dtype).reshape(o_ref.shape)

        if nk == 1:
            finish(part)
            return

        @pl.when(kk == 0)
        def _():
            acc[...] = part

        @pl.when(kk > 0)
        def _():
            acc[...] += part

        @pl.when(kk == nk - 1)
        def _():
            finish(acc[...])

    if mode == "tn":
        a_spec = pl.BlockSpec((tk, tm), lambda i, j, kk: (kk, i))
    else:
        a_spec = pl.BlockSpec((tm, tk), lambda i, j, kk: (i, kk))
    lead = () if b_layer is None else (None,)
    at = (lambda *ix: ix) if b_layer is None else (lambda *ix: (b_layer,) + ix)
    if b_shards:
        b_spec = pl.BlockSpec((None, tk, tn), lambda i, j, kk: (j, kk, 0))
    elif mode == "nt":
        b_spec = pl.BlockSpec(lead + (tn, tk), lambda i, j, kk: at(j, kk))
    else:
        b_spec = pl.BlockSpec(lead + (tk, tn), lambda i, j, kk: at(kk, j))
    o_spec = pl.BlockSpec((tm, tn), lambda i, j, kk: (i, j))
    in_specs, args = [a_spec, b_spec], [a, b]
    if add is not None:
        in_specs.append(o_spec)
        args.append(add)
    out_shape, aliases = jax.ShapeDtypeStruct((m, n), out_dtype), {}
    if into is not None:
        in_specs.append(pl.BlockSpec(memory_space=pl.ANY))
        aliases = {len(args): 0}
        args.append(buf)
        out_shape = jax.ShapeDtypeStruct(buf.shape, buf.dtype)
        if axis == 1:
            o_spec = pl.BlockSpec((N_CHIPS, None, ra, tn), lambda i, j, kk: (0, layer, 0, j))
        else:
            per = cb // tn
            o_spec = pl.BlockSpec((None, None, tm, tn), lambda i, j, kk: (j // per, layer, i, j % per))
    return pl.pallas_call(
        body, name=name, grid=(m // tm, n // tn, nk), in_specs=in_specs, out_specs=o_spec,
        out_shape=out_shape, input_output_aliases=aliases,
        scratch_shapes=[pltpu.VMEM((tm, tn) if nk > 1 else (SUBLANE, LANE), F32)],
        compiler_params=_params("parallel", "parallel", "arbitrary"),
    )(*args)


def _rowwise(fn, rows, pars, outs, accs, name, tm_cap=256):
    length = rows[0][0].shape[0]
    tm = _tile(length, tm_cap, LANE)
    n = length // tm
    in_specs, args, counts = [], [], []
    for arr, c0, cw, shift in rows:
        bw = math.gcd(c0, cw) if c0 else cw
        assert bw % LANE == 0 or (c0 == 0 and cw == arr.shape[1]), (name, c0, cw)
        cnt = cw // bw
        counts.append(cnt)
        for j in range(cnt):
            in_specs.append(pl.BlockSpec(
                (tm, bw), lambda i, j=j, c0=c0, bw=bw, shift=shift: (jnp.clip(i + shift, 0, n - 1), c0 // bw + j)))
            args.append(arr)
    for p in pars:
        in_specs.append(pl.BlockSpec(p.shape, lambda i: (0, 0)))
        args.append(p)
    out_shape = [jax.ShapeDtypeStruct((length, w), dt) for w, dt in outs]
    out_specs = [pl.BlockSpec((tm, w), lambda i: (i, 0)) for w, _ in outs]
    out_shape += [jax.ShapeDtypeStruct((r, w), F32) for r, w in accs]
    out_specs += [pl.BlockSpec((r, w), lambda i: (0, 0)) for r, w in accs]
    n_in, n_out = len(args), len(outs)

    def body(*refs):
        i = pl.program_id(0)
        vals, p = [], 0
        for cnt in counts:
            blocks = [refs[p + j][...] for j in range(cnt)]
            p += cnt
            vals.append(blocks[0] if cnt == 1 else jnp.concatenate(blocks, axis=1))
        for _ in pars:
            vals.append(refs[p][...])
            p += 1
        res = fn((i, n), *vals)
        out_refs = refs[n_in:n_in + n_out]
        acc_refs = refs[n_in + n_out:]
        for r, v in zip(out_refs, res[:n_out]):
            r[...] = v.astype(r.dtype)
        if acc_refs:
            @pl.when(i == 0)
            def _():
                for r in acc_refs:
                    r[...] = jnp.zeros_like(r)
            for r, v in zip(acc_refs, res[n_out:]):
                r[...] += v

    res = pl.pallas_call(
        body, name=name, grid=(n,), in_specs=in_specs, out_specs=out_specs, out_shape=out_shape,
        compiler_params=_params("arbitrary"),
    )(*args)
    return res


def _rms(x, g):
    return x * lax.rsqrt(jnp.mean(x * x, axis=-1, keepdims=True) + NORM_EPS) * g


def _rms_fwd(x, g, name):
    return _rowwise(lambda ctx, xv, gv: (_rms(xv, gv),), [(x, 0, D_MODEL, 0)], [g],
                    [(D_MODEL, BF16)], [], name)[0]


def _rms_bwd(x, g, dh, dres, name):
    def fn(ctx, xv, dhv, drv, gv):
        _, vjp = jax.vjp(_rms, xv, gv)
        dx, dg = vjp(dhv)
        return dx + drv, dg
    return _rowwise(fn, [(x, 0, D_MODEL, 0), (dh, 0, D_MODEL, 0), (dres, 0, D_MODEL, 0)], [g],
                    [(D_MODEL, F32)], [(1, D_MODEL)], name)


def _rope_tables(length):
    pos = jnp.arange(length, dtype=F32)
    inv_freq = ROPE_THETA ** (-jnp.arange(0, ROT_DIM, 2, dtype=F32) / ROT_DIM)
    ang = pos[:, None] * inv_freq[None, :]
    cos, sin = jnp.cos(ang), jnp.sin(ang)
    half = ROT_DIM // 2
    ones = jnp.ones((length, HEAD_DIM - ROT_DIM), F32)
    zeros = jnp.zeros_like(ones)
    zh = jnp.zeros((length, half), F32)
    c64 = jnp.concatenate([cos, cos, ones], axis=1)
    s1 = jnp.concatenate([-sin, zh, zeros], axis=1)
    s2 = jnp.concatenate([zh, sin, zeros], axis=1)
    tile2 = lambda t: jnp.concatenate([t, t], axis=1)
    return tile2(c64), tile2(s1), tile2(s2)


def _lane_chunks(t):
    return [t[:, j * LANE:(j + 1) * LANE] for j in range(t.shape[1] // LANE)]


def _rope(t, c, s1, s2, n_rot):
    half = ROT_DIM // 2
    out = []
    for j, ch in enumerate(_lane_chunks(t)):
        if j < n_rot:
            ch = ch * c + pltpu.roll(ch, LANE - half, 1) * s1 + pltpu.roll(ch, half, 1) * s2
        out.append(ch)
    return jnp.concatenate(out, axis=1)


def _unrope(d, c, s1, s2, n_rot):
    half = ROT_DIM // 2
    out = []
    for j, ch in enumerate(_lane_chunks(d)):
        if j < n_rot:
            ch = ch * c + pltpu.roll(ch * s1, half, 1) + pltpu.roll(ch * s2, LANE - half, 1)
        out.append(ch)
    return jnp.concatenate(out, axis=1)


N_ROT_CHUNKS = (ATTN_WIDTH + KV_WIDTH) // LANE
QKV_WIDTH = ATTN_WIDTH + 2 * KV_WIDTH


def _split_fwd(proj, tabs, name):
    def fn(ctx, t, c, s1, s2):
        return (_rope(t, c, s1, s2, N_ROT_CHUNKS),)
    rows = [(proj, 0, QKV_WIDTH, 0)] + [(t, 0, LANE, 0) for t in tabs]
    return _rowwise(fn, rows, [], [(QKV_WIDTH, BF16)], [], name, tm_cap=BLOCK)[0]


def _split_bwd(d, tabs, name):
    def fn(ctx, dqv, dkcv, dkpv, dvcv, dvpv, c, s1, s2):
        i, n = ctx
        keep = (i < n - 1).astype(F32)
        dd = jnp.concatenate([dqv, dkcv + keep * dkpv, dvcv + keep * dvpv], axis=1)
        return (_unrope(dd, c, s1, s2, N_ROT_CHUNKS),)
    col = lambda row: row * HEAD_DIM
    rows = [(d, 0, ATTN_WIDTH, 0), (d, col(D_KC), KV_WIDTH, 0), (d, col(D_KP), KV_WIDTH, 1),
            (d, col(D_VC), KV_WIDTH, 0), (d, col(D_VP), KV_WIDTH, 1)] + [(t, 0, LANE, 0) for t in tabs]
    return _rowwise(fn, rows, [], [(QKV_WIDTH, BF16)], [], name, tm_cap=BLOCK)[0]


N_HEAD_ROWS = N_Q_HEADS + 2 * N_KV_HEADS


def _att_scores(j, q_ref, kvp_ref, kvc_ref, sink_ref):
    n = pl.program_id(0)
    rows = Q_GROUP * BLOCK
    qs = q_ref[Q_GROUP * j:Q_GROUP * (j + 1)].reshape(rows, HEAD_DIM)
    kb = jnp.concatenate([kvp_ref[j], kvc_ref[j]], axis=0)
    s = lax.dot_general(qs, kb, _DOT_DIMS["nt"], preferred_element_type=F32) * (HEAD_DIM ** -0.5)
    r = lax.broadcasted_iota(jnp.int32, (rows, 2 * BLOCK), 0)
    kj = lax.broadcasted_iota(jnp.int32, (rows, 2 * BLOCK), 1)
    delta = (r % BLOCK) + BLOCK - kj
    ok = (delta >= 0) & (delta < WINDOW) & ((kj >= BLOCK) | (n > 0))
    s = jnp.where(ok, s, NEG_INF)
    rh = lax.broadcasted_iota(jnp.int32, (rows, 1), 0) // BLOCK
    sinks = sink_ref[...]
    lane = lax.broadcasted_iota(jnp.int32, sinks.shape, 1)
    srow = lax.broadcasted_iota(jnp.int32, sinks.shape, 0)
    sink = jnp.zeros((rows, 1), F32)
    for g in range(Q_GROUP):
        val = jnp.sum(jnp.where((lane == g) & (srow == j), sinks, 0.0), keepdims=True)
        sink = jnp.where(rh == g, val, sink)
    m = jnp.maximum(jnp.max(s, axis=-1, keepdims=True), sink)
    p = jnp.exp(s - m)
    psink = jnp.exp(sink - m)
    denom = jnp.sum(p, axis=-1, keepdims=True) + psink
    vb = jnp.concatenate([kvp_ref[N_KV_HEADS + j], kvc_ref[N_KV_HEADS + j]], axis=0)
    return qs, kb, vb, p / denom, psink / denom, rh


def _att_specs(length):
    nb = length // BLOCK
    kv_rows = 2 * N_KV_HEADS
    q_spec = pl.BlockSpec((N_Q_HEADS, BLOCK, HEAD_DIM), lambda n: (0, n, 0))
    prev = pl.BlockSpec((kv_rows, BLOCK, HEAD_DIM), lambda n: (N_Q_HEADS // kv_rows, jnp.maximum(n - 1, 0), 0))
    cur = pl.BlockSpec((kv_rows, BLOCK, HEAD_DIM), lambda n: (N_Q_HEADS // kv_rows, n, 0))
    sink_spec = pl.BlockSpec((N_KV_HEADS, Q_GROUP), lambda n: (0, 0))
    return nb, q_spec, prev, cur, sink_spec


def _att_fwd(heads, sinks, name):
    length = heads.shape[1]
    nb, q_spec, prev, cur, sink_spec = _att_specs(length)

    def body(q_ref, kvp_ref, kvc_ref, sink_ref, o_ref):
        for j in range(N_KV_HEADS):
            _, _, vb, p, _, _ = _att_scores(j, q_ref, kvp_ref, kvc_ref, sink_ref)
            o = jnp.dot(p.astype(BF16), vb, preferred_element_type=F32)
            o_ref[Q_GROUP * j:Q_GROUP * (j + 1)] = o.reshape(Q_GROUP, BLOCK, HEAD_DIM).astype(o_ref.dtype)

    return pl.pallas_call(
        body, name=name, grid=(nb,), in_specs=[q_spec, prev, cur, sink_spec], out_specs=q_spec,
        out_shape=jax.ShapeDtypeStruct((N_Q_HEADS, length, HEAD_DIM), BF16),
        compiler_params=_params("arbitrary"),
    )(heads, heads, heads, sinks)


D_KC, D_KP, D_VC, D_VP = (N_Q_HEADS + i * N_KV_HEADS for i in range(4))
N_DHEAD_ROWS = N_Q_HEADS + 4 * N_KV_HEADS


def _att_bwd(heads, sinks, do, name):
    length = heads.shape[1]
    nb, q_spec, prev, cur, sink_spec = _att_specs(length)

    def body(q_ref, kvp_ref, kvc_ref, sink_ref, do_ref, d_ref, dsink_ref):
        @pl.when(pl.program_id(0) == 0)
        def _():
            dsink_ref[...] = jnp.zeros_like(dsink_ref)

        for j in range(N_KV_HEADS):
            qs, kb, vb, p, psink, rh = _att_scores(j, q_ref, kvp_ref, kvc_ref, sink_ref)
            dob = do_ref[Q_GROUP * j:Q_GROUP * (j + 1)].reshape(Q_GROUP * BLOCK, HEAD_DIM).astype(BF16)
            dv = lax.dot_general(p.astype(BF16), dob, _DOT_DIMS["tn"], preferred_element_type=F32)
            dp = lax.dot_general(dob, vb, _DOT_DIMS["nt"], preferred_element_type=F32)
            dsum = jnp.sum(p * dp, axis=-1, keepdims=True)
            ds = (p * (dp - dsum) * (HEAD_DIM ** -0.5)).astype(BF16)
            dq = jnp.dot(ds, kb, preferred_element_type=F32)
            dk = lax.dot_general(ds, qs, _DOT_DIMS["tn"], preferred_element_type=F32)
            d_ref[Q_GROUP * j:Q_GROUP * (j + 1)] = dq.reshape(Q_GROUP, BLOCK, HEAD_DIM)
            d_ref[D_KP + j] = dk[:BLOCK]
            d_ref[D_KC + j] = dk[BLOCK:]
            d_ref[D_VP + j] = dv[:BLOCK]
            d_ref[D_VC + j] = dv[BLOCK:]
            dsr = -psink * dsum
            row = lax.broadcasted_iota(jnp.int32, (SUBLANE, LANE), 0)
            upd = jnp.zeros((SUBLANE, LANE), F32)
            for g in range(Q_GROUP):
                val = jnp.sum(jnp.where(rh == g, dsr, 0.0), keepdims=True)
                upd = jnp.where(row == g, val, upd)
            dsink_ref[j] += upd

    d_spec = pl.BlockSpec((N_DHEAD_ROWS, BLOCK, HEAD_DIM), lambda n: (0, n, 0))
    return pl.pallas_call(
        body, name=name, grid=(nb,),
        in_specs=[q_spec, prev, cur, sink_spec, q_spec],
        out_specs=[d_spec, pl.BlockSpec((N_KV_HEADS, SUBLANE, LANE), lambda n: (0, 0, 0))],
        out_shape=[jax.ShapeDtypeStruct((N_DHEAD_ROWS, length, HEAD_DIM), F32),
                   jax.ShapeDtypeStruct((N_KV_HEADS, SUBLANE, LANE), F32)],
        compiler_params=_params("arbitrary"),
    )(heads, heads, heads, sinks, do)


def _to_heads(t, heads):
    return t.reshape(t.shape[0], heads, HEAD_DIM).transpose(1, 0, 2)


def _from_heads(t):
    return t.transpose(1, 0, 2).reshape(t.shape[1], t.shape[0] * HEAD_DIM)


def _shift_down(z, s):
    t = lax.broadcasted_iota(jnp.int32, z.shape, 0)
    return jnp.where(t >= s, pltpu.roll(z, s, 0), 0.0)


def _shift_up(z, s):
    t = lax.broadcasted_iota(jnp.int32, z.shape, 0)
    return jnp.where(t < z.shape[0] - s, pltpu.roll(z, z.shape[0] - s, 0), 0.0)


def _conv_specs(length):
    col = lambda c0: pl.BlockSpec((length, LANE), lambda j, c0=c0: (0, c0 // LANE + j))
    w_spec = pl.BlockSpec((CONV_K, LANE), lambda j: (0, j))
    o_spec = pl.BlockSpec((length, LANE), lambda j: (0, j))
    return col, w_spec, o_spec


def _conv_fwd(proj, w, name):
    length = proj.shape[0]
    col, w_spec, o_spec = _conv_specs(length)

    def body(cb_ref, cc_ref, cx_ref, w_ref, o_ref):
        z = cc_ref[...] * cx_ref[...]
        s = w_ref[0:1, :] * _shift_down(z, 2) + w_ref[1:2, :] * _shift_down(z, 1) + w_ref[2:3, :] * z
        o_ref[...] = (cb_ref[...] * s).astype(o_ref.dtype)

    return pl.pallas_call(
        body, name=name, grid=(CONV_WIDTH // LANE,),
        in_specs=[col(C_CB), col(C_CC), col(C_CX), w_spec], out_specs=o_spec,
        out_shape=jax.ShapeDtypeStruct((length, CONV_WIDTH), BF16),
        compiler_params=_params("arbitrary"),
    )(proj, proj, proj, w)


def _conv_bwd(proj, w, dy, name):
    length = proj.shape[0]
    col, w_spec, o_spec = _conv_specs(length)
    dw_spec = pl.BlockSpec((1, LANE), lambda j: (0, j))

    def body(cb_ref, cc_ref, cx_ref, w_ref, dy_ref, dcb_ref, dcc_ref, dcx_ref, dw0_ref, dw1_ref, dw2_ref):
        cc, cx, dyv = cc_ref[...], cx_ref[...], dy_ref[...]
        z = cc * cx
        w0, w1, w2 = w_ref[0:1, :], w_ref[1:2, :], w_ref[2:3, :]
        z1, z2 = _shift_down(z, 1), _shift_down(z, 2)
        s = w0 * z2 + w1 * z1 + w2 * z
        dcb_ref[...] = (dyv * s).astype(dcb_ref.dtype)
        ds = dyv * cb_ref[...]
        dw0_ref[...] = jnp.sum(ds * z2, axis=0, keepdims=True)
        dw1_ref[...] = jnp.sum(ds * z1, axis=0, keepdims=True)
        dw2_ref[...] = jnp.sum(ds * z, axis=0, keepdims=True)
        dz = w2 * ds + w1 * _shift_up(ds, 1) + w0 * _shift_up(ds, 2)
        dcc_ref[...] = (dz * cx).astype(dcc_ref.dtype)
        dcx_ref[...] = (dz * cc).astype(dcx_ref.dtype)

    act = jax.ShapeDtypeStruct((length, CONV_WIDTH), BF16)
    dws = jax.ShapeDtypeStruct((1, CONV_WIDTH), F32)
    return pl.pallas_call(
        body, name=name, grid=(CONV_WIDTH // LANE,),
        in_specs=[col(C_CB), col(C_CC), col(C_CX), w_spec, o_spec],
        out_specs=[o_spec, o_spec, o_spec, dw_spec, dw_spec, dw_spec],
        out_shape=[act, act, act, dws, dws, dws],
        compiler_params=_params("arbitrary"),
    )(proj, proj, proj, w, dy)


def _cmul(ar, ai, br, bi):
    return ar * br - ai * bi, ar * bi + ai * br


def _scan_tables(lr, li, reverse):
    pr, pi = [lr], [li]
    for _ in range(SCAN_ROWS - 1):
        nr, ni = _cmul(pr[-1], pi[-1], lr, li)
        pr.append(nr)
        pi.append(ni)
    row = jnp.arange(SCAN_ROWS)[:, None]
    mr, mi = [], []
    for s in (1, 2, 4):
        live = (row + s < SCAN_ROWS) if reverse else (row >= s)
        mr.append(jnp.where(live, pr[s - 1], 0.0))
        mi.append(jnp.where(live, pi[s - 1], 0.0))
    order = range(SCAN_ROWS - 1, -1, -1) if reverse else range(SCAN_ROWS)
    carry_r = jnp.concatenate([pr[d] for d in order], axis=0)
    carry_i = jnp.concatenate([pi[d] for d in order], axis=0)
    return jnp.stack(mr), jnp.stack(mi), carry_r, carry_i


N_SSM_CHUNKS = 4
CHUNK_STATES = SSM_LANES // N_SSM_CHUNKS
CHUNK_CHANNELS = SSM_WIDTH // N_SSM_CHUNKS


def _scan(b, lr, li, reverse, name, states=None):
    length = b.shape[0]
    mr, mi, cr, ci = _scan_tables(lr, li, reverse)
    nchunk = length // SCAN_CHUNK
    nblk = SCAN_CHUNK // SCAN_ROWS
    cw = CHUNK_STATES
    with_dlam = states is not None

    def body(*refs):
        if with_dlam:
            b_ref, mr_ref, mi_ref, cr_ref, ci_ref, s_ref, sp_ref, o_ref, dl_ref, carry = refs
        else:
            b_ref, mr_ref, mi_ref, cr_ref, ci_ref, o_ref, carry = refs
        step = pl.program_id(0)

        @pl.when(step == 0)
        def _():
            carry[...] = jnp.zeros_like(carry)
            if with_dlam:
                dl_ref[...] = jnp.zeros_like(dl_ref)

        blocks = range(nblk - 1, -1, -1) if reverse else range(nblk)
        for j in range(N_SSM_CHUNKS):
            re, im = slice(2 * cw * j, 2 * cw * j + cw), slice(2 * cw * j + cw, 2 * cw * (j + 1))
            tl = slice(cw * j, cw * (j + 1))
            c_r, c_i = carry[0:1, re], carry[0:1, im]
            acc_r = acc_i = jnp.zeros((SCAN_ROWS, cw), F32)
            for blk in blocks:
                r0 = blk * SCAN_ROWS
                xr = b_ref[r0:r0 + SCAN_ROWS, re]
                xi = b_ref[r0:r0 + SCAN_ROWS, im]
                for kk, s in enumerate((1, 2, 4)):
                    sh = SCAN_ROWS - s if reverse else s
                    rr, ri = pltpu.roll(xr, sh, 0), pltpu.roll(xi, sh, 0)
                    ar, ai = _cmul(mr_ref[kk, :, tl], mi_ref[kk, :, tl], rr, ri)
                    xr, xi = xr + ar, xi + ai
                ar, ai = _cmul(cr_ref[:, tl], ci_ref[:, tl], c_r, c_i)
                xr, xi = xr + ar, xi + ai
                o_ref[r0:r0 + SCAN_ROWS, re] = xr
                o_ref[r0:r0 + SCAN_ROWS, im] = xi
                edge = r0 if reverse else r0 + SCAN_ROWS - 1
                c_r = o_ref[edge:edge + 1, re]
                c_i = o_ref[edge:edge + 1, im]
                if with_dlam:
                    if r0 > 0:
                        pr, pi = s_ref[r0 - 1:r0 + SCAN_ROWS - 1, re], s_ref[r0 - 1:r0 + SCAN_ROWS - 1, im]
                    else:
                        live = (step < nchunk - 1).astype(F32)
                        row = lax.broadcasted_iota(jnp.int32, (SCAN_ROWS, cw), 0)
                        pr = jnp.where(row == 0, sp_ref[SCAN_ROWS - 1:SCAN_ROWS, re] * live,
                                       pltpu.roll(s_ref[0:SCAN_ROWS, re], 1, 0))
                        pi = jnp.where(row == 0, sp_ref[SCAN_ROWS - 1:SCAN_ROWS, im] * live,
                                       pltpu.roll(s_ref[0:SCAN_ROWS, im], 1, 0))
                    acc_r = acc_r + xr * pr + xi * pi
                    acc_i = acc_i + xi * pr - xr * pi
            carry[0:1, re] = c_r
            carry[0:1, im] = c_i
            if with_dlam:
                dl_ref[:, re] += acc_r
                dl_ref[:, im] += acc_i

    width = 2 * SSM_LANES
    chunk = (lambda i: (nchunk - 1 - i, 0)) if reverse else (lambda i: (i, 0))
    blk_spec = pl.BlockSpec((SCAN_CHUNK, width), chunk)
    m_spec = pl.BlockSpec((3, SCAN_ROWS, SSM_LANES), lambda i: (0, 0, 0))
    c_spec = pl.BlockSpec((SCAN_ROWS, SSM_LANES), lambda i: (0, 0))
    in_specs, args = [blk_spec, m_spec, m_spec, c_spec, c_spec], [b, mr, mi, cr, ci]
    out_specs, out_shape = blk_spec, jax.ShapeDtypeStruct(b.shape, F32)
    if with_dlam:
        assert reverse
        per = SCAN_CHUNK // SCAN_ROWS
        before = pl.BlockSpec((SCAN_ROWS, width), lambda i: (jnp.maximum((nchunk - 1 - i) * per - 1, 0), 0))
        in_specs += [blk_spec, before]
        args += [states, states]
        out_specs = [blk_spec, pl.BlockSpec((SCAN_ROWS, width), lambda i: (0, 0))]
        out_shape = [out_shape, jax.ShapeDtypeStruct((SCAN_ROWS, width), F32)]
    return pl.pallas_call(
        body, name=name, grid=(nchunk,), in_specs=in_specs, out_specs=out_specs, out_shape=out_shape,
        scratch_shapes=[pltpu.VMEM((SUBLANE, width), F32)],
        compiler_params=_params("arbitrary"),
    )(*args)


def _mm_bd(a, b, mode, name, out_dtype=F32, add=None):
    nc = N_SSM_CHUNKS
    if mode == "tn":
        k, wa, wb = a.shape[0], a.shape[1] // nc, b.shape[1] // nc
        tk = _tile(k, 1024, LANE)

        def body(a_ref, b_ref, o_ref):
            @pl.when(pl.program_id(1) == 0)
            def _():
                o_ref[...] = jnp.zeros_like(o_ref)

            o_ref[...] += lax.dot_general(a_ref[...].astype(BF16), b_ref[...].astype(BF16), _DOT_DIMS["tn"],
                                          preferred_element_type=F32)

        return pl.pallas_call(
            body, name=name, grid=(nc, k // tk),
            in_specs=[pl.BlockSpec((tk, wa), lambda j, kk: (kk, j)), pl.BlockSpec((tk, wb), lambda j, kk: (kk, j))],
            out_specs=pl.BlockSpec((None, wa, wb), lambda j, kk: (j, 0, 0)),
            out_shape=jax.ShapeDtypeStruct((nc, wa, wb), F32),
            compiler_params=_params("parallel", "arbitrary"),
        )(a, b)
    m, wa = a.shape[0], a.shape[1] // nc
    wo = b.shape[2] if mode == "nn" else b.shape[1]
    tm = _tile(m, 512, LANE)

    def body(a_ref, b_ref, *rest):
        r = lax.dot_general(a_ref[...].astype(BF16), b_ref[...].astype(BF16), _DOT_DIMS[mode],
                            preferred_element_type=F32)
        if add is not None:
            r = r + rest[0][...]
        rest[-1][...] = r.astype(out_dtype)

    o_spec = pl.BlockSpec((tm, wo), lambda i, j: (i, j))
    in_specs = [pl.BlockSpec((tm, wa), lambda i, j: (i, j)), pl.BlockSpec((None,) + b.shape[1:], lambda i, j: (j, 0, 0))]
    args = [a, b]
    if add is not None:
        in_specs.append(o_spec)
        args.append(add)
    return pl.pallas_call(
        body, name=name, grid=(m // tm, nc), in_specs=in_specs, out_specs=o_spec,
        out_shape=jax.ShapeDtypeStruct((m, nc * wo), out_dtype),
        compiler_params=_params("parallel", "parallel"),
    )(*args)


def _block_diag(t):
    g, a, b = t.shape
    per = g // N_SSM_CHUNKS
    eye = jnp.eye(per, dtype=t.dtype)
    t = t.reshape(N_SSM_CHUNKS, per, a, b)
    return (t[:, :, :, None, :] * eye[None, :, None, :, None]).reshape(N_SSM_CHUNKS, per * a, per * b)


def _ssm_prep(a_re, a_im, b_re, b_im, c_re, c_im, log_dt):
    dt = jnp.exp(log_dt)[:, None]
    er = jnp.exp(a_re * dt)
    lr, li = er * jnp.cos(a_im * dt), er * jnp.sin(a_im * dt)
    nr, ni = lr - 1.0, li
    den = a_re * a_re + a_im * a_im
    qr, qi = (nr * a_re + ni * a_im) / den, (ni * a_re - nr * a_im) / den
    bbr = qr[..., None] * b_re - qi[..., None] * b_im
    bbi = qr[..., None] * b_im + qi[..., None] * b_re
    bmat = jnp.concatenate([_block_diag(bbr.transpose(0, 2, 1)), _block_diag(bbi.transpose(0, 2, 1))], axis=2)
    cmat = jnp.concatenate([_block_diag(c_re.transpose(0, 2, 1)), -_block_diag(c_im.transpose(0, 2, 1))], axis=1)
    return lr.reshape(1, SSM_LANES), li.reshape(1, SSM_LANES), bmat, cmat


def _ssm_act(yc, u, d):
    return jax.nn.gelu(yc + d * u)


def _glu(ys, z):
    return ys * jax.nn.sigmoid(z)


def _merge(ya, yc, ys, gl, b):
    gates = jax.nn.sigmoid(gl + b)
    return gates[:, :D_MODEL] * ya + gates[:, D_MODEL:2 * D_MODEL] * yc + gates[:, 2 * D_MODEL:] * ys


def _swiglu(gu):
    return jax.nn.silu(gu[:, :FFN_HIDDEN]) * gu[:, FFN_HIDDEN:]


def _loss_fn(x, g, t):
    e = _rms(x, g) - t
    per_tok = jnp.mean(e * e, axis=-1, keepdims=True)
    return 0.5 * jnp.sum(per_tok, axis=0, keepdims=True)


def _vjp_rowwise(f, n_row, cot_dtype=F32):
    def fn(ctx, *vals):
        prim = vals[:n_row] + vals[n_row + 1:]
        _, vjp = jax.vjp(f, *prim)
        return vjp(vals[n_row].astype(cot_dtype))
    return fn


def _layer_fwd(i, x, w, tabs):
    nm = lambda s: "l%d_%s" % (i, s)
    sv = {"x": x}
    h = _rms_fwd(x, w["norm_mix"][i:i + 1], nm("rms_mix"))
    proj = _mm(h, w["w_in"], "nn", nm("mm_in"), b_layer=i)
    qkv = _split_fwd(proj, tabs, nm("rope"))
    heads = _to_heads(qkv, N_HEAD_ROWS)
    sinks = w["attn_sinks"][i].reshape(N_KV_HEADS, Q_GROUP)
    att = _from_heads(_att_fwd(heads, sinks, nm("att")))
    conv = _conv_fwd(proj, w["conv_w"][i], nm("conv"))
    lr, li, bmat, cmat = w["ssm"][i]
    u = proj[:, C_U:C_G]
    bu = _mm_bd(u, bmat, "nn", nm("mm_bu"))
    states = _scan(bu, lr, li, False, nm("scan"))
    yc = _mm_bd(states, cmat, "nn", nm("mm_c"))
    d = w["ssm_d"][i:i + 1]
    ys = _rowwise(lambda ctx, a, b, c: (_ssm_act(a, b, c),), [(yc, 0, SSM_WIDTH, 0), (u, 0, SSM_WIDTH, 0)], [d],
                  [(SSM_WIDTH, F32)], [], nm("ssm_act"))[0]
    z = _mm(ys, w["w_ssm_glu"], "nn", nm("mm_glu"), b_layer=i)
    sg = _rowwise(lambda ctx, a, b: (_glu(a, b),), [(ys, 0, SSM_WIDTH, 0), (z, 0, SSM_WIDTH, 0)], [],
                  [(SSM_WIDTH, BF16)], [], nm("glu"))[0]
    ya = _mm(att, w["w_attn_o"], "nn", nm("mm_ao"), b_layer=i)
    yv = _mm(conv, w["w_conv_o"], "nn", nm("mm_co"), b_layer=i)
    ym = _mm(sg, w["w_ssm_o"], "nn", nm("mm_so"), b_layer=i)
    bg = w["b_gate"][i:i + 1]
    merged = _rowwise(lambda ctx, a, b, c, gl, bb: (_merge(a, b, c, gl, bb),),
                      [(ya, 0, D_MODEL, 0), (yv, 0, D_MODEL, 0), (ym, 0, D_MODEL, 0), (proj, C_G, GATE_WIDTH, 0)],
                      [bg], [(D_MODEL, BF16)], [], nm("merge"))[0]
    x1 = _mm(merged, w["w_mix_o"], "nn", nm("mm_mix"), add=x, b_layer=i)
    h2 = _rms_fwd(x1, w["norm_ffn"][i:i + 1], nm("rms_ffn"))
    gu = _mm(h2, w["w_ffn_in"], "nn", nm("mm_ffn_in"), b_layer=i)
    act = _rowwise(lambda ctx, a: (_swiglu(a),), [(gu, 0, 2 * FFN_HIDDEN, 0)], [], [(FFN_HIDDEN, BF16)], [],
                   nm("swiglu"))[0]
    x2 = _mm(act, w["w_ffn_out"], "nn", nm("mm_ffn_out"), add=x1, b_layer=i, tk_cap=3072)
    sv.update(h=h, proj=proj, heads=heads, att=att, conv=conv, u=u, states=states, yc=yc, ys=ys, z=z, sg=sg,
              ya=ya, yv=yv, ym=ym, merged=merged, x1=x1, h2=h2, gu=gu, act=act)
    return x2, sv


def _layer_bwd(i, dx2, sv, w, tabs, gb):
    nm = lambda s: "l%d_b_%s" % (i, s)
    g = {}

    def wgrad(n, lhs, rhs, label, **kw):
        if n in gb:
            gb[n] = _mm(lhs, rhs, "tn", nm(label), into=(gb[n], i, BIG_AXIS[n]), **kw)
        else:
            g[n] = _mm(lhs, rhs, "tn", nm(label), **kw)

    dact = _mm(dx2, w["w_ffn_out"], "nt", nm("mm_dact"), b_layer=i)
    wgrad("w_ffn_out", sv["act"], dx2, "mm_gw_ffn_out", tn_cap=512)
    dgu = _rowwise(_vjp_rowwise(_swiglu, 1), [(sv["gu"], 0, 2 * FFN_HIDDEN, 0), (dact, 0, FFN_HIDDEN, 0)], [],
                   [(2 * FFN_HIDDEN, BF16)], [], nm("swiglu"))[0]
    dh2 = _mm(dgu, w["w_ffn_in"], "nt", nm("mm_dh2"), tk_cap=3072, b_layer=i)
    wgrad("w_ffn_in", sv["h2"], dgu, "mm_gw_ffn_in")
    dx1, g["norm_ffn"] = _rms_bwd(sv["x1"], w["norm_ffn"][i:i + 1], dh2, dx2, nm("rms_ffn"))
    dmerged = _mm(dx1, w["w_mix_o"], "nt", nm("mm_dmerged"), b_layer=i)
    wgrad("w_mix_o", sv["merged"], dx1, "mm_gw_mix", tn_cap=512)
    proj = sv["proj"]
    bg = w["b_gate"][i:i + 1]
    dya, dyv, dym, dgl, g["b_gate"] = _rowwise(
        _vjp_rowwise(_merge, 4),
        [(sv["ya"], 0, D_MODEL, 0), (sv["yv"], 0, D_MODEL, 0), (sv["ym"], 0, D_MODEL, 0),
         (proj, C_G, GATE_WIDTH, 0), (dmerged, 0, D_MODEL, 0)], [bg],
        [(D_MODEL, BF16), (D_MODEL, BF16), (D_MODEL, BF16), (GATE_WIDTH, BF16)], [(1, GATE_WIDTH)], nm("merge"))
    dsg = _mm(dym, w["w_ssm_o"], "nt", nm("mm_dsg"), b_layer=i)
    wgrad("w_ssm_o", sv["sg"], dym, "mm_gw_so")
    dys0, dz = _rowwise(_vjp_rowwise(_glu, 2), [(sv["ys"], 0, SSM_WIDTH, 0), (sv["z"], 0, SSM_WIDTH, 0),
                                                 (dsg, 0, SSM_WIDTH, 0)], [],
                        [(SSM_WIDTH, F32), (SSM_WIDTH, BF16)], [], nm("glu"))
    dys = _mm(dz, w["w_ssm_glu"], "nt", nm("mm_dys"), add=dys0, b_layer=i)
    wgrad("w_ssm_glu", sv["ys"], dz, "mm_gw_glu")
    d = w["ssm_d"][i:i + 1]
    dyc, du0, g["ssm_d"] = _rowwise(
        _vjp_rowwise(_ssm_act, 2), [(sv["yc"], 0, SSM_WIDTH, 0), (sv["u"], 0, SSM_WIDTH, 0), (dys, 0, SSM_WIDTH, 0)],
        [d], [(SSM_WIDTH, F32), (SSM_WIDTH, F32)], [(1, SSM_WIDTH)], nm("ssm_act"))
    lr, li, bmat, cmat = w["ssm"][i]
    dstates = _mm_bd(dyc, cmat, "nt", nm("mm_dstates"))
    g_cmat = _mm_bd(sv["states"], dyc, "tn", nm("mm_gc"))
    gs, dl = _scan(dstates, lr, -li, True, nm("scan"), states=sv["states"])
    g_lam = jnp.sum(dl, axis=0).reshape(N_SSM_CHUNKS, 2, CHUNK_STATES)
    du = _mm_bd(gs, bmat, "nt", nm("mm_du"), out_dtype=BF16, add=du0)
    g_bmat = _mm_bd(sv["u"], gs, "tn", nm("mm_gb"))
    g["ssm"] = (g_lam[:, 0].reshape(1, SSM_LANES), g_lam[:, 1].reshape(1, SSM_LANES), g_bmat, g_cmat)
    dconv = _mm(dyv, w["w_conv_o"], "nt", nm("mm_dconv"), b_layer=i)
    wgrad("w_conv_o", sv["conv"], dyv, "mm_gw_co")
    dcb, dcc, dcx, dw0, dw1, dw2 = _conv_bwd(proj, w["conv_w"][i], dconv, nm("conv"))
    g["conv_w"] = jnp.concatenate([dw0, dw1, dw2], axis=0)
    datt = _mm(dya, w["w_attn_o"], "nt", nm("mm_datt"), b_layer=i)
    wgrad("w_attn_o", sv["att"], dya, "mm_gw_ao")
    sinks = w["attn_sinks"][i].reshape(N_KV_HEADS, Q_GROUP)
    dheads, dsk = _att_bwd(sv["heads"], sinks, _to_heads(datt, N_Q_HEADS), nm("att"))
    g["attn_sinks"] = dsk[:, :Q_GROUP, 0].reshape(N_Q_HEADS)
    dqkv = _split_bwd(_from_heads(dheads), tabs, nm("rope"))
    dproj = jnp.concatenate([dqkv, dcb, dcc, dcx, du, dgl], axis=1)
    dh = _mm(dproj, w["w_in"], "nt", nm("mm_dh"), tk_cap=3072, b_layer=i)
    dproj_shards = dproj.reshape(dproj.shape[0], N_CHIPS, IN_COLS // N_CHIPS).transpose(1, 0, 2)
    wgrad("w_in", sv["h"], dproj_shards, "mm_gw_in", b_shards=True)
    dx, g["norm_mix"] = _rms_bwd(sv["x"], w["norm_mix"][i:i + 1], dh, dx1, nm("rms_mix"))
    return dx, g


def _local_step(x, target, w):
    length = x.shape[0]
    tabs = _rope_tables(length)
    ssm_names = ("ssm_a_re", "ssm_a_im", "ssm_b_re", "ssm_b_im", "ssm_c_re", "ssm_c_im", "ssm_log_dt")
    w = dict(w)
    preps = [jax.vjp(_ssm_prep, *[w[n][i] for n in ssm_names]) for i in range(DEPTH)]
    w["ssm"] = [p[0] for p in preps]
    saved = []
    for i in range(DEPTH):
        x, sv = _layer_fwd(i, x, w, tabs)
        saved.append(sv)

    def loss_fn(ctx, xv, tv, gv):
        val, vjp = jax.vjp(_loss_fn, xv, gv, tv)
        dx, dg, _ = vjp(jnp.ones((1, 1), F32))
        return dx, dg, val + jnp.zeros((1, LANE), F32)

    gfin = w["norm_final"].reshape(1, D_MODEL)
    dx, g_final, loss = _rowwise(loss_fn, [(x, 0, D_MODEL, 0), (target, 0, D_MODEL, 0)], [gfin],
                                 [(D_MODEL, F32)], [(1, D_MODEL), (1, LANE)], "loss")
    gb = {}
    for n in BIG:
        depth, ra, cb = w[n].shape
        gb[n] = lax.empty((N_CHIPS, depth) + ((ra // N_CHIPS, cb) if BIG_AXIS[n] == 1 else (ra, cb // N_CHIPS)), F32)
    layer_grads = [None] * DEPTH
    for i in reversed(range(DEPTH)):
        dx, layer_grads[i] = _layer_bwd(i, dx, saved[i], w, tabs, gb)
    grads = dict(gb)
    for n in layer_grads[0]:
        if n != "ssm":
            grads[n] = jnp.stack([lg[n] for lg in layer_grads])
    ssm_g = [preps[i][1](layer_grads[i]["ssm"]) for i in range(DEPTH)]
    for j, n in enumerate(ssm_names):
        grads[n] = jnp.stack([sg[j] for sg in ssm_g])
    grads["norm_final"] = g_final.reshape(D_MODEL)
    for n in ("norm_mix", "norm_ffn", "b_gate", "ssm_d"):
        grads[n] = grads[n].reshape(grads[n].shape[0], -1)
    return loss, dx, grads


COLS = 1024
ANY_SPEC = pl.BlockSpec(memory_space=pl.ANY)


def _place():
    return lax.axis_index("x"), lax.axis_index("y"), lax.axis_index("c")


def _other_chips(x, y):
    return [(1 - x, y), (x, 1 - y), (1 - x, 1 - y)]


def _remote(src, dst, send_sems, recv_sems, k, to):
    return pltpu.make_async_remote_copy(src_ref=src, dst_ref=dst, send_sem=send_sems.at[k], recv_sem=recv_sems.at[k],
                                        device_id=to, device_id_type=MESH_ID)


def _comm_call(body, name, out_shape, n_sems, n_local, args):
    return pl.pallas_call(
        body, name=name, out_shape=out_shape, in_specs=[ANY_SPEC] * len(args),
        out_specs=[ANY_SPEC] * len(out_shape),
        scratch_shapes=[pltpu.SemaphoreType.DMA((n_sems,)), pltpu.SemaphoreType.DMA((n_sems,)),
                        pltpu.SemaphoreType.DMA((n_local,))],
    )(*args)


def _gather_weights(shards, name):
    nt = len(shards)
    per = 8
    hds = [s.shape[0] // 2 for s in shards]

    def body(*refs):
        srcs, outs = refs[:nt], refs[nt:2 * nt]
        send_sems, recv_sems, _ = refs[2 * nt:]
        x, y, c = _place()
        me, sibling = (x, y, c), (x, y, 1 - c)
        xn, yn, dg = _other_chips(x, y)

        def blk(t, chip, hc, part=None):
            lo, n = hc * hds[t], hds[t]
            if part is not None:
                first_n = (n + 1) // 2
                lo, n = (lo, first_n) if part == 0 else (lo + first_n, n - first_n)
            return outs[t].at[2 * chip[0] + chip[1], pl.ds(lo, n)] if n else None

        def copy(t, k, src, dst, to):
            return _remote(src, dst, send_sems, recv_sems, per * t + k, to)

        def arrived(t, k, ref):
            copy(t, k, ref, ref, me).wait_recv()

        sent = []

        def start(t, k, ref, to):
            if ref is not None:
                sent.append(copy(t, k, ref, ref, to))
                sent[-1].start()

        for t in range(nt):
            own = srcs[t].at[pl.ds(c * hds[t], hds[t])]
            for k, chip in enumerate((xn, yn)):
                sent.append(copy(t, k, own, blk(t, (x, y), c), (*chip, c)))
                sent[-1].start()
        for t in range(nt):
            arrived(t, 0, blk(t, xn, c))
            start(t, 3, blk(t, xn, c, 1), (*yn, c))
            start(t, 4, blk(t, xn, c), sibling)
            arrived(t, 1, blk(t, yn, c))
            start(t, 2, blk(t, yn, c, 0), (*xn, c))
            start(t, 5, blk(t, yn, c), sibling)
        for t in range(nt):
            arrived(t, 2, blk(t, dg, c, 0))
            start(t, 6, blk(t, dg, c, 0), sibling)
            if blk(t, dg, c, 1) is not None:
                arrived(t, 3, blk(t, dg, c, 1))
                start(t, 7, blk(t, dg, c, 1), sibling)
        for t in range(nt):
            arrived(t, 4, blk(t, xn, 1 - c))
            arrived(t, 5, blk(t, yn, 1 - c))
            arrived(t, 6, blk(t, dg, 1 - c, 0))
            if blk(t, dg, 1 - c, 1) is not None:
                arrived(t, 7, blk(t, dg, 1 - c, 1))
        for cp in sent:
            cp.wait_send()

    out_shape = [jax.ShapeDtypeStruct((N_CHIPS,) + s.shape, s.dtype) for s in shards]
    return _comm_call(body, name, out_shape, per * nt, 1, shards)


def _swap_halves(gs, name):
    nt = len(gs)
    hds = [g.shape[1] // 2 for g in gs]

    def body(*refs):
        srcs, outs = refs[:nt], refs[nt:2 * nt]
        send_sems, recv_sems, _ = refs[2 * nt:]
        x, y, c = _place()
        cps = [_remote(srcs[t].at[s, pl.ds((1 - c) * hds[t], hds[t])], outs[t].at[s], send_sems, recv_sems,
                       N_CHIPS * t + s, (x, y, 1 - c)) for t in range(nt) for s in range(N_CHIPS)]
        for cp in cps:
            cp.start()
        for cp in cps:
            cp.wait()

    out_shape = [jax.ShapeDtypeStruct((N_CHIPS, g.shape[1] // 2) + g.shape[2:], g.dtype) for g in gs]
    return _comm_call(body, name, out_shape, N_CHIPS * nt, 1, gs)


def _exchange_shards(parts, name, swap=()):
    nt, ns = len(parts), len(swap)
    hds = [g.shape[1] // 2 for g in swap]

    def body(*refs):
        srcs, swap_srcs = refs[:nt], refs[nt:nt + ns]
        outs, swap_outs = refs[nt + ns:2 * nt + ns], refs[2 * nt + ns:2 * (nt + ns)]
        send_sems, recv_sems, _ = refs[2 * (nt + ns):]
        x, y, c = _place()
        cps = [_remote(srcs[t].at[2 * chip[0] + chip[1]], outs[t].at[j], send_sems, recv_sems, 3 * t + j, (*chip, c))
               for t in range(nt) for j, chip in enumerate(_other_chips(x, y))]
        cps += [_remote(swap_srcs[t].at[s, pl.ds((1 - c) * hds[t], hds[t])], swap_outs[t].at[s], send_sems, recv_sems,
                        3 * nt + N_CHIPS * t + s, (x, y, 1 - c)) for t in range(ns) for s in range(N_CHIPS)]
        for cp in cps:
            cp.start()
        for cp in cps:
            cp.wait()

    out_shape = [jax.ShapeDtypeStruct((N_CHIPS - 1,) + p.shape[1:], p.dtype) for p in parts]
    out_shape += [jax.ShapeDtypeStruct((N_CHIPS, g.shape[1] // 2) + g.shape[2:], g.dtype) for g in swap]
    res = _comm_call(body, name, out_shape, 3 * nt + N_CHIPS * ns, 1, list(parts) + list(swap))
    return res[:nt], res[nt:]


def _join_halves(reds, name):
    nt = len(reds)
    hds = [r.shape[1] for r in reds]

    def body(*refs):
        srcs, outs = refs[:nt], refs[nt:2 * nt]
        send_sems, recv_sems, _ = refs[2 * nt:]
        x, y, c = _place()
        cps = [_remote(srcs[t].at[0], outs[t].at[pl.ds(c * hds[t], hds[t])], send_sems, recv_sems, t, (x, y, 1 - c))
               for t in range(nt)]
        for cp in cps:
            cp.start()
        for t in range(nt):
            _remote(srcs[t].at[0], outs[t].at[pl.ds((1 - c) * hds[t], hds[t])], send_sems, recv_sems, t,
                    (x, y, c)).wait_recv()
        for cp in cps:
            cp.wait_send()

    out_shape = [jax.ShapeDtypeStruct((2 * r.shape[1],) + r.shape[2:], r.dtype) for r in reds]
    return _comm_call(body, name, out_shape, nt, 1, reds)


SUM_BLOCK_ELEMS = 256 * 1024


def _sum_windows(parts, lead, name, out_dtypes=(F32,)):
    a, b = parts[0][0].shape[2:]
    ta = _tile(a, max(SUBLANE, SUM_BLOCK_ELEMS // b), 2 * SUBLANE)
    offs = jnp.stack([jnp.stack([jnp.asarray(o, jnp.int32) for o in off]) for _, off in parts])
    n_in = len(parts)

    def body(off_ref, *refs):
        acc = refs[0][...].astype(F32)
        for r in refs[1:n_in]:
            acc = acc + r[...].astype(F32)
        for r in refs[n_in:]:
            r[...] = acc.astype(r.dtype)

    in_specs = [pl.BlockSpec((1, 1, ta, b), lambda p, q, i, off, k=k: (off[k, 0] + p, off[k, 1] + q, i, 0))
                for k in range(n_in)]
    o_spec = pl.BlockSpec((1, 1, ta, b), lambda p, q, i, off: (p, q, i, 0))
    return pl.pallas_call(
        body, name=name, out_shape=[jax.ShapeDtypeStruct(tuple(lead) + (a, b), dt) for dt in out_dtypes],
        grid_spec=pltpu.PrefetchScalarGridSpec(
            num_scalar_prefetch=1, grid=tuple(lead) + (a // ta,), in_specs=in_specs,
            out_specs=[o_spec] * len(out_dtypes)),
        compiler_params=_params("arbitrary", "arbitrary", "arbitrary"),
    )(offs, *[arr for arr, _ in parts])


def _reduce_scatter(gs, names, wire, groups):
    x, y, c = _place()
    theirs = dict(zip(groups[0], _swap_halves([gs[i] for i in groups[0]], "rs_swap_halves")))
    pairs, others = {}, {}
    for k, group in enumerate(groups):
        for i in group:
            g = gs[i]
            pairs[i] = _sum_windows([(g, (0, c * (g.shape[1] // 2))), (theirs[i], (0, 0))],
                                    (N_CHIPS, g.shape[1] // 2), "rs_sum_pair_" + names[i], (F32, wire[i]))
        nxt = groups[k + 1] if k + 1 < len(groups) else []
        got, swapped = _exchange_shards([pairs[i][1] for i in group], "rs_exchange_%d" % k, [gs[i] for i in nxt])
        others.update(zip(group, got))
        theirs.update(zip(nxt, swapped))
    pairs, others = [pairs[i] for i in range(len(gs))], [others[i] for i in range(len(gs))]
    reds = [_sum_windows([(p[0], (2 * x + y, 0))] + [(o, (j, 0)) for j in range(N_CHIPS - 1)], (1, p[0].shape[1]),
                         "rs_sum_chips_" + n)[0] for p, o, n in zip(pairs, others, names)]
    joined = _join_halves(reds, "rs_join")
    return [lax.dynamic_update_slice_in_dim(j, r[0], c * r.shape[1], axis=0) for j, r in zip(joined, reds)]


def _adamw(wt, g, m, v, name):
    cols = wt.shape[-1]
    r = wt.size // cols
    tr = _tile(r, max(SUBLANE, 2 * SUM_BLOCK_ELEMS // max(cols, LANE)), SUBLANE)

    def body(w_ref, g_ref, m_ref, v_ref, d_ref, nm_ref, nv_ref):
        gv = g_ref[...]
        mn = ADAM_B1 * m_ref[...] + (1.0 - ADAM_B1) * gv
        vn = ADAM_B2 * v_ref[...] + (1.0 - ADAM_B2) * jnp.square(gv)
        m_hat = mn / (1.0 - ADAM_B1 ** ADAM_STEP)
        v_hat = vn / (1.0 - ADAM_B2 ** ADAM_STEP)
        d_ref[...] = -ADAM_LR * (m_hat / (jnp.sqrt(v_hat) + ADAM_EPS) + ADAM_WD * w_ref[...])
        nm_ref[...] = mn
        nv_ref[...] = vn

    spec = pl.BlockSpec((tr, cols), lambda i: (i, 0))
    shp = jax.ShapeDtypeStruct((r, cols), F32)
    res = pl.pallas_call(
        body, name=name, grid=(r // tr,), in_specs=[spec] * 4, out_specs=[spec] * 3, out_shape=[shp] * 3,
        compiler_params=_params("parallel"),
    )(*[t.reshape(r, cols) for t in (wt, g, m, v)])
    return [t.reshape(wt.shape) for t in res]


def _join_shards(n, piece):
    _, depth, a, b = piece.shape
    if BIG_AXIS[n] == 2:
        return piece.transpose(1, 2, 0, 3).reshape(depth, a, N_CHIPS * b)
    return piece.transpose(1, 0, 2, 3).reshape(depth, N_CHIPS * a, b)


RS_GROUPS = (("w_ffn_out", "w_mix_o", "small"), ("w_ffn_in",), ("w_in", "w_attn_o", "w_conv_o", "w_ssm_glu", "w_ssm_o"))


SMALL_PART_ROWS = 2 * SUBLANE


def _rows_of(t):
    return -(-t.size // COLS)


def _pack_small(ts):
    rows = [jnp.pad(t.reshape(-1), (0, _rows_of(t) * COLS - t.size)).reshape(-1, COLS) for t in ts]
    total = sum(r.shape[0] for r in rows)
    part = -(-total // (N_DEV * SMALL_PART_ROWS)) * SMALL_PART_ROWS
    rows.append(jnp.zeros((N_DEV * part - total, COLS), F32))
    return jnp.concatenate(rows, axis=0).reshape(N_CHIPS, 2, part, COLS)


def _unpack_small(buf, like):
    buf, out, r0 = buf.reshape(-1, COLS), [], 0
    for t in like:
        out.append(buf[r0:r0 + _rows_of(t)].reshape(-1)[:t.size].reshape(t.shape))
        r0 += _rows_of(t)
    return out


def kernel(x, norm_mix, w_in, b_gate, attn_sinks, w_attn_o, conv_w, w_conv_o, ssm_a_re, ssm_a_im, ssm_b_re, ssm_b_im, ssm_c_re, ssm_c_im, ssm_d, ssm_log_dt, w_ssm_glu, w_ssm_o, w_mix_o, norm_ffn, w_ffn_in, w_ffn_out, norm_final, loss_target, m_norm_mix, m_w_in, m_b_gate, m_attn_sinks, m_w_attn_o, m_conv_w, m_w_conv_o, m_ssm_a_re, m_ssm_a_im, m_ssm_b_re, m_ssm_b_im, m_ssm_c_re, m_ssm_c_im, m_ssm_d, m_ssm_log_dt, m_w_ssm_glu, m_w_ssm_o, m_w_mix_o, m_norm_ffn, m_w_ffn_in, m_w_ffn_out, m_norm_final, v_norm_mix, v_w_in, v_b_gate, v_attn_sinks, v_w_attn_o, v_conv_w, v_w_conv_o, v_ssm_a_re, v_ssm_a_im, v_ssm_b_re, v_ssm_b_im, v_ssm_c_re, v_ssm_c_im, v_ssm_d, v_ssm_log_dt, v_w_ssm_glu, v_w_ssm_o, v_w_mix_o, v_norm_ffn, v_w_ffn_in, v_w_ffn_out, v_norm_final):
    a = dict(zip(ARG_NAMES, (
        x, norm_mix, w_in, b_gate, attn_sinks, w_attn_o, conv_w, w_conv_o, ssm_a_re, ssm_a_im, ssm_b_re, ssm_b_im,
        ssm_c_re, ssm_c_im, ssm_d, ssm_log_dt, w_ssm_glu, w_ssm_o, w_mix_o, norm_ffn, w_ffn_in, w_ffn_out, norm_final,
        loss_target, m_norm_mix, m_w_in, m_b_gate, m_attn_sinks, m_w_attn_o, m_conv_w, m_w_conv_o, m_ssm_a_re,
        m_ssm_a_im, m_ssm_b_re, m_ssm_b_im, m_ssm_c_re, m_ssm_c_im, m_ssm_d, m_ssm_log_dt, m_w_ssm_glu, m_w_ssm_o,
        m_w_mix_o, m_norm_ffn, m_w_ffn_in, m_w_ffn_out, m_norm_final, v_norm_mix, v_w_in, v_b_gate, v_attn_sinks,
        v_w_attn_o, v_conv_w, v_w_conv_o, v_ssm_a_re, v_ssm_a_im, v_ssm_b_re, v_ssm_b_im, v_ssm_c_re, v_ssm_c_im,
        v_ssm_d, v_ssm_log_dt, v_w_ssm_glu, v_w_ssm_o, v_w_mix_o, v_norm_ffn, v_w_ffn_in, v_w_ffn_out, v_norm_final)))
    px, py, _ = _place()
    chip = 2 * px + py

    gathered = BIG + ("conv_w",)
    own = [a[n].astype(BF16) for n in BIG] + [a["conv_w"]]
    gath = _gather_weights(own, "gather_weights")
    w = {n: _join_shards(n, lax.dynamic_update_slice_in_dim(p, o[None], chip, axis=0))
         for n, p, o in zip(gathered, gath, own)}
    for n in SMALL:
        w[n] = a[n]

    loss, dx, grads = _local_step(a["x"][0], a["loss_target"][0], w)

    small_names = SMALL + ("conv_w",)
    small = _pack_small([grads[n] for n in small_names])
    rs_names = BIG + ("small",)
    groups = [[rs_names.index(n) for n in grp] for grp in RS_GROUPS]
    rs = _reduce_scatter([grads[n] for n in BIG] + [small], rs_names, (BF16,) * len(BIG) + (F32,), groups)
    red = dict(zip(BIG, rs))
    small_all = _gather_weights([rs[-1]], "gather_small")[0]
    small_all = lax.dynamic_update_slice_in_dim(small_all, rs[-1][None], chip, axis=0)
    red.update(zip(small_names, _unpack_small(small_all, [grads[n] for n in small_names])))
    lane = a["conv_w"].shape[2]
    red["conv_w"] = lax.dynamic_slice_in_dim(red["conv_w"], chip * lane, lane, axis=2)

    loss_all = lax.psum(loss[0, 0], ("x", "y", "c"))
    deltas, new_m, new_v = [], [], []
    for n in WEIGHTS:
        d, mn, vn = _adamw(a[n], red[n], a["m_" + n], a["v_" + n], "adamw_" + n)
        deltas.append(d)
        new_m.append(mn)
        new_v.append(vn)
    return (loss_all, dx[None], *[red[n] for n in WEIGHTS], *deltas, *new_m, *new_v)
```

```python
import math

import jax
import jax.numpy as jnp
from jax import lax
from jax.experimental import pallas as pl
from jax.experimental.pallas import tpu as pltpu

F32 = jnp.float32
BF16 = jnp.bfloat16

D_MODEL = 1024
DEPTH = 4
N_Q_HEADS = 8
N_KV_HEADS = 2
HEAD_DIM = 64
Q_GROUP = N_Q_HEADS // N_KV_HEADS
WINDOW = 128
BLOCK = 128
ROPE_THETA = 500000.0
ROT_DIM = HEAD_DIM // 4
ATTN_WIDTH = N_Q_HEADS * HEAD_DIM
KV_WIDTH = N_KV_HEADS * HEAD_DIM
NEG_INF = -1e30
CONV_WIDTH = 512
CONV_K = 3
SSM_WIDTH = 512
SSM_GROUP = 16
SSM_GROUPS = 32
SSM_STATE = 64
SSM_LANES = SSM_GROUPS * SSM_STATE
GATE_WIDTH = 3 * D_MODEL
FFN_HIDDEN = 2816
NORM_EPS = 1e-6
IN_COLS = 5888
C_Q, C_K, C_V, C_CB, C_CC, C_CX, C_U, C_G = 0, 512, 640, 768, 1280, 1792, 2304, 2816

ADAM_LR = 0.001
ADAM_B1 = 0.9
ADAM_B2 = 0.999
ADAM_EPS = 1e-08
ADAM_WD = 0.01
ADAM_STEP = 10

N_CHIPS = 4
N_DEV = 8
MESH_ID = pl.DeviceIdType.MESH

VMEM_LIMIT_BYTES = 48 * 1024 * 1024
LANE = 128
SUBLANE = 8
SCAN_ROWS = 8
SCAN_CHUNK = 128

BIG = ("w_in", "w_attn_o", "w_conv_o", "w_ssm_glu", "w_ssm_o", "w_mix_o", "w_ffn_in", "w_ffn_out")
BIG_AXIS = {"w_in": 2, "w_attn_o": 2, "w_conv_o": 2, "w_ssm_glu": 1, "w_ssm_o": 2, "w_mix_o": 1,
            "w_ffn_in": 2, "w_ffn_out": 1, "conv_w": 2}
SMALL = ("norm_mix", "b_gate", "attn_sinks", "ssm_a_re", "ssm_a_im", "ssm_b_re", "ssm_b_im",
         "ssm_c_re", "ssm_c_im", "ssm_d", "ssm_log_dt", "norm_ffn", "norm_final")
WEIGHTS = ("norm_mix", "w_in", "b_gate", "attn_sinks", "w_attn_o", "conv_w", "w_conv_o", "ssm_a_re",
           "ssm_a_im", "ssm_b_re", "ssm_b_im", "ssm_c_re", "ssm_c_im", "ssm_d", "ssm_log_dt",
           "w_ssm_glu", "w_ssm_o", "w_mix_o", "norm_ffn", "w_ffn_in", "w_ffn_out", "norm_final")
ARG_NAMES = ("x",) + WEIGHTS + ("loss_target",) + tuple("m_" + n for n in WEIGHTS) + tuple(
    "v_" + n for n in WEIGHTS)


def _params(*sem):
    return pltpu.CompilerParams(dimension_semantics=sem if sem else None,
                                vmem_limit_bytes=VMEM_LIMIT_BYTES)


def _tile(dim, cap, align):
    t = min(cap, dim) // align * align
    while t >= align:
        if dim % t == 0:
            return t
        t -= align
    return dim


_DOT_DIMS = {"nn": (((1,), (0,)), ((), ())), "nt": (((1,), (1,)), ((), ())), "tn": (((0,), (0,)), ((), ()))}


def _mm(a, b, mode, name, out_dtype=F32, add=None, tm_cap=512, tn_cap=3072, tk_cap=1024, b_layer=None, into=None,
        b_shards=False):
    bshape = b.shape if b_layer is None else b.shape[1:]
    if b_shards:
        assert mode == "tn" and b_layer is None and into is not None and into[2] == 2
        bshape = (b.shape[1], N_CHIPS * b.shape[2])
    if mode == "nn":
        (m, k), (k2, n) = a.shape, bshape
    elif mode == "nt":
        (m, k), (n, k2) = a.shape, bshape
    else:
        (k, m), (k2, n) = a.shape, bshape
    assert k == k2, (name, a.shape, b.shape)
    tm, tn, tk = _tile(m, tm_cap, LANE), _tile(n, tn_cap, LANE), _tile(k, tk_cap, LANE)
    if into is not None:
        buf, layer, axis = into
        _, _, ra, cb = buf.shape
        if axis == 1:
            tm = m
        elif b_shards:
            tn = cb
        else:
            tn = _tile(cb, tn_cap, LANE)
            assert cb % tn == 0 and tn % LANE == 0, (name, cb, tn)
    nk = k // tk
    dims = _DOT_DIMS[mode]

    def body(a_ref, b_ref, *rest):
        rest = list(rest)
        add_ref = rest.pop(0) if add is not None else None
        if into is not None:
            rest.pop(0)
        o_ref, acc = rest
        kk = pl.program_id(2)
        part = lax.dot_general(a_ref[...].astype(BF16), b_ref[...].astype(BF16), dims, preferred_element_type=F32)

        def finish(r):
            if add is not None:
                r = r + add_ref[...]
            o_ref[...] = r.astype(o_ref.dtype).reshape(o_ref.shape)

        if nk == 1:
            finish(part)
            return

        @pl.when(kk == 0)
        def _():
            acc[...] = part

        @pl.when(kk > 0)
        def _():
            acc[...] += part

        @pl.when(kk == nk - 1)
        def _():
            finish(acc[...])

    if mode == "tn":
        a_spec = pl.BlockSpec((tk, tm), lambda i, j, kk: (kk, i))
    else:
        a_spec = pl.BlockSpec((tm, tk), lambda i, j, kk: (i, kk))
    lead = () if b_layer is None else (None,)
    at = (lambda *ix: ix) if b_layer is None else (lambda *ix: (b_layer,) + ix)
    if b_shards:
        b_spec = pl.BlockSpec((None, tk, tn), lambda i, j, kk: (j, kk, 0))
    elif mode == "nt":
        b_spec = pl.BlockSpec(lead + (tn, tk), lambda i, j, kk: at(j, kk))
    else:
        b_spec = pl.BlockSpec(lead + (tk, tn), lambda i, j, kk: at(kk, j))
    o_spec = pl.BlockSpec((tm, tn), lambda i, j, kk: (i, j))
    in_specs, args = [a_spec, b_spec], [a, b]
    if add is not None:
        in_specs.append(o_spec)
        args.append(add)
    out_shape, aliases = jax.ShapeDtypeStruct((m, n), out_dtype), {}
    if into is not None:
        in_specs.append(pl.BlockSpec(memory_space=pl.ANY))
        aliases = {len(args): 0}
        args.append(buf)
        out_shape = jax.ShapeDtypeStruct(buf.shape, buf.dtype)
        if axis == 1:
            o_spec = pl.BlockSpec((N_CHIPS, None, ra, tn), lambda i, j, kk: (0, layer, 0, j))
        else:
            per = cb // tn
            o_spec = pl.BlockSpec((None, None, tm, tn), lambda i, j, kk: (j // per, layer, i, j % per))
    return pl.pallas_call(
        body, name=name, grid=(m // tm, n // tn, nk), in_specs=in_specs, out_specs=o_spec,
        out_shape=out_shape, input_output_aliases=aliases,
        scratch_shapes=[pltpu.VMEM((tm, tn) if nk > 1 else (SUBLANE, LANE), F32)],
        compiler_params=_params("parallel", "parallel", "arbitrary"),
    )(*args)


def _rowwise(fn, rows, pars, outs, accs, name, tm_cap=256):
    length = rows[0][0].shape[0]
    tm = _tile(length, tm_cap, LANE)
    n = length // tm
    in_specs, args, counts = [], [], []
    for arr, c0, cw, shift in rows:
        bw = math.gcd(c0, cw) if c0 else cw
        assert bw % LANE == 0 or (c0 == 0 and cw == arr.shape[1]), (name, c0, cw)
        cnt = cw // bw
        counts.append(cnt)
        for j in range(cnt):
            in_specs.append(pl.BlockSpec(
                (tm, bw), lambda i, j=j, c0=c0, bw=bw, shift=shift: (jnp.clip(i + shift, 0, n - 1), c0 // bw + j)))
            args.append(arr)
    for p in pars:
        in_specs.append(pl.BlockSpec(p.shape, lambda i: (0, 0)))
        args.append(p)
    out_shape = [jax.ShapeDtypeStruct((length, w), dt) for w, dt in outs]
    out_specs = [pl.BlockSpec((tm, w), lambda i: (i, 0)) for w, _ in outs]
    out_shape += [jax.ShapeDtypeStruct((r, w), F32) for r, w in accs]
    out_specs += [pl.BlockSpec((r, w), lambda i: (0, 0)) for r, w in accs]
    n_in, n_out = len(args), len(outs)

    def body(*refs):
        i = pl.program_id(0)
        vals, p = [], 0
        for cnt in counts:
            blocks = [refs[p + j][...] for j in range(cnt)]
            p += cnt
            vals.append(blocks[0] if cnt == 1 else jnp.concatenate(blocks, axis=1))
        for _ in pars:
            vals.append(refs[p][...])
            p += 1
        res = fn((i, n), *vals)
        out_refs = refs[n_in:n_in + n_out]
        acc_refs = refs[n_in + n_out:]
        for r, v in zip(out_refs, res[:n_out]):
            r[...] = v.astype(r.dtype)
        if acc_refs:
            @pl.when(i == 0)
            def _():
                for r in acc_refs:
                    r[...] = jnp.zeros_like(r)
            for r, v in zip(acc_refs, res[n_out:]):
                r[...] += v

    res = pl.pallas_call(
        body, name=name, grid=(n,), in_specs=in_specs, out_specs=out_specs, out_shape=out_shape,
        compiler_params=_params("arbitrary"),
    )(*args)
    return res


def _rms(x, g):
    return x * lax.rsqrt(jnp.mean(x * x, axis=-1, keepdims=True) + NORM_EPS) * g


def _rms_fwd(x, g, name):
    return _rowwise(lambda ctx, xv, gv: (_rms(xv, gv),), [(x, 0, D_MODEL, 0)], [g],
                    [(D_MODEL, BF16)], [], name)[0]


def _rms_bwd(x, g, dh, dres, name):
    def fn(ctx, xv, dhv, drv, gv):
        _, vjp = jax.vjp(_rms, xv, gv)
        dx, dg = vjp(dhv)
        return dx + drv, dg
    return _rowwise(fn, [(x, 0, D_MODEL, 0), (dh, 0, D_MODEL, 0), (dres, 0, D_MODEL, 0)], [g],
                    [(D_MODEL, F32)], [(1, D_MODEL)], name)


def _rope_tables(length):
    pos = jnp.arange(length, dtype=F32)
    inv_freq = ROPE_THETA ** (-jnp.arange(0, ROT_DIM, 2, dtype=F32) / ROT_DIM)
    ang = pos[:, None] * inv_freq[None, :]
    cos, sin = jnp.cos(ang), jnp.sin(ang)
    half = ROT_DIM // 2
    ones = jnp.ones((length, HEAD_DIM - ROT_DIM), F32)
    zeros = jnp.zeros_like(ones)
    zh = jnp.zeros((length, half), F32)
    c64 = jnp.concatenate([cos, cos, ones], axis=1)
    s1 = jnp.concatenate([-sin, zh, zeros], axis=1)
    s2 = jnp.concatenate([zh, sin, zeros], axis=1)
    tile2 = lambda t: jnp.concatenate([t, t], axis=1)
    return tile2(c64), tile2(s1), tile2(s2)


def _lane_chunks(t):
    return [t[:, j * LANE:(j + 1) * LANE] for j in range(t.shape[1] // LANE)]


def _rope(t, c, s1, s2, n_rot):
    half = ROT_DIM // 2
    out = []
    for j, ch in enumerate(_lane_chunks(t)):
        if j < n_rot:
            ch = ch * c + pltpu.roll(ch, LANE - half, 1) * s1 + pltpu.roll(ch, half, 1) * s2
        out.append(ch)
    return jnp.concatenate(out, axis=1)


def _unrope(d, c, s1, s2, n_rot):
    half = ROT_DIM // 2
    out = []
    for j, ch in enumerate(_lane_chunks(d)):
        if j < n_rot:
            ch = ch * c + pltpu.roll(ch * s1, half, 1) + pltpu.roll(ch * s2, LANE - half, 1)
        out.append(ch)
    return jnp.concatenate(out, axis=1)


N_ROT_CHUNKS = (ATTN_WIDTH + KV_WIDTH) // LANE
QKV_WIDTH = ATTN_WIDTH + 2 * KV_WIDTH


def _split_fwd(proj, tabs, name):
    def fn(ctx, t, c, s1, s2):
        return (_rope(t, c, s1, s2, N_ROT_CHUNKS),)
    rows = [(proj, 0, QKV_WIDTH, 0)] + [(t, 0, LANE, 0) for t in tabs]
    return _rowwise(fn, rows, [], [(QKV_WIDTH, BF16)], [], name, tm_cap=BLOCK)[0]


def _split_bwd(d, tabs, name):
    def fn(ctx, dqv, dkcv, dkpv, dvcv, dvpv, c, s1, s2):
        i, n = ctx
        keep = (i < n - 1).astype(F32)
        dd = jnp.concatenate([dqv, dkcv + keep * dkpv, dvcv + keep * dvpv], axis=1)
        return (_unrope(dd, c, s1, s2, N_ROT_CHUNKS),)
    col = lambda row: row * HEAD_DIM
    rows = [(d, 0, ATTN_WIDTH, 0), (d, col(D_KC), KV_WIDTH, 0), (d, col(D_KP), KV_WIDTH, 1),
            (d, col(D_VC), KV_WIDTH, 0), (d, col(D_VP), KV_WIDTH, 1)] + [(t, 0, LANE, 0) for t in tabs]
    return _rowwise(fn, rows, [], [(QKV_WIDTH, BF16)], [], name, tm_cap=BLOCK)[0]


N_HEAD_ROWS = N_Q_HEADS + 2 * N_KV_HEADS


def _att_scores(j, q_ref, kvp_ref, kvc_ref, sink_ref):
    n = pl.program_id(0)
    rows = Q_GROUP * BLOCK
    qs = q_ref[Q_GROUP * j:Q_GROUP * (j + 1)].reshape(rows, HEAD_DIM)
    kb = jnp.concatenate([kvp_ref[j], kvc_ref[j]], axis=0)
    s = lax.dot_general(qs, kb, _DOT_DIMS["nt"], preferred_element_type=F32) * (HEAD_DIM ** -0.5)
    r = lax.broadcasted_iota(jnp.int32, (rows, 2 * BLOCK), 0)
    kj = lax.broadcasted_iota(jnp.int32, (rows, 2 * BLOCK), 1)
    delta = (r % BLOCK) + BLOCK - kj
    ok = (delta >= 0) & (delta < WINDOW) & ((kj >= BLOCK) | (n > 0))
    s = jnp.where(ok, s, NEG_INF)
    rh = lax.broadcasted_iota(jnp.int32, (rows, 1), 0) // BLOCK
    sinks = sink_ref[...]
    lane = lax.broadcasted_iota(jnp.int32, sinks.shape, 1)
    srow = lax.broadcasted_iota(jnp.int32, sinks.shape, 0)
    sink = jnp.zeros((rows, 1), F32)
    for g in range(Q_GROUP):
        val = jnp.sum(jnp.where((lane == g) & (srow == j), sinks, 0.0), keepdims=True)
        sink = jnp.where(rh == g, val, sink)
    m = jnp.maximum(jnp.max(s, axis=-1, keepdims=True), sink)
    p = jnp.exp(s - m)
    psink = jnp.exp(sink - m)
    denom = jnp.sum(p, axis=-1, keepdims=True) + psink
    vb = jnp.concatenate([kvp_ref[N_KV_HEADS + j], kvc_ref[N_KV_HEADS + j]], axis=0)
    return qs, kb, vb, p / denom, psink / denom, rh


def _att_specs(length):
    nb = length // BLOCK
    kv_rows = 2 * N_KV_HEADS
    q_spec = pl.BlockSpec((N_Q_HEADS, BLOCK, HEAD_DIM), lambda n: (0, n, 0))
    prev = pl.BlockSpec((kv_rows, BLOCK, HEAD_DIM), lambda n: (N_Q_HEADS // kv_rows, jnp.maximum(n - 1, 0), 0))
    cur = pl.BlockSpec((kv_rows, BLOCK, HEAD_DIM), lambda n: (N_Q_HEADS // kv_rows, n, 0))
    sink_spec = pl.BlockSpec((N_KV_HEADS, Q_GROUP), lambda n: (0, 0))
    return nb, q_spec, prev, cur, sink_spec


def _att_fwd(heads, sinks, name):
    length = heads.shape[1]
    nb, q_spec, prev, cur, sink_spec = _att_specs(length)

    def body(q_ref, kvp_ref, kvc_ref, sink_ref, o_ref):
        for j in range(N_KV_HEADS):
            _, _, vb, p, _, _ = _att_scores(j, q_ref, kvp_ref, kvc_ref, sink_ref)
            o = jnp.dot(p.astype(BF16), vb, preferred_element_type=F32)
            o_ref[Q_GROUP * j:Q_GROUP * (j + 1)] = o.reshape(Q_GROUP, BLOCK, HEAD_DIM).astype(o_ref.dtype)

    return pl.pallas_call(
        body, name=name, grid=(nb,), in_specs=[q_spec, prev, cur, sink_spec], out_specs=q_spec,
        out_shape=jax.ShapeDtypeStruct((N_Q_HEADS, length, HEAD_DIM), BF16),
        compiler_params=_params("arbitrary"),
    )(heads, heads, heads, sinks)


D_KC, D_KP, D_VC, D_VP = (N_Q_HEADS + i * N_KV_HEADS for i in range(4))
N_DHEAD_ROWS = N_Q_HEADS + 4 * N_KV_HEADS


def _att_bwd(heads, sinks, do, name):
    length = heads.shape[1]
    nb, q_spec, prev, cur, sink_spec = _att_specs(length)

    def body(q_ref, kvp_ref, kvc_ref, sink_ref, do_ref, d_ref, dsink_ref):
        @pl.when(pl.program_id(0) == 0)
        def _():
            dsink_ref[...] = jnp.zeros_like(dsink_ref)

        for j in range(N_KV_HEADS):
            qs, kb, vb, p, psink, rh = _att_scores(j, q_ref, kvp_ref, kvc_ref, sink_ref)
            dob = do_ref[Q_GROUP * j:Q_GROUP * (j + 1)].reshape(Q_GROUP * BLOCK, HEAD_DIM).astype(BF16)
            dv = lax.dot_general(p.astype(BF16), dob, _DOT_DIMS["tn"], preferred_element_type=F32)
            dp = lax.dot_general(dob, vb, _DOT_DIMS["nt"], preferred_element_type=F32)
            dsum = jnp.sum(p * dp, axis=-1, keepdims=True)
            ds = (p * (dp - dsum) * (HEAD_DIM ** -0.5)).astype(BF16)
            dq = jnp.dot(ds, kb, preferred_element_type=F32)
            dk = lax.dot_general(ds, qs, _DOT_DIMS["tn"], preferred_element_type=F32)
            d_ref[Q_GROUP * j:Q_GROUP * (j + 1)] = dq.reshape(Q_GROUP, BLOCK, HEAD_DIM)
            d_ref[D_KP + j] = dk[:BLOCK]
            d_ref[D_KC + j] = dk[BLOCK:]
            d_ref[D_VP + j] = dv[:BLOCK]
            d_ref[D_VC + j] = dv[BLOCK:]
            dsr = -psink * dsum
            row = lax.broadcasted_iota(jnp.int32, (SUBLANE, LANE), 0)
            upd = jnp.zeros((SUBLANE, LANE), F32)
            for g in range(Q_GROUP):
                val = jnp.sum(jnp.where(rh == g, dsr, 0.0), keepdims=True)
                upd = jnp.where(row == g, val, upd)
            dsink_ref[j] += upd

    d_spec = pl.BlockSpec((N_DHEAD_ROWS, BLOCK, HEAD_DIM), lambda n: (0, n, 0))
    return pl.pallas_call(
        body, name=name, grid=(nb,),
        in_specs=[q_spec, prev, cur, sink_spec, q_spec],
        out_specs=[d_spec, pl.BlockSpec((N_KV_HEADS, SUBLANE, LANE), lambda n: (0, 0, 0))],
        out_shape=[jax.ShapeDtypeStruct((N_DHEAD_ROWS, length, HEAD_DIM), F32),
                   jax.ShapeDtypeStruct((N_KV_HEADS, SUBLANE, LANE), F32)],
        compiler_params=_params("arbitrary"),
    )(heads, heads, heads, sinks, do)


def _to_heads(t, heads):
    return t.reshape(t.shape[0], heads, HEAD_DIM).transpose(1, 0, 2)


def _from_heads(t):
    return t.transpose(1, 0, 2).reshape(t.shape[1], t.shape[0] * HEAD_DIM)


def _shift_down(z, s):
    t = lax.broadcasted_iota(jnp.int32, z.shape, 0)
    return jnp.where(t >= s, pltpu.roll(z, s, 0), 0.0)


def _shift_up(z, s):
    t = lax.broadcasted_iota(jnp.int32, z.shape, 0)
    return jnp.where(t < z.shape[0] - s, pltpu.roll(z, z.shape[0] - s, 0), 0.0)


def _conv_specs(length):
    col = lambda c0: pl.BlockSpec((length, LANE), lambda j, c0=c0: (0, c0 // LANE + j))
    w_spec = pl.BlockSpec((CONV_K, LANE), lambda j: (0, j))
    o_spec = pl.BlockSpec((length, LANE), lambda j: (0, j))
    return col, w_spec, o_spec


def _conv_fwd(proj, w, name):
    length = proj.shape[0]
    col, w_spec, o_spec = _conv_specs(length)

    def body(cb_ref, cc_ref, cx_ref, w_ref, o_ref):
        z = cc_ref[...] * cx_ref[...]
        s = w_ref[0:1, :] * _shift_down(z, 2) + w_ref[1:2, :] * _shift_down(z, 1) + w_ref[2:3, :] * z
        o_ref[...] = (cb_ref[...] * s).astype(o_ref.dtype)

    return pl.pallas_call(
        body, name=name, grid=(CONV_WIDTH // LANE,),
        in_specs=[col(C_CB), col(C_CC), col(C_CX), w_spec], out_specs=o_spec,
        out_shape=jax.ShapeDtypeStruct((length, CONV_WIDTH), BF16),
        compiler_params=_params("arbitrary"),
    )(proj, proj, proj, w)


def _conv_bwd(proj, w, dy, name):
    length = proj.shape[0]
    col, w_spec, o_spec = _conv_specs(length)
    dw_spec = pl.BlockSpec((1, LANE), lambda j: (0, j))

    def body(cb_ref, cc_ref, cx_ref, w_ref, dy_ref, dcb_ref, dcc_ref, dcx_ref, dw0_ref, dw1_ref, dw2_ref):
        cc, cx, dyv = cc_ref[...], cx_ref[...], dy_ref[...]
        z = cc * cx
        w0, w1, w2 = w_ref[0:1, :], w_ref[1:2, :], w_ref[2:3, :]
        z1, z2 = _shift_down(z, 1), _shift_down(z, 2)
        s = w0 * z2 + w1 * z1 + w2 * z
        dcb_ref[...] = (dyv * s).astype(dcb_ref.dtype)
        ds = dyv * cb_ref[...]
        dw0_ref[...] = jnp.sum(ds * z2, axis=0, keepdims=True)
        dw1_ref[...] = jnp.sum(ds * z1, axis=0, keepdims=True)
        dw2_ref[...] = jnp.sum(ds * z, axis=0, keepdims=True)
        dz = w2 * ds + w1 * _shift_up(ds, 1) + w0 * _shift_up(ds, 2)
        dcc_ref[...] = (dz * cx).astype(dcc_ref.dtype)
        dcx_ref[...] = (dz * cc).astype(dcx_ref.dtype)

    act = jax.ShapeDtypeStruct((length, CONV_WIDTH), BF16)
    dws = jax.ShapeDtypeStruct((1, CONV_WIDTH), F32)
    return pl.pallas_call(
        body, name=name, grid=(CONV_WIDTH // LANE,),
        in_specs=[col(C_CB), col(C_CC), col(C_CX), w_spec, o_spec],
        out_specs=[o_spec, o_spec, o_spec, dw_spec, dw_spec, dw_spec],
        out_shape=[act, act, act, dws, dws, dws],
        compiler_params=_params("arbitrary"),
    )(proj, proj, proj, w, dy)


def _cmul(ar, ai, br, bi):
    return ar * br - ai * bi, ar * bi + ai * br


def _scan_tables(lr, li, reverse):
    pr, pi = [lr], [li]
    for _ in range(SCAN_ROWS - 1):
        nr, ni = _cmul(pr[-1], pi[-1], lr, li)
        pr.append(nr)
        pi.append(ni)
    row = jnp.arange(SCAN_ROWS)[:, None]
    mr, mi = [], []
    for s in (1, 2, 4):
        live = (row + s < SCAN_ROWS) if reverse else (row >= s)
        mr.append(jnp.where(live, pr[s - 1], 0.0))
        mi.append(jnp.where(live, pi[s - 1], 0.0))
    order = range(SCAN_ROWS - 1, -1, -1) if reverse else range(SCAN_ROWS)
    carry_r = jnp.concatenate([pr[d] for d in order], axis=0)
    carry_i = jnp.concatenate([pi[d] for d in order], axis=0)
    return jnp.stack(mr), jnp.stack(mi), carry_r, carry_i


N_SSM_CHUNKS = 4
CHUNK_STATES = SSM_LANES // N_SSM_CHUNKS
CHUNK_CHANNELS = SSM_WIDTH // N_SSM_CHUNKS


def _scan(b, lr, li, reverse, name, states=None):
    length = b.shape[0]
    mr, mi, cr, ci = _scan_tables(lr, li, reverse)
    nchunk = length // SCAN_CHUNK
    nblk = SCAN_CHUNK // SCAN_ROWS
    cw = CHUNK_STATES
    with_dlam = states is not None

    def body(*refs):
        if with_dlam:
            b_ref, mr_ref, mi_ref, cr_ref, ci_ref, s_ref, sp_ref, o_ref, dl_ref, carry = refs
        else:
            b_ref, mr_ref, mi_ref, cr_ref, ci_ref, o_ref, carry = refs
        step = pl.program_id(0)

        @pl.when(step == 0)
        def _():
            carry[...] = jnp.zeros_like(carry)
            if with_dlam:
                dl_ref[...] = jnp.zeros_like(dl_ref)

        blocks = range(nblk - 1, -1, -1) if reverse else range(nblk)
        for j in range(N_SSM_CHUNKS):
            re, im = slice(2 * cw * j, 2 * cw * j + cw), slice(2 * cw * j + cw, 2 * cw * (j + 1))
            tl = slice(cw * j, cw * (j + 1))
            c_r, c_i = carry[0:1, re], carry[0:1, im]
            acc_r = acc_i = jnp.zeros((SCAN_ROWS, cw), F32)
            for blk in blocks:
                r0 = blk * SCAN_ROWS
                xr = b_ref[r0:r0 + SCAN_ROWS, re]
                xi = b_ref[r0:r0 + SCAN_ROWS, im]
                for kk, s in enumerate((1, 2, 4)):
                    sh = SCAN_ROWS - s if reverse else s
                    rr, ri = pltpu.roll(xr, sh, 0), pltpu.roll(xi, sh, 0)
                    ar, ai = _cmul(mr_ref[kk, :, tl], mi_ref[kk, :, tl], rr, ri)
                    xr, xi = xr + ar, xi + ai
                ar, ai = _cmul(cr_ref[:, tl], ci_ref[:, tl], c_r, c_i)
                xr, xi = xr + ar, xi + ai
                o_ref[r0:r0 + SCAN_ROWS, re] = xr
                o_ref[r0:r0 + SCAN_ROWS, im] = xi
                edge = r0 if reverse else r0 + SCAN_ROWS - 1
                c_r = o_ref[edge:edge + 1, re]
                c_i = o_ref[edge:edge + 1, im]
                if with_dlam:
                    if r0 > 0:
                        pr, pi = s_ref[r0 - 1:r0 + SCAN_ROWS - 1, re], s_ref[r0 - 1:r0 + SCAN_ROWS - 1, im]
                    else:
                        live = (step < nchunk - 1).astype(F32)
                        row = lax.broadcasted_iota(jnp.int32, (SCAN_ROWS, cw), 0)
                        pr = jnp.where(row == 0, sp_ref[SCAN_ROWS - 1:SCAN_ROWS, re] * live,
                                       pltpu.roll(s_ref[0:SCAN_ROWS, re], 1, 0))
                        pi = jnp.where(row == 0, sp_ref[SCAN_ROWS - 1:SCAN_ROWS, im] * live,
                                       pltpu.roll(s_ref[0:SCAN_ROWS, im], 1, 0))
                    acc_r = acc_r + xr * pr + xi * pi
                    acc_i = acc_i + xi * pr - xr * pi
            carry[0:1, re] = c_r
            carry[0:1, im] = c_i
            if with_dlam:
                dl_ref[:, re] += acc_r
                dl_ref[:, im] += acc_i

    width = 2 * SSM_LANES
    chunk = (lambda i: (nchunk - 1 - i, 0)) if reverse else (lambda i: (i, 0))
    blk_spec = pl.BlockSpec((SCAN_CHUNK, width), chunk)
    m_spec = pl.BlockSpec((3, SCAN_ROWS, SSM_LANES), lambda i: (0, 0, 0))
    c_spec = pl.BlockSpec((SCAN_ROWS, SSM_LANES), lambda i: (0, 0))
    in_specs, args = [blk_spec, m_spec, m_spec, c_spec, c_spec], [b, mr, mi, cr, ci]
    out_specs, out_shape = blk_spec, jax.ShapeDtypeStruct(b.shape, F32)
    if with_dlam:
        assert reverse
        per = SCAN_CHUNK // SCAN_ROWS
        before = pl.BlockSpec((SCAN_ROWS, width), lambda i: (jnp.maximum((nchunk - 1 - i) * per - 1, 0), 0))
        in_specs += [blk_spec, before]
        args += [states, states]
        out_specs = [blk_spec, pl.BlockSpec((SCAN_ROWS, width), lambda i: (0, 0))]
        out_shape = [out_shape, jax.ShapeDtypeStruct((SCAN_ROWS, width), F32)]
    return pl.pallas_call(
        body, name=name, grid=(nchunk,), in_specs=in_specs, out_specs=out_specs, out_shape=out_shape,
        scratch_shapes=[pltpu.VMEM((SUBLANE, width), F32)],
        compiler_params=_params("arbitrary"),
    )(*args)


def _mm_bd(a, b, mode, name, out_dtype=F32, add=None):
    nc = N_SSM_CHUNKS
    if mode == "tn":
        k, wa, wb = a.shape[0], a.shape[1] // nc, b.shape[1] // nc
        tk = _tile(k, 1024, LANE)

        def body(a_ref, b_ref, o_ref):
            @pl.when(pl.program_id(1) == 0)
            def _():
                o_ref[...] = jnp.zeros_like(o_ref)

            o_ref[...] += lax.dot_general(a_ref[...].astype(BF16), b_ref[...].astype(BF16), _DOT_DIMS["tn"],
                                          preferred_element_type=F32)

        return pl.pallas_call(
            body, name=name, grid=(nc, k // tk),
            in_specs=[pl.BlockSpec((tk, wa), lambda j, kk: (kk, j)), pl.BlockSpec((tk, wb), lambda j, kk: (kk, j))],
            out_specs=pl.BlockSpec((None, wa, wb), lambda j, kk: (j, 0, 0)),
            out_shape=jax.ShapeDtypeStruct((nc, wa, wb), F32),
            compiler_params=_params("parallel", "arbitrary"),
        )(a, b)
    m, wa = a.shape[0], a.shape[1] // nc
    wo = b.shape[2] if mode == "nn" else b.shape[1]
    tm = _tile(m, 512, LANE)

    def body(a_ref, b_ref, *rest):
        r = lax.dot_general(a_ref[...].astype(BF16), b_ref[...].astype(BF16), _DOT_DIMS[mode],
                            preferred_element_type=F32)
        if add is not None:
            r = r + rest[0][...]
        rest[-1][...] = r.astype(out_dtype)

    o_spec = pl.BlockSpec((tm, wo), lambda i, j: (i, j))
    in_specs = [pl.BlockSpec((tm, wa), lambda i, j: (i, j)), pl.BlockSpec((None,) + b.shape[1:], lambda i, j: (j, 0, 0))]
    args = [a, b]
    if add is not None:
        in_specs.append(o_spec)
        args.append(add)
    return pl.pallas_call(
        body, name=name, grid=(m // tm, nc), in_specs=in_specs, out_specs=o_spec,
        out_shape=jax.ShapeDtypeStruct((m, nc * wo), out_dtype),
        compiler_params=_params("parallel", "parallel"),
    )(*args)


def _block_diag(t):
    g, a, b = t.shape
    per = g // N_SSM_CHUNKS
    eye = jnp.eye(per, dtype=t.dtype)
    t = t.reshape(N_SSM_CHUNKS, per, a, b)
    return (t[:, :, :, None, :] * eye[None, :, None, :, None]).reshape(N_SSM_CHUNKS, per * a, per * b)


def _ssm_prep(a_re, a_im, b_re, b_im, c_re, c_im, log_dt):
    dt = jnp.exp(log_dt)[:, None]
    er = jnp.exp(a_re * dt)
    lr, li = er * jnp.cos(a_im * dt), er * jnp.sin(a_im * dt)
    nr, ni = lr - 1.0, li
    den = a_re * a_re + a_im * a_im
    qr, qi = (nr * a_re + ni * a_im) / den, (ni * a_re - nr * a_im) / den
    bbr = qr[..., None] * b_re - qi[..., None] * b_im
    bbi = qr[..., None] * b_im + qi[..., None] * b_re
    bmat = jnp.concatenate([_block_diag(bbr.transpose(0, 2, 1)), _block_diag(bbi.transpose(0, 2, 1))], axis=2)
    cmat = jnp.concatenate([_block_diag(c_re.transpose(0, 2, 1)), -_block_diag(c_im.transpose(0, 2, 1))], axis=1)
    return lr.reshape(1, SSM_LANES), li.reshape(1, SSM_LANES), bmat, cmat


def _ssm_act(yc, u, d):
    return jax.nn.gelu(yc + d * u)


def _glu(ys, z):
    return ys * jax.nn.sigmoid(z)


def _merge(ya, yc, ys, gl, b):
    gates = jax.nn.sigmoid(gl + b)
    return gates[:, :D_MODEL] * ya + gates[:, D_MODEL:2 * D_MODEL] * yc + gates[:, 2 * D_MODEL:] * ys


def _swiglu(gt, up):
    return jax.nn.silu(gt) * up


def _mm_swiglu(a, b, layer, name):
    m, k = a.shape
    hid = b.shape[2] // 2
    tm, tn = _tile(m, 512, LANE), _tile(hid, 1536, LANE)
    nj = hid // tn

    def body(a_ref, bg_ref, bu_ref, g_ref, u_ref, act_ref):
        av = a_ref[...].astype(BF16)
        gt = jnp.dot(av, bg_ref[...].astype(BF16), preferred_element_type=F32)
        up = jnp.dot(av, bu_ref[...].astype(BF16), preferred_element_type=F32)
        g_ref[...] = gt
        u_ref[...] = up
        act_ref[...] = _swiglu(gt, up).astype(act_ref.dtype)

    o_spec = pl.BlockSpec((tm, tn), lambda i, j: (i, j))
    half = jax.ShapeDtypeStruct((m, hid), F32)
    return pl.pallas_call(
        body, name=name, grid=(m // tm, nj),
        in_specs=[pl.BlockSpec((tm, k), lambda i, j: (i, 0)),
                  pl.BlockSpec((None, k, tn), lambda i, j: (layer, 0, j)),
                  pl.BlockSpec((None, k, tn), lambda i, j: (layer, 0, j + nj))],
        out_specs=[o_spec, o_spec, o_spec], out_shape=[half, half, jax.ShapeDtypeStruct((m, hid), BF16)],
        compiler_params=_params("parallel", "parallel"),
    )(a, b, b)


def _loss_fn(x, g, t):
    e = _rms(x, g) - t
    per_tok = jnp.mean(e * e, axis=-1, keepdims=True)
    return 0.5 * jnp.sum(per_tok, axis=0, keepdims=True)


def _vjp_rowwise(f, n_row, cot_dtype=F32):
    def fn(ctx, *vals):
        prim = vals[:n_row] + vals[n_row + 1:]
        _, vjp = jax.vjp(f, *prim)
        return vjp(vals[n_row].astype(cot_dtype))
    return fn


def _layer_fwd(i, x, w, tabs):
    nm = lambda s: "l%d_%s" % (i, s)
    sv = {"x": x}
    h = _rms_fwd(x, w["norm_mix"][i:i + 1], nm("rms_mix"))
    proj = _mm(h, w["w_in"], "nn", nm("mm_in"), b_layer=i)
    qkv = _split_fwd(proj, tabs, nm("rope"))
    heads = _to_heads(qkv, N_HEAD_ROWS)
    sinks = w["attn_sinks"][i].reshape(N_KV_HEADS, Q_GROUP)
    att = _from_heads(_att_fwd(heads, sinks, nm("att")))
    conv = _conv_fwd(proj, w["conv_w"][i], nm("conv"))
    lr, li, bmat, cmat = w["ssm"][i]
    u = proj[:, C_U:C_G]
    bu = _mm_bd(u, bmat, "nn", nm("mm_bu"))
    states = _scan(bu, lr, li, False, nm("scan"))
    yc = _mm_bd(states, cmat, "nn", nm("mm_c"))
    d = w["ssm_d"][i:i + 1]
    ys = _rowwise(lambda ctx, a, b, c: (_ssm_act(a, b, c),), [(yc, 0, SSM_WIDTH, 0), (u, 0, SSM_WIDTH, 0)], [d],
                  [(SSM_WIDTH, F32)], [], nm("ssm_act"))[0]
    z = _mm(ys, w["w_ssm_glu"], "nn", nm("mm_glu"), b_layer=i)
    sg = _rowwise(lambda ctx, a, b: (_glu(a, b),), [(ys, 0, SSM_WIDTH, 0), (z, 0, SSM_WIDTH, 0)], [],
                  [(SSM_WIDTH, BF16)], [], nm("glu"))[0]
    ya = _mm(att, w["w_attn_o"], "nn", nm("mm_ao"), b_layer=i)
    yv = _mm(conv, w["w_conv_o"], "nn", nm("mm_co"), b_layer=i)
    ym = _mm(sg, w["w_ssm_o"], "nn", nm("mm_so"), b_layer=i)
    bg = w["b_gate"][i:i + 1]
    merged = _rowwise(lambda ctx, a, b, c, gl, bb: (_merge(a, b, c, gl, bb),),
                      [(ya, 0, D_MODEL, 0), (yv, 0, D_MODEL, 0), (ym, 0, D_MODEL, 0), (proj, C_G, GATE_WIDTH, 0)],
                      [bg], [(D_MODEL, BF16)], [], nm("merge"))[0]
    x1 = _mm(merged, w["w_mix_o"], "nn", nm("mm_mix"), add=x, b_layer=i)
    h2 = _rms_fwd(x1, w["norm_ffn"][i:i + 1], nm("rms_ffn"))
    gt, up, act = _mm_swiglu(h2, w["w_ffn_in"], i, nm("mm_ffn_in"))
    x2 = _mm(act, w["w_ffn_out"], "nn", nm("mm_ffn_out"), add=x1, b_layer=i, tk_cap=3072)
    sv.update(h=h, proj=proj, heads=heads, att=att, conv=conv, u=u, states=states, yc=yc, ys=ys, z=z, sg=sg,
              ya=ya, yv=yv, ym=ym, merged=merged, x1=x1, h2=h2, gt=gt, up=up, act=act)
    return x2, sv


def _layer_bwd(i, dx2, sv, w, tabs, gb):
    nm = lambda s: "l%d_b_%s" % (i, s)
    g = {}

    def wgrad(n, lhs, rhs, label, **kw):
        if n in gb:
            gb[n] = _mm(lhs, rhs, "tn", nm(label), into=(gb[n], i, BIG_AXIS[n]), **kw)
        else:
            g[n] = _mm(lhs, rhs, "tn", nm(label), **kw)

    dact = _mm(dx2, w["w_ffn_out"], "nt", nm("mm_dact"), b_layer=i)
    wgrad("w_ffn_out", sv["act"], dx2, "mm_gw_ffn_out", tn_cap=512)
    def swiglu_bwd(ctx, gtv, upv, dactv):
        _, vjp = jax.vjp(_swiglu, gtv, upv)
        return (jnp.concatenate(vjp(dactv), axis=1),)

    dgu = _rowwise(swiglu_bwd, [(sv["gt"], 0, FFN_HIDDEN, 0), (sv["up"], 0, FFN_HIDDEN, 0),
                                (dact, 0, FFN_HIDDEN, 0)], [], [(2 * FFN_HIDDEN, BF16)], [], nm("swiglu"))[0]
    dh2 = _mm(dgu, w["w_ffn_in"], "nt", nm("mm_dh2"), tk_cap=3072, b_layer=i)
    wgrad("w_ffn_in", sv["h2"], dgu, "mm_gw_ffn_in")
    dx1, g["norm_ffn"] = _rms_bwd(sv["x1"], w["norm_ffn"][i:i + 1], dh2, dx2, nm("rms_ffn"))
    dmerged = _mm(dx1, w["w_mix_o"], "nt", nm("mm_dmerged"), b_layer=i)
    wgrad("w_mix_o", sv["merged"], dx1, "mm_gw_mix", tn_cap=512)
    proj = sv["proj"]
    bg = w["b_gate"][i:i + 1]
    dya, dyv, dym, dgl, g["b_gate"] = _rowwise(
        _vjp_rowwise(_merge, 4),
        [(sv["ya"], 0, D_MODEL, 0), (sv["yv"], 0, D_MODEL, 0), (sv["ym"], 0, D_MODEL, 0),
         (proj, C_G, GATE_WIDTH, 0), (dmerged, 0, D_MODEL, 0)], [bg],
        [(D_MODEL, BF16), (D_MODEL, BF16), (D_MODEL, BF16), (GATE_WIDTH, BF16)], [(1, GATE_WIDTH)], nm("merge"))
    dsg = _mm(dym, w["w_ssm_o"], "nt", nm("mm_dsg"), b_layer=i)
    wgrad("w_ssm_o", sv["sg"], dym, "mm_gw_so")
    dys0, dz = _rowwise(_vjp_rowwise(_glu, 2), [(sv["ys"], 0, SSM_WIDTH, 0), (sv["z"], 0, SSM_WIDTH, 0),
                                                 (dsg, 0, SSM_WIDTH, 0)], [],
                        [(SSM_WIDTH, F32), (SSM_WIDTH, BF16)], [], nm("glu"))
    dys = _mm(dz, w["w_ssm_glu"], "nt", nm("mm_dys"), add=dys0, b_layer=i)
    wgrad("w_ssm_glu", sv["ys"], dz, "mm_gw_glu")
    d = w["ssm_d"][i:i + 1]
    dyc, du0, g["ssm_d"] = _rowwise(
        _vjp_rowwise(_ssm_act, 2), [(sv["yc"], 0, SSM_WIDTH, 0), (sv["u"], 0, SSM_WIDTH, 0), (dys, 0, SSM_WIDTH, 0)],
        [d], [(SSM_WIDTH, F32), (SSM_WIDTH, F32)], [(1, SSM_WIDTH)], nm("ssm_act"))
    lr, li, bmat, cmat = w["ssm"][i]
    dstates = _mm_bd(dyc, cmat, "nt", nm("mm_dstates"))
    g_cmat = _mm_bd(sv["states"], dyc, "tn", nm("mm_gc"))
    gs, dl = _scan(dstates, lr, -li, True, nm("scan"), states=sv["states"])
    g_lam = jnp.sum(dl, axis=0).reshape(N_SSM_CHUNKS, 2, CHUNK_STATES)
    du = _mm_bd(gs, bmat, "nt", nm("mm_du"), out_dtype=BF16, add=du0)
    g_bmat = _mm_bd(sv["u"], gs, "tn", nm("mm_gb"))
    g["ssm"] = (g_lam[:, 0].reshape(1, SSM_LANES), g_lam[:, 1].reshape(1, SSM_LANES), g_bmat, g_cmat)
    dconv = _mm(dyv, w["w_conv_o"], "nt", nm("mm_dconv"), b_layer=i)
    wgrad("w_conv_o", sv["conv"], dyv, "mm_gw_co")
    dcb, dcc, dcx, dw0, dw1, dw2 = _conv_bwd(proj, w["conv_w"][i], dconv, nm("conv"))
    g["conv_w"] = jnp.concatenate([dw0, dw1, dw2], axis=0)
    datt = _mm(dya, w["w_attn_o"], "nt", nm("mm_datt"), b_layer=i)
    wgrad("w_attn_o", sv["att"], dya, "mm_gw_ao")
    sinks = w["attn_sinks"][i].reshape(N_KV_HEADS, Q_GROUP)
    dheads, dsk = _att_bwd(sv["heads"], sinks, _to_heads(datt, N_Q_HEADS), nm("att"))
    g["attn_sinks"] = dsk[:, :Q_GROUP, 0].reshape(N_Q_HEADS)
    dqkv = _split_bwd(_from_heads(dheads), tabs, nm("rope"))
    dproj = jnp.concatenate([dqkv, dcb, dcc, dcx, du, dgl], axis=1)
    dh = _mm(dproj, w["w_in"], "nt", nm("mm_dh"), tk_cap=3072, b_layer=i)
    dproj_shards = dproj.reshape(dproj.shape[0], N_CHIPS, IN_COLS // N_CHIPS).transpose(1, 0, 2)
    wgrad("w_in", sv["h"], dproj_shards, "mm_gw_in", b_shards=True)
    dx, g["norm_mix"] = _rms_bwd(sv["x"], w["norm_mix"][i:i + 1], dh, dx1, nm("rms_mix"))
    return dx, g


def _local_step(x, target, w):
    length = x.shape[0]
    tabs = _rope_tables(length)
    ssm_names = ("ssm_a_re", "ssm_a_im", "ssm_b_re", "ssm_b_im", "ssm_c_re", "ssm_c_im", "ssm_log_dt")
    w = dict(w)
    preps = [jax.vjp(_ssm_prep, *[w[n][i] for n in ssm_names]) for i in range(DEPTH)]
    w["ssm"] = [p[0] for p in preps]
    saved = []
    for i in range(DEPTH):
        x, sv = _layer_fwd(i, x, w, tabs)
        saved.append(sv)

    def loss_fn(ctx, xv, tv, gv):
        val, vjp = jax.vjp(_loss_fn, xv, gv, tv)
        dx, dg, _ = vjp(jnp.ones((1, 1), F32))
        return dx, dg, val + jnp.zeros((1, LANE), F32)

    gfin = w["norm_final"].reshape(1, D_MODEL)
    dx, g_final, loss = _rowwise(loss_fn, [(x, 0, D_MODEL, 0), (target, 0, D_MODEL, 0)], [gfin],
                                 [(D_MODEL, F32)], [(1, D_MODEL), (1, LANE)], "loss")
    gb = {}
    for n in BIG:
        depth, ra, cb = w[n].shape
        gb[n] = lax.empty((N_CHIPS, depth) + ((ra // N_CHIPS, cb) if BIG_AXIS[n] == 1 else (ra, cb // N_CHIPS)), F32)
    layer_grads = [None] * DEPTH
    for i in reversed(range(DEPTH)):
        dx, layer_grads[i] = _layer_bwd(i, dx, saved[i], w, tabs, gb)
    grads = dict(gb)
    for n in layer_grads[0]:
        if n != "ssm":
            grads[n] = jnp.stack([lg[n] for lg in layer_grads])
    ssm_g = [preps[i][1](layer_grads[i]["ssm"]) for i in range(DEPTH)]
    for j, n in enumerate(ssm_names):
        grads[n] = jnp.stack([sg[j] for sg in ssm_g])
    grads["norm_final"] = g_final.reshape(D_MODEL)
    for n in ("norm_mix", "norm_ffn", "b_gate", "ssm_d"):
        grads[n] = grads[n].reshape(grads[n].shape[0], -1)
    return loss, dx, grads


COLS = 1024
ANY_SPEC = pl.BlockSpec(memory_space=pl.ANY)


def _place():
    return lax.axis_index("x"), lax.axis_index("y"), lax.axis_index("c")


def _other_chips(x, y):
    return [(1 - x, y), (x, 1 - y), (1 - x, 1 - y)]


def _remote(src, dst, send_sems, recv_sems, k, to):
    return pltpu.make_async_remote_copy(src_ref=src, dst_ref=dst, send_sem=send_sems.at[k], recv_sem=recv_sems.at[k],
                                        device_id=to, device_id_type=MESH_ID)


def _comm_call(body, name, out_shape, n_sems, n_local, args):
    return pl.pallas_call(
        body, name=name, out_shape=out_shape, in_specs=[ANY_SPEC] * len(args),
        out_specs=[ANY_SPEC] * len(out_shape),
        scratch_shapes=[pltpu.SemaphoreType.DMA((n_sems,)), pltpu.SemaphoreType.DMA((n_sems,)),
                        pltpu.SemaphoreType.DMA((n_local,))],
    )(*args)


def _gather_weights(shards, name):
    nt = len(shards)
    per = 9
    hds = [s.shape[0] // 2 for s in shards]

    def body(*refs):
        srcs, outs = refs[:nt], refs[nt:2 * nt]
        send_sems, recv_sems, _ = refs[2 * nt:]
        x, y, c = _place()
        me, sibling = (x, y, c), (x, y, 1 - c)
        xn, yn, dg = _other_chips(x, y)

        def blk(t, chip, hc, part=None):
            lo, n = hc * hds[t], hds[t]
            if part is not None:
                first_n = (n + 1) // 2
                lo, n = (lo, first_n) if part == 0 else (lo + first_n, n - first_n)
            return outs[t].at[2 * chip[0] + chip[1], pl.ds(lo, n)] if n else None

        def copy(t, k, src, dst, to):
            return _remote(src, dst, send_sems, recv_sems, per * t + k, to)

        def arrived(t, k, ref):
            copy(t, k, ref, ref, me).wait_recv()

        sent = []

        def start(t, k, ref, to):
            if ref is not None:
                sent.append(copy(t, k, ref, ref, to))
                sent[-1].start()

        for t in range(nt):
            own = srcs[t].at[pl.ds(c * hds[t], hds[t])]
            for k, chip in enumerate((xn, yn)):
                sent.append(copy(t, k, own, blk(t, (x, y), c), (*chip, c)))
                sent[-1].start()
        for t in range(nt):
            sent.append(copy(t, 8, srcs[t], outs[t].at[2 * x + y], sibling))
            sent[-1].start()
        for t in range(nt):
            arrived(t, 0, blk(t, xn, c))
            start(t, 3, blk(t, xn, c, 1), (*yn, c))
            start(t, 4, blk(t, xn, c), sibling)
            arrived(t, 1, blk(t, yn, c))
            start(t, 2, blk(t, yn, c, 0), (*xn, c))
            start(t, 5, blk(t, yn, c), sibling)
        for t in range(nt):
            arrived(t, 2, blk(t, dg, c, 0))
            start(t, 6, blk(t, dg, c, 0), sibling)
            if blk(t, dg, c, 1) is not None:
                arrived(t, 3, blk(t, dg, c, 1))
                start(t, 7, blk(t, dg, c, 1), sibling)
        for t in range(nt):
            arrived(t, 4, blk(t, xn, 1 - c))
            arrived(t, 5, blk(t, yn, 1 - c))
            arrived(t, 6, blk(t, dg, 1 - c, 0))
            if blk(t, dg, 1 - c, 1) is not None:
                arrived(t, 7, blk(t, dg, 1 - c, 1))
            arrived(t, 8, outs[t].at[2 * x + y])
        for cp in sent:
            cp.wait_send()

    out_shape = [jax.ShapeDtypeStruct((N_CHIPS,) + s.shape, s.dtype) for s in shards]
    return _comm_call(body, name, out_shape, per * nt, 1, shards)


def _swap_halves(gs, name):
    nt = len(gs)
    hds = [g.shape[1] // 2 for g in gs]

    def body(*refs):
        srcs, outs = refs[:nt], refs[nt:2 * nt]
        send_sems, recv_sems, _ = refs[2 * nt:]
        x, y, c = _place()
        cps = [_remote(srcs[t].at[s, pl.ds((1 - c) * hds[t], hds[t])], outs[t].at[s], send_sems, recv_sems,
                       N_CHIPS * t + s, (x, y, 1 - c)) for t in range(nt) for s in range(N_CHIPS)]
        for cp in cps:
            cp.start()
        for cp in cps:
            cp.wait()

    out_shape = [jax.ShapeDtypeStruct((N_CHIPS, g.shape[1] // 2) + g.shape[2:], g.dtype) for g in gs]
    return _comm_call(body, name, out_shape, N_CHIPS * nt, 1, gs)


def _exchange_shards(parts, name, swap=()):
    nt, ns = len(parts), len(swap)
    hds = [g.shape[1] // 2 for g in swap]

    def body(*refs):
        srcs, swap_srcs = refs[:nt], refs[nt:nt + ns]
        outs, swap_outs = refs[nt + ns:2 * nt + ns], refs[2 * nt + ns:2 * (nt + ns)]
        send_sems, recv_sems, _ = refs[2 * (nt + ns):]
        x, y, c = _place()
        cps = [_remote(srcs[t].at[2 * chip[0] + chip[1]], outs[t].at[j], send_sems, recv_sems, 3 * t + j, (*chip, c))
               for t in range(nt) for j, chip in enumerate(_other_chips(x, y))]
        cps += [_remote(swap_srcs[t].at[s, pl.ds((1 - c) * hds[t], hds[t])], swap_outs[t].at[s], send_sems, recv_sems,
                        3 * nt + N_CHIPS * t + s, (x, y, 1 - c)) for t in range(ns) for s in range(N_CHIPS)]
        for cp in cps:
            cp.start()
        for cp in cps:
            cp.wait()

    out_shape = [jax.ShapeDtypeStruct((N_CHIPS - 1,) + p.shape[1:], p.dtype) for p in parts]
    out_shape += [jax.ShapeDtypeStruct((N_CHIPS, g.shape[1] // 2) + g.shape[2:], g.dtype) for g in swap]
    res = _comm_call(body, name, out_shape, 3 * nt + N_CHIPS * ns, 1, list(parts) + list(swap))
    return res[:nt], res[nt:]


def _join_halves(reds, name):
    nt = len(reds)
    hds = [r.shape[1] for r in reds]

    def body(*refs):
        srcs, outs = refs[:nt], refs[nt:2 * nt]
        send_sems, recv_sems, _ = refs[2 * nt:]
        x, y, c = _place()
        cps = [_remote(srcs[t].at[0], outs[t].at[pl.ds(c * hds[t], hds[t])], send_sems, recv_sems, t, (x, y, 1 - c))
               for t in range(nt)]
        for cp in cps:
            cp.start()
        for t in range(nt):
            _remote(srcs[t].at[0], outs[t].at[pl.ds((1 - c) * hds[t], hds[t])], send_sems, recv_sems, t,
                    (x, y, c)).wait_recv()
        for cp in cps:
            cp.wait_send()

    out_shape = [jax.ShapeDtypeStruct((2 * r.shape[1],) + r.shape[2:], r.dtype) for r in reds]
    return _comm_call(body, name, out_shape, nt, 1, reds)


SUM_BLOCK_ELEMS = 768 * 1024
ADAMW_BLOCK_ELEMS = 512 * 1024


def _sum_windows(parts, lead, name, out_dtypes=(F32,)):
    a, b = parts[0][0].shape[2:]
    ta = _tile(a, max(SUBLANE, SUM_BLOCK_ELEMS // b), 2 * SUBLANE)
    offs = jnp.stack([jnp.stack([jnp.asarray(o, jnp.int32) for o in off]) for _, off in parts])
    n_in = len(parts)

    def body(off_ref, *refs):
        acc = refs[0][...].astype(F32)
        for r in refs[1:n_in]:
            acc = acc + r[...].astype(F32)
        for r in refs[n_in:]:
            r[...] = acc.astype(r.dtype)

    in_specs = [pl.BlockSpec((1, 1, ta, b), lambda p, q, i, off, k=k: (off[k, 0] + p, off[k, 1] + q, i, 0))
                for k in range(n_in)]
    o_spec = pl.BlockSpec((1, 1, ta, b), lambda p, q, i, off: (p, q, i, 0))
    return pl.pallas_call(
        body, name=name, out_shape=[jax.ShapeDtypeStruct(tuple(lead) + (a, b), dt) for dt in out_dtypes],
        grid_spec=pltpu.PrefetchScalarGridSpec(
            num_scalar_prefetch=1, grid=tuple(lead) + (a // ta,), in_specs=in_specs,
            out_specs=[o_spec] * len(out_dtypes)),
        compiler_params=_params("arbitrary", "arbitrary", "arbitrary"),
    )(offs, *[arr for arr, _ in parts])


def _reduce_scatter(gs, names, wire, groups):
    x, y, c = _place()
    theirs = dict(zip(groups[0], _swap_halves([gs[i] for i in groups[0]], "rs_swap_halves")))
    pairs, others = {}, {}
    for k, group in enumerate(groups):
        for i in group:
            g = gs[i]
            pairs[i] = _sum_windows([(g, (0, c * (g.shape[1] // 2))), (theirs[i], (0, 0))],
                                    (N_CHIPS, g.shape[1] // 2), "rs_sum_pair_" + names[i], (F32, wire[i]))
        nxt = groups[k + 1] if k + 1 < len(groups) else []
        got, swapped = _exchange_shards([pairs[i][1] for i in group], "rs_exchange_%d" % k, [gs[i] for i in nxt])
        others.update(zip(group, got))
        theirs.update(zip(nxt, swapped))
    pairs, others = [pairs[i] for i in range(len(gs))], [others[i] for i in range(len(gs))]
    reds = [_sum_windows([(p[0], (2 * x + y, 0))] + [(o, (j, 0)) for j in range(N_CHIPS - 1)], (1, p[0].shape[1]),
                         "rs_sum_chips_" + n)[0] for p, o, n in zip(pairs, others, names)]
    joined = _join_halves(reds, "rs_join")
    return [lax.dynamic_update_slice_in_dim(j, r[0], c * r.shape[1], axis=0) for j, r in zip(joined, reds)]


def _adamw(wt, g, m, v, name):
    cols = wt.shape[-1]
    r = wt.size // cols
    tr = _tile(r, max(SUBLANE, ADAMW_BLOCK_ELEMS // max(cols, LANE)), SUBLANE)

    def body(w_ref, g_ref, m_ref, v_ref, d_ref, nm_ref, nv_ref):
        gv = g_ref[...]
        mn = ADAM_B1 * m_ref[...] + (1.0 - ADAM_B1) * gv
        vn = ADAM_B2 * v_ref[...] + (1.0 - ADAM_B2) * jnp.square(gv)
        m_hat = mn / (1.0 - ADAM_B1 ** ADAM_STEP)
        v_hat = vn / (1.0 - ADAM_B2 ** ADAM_STEP)
        d_ref[...] = -ADAM_LR * (m_hat / (jnp.sqrt(v_hat) + ADAM_EPS) + ADAM_WD * w_ref[...])
        nm_ref[...] = mn
        nv_ref[...] = vn

    spec = pl.BlockSpec((tr, cols), lambda i: (i, 0))
    shp = jax.ShapeDtypeStruct((r, cols), F32)
    res = pl.pallas_call(
        body, name=name, grid=(r // tr,), in_specs=[spec] * 4, out_specs=[spec] * 3, out_shape=[shp] * 3,
        compiler_params=_params("parallel"),
    )(*[t.reshape(r, cols) for t in (wt, g, m, v)])
    return [t.reshape(wt.shape) for t in res]


def _join_shards(n, piece):
    _, depth, a, b = piece.shape
    if BIG_AXIS[n] == 2:
        return piece.transpose(1, 2, 0, 3).reshape(depth, a, N_CHIPS * b)
    return piece.transpose(1, 0, 2, 3).reshape(depth, N_CHIPS * a, b)


RS_GROUPS = (("w_ffn_out", "w_mix_o", "small"), ("w_ffn_in",), ("w_in", "w_attn_o", "w_conv_o", "w_ssm_glu", "w_ssm_o"))


SMALL_PART_ROWS = 2 * SUBLANE


def _rows_of(t):
    return -(-t.size // COLS)


def _pack_small(ts):
    rows = [jnp.pad(t.reshape(-1), (0, _rows_of(t) * COLS - t.size)).reshape(-1, COLS) for t in ts]
    total = sum(r.shape[0] for r in rows)
    part = -(-total // (N_DEV * SMALL_PART_ROWS)) * SMALL_PART_ROWS
    rows.append(jnp.zeros((N_DEV * part - total, COLS), F32))
    return jnp.concatenate(rows, axis=0).reshape(N_CHIPS, 2, part, COLS)


def _unpack_small(buf, like):
    buf, out, r0 = buf.reshape(-1, COLS), [], 0
    for t in like:
        out.append(buf[r0:r0 + _rows_of(t)].reshape(-1)[:t.size].reshape(t.shape))
        r0 += _rows_of(t)
    return out


def kernel(x, norm_mix, w_in, b_gate, attn_sinks, w_attn_o, conv_w, w_conv_o, ssm_a_re, ssm_a_im, ssm_b_re, ssm_b_im, ssm_c_re, ssm_c_im, ssm_d, ssm_log_dt, w_ssm_glu, w_ssm_o, w_mix_o, norm_ffn, w_ffn_in, w_ffn_out, norm_final, loss_target, m_norm_mix, m_w_in, m_b_gate, m_attn_sinks, m_w_attn_o, m_conv_w, m_w_conv_o, m_ssm_a_re, m_ssm_a_im, m_ssm_b_re, m_ssm_b_im, m_ssm_c_re, m_ssm_c_im, m_ssm_d, m_ssm_log_dt, m_w_ssm_glu, m_w_ssm_o, m_w_mix_o, m_norm_ffn, m_w_ffn_in, m_w_ffn_out, m_norm_final, v_norm_mix, v_w_in, v_b_gate, v_attn_sinks, v_w_attn_o, v_conv_w, v_w_conv_o, v_ssm_a_re, v_ssm_a_im, v_ssm_b_re, v_ssm_b_im, v_ssm_c_re, v_ssm_c_im, v_ssm_d, v_ssm_log_dt, v_w_ssm_glu, v_w_ssm_o, v_w_mix_o, v_norm_ffn, v_w_ffn_in, v_w_ffn_out, v_norm_final):
    a = dict(zip(ARG_NAMES, (
        x, norm_mix, w_in, b_gate, attn_sinks, w_attn_o, conv_w, w_conv_o, ssm_a_re, ssm_a_im, ssm_b_re, ssm_b_im,
        ssm_c_re, ssm_c_im, ssm_d, ssm_log_dt, w_ssm_glu, w_ssm_o, w_mix_o, norm_ffn, w_ffn_in, w_ffn_out, norm_final,
        loss_target, m_norm_mix, m_w_in, m_b_gate, m_attn_sinks, m_w_attn_o, m_conv_w, m_w_conv_o, m_ssm_a_re,
        m_ssm_a_im, m_ssm_b_re, m_ssm_b_im, m_ssm_c_re, m_ssm_c_im, m_ssm_d, m_ssm_log_dt, m_w_ssm_glu, m_w_ssm_o,
        m_w_mix_o, m_norm_ffn, m_w_ffn_in, m_w_ffn_out, m_norm_final, v_norm_mix, v_w_in, v_b_gate, v_attn_sinks,
        v_w_attn_o, v_conv_w, v_w_conv_o, v_ssm_a_re, v_ssm_a_im, v_ssm_b_re, v_ssm_b_im, v_ssm_c_re, v_ssm_c_im,
        v_ssm_d, v_ssm_log_dt, v_w_ssm_glu, v_w_ssm_o, v_w_mix_o, v_norm_ffn, v_w_ffn_in, v_w_ffn_out, v_norm_final)))
    px, py, _ = _place()
    chip = 2 * px + py

    gathered = BIG + ("conv_w",)
    own = [a[n].astype(BF16) for n in BIG] + [a["conv_w"]]
    gath = _gather_weights(own, "gather_weights")
    w = {n: _join_shards(n, p) for n, p in zip(gathered, gath)}
    for n in SMALL:
        w[n] = a[n]

    loss, dx, grads = _local_step(a["x"][0], a["loss_target"][0], w)

    small_names = SMALL + ("conv_w",)
    small = _pack_small([grads[n] for n in small_names])
    rs_names = BIG + ("small",)
    groups = [[rs_names.index(n) for n in grp] for grp in RS_GROUPS]
    rs = _reduce_scatter([grads[n] for n in BIG] + [small], rs_names, (BF16,) * len(BIG) + (F32,), groups)
    red = dict(zip(BIG, rs))
    small_all = _gather_weights([rs[-1]], "gather_small")[0]
    red.update(zip(small_names, _unpack_small(small_all, [grads[n] for n in small_names])))
    lane = a["conv_w"].shape[2]
    red["conv_w"] = lax.dynamic_slice_in_dim(red["conv_w"], chip * lane, lane, axis=2)

    loss_all = lax.psum(loss[0, 0], ("x", "y", "c"))
    deltas, new_m, new_v = [], [], []
    for n in WEIGHTS:
        d, mn, vn = _adamw(a[n], red[n], a["m_" + n], a["v_" + n], "adamw_" + n)
        deltas.append(d)
        new_m.append(mn)
        new_v.append(vn)
    return (loss_all, dx[None], *[red[n] for n in WEIGHTS], *deltas, *new_m, *new_v)
```

```python
import math

import jax
import jax.numpy as jnp
from jax import lax
from jax.experimental import pallas as pl
from jax.experimental.pallas import tpu as pltpu

F32 = jnp.float32
BF16 = jnp.bfloat16

D_MODEL = 1024
DEPTH = 4
N_Q_HEADS = 8
N_KV_HEADS = 2
HEAD_DIM = 64
Q_GROUP = N_Q_HEADS // N_KV_HEADS
WINDOW = 128
BLOCK = 128
ROPE_THETA = 500000.0
ROT_DIM = HEAD_DIM // 4
ATTN_WIDTH = N_Q_HEADS * HEAD_DIM
KV_WIDTH = N_KV_HEADS * HEAD_DIM
NEG_INF = -1e30
CONV_WIDTH = 512
CONV_K = 3
SSM_WIDTH = 512
SSM_GROUP = 16
SSM_GROUPS = 32
SSM_STATE = 64
SSM_LANES = SSM_GROUPS * SSM_STATE
GATE_WIDTH = 3 * D_MODEL
FFN_HIDDEN = 2816
NORM_EPS = 1e-6
IN_COLS = 5888
C_Q, C_K, C_V, C_CB, C_CC, C_CX, C_U, C_G = 0, 512, 640, 768, 1280, 1792, 2304, 2816

ADAM_LR = 0.001
ADAM_B1 = 0.9
ADAM_B2 = 0.999
ADAM_EPS = 1e-08
ADAM_WD = 0.01
ADAM_STEP = 10

N_CHIPS = 4
N_DEV = 8
MESH_ID = pl.DeviceIdType.MESH

VMEM_LIMIT_BYTES = 48 * 1024 * 1024
LANE = 128
SUBLANE = 8
SCAN_ROWS = 8
SCAN_CHUNK = 128

BIG = ("w_in", "w_attn_o", "w_conv_o", "w_ssm_glu", "w_ssm_o", "w_mix_o", "w_ffn_in", "w_ffn_out")
BIG_AXIS = {"w_in": 2, "w_attn_o": 2, "w_conv_o": 2, "w_ssm_glu": 1, "w_ssm_o": 2, "w_mix_o": 1,
            "w_ffn_in": 2, "w_ffn_out": 1, "conv_w": 2}
SMALL = ("norm_mix", "b_gate", "attn_sinks", "ssm_a_re", "ssm_a_im", "ssm_b_re", "ssm_b_im",
         "ssm_c_re", "ssm_c_im", "ssm_d", "ssm_log_dt", "norm_ffn", "norm_final")
WEIGHTS = ("norm_mix", "w_in", "b_gate", "attn_sinks", "w_attn_o", "conv_w", "w_conv_o", "ssm_a_re",
           "ssm_a_im", "ssm_b_re", "ssm_b_im", "ssm_c_re", "ssm_c_im", "ssm_d", "ssm_log_dt",
           "w_ssm_glu", "w_ssm_o", "w_mix_o", "norm_ffn", "w_ffn_in", "w_ffn_out", "norm_final")
ARG_NAMES = ("x",) + WEIGHTS + ("loss_target",) + tuple("m_" + n for n in WEIGHTS) + tuple(
    "v_" + n for n in WEIGHTS)


def _params(*sem):
    return pltpu.CompilerParams(dimension_semantics=sem if sem else None,
                                vmem_limit_bytes=VMEM_LIMIT_BYTES)


def _tile(dim, cap, align):
    t = min(cap, dim) // align * align
    while t >= align:
        if dim % t == 0:
            return t
        t -= align
    return dim


_DOT_DIMS = {"nn": (((1,), (0,)), ((), ())), "nt": (((1,), (1,)), ((), ())), "tn": (((0,), (0,)), ((), ()))}


def _mm(a, b, mode, name, out_dtype=F32, add=None, tm_cap=512, tn_cap=3072, tk_cap=1024, b_layer=None, into=None,
        b_shards=False):
    bshape = b.shape if b_layer is None else b.shape[1:]
    if b_shards:
        assert mode == "tn" and b_layer is None and into is not None and into[2] == 2
        bshape = (b.shape[1], N_CHIPS * b.shape[2])
    if mode == "nn":
        (m, k), (k2, n) = a.shape, bshape
    elif mode == "nt":
        (m, k), (n, k2) = a.shape, bshape
    else:
        (k, m), (k2, n) = a.shape, bshape
    assert k == k2, (name, a.shape, b.shape)
    tm, tn, tk = _tile(m, tm_cap, LANE), _tile(n, tn_cap, LANE), _tile(k, tk_cap, LANE)
    if into is not None:
        buf, layer, axis = into
        _, _, ra, cb = buf.shape
        if axis == 1:
            tm = m
        elif b_shards:
            tn = cb
        else:
            tn = _tile(cb, tn_cap, LANE)
            assert cb % tn == 0 and tn % LANE == 0, (name, cb, tn)
    nk = k // tk
    dims = _DOT_DIMS[mode]

    def body(a_ref, b_ref, *rest):
        rest = list(rest)
        add_ref = rest.pop(0) if add is not None else None
        if into is not None:
            rest.pop(0)
        o_ref, acc = rest
        kk = pl.program_id(2)
        part = lax.dot_general(a_ref[...].astype(BF16), b_ref[...].astype(BF16), dims, preferred_element_type=F32)

        def finish(r):
            if add is not None:
                r = r + add_ref[...]
            o_ref[...] = r.astype(o_ref.dtype).reshape(o_ref.shape)

        if nk == 1:
            finish(part)
            return

        @pl.when(kk == 0)
        def _():
            acc[...] = part

        @pl.when(kk > 0)
        def _():
            acc[...] += part

        @pl.when(kk == nk - 1)
        def _():
            finish(acc[...])

    if mode == "tn":
        a_spec = pl.BlockSpec((tk, tm), lambda i, j, kk: (kk, i))
    else:
        a_spec = pl.BlockSpec((tm, tk), lambda i, j, kk: (i, kk))
    lead = () if b_layer is None else (None,)
    at = (lambda *ix: ix) if b_layer is None else (lambda *ix: (b_layer,) + ix)
    if b_shards:
        b_spec = pl.BlockSpec((None, tk, tn), lambda i, j, kk: (j, kk, 0))
    elif mode == "nt":
        b_spec = pl.BlockSpec(lead + (tn, tk), lambda i, j, kk: at(j, kk))
    else:
        b_spec = pl.BlockSpec(lead + (tk, tn), lambda i, j, kk: at(kk, j))
    o_spec = pl.BlockSpec((tm, tn), lambda i, j, kk: (i, j))
    in_specs, args = [a_spec, b_spec], [a, b]
    if add is not None:
        in_specs.append(o_spec)
        args.append(add)
    out_shape, aliases = jax.ShapeDtypeStruct((m, n), out_dtype), {}
    if into is not None:
        in_specs.append(pl.BlockSpec(memory_space=pl.ANY))
        aliases = {len(args): 0}
        args.append(buf)
        out_shape = jax.ShapeDtypeStruct(buf.shape, buf.dtype)
        if axis == 1:
            o_spec = pl.BlockSpec((N_CHIPS, None, ra, tn), lambda i, j, kk: (0, layer, 0, j))
        else:
            per = cb // tn
            o_spec = pl.BlockSpec((None, None, tm, tn), lambda i, j, kk: (j // per, layer, i, j % per))
    return pl.pallas_call(
        body, name=name, grid=(m // tm, n // tn, nk), in_specs=in_specs, out_specs=o_spec,
        out_shape=out_shape, input_output_aliases=aliases,
        scratch_shapes=[pltpu.VMEM((tm, tn) if nk > 1 else (SUBLANE, LANE), F32)],
        compiler_params=_params("parallel", "parallel", "arbitrary"),
    )(*args)


def _rowwise(fn, rows, pars, outs, accs, name, tm_cap=256):
    length = rows[0][0].shape[0]
    tm = _tile(length, tm_cap, LANE)
    n = length // tm
    in_specs, args, counts = [], [], []
    for arr, c0, cw, shift in rows:
        bw = math.gcd(c0, cw) if c0 else cw
        assert bw % LANE == 0 or (c0 == 0 and cw == arr.shape[1]), (name, c0, cw)
        cnt = cw // bw
        counts.append(cnt)
        for j in range(cnt):
            in_specs.append(pl.BlockSpec(
                (tm, bw), lambda i, j=j, c0=c0, bw=bw, shift=shift: (jnp.clip(i + shift, 0, n - 1), c0 // bw + j)))
            args.append(arr)
    for p in pars:
        in_specs.append(pl.BlockSpec(p.shape, lambda i: (0, 0)))
        args.append(p)
    out_shape = [jax.ShapeDtypeStruct((length, w), dt) for w, dt in outs]
    out_specs = [pl.BlockSpec((tm, w), lambda i: (i, 0)) for w, _ in outs]
    out_shape += [jax.ShapeDtypeStruct((r, w), F32) for r, w in accs]
    out_specs += [pl.BlockSpec((r, w), lambda i: (0, 0)) for r, w in accs]
    n_in, n_out = len(args), len(outs)

    def body(*refs):
        i = pl.program_id(0)
        vals, p = [], 0
        for cnt in counts:
            blocks = [refs[p + j][...] for j in range(cnt)]
            p += cnt
            vals.append(blocks[0] if cnt == 1 else jnp.concatenate(blocks, axis=1))
        for _ in pars:
            vals.append(refs[p][...])
            p += 1
        res = fn((i, n), *vals)
        out_refs = refs[n_in:n_in + n_out]
        acc_refs = refs[n_in + n_out:]
        for r, v in zip(out_refs, res[:n_out]):
            r[...] = v.astype(r.dtype)
        if acc_refs:
            @pl.when(i == 0)
            def _():
                for r in acc_refs:
                    r[...] = jnp.zeros_like(r)
            for r, v in zip(acc_refs, res[n_out:]):
                r[...] += v

    res = pl.pallas_call(
        body, name=name, grid=(n,), in_specs=in_specs, out_specs=out_specs, out_shape=out_shape,
        compiler_params=_params("arbitrary"),
    )(*args)
    return res


def _rms(x, g):
    return x * lax.rsqrt(jnp.mean(x * x, axis=-1, keepdims=True) + NORM_EPS) * g


def _rms_fwd(x, g, name):
    return _rowwise(lambda ctx, xv, gv: (_rms(xv, gv),), [(x, 0, D_MODEL, 0)], [g],
                    [(D_MODEL, BF16)], [], name)[0]


def _rms_bwd(x, g, dh, dres, name):
    def fn(ctx, xv, dhv, drv, gv):
        _, vjp = jax.vjp(_rms, xv, gv)
        dx, dg = vjp(dhv)
        return dx + drv, dg
    return _rowwise(fn, [(x, 0, D_MODEL, 0), (dh, 0, D_MODEL, 0), (dres, 0, D_MODEL, 0)], [g],
                    [(D_MODEL, F32)], [(1, D_MODEL)], name)


def _rope_tables(length):
    pos = jnp.arange(length, dtype=F32)
    inv_freq = ROPE_THETA ** (-jnp.arange(0, ROT_DIM, 2, dtype=F32) / ROT_DIM)
    ang = pos[:, None] * inv_freq[None, :]
    cos, sin = jnp.cos(ang), jnp.sin(ang)
    half = ROT_DIM // 2
    ones = jnp.ones((length, HEAD_DIM - ROT_DIM), F32)
    zeros = jnp.zeros_like(ones)
    zh = jnp.zeros((length, half), F32)
    c64 = jnp.concatenate([cos, cos, ones], axis=1)
    s1 = jnp.concatenate([-sin, zh, zeros], axis=1)
    s2 = jnp.concatenate([zh, sin, zeros], axis=1)
    tile2 = lambda t: jnp.concatenate([t, t], axis=1)
    return tile2(c64), tile2(s1), tile2(s2)


def _lane_chunks(t):
    return [t[:, j * LANE:(j + 1) * LANE] for j in range(t.shape[1] // LANE)]


def _rope(t, c, s1, s2, n_rot):
    half = ROT_DIM // 2
    out = []
    for j, ch in enumerate(_lane_chunks(t)):
        if j < n_rot:
            ch = ch * c + pltpu.roll(ch, LANE - half, 1) * s1 + pltpu.roll(ch, half, 1) * s2
        out.append(ch)
    return jnp.concatenate(out, axis=1)


def _unrope(d, c, s1, s2, n_rot):
    half = ROT_DIM // 2
    out = []
    for j, ch in enumerate(_lane_chunks(d)):
        if j < n_rot:
            ch = ch * c + pltpu.roll(ch * s1, half, 1) + pltpu.roll(ch * s2, LANE - half, 1)
        out.append(ch)
    return jnp.concatenate(out, axis=1)


N_ROT_CHUNKS = (ATTN_WIDTH + KV_WIDTH) // LANE
QKV_WIDTH = ATTN_WIDTH + 2 * KV_WIDTH


def _split_fwd(proj, tabs, name):
    def fn(ctx, t, c, s1, s2):
        return (_rope(t, c, s1, s2, N_ROT_CHUNKS),)
    rows = [(proj, 0, QKV_WIDTH, 0)] + [(t, 0, LANE, 0) for t in tabs]
    return _rowwise(fn, rows, [], [(QKV_WIDTH, BF16)], [], name, tm_cap=BLOCK)[0]


def _split_bwd(d, tabs, name):
    def fn(ctx, dqv, dkcv, dkpv, dvcv, dvpv, c, s1, s2):
        i, n = ctx
        keep = (i < n - 1).astype(F32)
        dd = jnp.concatenate([dqv, dkcv + keep * dkpv, dvcv + keep * dvpv], axis=1)
        return (_unrope(dd, c, s1, s2, N_ROT_CHUNKS),)
    col = lambda row: row * HEAD_DIM
    rows = [(d, 0, ATTN_WIDTH, 0), (d, col(D_KC), KV_WIDTH, 0), (d, col(D_KP), KV_WIDTH, 1),
            (d, col(D_VC), KV_WIDTH, 0), (d, col(D_VP), KV_WIDTH, 1)] + [(t, 0, LANE, 0) for t in tabs]
    return _rowwise(fn, rows, [], [(QKV_WIDTH, BF16)], [], name, tm_cap=BLOCK)[0]


N_HEAD_ROWS = N_Q_HEADS + 2 * N_KV_HEADS


def _att_scores(j, q_ref, kvp_ref, kvc_ref, sink_ref):
    n = pl.program_id(0)
    rows = Q_GROUP * BLOCK
    qs = q_ref[Q_GROUP * j:Q_GROUP * (j + 1)].reshape(rows, HEAD_DIM)
    kb = jnp.concatenate([kvp_ref[j], kvc_ref[j]], axis=0)
    s = lax.dot_general(qs, kb, _DOT_DIMS["nt"], preferred_element_type=F32) * (HEAD_DIM ** -0.5)
    r = lax.broadcasted_iota(jnp.int32, (rows, 2 * BLOCK), 0)
    kj = lax.broadcasted_iota(jnp.int32, (rows, 2 * BLOCK), 1)
    delta = (r % BLOCK) + BLOCK - kj
    ok = (delta >= 0) & (delta < WINDOW) & ((kj >= BLOCK) | (n > 0))
    s = jnp.where(ok, s, NEG_INF)
    rh = lax.broadcasted_iota(jnp.int32, (rows, 1), 0) // BLOCK
    sinks = sink_ref[...]
    lane = lax.broadcasted_iota(jnp.int32, sinks.shape, 1)
    srow = lax.broadcasted_iota(jnp.int32, sinks.shape, 0)
    sink = jnp.zeros((rows, 1), F32)
    for g in range(Q_GROUP):
        val = jnp.sum(jnp.where((lane == g) & (srow == j), sinks, 0.0), keepdims=True)
        sink = jnp.where(rh == g, val, sink)
    m = jnp.maximum(jnp.max(s, axis=-1, keepdims=True), sink)
    p = jnp.exp(s - m)
    psink = jnp.exp(sink - m)
    denom = jnp.sum(p, axis=-1, keepdims=True) + psink
    vb = jnp.concatenate([kvp_ref[N_KV_HEADS + j], kvc_ref[N_KV_HEADS + j]], axis=0)
    return qs, kb, vb, p / denom, psink / denom, rh


def _att_specs(length):
    nb = length // BLOCK
    kv_rows = 2 * N_KV_HEADS
    q_spec = pl.BlockSpec((N_Q_HEADS, BLOCK, HEAD_DIM), lambda n: (0, n, 0))
    prev = pl.BlockSpec((kv_rows, BLOCK, HEAD_DIM), lambda n: (N_Q_HEADS // kv_rows, jnp.maximum(n - 1, 0), 0))
    cur = pl.BlockSpec((kv_rows, BLOCK, HEAD_DIM), lambda n: (N_Q_HEADS // kv_rows, n, 0))
    sink_spec = pl.BlockSpec((N_KV_HEADS, Q_GROUP), lambda n: (0, 0))
    return nb, q_spec, prev, cur, sink_spec


def _att_fwd(heads, sinks, name):
    length = heads.shape[1]
    nb, q_spec, prev, cur, sink_spec = _att_specs(length)

    def body(q_ref, kvp_ref, kvc_ref, sink_ref, o_ref):
        for j in range(N_KV_HEADS):
            _, _, vb, p, _, _ = _att_scores(j, q_ref, kvp_ref, kvc_ref, sink_ref)
            o = jnp.dot(p.astype(BF16), vb, preferred_element_type=F32)
            o_ref[Q_GROUP * j:Q_GROUP * (j + 1)] = o.reshape(Q_GROUP, BLOCK, HEAD_DIM).astype(o_ref.dtype)

    return pl.pallas_call(
        body, name=name, grid=(nb,), in_specs=[q_spec, prev, cur, sink_spec], out_specs=q_spec,
        out_shape=jax.ShapeDtypeStruct((N_Q_HEADS, length, HEAD_DIM), BF16),
        compiler_params=_params("arbitrary"),
    )(heads, heads, heads, sinks)


D_KC, D_KP, D_VC, D_VP = (N_Q_HEADS + i * N_KV_HEADS for i in range(4))
N_DHEAD_ROWS = N_Q_HEADS + 4 * N_KV_HEADS


def _att_bwd(heads, sinks, do, name):
    length = heads.shape[1]
    nb, q_spec, prev, cur, sink_spec = _att_specs(length)

    def body(q_ref, kvp_ref, kvc_ref, sink_ref, do_ref, d_ref, dsink_ref):
        @pl.when(pl.program_id(0) == 0)
        def _():
            dsink_ref[...] = jnp.zeros_like(dsink_ref)

        for j in range(N_KV_HEADS):
            qs, kb, vb, p, psink, rh = _att_scores(j, q_ref, kvp_ref, kvc_ref, sink_ref)
            dob = do_ref[Q_GROUP * j:Q_GROUP * (j + 1)].reshape(Q_GROUP * BLOCK, HEAD_DIM).astype(BF16)
            dv = lax.dot_general(p.astype(BF16), dob, _DOT_DIMS["tn"], preferred_element_type=F32)
            dp = lax.dot_general(dob, vb, _DOT_DIMS["nt"], preferred_element_type=F32)
            dsum = jnp.sum(p * dp, axis=-1, keepdims=True)
            ds = (p * (dp - dsum) * (HEAD_DIM ** -0.5)).astype(BF16)
            dq = jnp.dot(ds, kb, preferred_element_type=F32)
            dk = lax.dot_general(ds, qs, _DOT_DIMS["tn"], preferred_element_type=F32)
            d_ref[Q_GROUP * j:Q_GROUP * (j + 1)] = dq.reshape(Q_GROUP, BLOCK, HEAD_DIM)
            d_ref[D_KP + j] = dk[:BLOCK]
            d_ref[D_KC + j] = dk[BLOCK:]
            d_ref[D_VP + j] = dv[:BLOCK]
            d_ref[D_VC + j] = dv[BLOCK:]
            dsr = -psink * dsum
            row = lax.broadcasted_iota(jnp.int32, (SUBLANE, LANE), 0)
            upd = jnp.zeros((SUBLANE, LANE), F32)
            for g in range(Q_GROUP):
                val = jnp.sum(jnp.where(rh == g, dsr, 0.0), keepdims=True)
                upd = jnp.where(row == g, val, upd)
            dsink_ref[j] += upd

    d_spec = pl.BlockSpec((N_DHEAD_ROWS, BLOCK, HEAD_DIM), lambda n: (0, n, 0))
    return pl.pallas_call(
        body, name=name, grid=(nb,),
        in_specs=[q_spec, prev, cur, sink_spec, q_spec],
        out_specs=[d_spec, pl.BlockSpec((N_KV_HEADS, SUBLANE, LANE), lambda n: (0, 0, 0))],
        out_shape=[jax.ShapeDtypeStruct((N_DHEAD_ROWS, length, HEAD_DIM), F32),
                   jax.ShapeDtypeStruct((N_KV_HEADS, SUBLANE, LANE), F32)],
        compiler_params=_params("arbitrary"),
    )(heads, heads, heads, sinks, do)


def _to_heads(t, heads):
    return t.reshape(t.shape[0], heads, HEAD_DIM).transpose(1, 0, 2)


def _from_heads(t):
    return t.transpose(1, 0, 2).reshape(t.shape[1], t.shape[0] * HEAD_DIM)


def _shift_down(z, s):
    t = lax.broadcasted_iota(jnp.int32, z.shape, 0)
    return jnp.where(t >= s, pltpu.roll(z, s, 0), 0.0)


def _shift_up(z, s):
    t = lax.broadcasted_iota(jnp.int32, z.shape, 0)
    return jnp.where(t < z.shape[0] - s, pltpu.roll(z, z.shape[0] - s, 0), 0.0)


def _conv_specs(length):
    col = lambda c0: pl.BlockSpec((length, LANE), lambda j, c0=c0: (0, c0 // LANE + j))
    w_spec = pl.BlockSpec((CONV_K, LANE), lambda j: (0, j))
    o_spec = pl.BlockSpec((length, LANE), lambda j: (0, j))
    return col, w_spec, o_spec


def _conv_fwd(proj, w, name):
    length = proj.shape[0]
    col, w_spec, o_spec = _conv_specs(length)

    def body(cb_ref, cc_ref, cx_ref, w_ref, o_ref):
        z = cc_ref[...] * cx_ref[...]
        s = w_ref[0:1, :] * _shift_down(z, 2) + w_ref[1:2, :] * _shift_down(z, 1) + w_ref[2:3, :] * z
        o_ref[...] = (cb_ref[...] * s).astype(o_ref.dtype)

    return pl.pallas_call(
        body, name=name, grid=(CONV_WIDTH // LANE,),
        in_specs=[col(C_CB), col(C_CC), col(C_CX), w_spec], out_specs=o_spec,
        out_shape=jax.ShapeDtypeStruct((length, CONV_WIDTH), BF16),
        compiler_params=_params("arbitrary"),
    )(proj, proj, proj, w)


def _conv_bwd(proj, w, dy, name):
    length = proj.shape[0]
    col, w_spec, o_spec = _conv_specs(length)
    dw_spec = pl.BlockSpec((1, LANE), lambda j: (0, j))

    def body(cb_ref, cc_ref, cx_ref, w_ref, dy_ref, dcb_ref, dcc_ref, dcx_ref, dw0_ref, dw1_ref, dw2_ref):
        cc, cx, dyv = cc_ref[...], cx_ref[...], dy_ref[...]
        z = cc * cx
        w0, w1, w2 = w_ref[0:1, :], w_ref[1:2, :], w_ref[2:3, :]
        z1, z2 = _shift_down(z, 1), _shift_down(z, 2)
        s = w0 * z2 + w1 * z1 + w2 * z
        dcb_ref[...] = (dyv * s).astype(dcb_ref.dtype)
        ds = dyv * cb_ref[...]
        dw0_ref[...] = jnp.sum(ds * z2, axis=0, keepdims=True)
        dw1_ref[...] = jnp.sum(ds * z1, axis=0, keepdims=True)
        dw2_ref[...] = jnp.sum(ds * z, axis=0, keepdims=True)
        dz = w2 * ds + w1 * _shift_up(ds, 1) + w0 * _shift_up(ds, 2)
        dcc_ref[...] = (dz * cx).astype(dcc_ref.dtype)
        dcx_ref[...] = (dz * cc).astype(dcx_ref.dtype)

    act = jax.ShapeDtypeStruct((length, CONV_WIDTH), BF16)
    dws = jax.ShapeDtypeStruct((1, CONV_WIDTH), F32)
    return pl.pallas_call(
        body, name=name, grid=(CONV_WIDTH // LANE,),
        in_specs=[col(C_CB), col(C_CC), col(C_CX), w_spec, o_spec],
        out_specs=[o_spec, o_spec, o_spec, dw_spec, dw_spec, dw_spec],
        out_shape=[act, act, act, dws, dws, dws],
        compiler_params=_params("arbitrary"),
    )(proj, proj, proj, w, dy)


def _cmul(ar, ai, br, bi):
    return ar * br - ai * bi, ar * bi + ai * br


def _scan_tables(lr, li, reverse):
    pr, pi = [lr], [li]
    for _ in range(SCAN_ROWS - 1):
        nr, ni = _cmul(pr[-1], pi[-1], lr, li)
        pr.append(nr)
        pi.append(ni)
    row = jnp.arange(SCAN_ROWS)[:, None]
    mr, mi = [], []
    for s in (1, 2, 4):
        live = (row + s < SCAN_ROWS) if reverse else (row >= s)
        mr.append(jnp.where(live, pr[s - 1], 0.0))
        mi.append(jnp.where(live, pi[s - 1], 0.0))
    order = range(SCAN_ROWS - 1, -1, -1) if reverse else range(SCAN_ROWS)
    carry_r = jnp.concatenate([pr[d] for d in order], axis=0)
    carry_i = jnp.concatenate([pi[d] for d in order], axis=0)
    return jnp.stack(mr), jnp.stack(mi), carry_r, carry_i


N_SSM_CHUNKS = 4
CHUNK_STATES = SSM_LANES // N_SSM_CHUNKS
CHUNK_CHANNELS = SSM_WIDTH // N_SSM_CHUNKS


def _scan(b, lr, li, reverse, name, states=None):
    length = b.shape[0]
    mr, mi, cr, ci = _scan_tables(lr, li, reverse)
    nchunk = length // SCAN_CHUNK
    nblk = SCAN_CHUNK // SCAN_ROWS
    cw = CHUNK_STATES
    with_dlam = states is not None

    def body(*refs):
        if with_dlam:
            b_ref, mr_ref, mi_ref, cr_ref, ci_ref, s_ref, sp_ref, o_ref, dl_ref, carry = refs
        else:
            b_ref, mr_ref, mi_ref, cr_ref, ci_ref, o_ref, carry = refs
        step = pl.program_id(0)

        @pl.when(step == 0)
        def _():
            carry[...] = jnp.zeros_like(carry)
            if with_dlam:
                dl_ref[...] = jnp.zeros_like(dl_ref)

        blocks = range(nblk - 1, -1, -1) if reverse else range(nblk)
        for j in range(N_SSM_CHUNKS):
            re, im = slice(2 * cw * j, 2 * cw * j + cw), slice(2 * cw * j + cw, 2 * cw * (j + 1))
            tl = slice(cw * j, cw * (j + 1))
            c_r, c_i = carry[0:1, re], carry[0:1, im]
            acc_r = acc_i = jnp.zeros((SCAN_ROWS, cw), F32)
            for blk in blocks:
                r0 = blk * SCAN_ROWS
                xr = b_ref[r0:r0 + SCAN_ROWS, re]
                xi = b_ref[r0:r0 + SCAN_ROWS, im]
                for kk, s in enumerate((1, 2, 4)):
                    sh = SCAN_ROWS - s if reverse else s
                    rr, ri = pltpu.roll(xr, sh, 0), pltpu.roll(xi, sh, 0)
                    ar, ai = _cmul(mr_ref[kk, :, tl], mi_ref[kk, :, tl], rr, ri)
                    xr, xi = xr + ar, xi + ai
                ar, ai = _cmul(cr_ref[:, tl], ci_ref[:, tl], c_r, c_i)
                xr, xi = xr + ar, xi + ai
                o_ref[r0:r0 + SCAN_ROWS, re] = xr
                o_ref[r0:r0 + SCAN_ROWS, im] = xi
                edge = r0 if reverse else r0 + SCAN_ROWS - 1
                c_r = o_ref[edge:edge + 1, re]
                c_i = o_ref[edge:edge + 1, im]
                if with_dlam:
                    if r0 > 0:
                        pr, pi = s_ref[r0 - 1:r0 + SCAN_ROWS - 1, re], s_ref[r0 - 1:r0 + SCAN_ROWS - 1, im]
                    else:
                        live = (step < nchunk - 1).astype(F32)
                        row = lax.broadcasted_iota(jnp.int32, (SCAN_ROWS, cw), 0)
                        pr = jnp.where(row == 0, sp_ref[SCAN_ROWS - 1:SCAN_ROWS, re] * live,
                                       pltpu.roll(s_ref[0:SCAN_ROWS, re], 1, 0))
                        pi = jnp.where(row == 0, sp_ref[SCAN_ROWS - 1:SCAN_ROWS, im] * live,
                                       pltpu.roll(s_ref[0:SCAN_ROWS, im], 1, 0))
                    acc_r = acc_r + xr * pr + xi * pi
                    acc_i = acc_i + xi * pr - xr * pi
            carry[0:1, re] = c_r
            carry[0:1, im] = c_i
            if with_dlam:
                dl_ref[:, re] += acc_r
                dl_ref[:, im] += acc_i

    width = 2 * SSM_LANES
    chunk = (lambda i: (nchunk - 1 - i, 0)) if reverse else (lambda i: (i, 0))
    blk_spec = pl.BlockSpec((SCAN_CHUNK, width), chunk)
    m_spec = pl.BlockSpec((3, SCAN_ROWS, SSM_LANES), lambda i: (0, 0, 0))
    c_spec = pl.BlockSpec((SCAN_ROWS, SSM_LANES), lambda i: (0, 0))
    in_specs, args = [blk_spec, m_spec, m_spec, c_spec, c_spec], [b, mr, mi, cr, ci]
    out_specs, out_shape = blk_spec, jax.ShapeDtypeStruct(b.shape, F32)
    if with_dlam:
        assert reverse
        per = SCAN_CHUNK // SCAN_ROWS
        before = pl.BlockSpec((SCAN_ROWS, width), lambda i: (jnp.maximum((nchunk - 1 - i) * per - 1, 0), 0))
        in_specs += [blk_spec, before]
        args += [states, states]
        out_specs = [blk_spec, pl.BlockSpec((SCAN_ROWS, width), lambda i: (0, 0))]
        out_shape = [out_shape, jax.ShapeDtypeStruct((SCAN_ROWS, width), F32)]
    return pl.pallas_call(
        body, name=name, grid=(nchunk,), in_specs=in_specs, out_specs=out_specs, out_shape=out_shape,
        scratch_shapes=[pltpu.VMEM((SUBLANE, width), F32)],
        compiler_params=_params("arbitrary"),
    )(*args)


def _mm_bd(a, b, mode, name, out_dtype=F32, add=None):
    nc = N_SSM_CHUNKS
    if mode == "tn":
        k, wa, wb = a.shape[0], a.shape[1] // nc, b.shape[1] // nc
        tk = _tile(k, 1024, LANE)

        def body(a_ref, b_ref, o_ref):
            @pl.when(pl.program_id(1) == 0)
            def _():
                o_ref[...] = jnp.zeros_like(o_ref)

            o_ref[...] += lax.dot_general(a_ref[...].astype(BF16), b_ref[...].astype(BF16), _DOT_DIMS["tn"],
                                          preferred_element_type=F32)

        return pl.pallas_call(
            body, name=name, grid=(nc, k // tk),
            in_specs=[pl.BlockSpec((tk, wa), lambda j, kk: (kk, j)), pl.BlockSpec((tk, wb), lambda j, kk: (kk, j))],
            out_specs=pl.BlockSpec((None, wa, wb), lambda j, kk: (j, 0, 0)),
            out_shape=jax.ShapeDtypeStruct((nc, wa, wb), F32),
            compiler_params=_params("parallel", "arbitrary"),
        )(a, b)
    m, wa = a.shape[0], a.shape[1] // nc
    wo = b.shape[2] if mode == "nn" else b.shape[1]
    tm = _tile(m, 512, LANE)

    def body(a_ref, b_ref, *rest):
        r = lax.dot_general(a_ref[...].astype(BF16), b_ref[...].astype(BF16), _DOT_DIMS[mode],
                            preferred_element_type=F32)
        if add is not None:
            r = r + rest[0][...]
        rest[-1][...] = r.astype(out_dtype)

    o_spec = pl.BlockSpec((tm, wo), lambda i, j: (i, j))
    in_specs = [pl.BlockSpec((tm, wa), lambda i, j: (i, j)), pl.BlockSpec((None,) + b.shape[1:], lambda i, j: (j, 0, 0))]
    args = [a, b]
    if add is not None:
        in_specs.append(o_spec)
        args.append(add)
    return pl.pallas_call(
        body, name=name, grid=(m // tm, nc), in_specs=in_specs, out_specs=o_spec,
        out_shape=jax.ShapeDtypeStruct((m, nc * wo), out_dtype),
        compiler_params=_params("parallel", "parallel"),
    )(*args)


def _block_diag(t):
    g, a, b = t.shape
    per = g // N_SSM_CHUNKS
    eye = jnp.eye(per, dtype=t.dtype)
    t = t.reshape(N_SSM_CHUNKS, per, a, b)
    return (t[:, :, :, None, :] * eye[None, :, None, :, None]).reshape(N_SSM_CHUNKS, per * a, per * b)


def _ssm_prep(a_re, a_im, b_re, b_im, c_re, c_im, log_dt):
    dt = jnp.exp(log_dt)[:, None]
    er = jnp.exp(a_re * dt)
    lr, li = er * jnp.cos(a_im * dt), er * jnp.sin(a_im * dt)
    nr, ni = lr - 1.0, li
    den = a_re * a_re + a_im * a_im
    qr, qi = (nr * a_re + ni * a_im) / den, (ni * a_re - nr * a_im) / den
    bbr = qr[..., None] * b_re - qi[..., None] * b_im
    bbi = qr[..., None] * b_im + qi[..., None] * b_re
    bmat = jnp.concatenate([_block_diag(bbr.transpose(0, 2, 1)), _block_diag(bbi.transpose(0, 2, 1))], axis=2)
    cmat = jnp.concatenate([_block_diag(c_re.transpose(0, 2, 1)), -_block_diag(c_im.transpose(0, 2, 1))], axis=1)
    return lr.reshape(1, SSM_LANES), li.reshape(1, SSM_LANES), bmat, cmat


def _ssm_act(yc, u, d):
    return jax.nn.gelu(yc + d * u)


def _glu(ys, z):
    return ys * jax.nn.sigmoid(z)


def _merge(ya, yc, ys, gl, b):
    gates = jax.nn.sigmoid(gl + b)
    return gates[:, :D_MODEL] * ya + gates[:, D_MODEL:2 * D_MODEL] * yc + gates[:, 2 * D_MODEL:] * ys


def _swiglu(gt, up):
    return jax.nn.silu(gt) * up


def _mm_swiglu(a, b, layer, name):
    m, k = a.shape
    hid = b.shape[2] // 2
    tm, tn = _tile(m, 512, LANE), _tile(hid, 1536, LANE)
    nj = hid // tn

    def body(a_ref, bg_ref, bu_ref, g_ref, u_ref, act_ref):
        av = a_ref[...].astype(BF16)
        gt = jnp.dot(av, bg_ref[...].astype(BF16), preferred_element_type=F32)
        up = jnp.dot(av, bu_ref[...].astype(BF16), preferred_element_type=F32)
        g_ref[...] = gt
        u_ref[...] = up
        act_ref[...] = _swiglu(gt, up).astype(act_ref.dtype)

    o_spec = pl.BlockSpec((tm, tn), lambda i, j: (i, j))
    half = jax.ShapeDtypeStruct((m, hid), F32)
    return pl.pallas_call(
        body, name=name, grid=(m // tm, nj),
        in_specs=[pl.BlockSpec((tm, k), lambda i, j: (i, 0)),
                  pl.BlockSpec((None, k, tn), lambda i, j: (layer, 0, j)),
                  pl.BlockSpec((None, k, tn), lambda i, j: (layer, 0, j + nj))],
        out_specs=[o_spec, o_spec, o_spec], out_shape=[half, half, jax.ShapeDtypeStruct((m, hid), BF16)],
        compiler_params=_params("parallel", "parallel"),
    )(a, b, b)


def _mm_dswiglu(dy, b, layer, gt, up, name):
    m, n = dy.shape
    hid = b.shape[1]
    tm = _tile(m, 256, LANE)

    def body(dy_ref, b_ref, g_ref, u_ref, o_ref):
        dact = lax.dot_general(dy_ref[...].astype(BF16), b_ref[...].astype(BF16), _DOT_DIMS["nt"],
                               preferred_element_type=F32)
        _, vjp = jax.vjp(_swiglu, g_ref[...], u_ref[...])
        o_ref[...] = jnp.concatenate(vjp(dact), axis=1).astype(o_ref.dtype)

    row = lambda w: pl.BlockSpec((tm, w), lambda i: (i, 0))
    return pl.pallas_call(
        body, name=name, grid=(m // tm,),
        in_specs=[row(n), pl.BlockSpec((None, hid, n), lambda i: (layer, 0, 0)), row(hid), row(hid)],
        out_specs=row(2 * hid), out_shape=jax.ShapeDtypeStruct((m, 2 * hid), BF16),
        compiler_params=_params("parallel"),
    )(dy, b, gt, up)


def _loss_fn(x, g, t):
    e = _rms(x, g) - t
    per_tok = jnp.mean(e * e, axis=-1, keepdims=True)
    return 0.5 * jnp.sum(per_tok, axis=0, keepdims=True)


def _vjp_rowwise(f, n_row, cot_dtype=F32):
    def fn(ctx, *vals):
        prim = vals[:n_row] + vals[n_row + 1:]
        _, vjp = jax.vjp(f, *prim)
        return vjp(vals[n_row].astype(cot_dtype))
    return fn


def _layer_fwd(i, x, w, tabs):
    nm = lambda s: "l%d_%s" % (i, s)
    sv = {"x": x}
    h = _rms_fwd(x, w["norm_mix"][i:i + 1], nm("rms_mix"))
    proj = _mm(h, w["w_in"], "nn", nm("mm_in"), b_layer=i)
    qkv = _split_fwd(proj, tabs, nm("rope"))
    heads = _to_heads(qkv, N_HEAD_ROWS)
    sinks = w["attn_sinks"][i].reshape(N_KV_HEADS, Q_GROUP)
    att = _from_heads(_att_fwd(heads, sinks, nm("att")))
    conv = _conv_fwd(proj, w["conv_w"][i], nm("conv"))
    lr, li, bmat, cmat = w["ssm"][i]
    u = proj[:, C_U:C_G]
    bu = _mm_bd(u, bmat, "nn", nm("mm_bu"))
    states = _scan(bu, lr, li, False, nm("scan"))
    yc = _mm_bd(states, cmat, "nn", nm("mm_c"))
    d = w["ssm_d"][i:i + 1]
    ys = _rowwise(lambda ctx, a, b, c: (_ssm_act(a, b, c),), [(yc, 0, SSM_WIDTH, 0), (u, 0, SSM_WIDTH, 0)], [d],
                  [(SSM_WIDTH, F32)], [], nm("ssm_act"))[0]
    z = _mm(ys, w["w_ssm_glu"], "nn", nm("mm_glu"), b_layer=i)
    sg = _rowwise(lambda ctx, a, b: (_glu(a, b),), [(ys, 0, SSM_WIDTH, 0), (z, 0, SSM_WIDTH, 0)], [],
                  [(SSM_WIDTH, BF16)], [], nm("glu"))[0]
    ya = _mm(att, w["w_attn_o"], "nn", nm("mm_ao"), b_layer=i)
    yv = _mm(conv, w["w_conv_o"], "nn", nm("mm_co"), b_layer=i)
    ym = _mm(sg, w["w_ssm_o"], "nn", nm("mm_so"), b_layer=i)
    bg = w["b_gate"][i:i + 1]
    merged = _rowwise(lambda ctx, a, b, c, gl, bb: (_merge(a, b, c, gl, bb),),
                      [(ya, 0, D_MODEL, 0), (yv, 0, D_MODEL, 0), (ym, 0, D_MODEL, 0), (proj, C_G, GATE_WIDTH, 0)],
                      [bg], [(D_MODEL, BF16)], [], nm("merge"))[0]
    x1 = _mm(merged, w["w_mix_o"], "nn", nm("mm_mix"), add=x, b_layer=i)
    h2 = _rms_fwd(x1, w["norm_ffn"][i:i + 1], nm("rms_ffn"))
    gt, up, act = _mm_swiglu(h2, w["w_ffn_in"], i, nm("mm_ffn_in"))
    x2 = _mm(act, w["w_ffn_out"], "nn", nm("mm_ffn_out"), add=x1, b_layer=i, tk_cap=3072)
    sv.update(h=h, proj=proj, heads=heads, att=att, conv=conv, u=u, states=states, yc=yc, ys=ys, z=z, sg=sg,
              ya=ya, yv=yv, ym=ym, merged=merged, x1=x1, h2=h2, gt=gt, up=up, act=act)
    return x2, sv


def _layer_bwd(i, dx2, sv, w, tabs, gb):
    nm = lambda s: "l%d_b_%s" % (i, s)
    g = {}

    def wgrad(n, lhs, rhs, label, **kw):
        if n in gb:
            gb[n] = _mm(lhs, rhs, "tn", nm(label), into=(gb[n], i, BIG_AXIS[n]), **kw)
        else:
            g[n] = _mm(lhs, rhs, "tn", nm(label), **kw)

    wgrad("w_ffn_out", sv["act"], dx2, "mm_gw_ffn_out", tn_cap=512)
    dgu = _mm_dswiglu(dx2, w["w_ffn_out"], i, sv["gt"], sv["up"], nm("mm_dswiglu"))
    dh2 = _mm(dgu, w["w_ffn_in"], "nt", nm("mm_dh2"), tk_cap=3072, b_layer=i)
    wgrad("w_ffn_in", sv["h2"], dgu, "mm_gw_ffn_in")
    dx1, g["norm_ffn"] = _rms_bwd(sv["x1"], w["norm_ffn"][i:i + 1], dh2, dx2, nm("rms_ffn"))
    dmerged = _mm(dx1, w["w_mix_o"], "nt", nm("mm_dmerged"), b_layer=i)
    wgrad("w_mix_o", sv["merged"], dx1, "mm_gw_mix", tn_cap=512)
    proj = sv["proj"]
    bg = w["b_gate"][i:i + 1]
    dya, dyv, dym, dgl, g["b_gate"] = _rowwise(
        _vjp_rowwise(_merge, 4),
        [(sv["ya"], 0, D_MODEL, 0), (sv["yv"], 0, D_MODEL, 0), (sv["ym"], 0, D_MODEL, 0),
         (proj, C_G, GATE_WIDTH, 0), (dmerged, 0, D_MODEL, 0)], [bg],
        [(D_MODEL, BF16), (D_MODEL, BF16), (D_MODEL, BF16), (GATE_WIDTH, BF16)], [(1, GATE_WIDTH)], nm("merge"))
    dsg = _mm(dym, w["w_ssm_o"], "nt", nm("mm_dsg"), b_layer=i)
    wgrad("w_ssm_o", sv["sg"], dym, "mm_gw_so")
    dys0, dz = _rowwise(_vjp_rowwise(_glu, 2), [(sv["ys"], 0, SSM_WIDTH, 0), (sv["z"], 0, SSM_WIDTH, 0),
                                                 (dsg, 0, SSM_WIDTH, 0)], [],
                        [(SSM_WIDTH, F32), (SSM_WIDTH, BF16)], [], nm("glu"))
    dys = _mm(dz, w["w_ssm_glu"], "nt", nm("mm_dys"), add=dys0, b_layer=i)
    wgrad("w_ssm_glu", sv["ys"], dz, "mm_gw_glu")
    d = w["ssm_d"][i:i + 1]
    dyc, du0, g["ssm_d"] = _rowwise(
        _vjp_rowwise(_ssm_act, 2), [(sv["yc"], 0, SSM_WIDTH, 0), (sv["u"], 0, SSM_WIDTH, 0), (dys, 0, SSM_WIDTH, 0)],
        [d], [(SSM_WIDTH, F32), (SSM_WIDTH, F32)], [(1, SSM_WIDTH)], nm("ssm_act"))
    lr, li, bmat, cmat = w["ssm"][i]
    dstates = _mm_bd(dyc, cmat, "nt", nm("mm_dstates"))
    g_cmat = _mm_bd(sv["states"], dyc, "tn", nm("mm_gc"))
    gs, dl = _scan(dstates, lr, -li, True, nm("scan"), states=sv["states"])
    g_lam = jnp.sum(dl, axis=0).reshape(N_SSM_CHUNKS, 2, CHUNK_STATES)
    du = _mm_bd(gs, bmat, "nt", nm("mm_du"), out_dtype=BF16, add=du0)
    g_bmat = _mm_bd(sv["u"], gs, "tn", nm("mm_gb"))
    g["ssm"] = (g_lam[:, 0].reshape(1, SSM_LANES), g_lam[:, 1].reshape(1, SSM_LANES), g_bmat, g_cmat)
    dconv = _mm(dyv, w["w_conv_o"], "nt", nm("mm_dconv"), b_layer=i)
    wgrad("w_conv_o", sv["conv"], dyv, "mm_gw_co")
    dcb, dcc, dcx, dw0, dw1, dw2 = _conv_bwd(proj, w["conv_w"][i], dconv, nm("conv"))
    g["conv_w"] = jnp.concatenate([dw0, dw1, dw2], axis=0)
    datt = _mm(dya, w["w_attn_o"], "nt", nm("mm_datt"), b_layer=i)
    wgrad("w_attn_o", sv["att"], dya, "mm_gw_ao")
    sinks = w["attn_sinks"][i].reshape(N_KV_HEADS, Q_GROUP)
    dheads, dsk = _att_bwd(sv["heads"], sinks, _to_heads(datt, N_Q_HEADS), nm("att"))
    g["attn_sinks"] = dsk[:, :Q_GROUP, 0].reshape(N_Q_HEADS)
    dqkv = _split_bwd(_from_heads(dheads), tabs, nm("rope"))
    dproj = jnp.concatenate([dqkv, dcb, dcc, dcx, du, dgl], axis=1)
    dh = _mm(dproj, w["w_in"], "nt", nm("mm_dh"), tk_cap=3072, b_layer=i)
    dproj_shards = dproj.reshape(dproj.shape[0], N_CHIPS, IN_COLS // N_CHIPS).transpose(1, 0, 2)
    wgrad("w_in", sv["h"], dproj_shards, "mm_gw_in", b_shards=True)
    dx, g["norm_mix"] = _rms_bwd(sv["x"], w["norm_mix"][i:i + 1], dh, dx1, nm("rms_mix"))
    return dx, g


def _local_step(x, target, w):
    length = x.shape[0]
    tabs = _rope_tables(length)
    ssm_names = ("ssm_a_re", "ssm_a_im", "ssm_b_re", "ssm_b_im", "ssm_c_re", "ssm_c_im", "ssm_log_dt")
    w = dict(w)
    preps = [jax.vjp(_ssm_prep, *[w[n][i] for n in ssm_names]) for i in range(DEPTH)]
    w["ssm"] = [p[0] for p in preps]
    saved = []
    for i in range(DEPTH):
        x, sv = _layer_fwd(i, x, w, tabs)
        saved.append(sv)

    def loss_fn(ctx, xv, tv, gv):
        val, vjp = jax.vjp(_loss_fn, xv, gv, tv)
        dx, dg, _ = vjp(jnp.ones((1, 1), F32))
        return dx, dg, val + jnp.zeros((1, LANE), F32)

    gfin = w["norm_final"].reshape(1, D_MODEL)
    dx, g_final, loss = _rowwise(loss_fn, [(x, 0, D_MODEL, 0), (target, 0, D_MODEL, 0)], [gfin],
                                 [(D_MODEL, F32)], [(1, D_MODEL), (1, LANE)], "loss")
    gb = {}
    for n in BIG:
        depth, ra, cb = w[n].shape
        gb[n] = lax.empty((N_CHIPS, depth) + ((ra // N_CHIPS, cb) if BIG_AXIS[n] == 1 else (ra, cb // N_CHIPS)), F32)
    layer_grads = [None] * DEPTH
    for i in reversed(range(DEPTH)):
        dx, layer_grads[i] = _layer_bwd(i, dx, saved[i], w, tabs, gb)
    grads = dict(gb)
    for n in layer_grads[0]:
        if n != "ssm":
            grads[n] = jnp.stack([lg[n] for lg in layer_grads])
    ssm_g = [preps[i][1](layer_grads[i]["ssm"]) for i in range(DEPTH)]
    for j, n in enumerate(ssm_names):
        grads[n] = jnp.stack([sg[j] for sg in ssm_g])
    grads["norm_final"] = g_final.reshape(D_MODEL)
    for n in ("norm_mix", "norm_ffn", "b_gate", "ssm_d"):
        grads[n] = grads[n].reshape(grads[n].shape[0], -1)
    return loss, dx, grads


COLS = 1024
ANY_SPEC = pl.BlockSpec(memory_space=pl.ANY)


def _place():
    return lax.axis_index("x"), lax.axis_index("y"), lax.axis_index("c")


def _other_chips(x, y):
    return [(1 - x, y), (x, 1 - y), (1 - x, 1 - y)]


def _remote(src, dst, send_sems, recv_sems, k, to):
    return pltpu.make_async_remote_copy(src_ref=src, dst_ref=dst, send_sem=send_sems.at[k], recv_sem=recv_sems.at[k],
                                        device_id=to, device_id_type=MESH_ID)


def _comm_call(body, name, out_shape, n_sems, n_local, args):
    return pl.pallas_call(
        body, name=name, out_shape=out_shape, in_specs=[ANY_SPEC] * len(args),
        out_specs=[ANY_SPEC] * len(out_shape),
        scratch_shapes=[pltpu.SemaphoreType.DMA((n_sems,)), pltpu.SemaphoreType.DMA((n_sems,)),
                        pltpu.SemaphoreType.DMA((n_local,))],
    )(*args)


def _gather_weights(shards, name):
    nt = len(shards)
    per = 9
    hds = [s.shape[0] // 2 for s in shards]

    def body(*refs):
        srcs, outs = refs[:nt], refs[nt:2 * nt]
        send_sems, recv_sems, _ = refs[2 * nt:]
        x, y, c = _place()
        me, sibling = (x, y, c), (x, y, 1 - c)
        xn, yn, dg = _other_chips(x, y)

        def blk(t, chip, hc, part=None):
            lo, n = hc * hds[t], hds[t]
            if part is not None:
                first_n = (n + 1) // 2
                lo, n = (lo, first_n) if part == 0 else (lo + first_n, n - first_n)
            return outs[t].at[2 * chip[0] + chip[1], pl.ds(lo, n)] if n else None

        def copy(t, k, src, dst, to):
            return _remote(src, dst, send_sems, recv_sems, per * t + k, to)

        def arrived(t, k, ref):
            copy(t, k, ref, ref, me).wait_recv()

        sent = []

        def start(t, k, ref, to):
            if ref is not None:
                sent.append(copy(t, k, ref, ref, to))
                sent[-1].start()

        for t in range(nt):
            own = srcs[t].at[pl.ds(c * hds[t], hds[t])]
            for k, chip in enumerate((xn, yn)):
                sent.append(copy(t, k, own, blk(t, (x, y), c), (*chip, c)))
                sent[-1].start()
        for t in range(nt):
            sent.append(copy(t, 8, srcs[t], outs[t].at[2 * x + y], sibling))
            sent[-1].start()
        for t in range(nt):
            arrived(t, 0, blk(t, xn, c))
            start(t, 3, blk(t, xn, c, 1), (*yn, c))
            start(t, 4, blk(t, xn, c), sibling)
            arrived(t, 1, blk(t, yn, c))
            start(t, 2, blk(t, yn, c, 0), (*xn, c))
            start(t, 5, blk(t, yn, c), sibling)
        for t in range(nt):
            arrived(t, 2, blk(t, dg, c, 0))
            start(t, 6, blk(t, dg, c, 0), sibling)
            if blk(t, dg, c, 1) is not None:
                arrived(t, 3, blk(t, dg, c, 1))
                start(t, 7, blk(t, dg, c, 1), sibling)
        for t in range(nt):
            arrived(t, 4, blk(t, xn, 1 - c))
            arrived(t, 5, blk(t, yn, 1 - c))
            arrived(t, 6, blk(t, dg, 1 - c, 0))
            if blk(t, dg, 1 - c, 1) is not None:
                arrived(t, 7, blk(t, dg, 1 - c, 1))
            arrived(t, 8, outs[t].at[2 * x + y])
        for cp in sent:
            cp.wait_send()

    out_shape = [jax.ShapeDtypeStruct((N_CHIPS,) + s.shape, s.dtype) for s in shards]
    return _comm_call(body, name, out_shape, per * nt, 1, shards)


def _swap_halves(gs, name):
    nt = len(gs)
    hds = [g.shape[1] // 2 for g in gs]

    def body(*refs):
        srcs, outs = refs[:nt], refs[nt:2 * nt]
        send_sems, recv_sems, _ = refs[2 * nt:]
        x, y, c = _place()
        cps = [_remote(srcs[t].at[s, pl.ds((1 - c) * hds[t], hds[t])], outs[t].at[s], send_sems, recv_sems,
                       N_CHIPS * t + s, (x, y, 1 - c)) for t in range(nt) for s in range(N_CHIPS)]
        for cp in cps:
            cp.start()
        for cp in cps:
            cp.wait()

    out_shape = [jax.ShapeDtypeStruct((N_CHIPS, g.shape[1] // 2) + g.shape[2:], g.dtype) for g in gs]
    return _comm_call(body, name, out_shape, N_CHIPS * nt, 1, gs)


def _exchange_shards(parts, name, swap=()):
    nt, ns = len(parts), len(swap)
    hds = [g.shape[1] // 2 for g in swap]

    def body(*refs):
        srcs, swap_srcs = refs[:nt], refs[nt:nt + ns]
        outs, swap_outs = refs[nt + ns:2 * nt + ns], refs[2 * nt + ns:2 * (nt + ns)]
        send_sems, recv_sems, _ = refs[2 * (nt + ns):]
        x, y, c = _place()
        cps = [_remote(srcs[t].at[2 * chip[0] + chip[1]], outs[t].at[j], send_sems, recv_sems, 3 * t + j, (*chip, c))
               for t in range(nt) for j, chip in enumerate(_other_chips(x, y))]
        cps += [_remote(swap_srcs[t].at[s, pl.ds((1 - c) * hds[t], hds[t])], swap_outs[t].at[s], send_sems, recv_sems,
                        3 * nt + N_CHIPS * t + s, (x, y, 1 - c)) for t in range(ns) for s in range(N_CHIPS)]
        for cp in cps:
            cp.start()
        for cp in cps:
            cp.wait()

    out_shape = [jax.ShapeDtypeStruct((N_CHIPS - 1,) + p.shape[1:], p.dtype) for p in parts]
    out_shape += [jax.ShapeDtypeStruct((N_CHIPS, g.shape[1] // 2) + g.shape[2:], g.dtype) for g in swap]
    res = _comm_call(body, name, out_shape, 3 * nt + N_CHIPS * ns, 1, list(parts) + list(swap))
    return res[:nt], res[nt:]


def _join_halves(reds, name):
    nt = len(reds)
    hds = [r.shape[1] for r in reds]

    def body(*refs):
        srcs, outs = refs[:nt], refs[nt:2 * nt]
        send_sems, recv_sems, _ = refs[2 * nt:]
        x, y, c = _place()
        cps = [_remote(srcs[t].at[0], outs[t].at[pl.ds(c * hds[t], hds[t])], send_sems, recv_sems, t, (x, y, 1 - c))
               for t in range(nt)]
        for cp in cps:
            cp.start()
        for t in range(nt):
            _remote(srcs[t].at[0], outs[t].at[pl.ds((1 - c) * hds[t], hds[t])], send_sems, recv_sems, t,
                    (x, y, c)).wait_recv()
        for cp in cps:
            cp.wait_send()

    out_shape = [jax.ShapeDtypeStruct((2 * r.shape[1],) + r.shape[2:], r.dtype) for r in reds]
    return _comm_call(body, name, out_shape, nt, 1, reds)


SUM_BLOCK_ELEMS = 768 * 1024
ADAMW_BLOCK_ELEMS = 512 * 1024


def _sum_windows(parts, lead, name, out_dtypes=(F32,)):
    a, b = parts[0][0].shape[2:]
    ta = _tile(a, max(SUBLANE, SUM_BLOCK_ELEMS // b), 2 * SUBLANE)
    offs = jnp.stack([jnp.stack([jnp.asarray(o, jnp.int32) for o in off]) for _, off in parts])
    n_in = len(parts)

    def body(off_ref, *refs):
        acc = refs[0][...].astype(F32)
        for r in refs[1:n_in]:
            acc = acc + r[...].astype(F32)
        for r in refs[n_in:]:
            r[...] = acc.astype(r.dtype)

    in_specs = [pl.BlockSpec((1, 1, ta, b), lambda p, q, i, off, k=k: (off[k, 0] + p, off[k, 1] + q, i, 0))
                for k in range(n_in)]
    o_spec = pl.BlockSpec((1, 1, ta, b), lambda p, q, i, off: (p, q, i, 0))
    return pl.pallas_call(
        body, name=name, out_shape=[jax.ShapeDtypeStruct(tuple(lead) + (a, b), dt) for dt in out_dtypes],
        grid_spec=pltpu.PrefetchScalarGridSpec(
            num_scalar_prefetch=1, grid=tuple(lead) + (a // ta,), in_specs=in_specs,
            out_specs=[o_spec] * len(out_dtypes)),
        compiler_params=_params("arbitrary", "arbitrary", "arbitrary"),
    )(offs, *[arr for arr, _ in parts])


def _reduce_scatter(gs, names, wire, groups):
    x, y, c = _place()
    theirs = dict(zip(groups[0], _swap_halves([gs[i] for i in groups[0]], "rs_swap_halves")))
    pairs, others = {}, {}
    for k, group in enumerate(groups):
        for i in group:
            g = gs[i]
            pairs[i] = _sum_windows([(g, (0, c * (g.shape[1] // 2))), (theirs[i], (0, 0))],
                                    (N_CHIPS, g.shape[1] // 2), "rs_sum_pair_" + names[i], (F32, wire[i]))
        nxt = groups[k + 1] if k + 1 < len(groups) else []
        got, swapped = _exchange_shards([pairs[i][1] for i in group], "rs_exchange_%d" % k, [gs[i] for i in nxt])
        others.update(zip(group, got))
        theirs.update(zip(nxt, swapped))
    pairs, others = [pairs[i] for i in range(len(gs))], [others[i] for i in range(len(gs))]
    reds = [_sum_windows([(p[0], (2 * x + y, 0))] + [(o, (j, 0)) for j in range(N_CHIPS - 1)], (1, p[0].shape[1]),
                         "rs_sum_chips_" + n)[0] for p, o, n in zip(pairs, others, names)]
    joined = _join_halves(reds, "rs_join")
    return [lax.dynamic_update_slice_in_dim(j, r[0], c * r.shape[1], axis=0) for j, r in zip(joined, reds)]


def _adamw(wt, g, m, v, name):
    cols = wt.shape[-1]
    r = wt.size // cols
    tr = _tile(r, max(SUBLANE, ADAMW_BLOCK_ELEMS // max(cols, LANE)), SUBLANE)

    def body(w_ref, g_ref, m_ref, v_ref, d_ref, nm_ref, nv_ref):
        gv = g_ref[...]
        mn = ADAM_B1 * m_ref[...] + (1.0 - ADAM_B1) * gv
        vn = ADAM_B2 * v_ref[...] + (1.0 - ADAM_B2) * jnp.square(gv)
        m_hat = mn / (1.0 - ADAM_B1 ** ADAM_STEP)
        v_hat = vn / (1.0 - ADAM_B2 ** ADAM_STEP)
        d_ref[...] = -ADAM_LR * (m_hat / (jnp.sqrt(v_hat) + ADAM_EPS) + ADAM_WD * w_ref[...])
        nm_ref[...] = mn
        nv_ref[...] = vn

    spec = pl.BlockSpec((tr, cols), lambda i: (i, 0))
    shp = jax.ShapeDtypeStruct((r, cols), F32)
    res = pl.pallas_call(
        body, name=name, grid=(r // tr,), in_specs=[spec] * 4, out_specs=[spec] * 3, out_shape=[shp] * 3,
        compiler_params=_params("parallel"),
    )(*[t.reshape(r, cols) for t in (wt, g, m, v)])
    return [t.reshape(wt.shape) for t in res]


def _join_shards(n, piece):
    _, depth, a, b = piece.shape
    if BIG_AXIS[n] == 2:
        return piece.transpose(1, 2, 0, 3).reshape(depth, a, N_CHIPS * b)
    return piece.transpose(1, 0, 2, 3).reshape(depth, N_CHIPS * a, b)


RS_GROUPS = (("w_ffn_out", "w_mix_o", "small"), ("w_ffn_in",), ("w_in", "w_attn_o", "w_conv_o", "w_ssm_glu", "w_ssm_o"))


SMALL_PART_ROWS = 2 * SUBLANE


def _rows_of(t):
    return -(-t.size // COLS)


def _pack_small(ts):
    rows = [jnp.pad(t.reshape(-1), (0, _rows_of(t) * COLS - t.size)).reshape(-1, COLS) for t in ts]
    total = sum(r.shape[0] for r in rows)
    part = -(-total // (N_DEV * SMALL_PART_ROWS)) * SMALL_PART_ROWS
    rows.append(jnp.zeros((N_DEV * part - total, COLS), F32))
    return jnp.concatenate(rows, axis=0).reshape(N_CHIPS, 2, part, COLS)


def _unpack_small(buf, like):
    buf, out, r0 = buf.reshape(-1, COLS), [], 0
    for t in like:
        out.append(buf[r0:r0 + _rows_of(t)].reshape(-1)[:t.size].reshape(t.shape))
        r0 += _rows_of(t)
    return out


def kernel(x, norm_mix, w_in, b_gate, attn_sinks, w_attn_o, conv_w, w_conv_o, ssm_a_re, ssm_a_im, ssm_b_re, ssm_b_im, ssm_c_re, ssm_c_im, ssm_d, ssm_log_dt, w_ssm_glu, w_ssm_o, w_mix_o, norm_ffn, w_ffn_in, w_ffn_out, norm_final, loss_target, m_norm_mix, m_w_in, m_b_gate, m_attn_sinks, m_w_attn_o, m_conv_w, m_w_conv_o, m_ssm_a_re, m_ssm_a_im, m_ssm_b_re, m_ssm_b_im, m_ssm_c_re, m_ssm_c_im, m_ssm_d, m_ssm_log_dt, m_w_ssm_glu, m_w_ssm_o, m_w_mix_o, m_norm_ffn, m_w_ffn_in, m_w_ffn_out, m_norm_final, v_norm_mix, v_w_in, v_b_gate, v_attn_sinks, v_w_attn_o, v_conv_w, v_w_conv_o, v_ssm_a_re, v_ssm_a_im, v_ssm_b_re, v_ssm_b_im, v_ssm_c_re, v_ssm_c_im, v_ssm_d, v_ssm_log_dt, v_w_ssm_glu, v_w_ssm_o, v_w_mix_o, v_norm_ffn, v_w_ffn_in, v_w_ffn_out, v_norm_final):
    a = dict(zip(ARG_NAMES, (
        x, norm_mix, w_in, b_gate, attn_sinks, w_attn_o, conv_w, w_conv_o, ssm_a_re, ssm_a_im, ssm_b_re, ssm_b_im,
        ssm_c_re, ssm_c_im, ssm_d, ssm_log_dt, w_ssm_glu, w_ssm_o, w_mix_o, norm_ffn, w_ffn_in, w_ffn_out, norm_final,
        loss_target, m_norm_mix, m_w_in, m_b_gate, m_attn_sinks, m_w_attn_o, m_conv_w, m_w_conv_o, m_ssm_a_re,
        m_ssm_a_im, m_ssm_b_re, m_ssm_b_im, m_ssm_c_re, m_ssm_c_im, m_ssm_d, m_ssm_log_dt, m_w_ssm_glu, m_w_ssm_o,
        m_w_mix_o, m_norm_ffn, m_w_ffn_in, m_w_ffn_out, m_norm_final, v_norm_mix, v_w_in, v_b_gate, v_attn_sinks,
        v_w_attn_o, v_conv_w, v_w_conv_o, v_ssm_a_re, v_ssm_a_im, v_ssm_b_re, v_ssm_b_im, v_ssm_c_re, v_ssm_c_im,
        v_ssm_d, v_ssm_log_dt, v_w_ssm_glu, v_w_ssm_o, v_w_mix_o, v_norm_ffn, v_w_ffn_in, v_w_ffn_out, v_norm_final)))
    px, py, _ = _place()
    chip = 2 * px + py

    gathered = BIG + ("conv_w",)
    own = [a[n].astype(BF16) for n in BIG] + [a["conv_w"]]
    gath = _gather_weights(own, "gather_weights")
    w = {n: _join_shards(n, p) for n, p in zip(gathered, gath)}
    for n in SMALL:
        w[n] = a[n]

    loss, dx, grads = _local_step(a["x"][0], a["loss_target"][0], w)

    small_names = SMALL + ("conv_w",)
    small = _pack_small([grads[n] for n in small_names])
    rs_names = BIG + ("small",)
    groups = [[rs_names.index(n) for n in grp] for grp in RS_GROUPS]
    rs = _reduce_scatter([grads[n] for n in BIG] + [small], rs_names, (BF16,) * len(BIG) + (F32,), groups)
    red = dict(zip(BIG, rs))
    small_all = _gather_weights([rs[-1]], "gather_small")[0]
    red.update(zip(small_names, _unpack_small(small_all, [grads[n] for n in small_names])))
    lane = a["conv_w"].shape[2]
    red["conv_w"] = lax.dynamic_slice_in_dim(red["conv_w"], chip * lane, lane, axis=2)

    loss_all = lax.psum(loss[0, 0], ("x", "y", "c"))
    deltas, new_m, new_v = [], [], []
    for n in WEIGHTS:
        d, mn, vn = _adamw(a[n], red[n], a["m_" + n], a["v_" + n], "adamw_" + n)
        deltas.append(d)
        new_m.append(mn)
        new_v.append(vn)
    return (loss_all, dx[None], *[red[n] for n in WEIGHTS], *deltas, *new_m, *new_v)
```

```python
import math

import jax
import jax.numpy as jnp
from jax import lax
from jax.experimental import pallas as pl
from jax.experimental.pallas import tpu as pltpu

F32 = jnp.float32
BF16 = jnp.bfloat16

D_MODEL = 1024
DEPTH = 4
N_Q_HEADS = 8
N_KV_HEADS = 2
HEAD_DIM = 64
Q_GROUP = N_Q_HEADS // N_KV_HEADS
WINDOW = 128
BLOCK = 128
ROPE_THETA = 500000.0
ROT_DIM = HEAD_DIM // 4
ATTN_WIDTH = N_Q_HEADS * HEAD_DIM
KV_WIDTH = N_KV_HEADS * HEAD_DIM
NEG_INF = -1e30
CONV_WIDTH = 512
CONV_K = 3
SSM_WIDTH = 512
SSM_GROUP = 16
SSM_GROUPS = 32
SSM_STATE = 64
SSM_LANES = SSM_GROUPS * SSM_STATE
GATE_WIDTH = 3 * D_MODEL
FFN_HIDDEN = 2816
NORM_EPS = 1e-6
IN_COLS = 5888
C_Q, C_K, C_V, C_CB, C_CC, C_CX, C_U, C_G = 0, 512, 640, 768, 1280, 1792, 2304, 2816

ADAM_LR = 0.001
ADAM_B1 = 0.9
ADAM_B2 = 0.999
ADAM_EPS = 1e-08
ADAM_WD = 0.01
ADAM_STEP = 10

N_CHIPS = 4
N_DEV = 8
MESH_ID = pl.DeviceIdType.MESH

VMEM_LIMIT_BYTES = 48 * 1024 * 1024
LANE = 128
SUBLANE = 8
SCAN_ROWS = 8
SCAN_CHUNK = 128

BIG = ("w_in", "w_attn_o", "w_conv_o", "w_ssm_glu", "w_ssm_o", "w_mix_o", "w_ffn_in", "w_ffn_out")
BIG_AXIS = {"w_in": 2, "w_attn_o": 2, "w_conv_o": 2, "w_ssm_glu": 1, "w_ssm_o": 2, "w_mix_o": 1,
            "w_ffn_in": 2, "w_ffn_out": 1, "conv_w": 2}
SMALL = ("norm_mix", "b_gate", "attn_sinks", "ssm_a_re", "ssm_a_im", "ssm_b_re", "ssm_b_im",
         "ssm_c_re", "ssm_c_im", "ssm_d", "ssm_log_dt", "norm_ffn", "norm_final")
WEIGHTS = ("norm_mix", "w_in", "b_gate", "attn_sinks", "w_attn_o", "conv_w", "w_conv_o", "ssm_a_re",
           "ssm_a_im", "ssm_b_re", "ssm_b_im", "ssm_c_re", "ssm_c_im", "ssm_d", "ssm_log_dt",
           "w_ssm_glu", "w_ssm_o", "w_mix_o", "norm_ffn", "w_ffn_in", "w_ffn_out", "norm_final")
ARG_NAMES = ("x",) + WEIGHTS + ("loss_target",) + tuple("m_" + n for n in WEIGHTS) + tuple(
    "v_" + n for n in WEIGHTS)


def _params(*sem):
    return pltpu.CompilerParams(dimension_semantics=sem if sem else None,
                                vmem_limit_bytes=VMEM_LIMIT_BYTES)


def _tile(dim, cap, align):
    t = min(cap, dim) // align * align
    while t >= align:
        if dim % t == 0:
            return t
        t -= align
    return dim


_DOT_DIMS = {"nn": (((1,), (0,)), ((), ())), "nt": (((1,), (1,)), ((), ())), "tn": (((0,), (0,)), ((), ()))}


def _mm(a, b, mode, name, out_dtype=F32, add=None, tm_cap=512, tn_cap=3072, tk_cap=1024, b_layer=None, into=None,
        b_shards=False):
    bshape = b.shape if b_layer is None else b.shape[1:]
    if b_shards:
        assert mode == "tn" and b_layer is None and into is not None and into[2] == 2
        bshape = (b.shape[1], N_CHIPS * b.shape[2])
    if mode == "nn":
        (m, k), (k2, n) = a.shape, bshape
    elif mode == "nt":
        (m, k), (n, k2) = a.shape, bshape
    else:
        (k, m), (k2, n) = a.shape, bshape
    assert k == k2, (name, a.shape, b.shape)
    tm, tn, tk = _tile(m, tm_cap, LANE), _tile(n, tn_cap, LANE), _tile(k, tk_cap, LANE)
    if into is not None:
        buf, layer, axis = into
        _, _, ra, cb = buf.shape
        if axis == 1:
            tm = m
        elif b_shards:
            tn = cb
        else:
            tn = _tile(cb, tn_cap, LANE)
            assert cb % tn == 0 and tn % LANE == 0, (name, cb, tn)
    nk = k // tk
    dims = _DOT_DIMS[mode]

    def body(a_ref, b_ref, *rest):
        rest = list(rest)
        add_ref = rest.pop(0) if add is not None else None
        if into is not None:
            rest.pop(0)
        o_ref, acc = rest
        kk = pl.program_id(2)
        part = lax.dot_general(a_ref[...].astype(BF16), b_ref[...].astype(BF16), dims, preferred_element_type=F32)

        def finish(r):
            if add is not None:
                r = r + add_ref[...]
            o_ref[...] = r.astype(o_ref.dtype).reshape(o_ref.shape)

        if nk == 1:
            finish(part)
            return

        @pl.when(kk == 0)
        def _():
            acc[...] = part

        @pl.when(kk > 0)
        def _():
            acc[...] += part

        @pl.when(kk == nk - 1)
        def _():
            finish(acc[...])

    if mode == "tn":
        a_spec = pl.BlockSpec((tk, tm), lambda i, j, kk: (kk, i))
    else:
        a_spec = pl.BlockSpec((tm, tk), lambda i, j, kk: (i, kk))
    lead = () if b_layer is None else (None,)
    at = (lambda *ix: ix) if b_layer is None else (lambda *ix: (b_layer,) + ix)
    if b_shards:
        b_spec = pl.BlockSpec((None, tk, tn), lambda i, j, kk: (j, kk, 0))
    elif mode == "nt":
        b_spec = pl.BlockSpec(lead + (tn, tk), lambda i, j, kk: at(j, kk))
    else:
        b_spec = pl.BlockSpec(lead + (tk, tn), lambda i, j, kk: at(kk, j))
    o_spec = pl.BlockSpec((tm, tn), lambda i, j, kk: (i, j))
    in_specs, args = [a_spec, b_spec], [a, b]
    if add is not None:
        in_specs.append(o_spec)
        args.append(add)
    out_shape, aliases = jax.ShapeDtypeStruct((m, n), out_dtype), {}
    if into is not None:
        in_specs.append(pl.BlockSpec(memory_space=pl.ANY))
        aliases = {len(args): 0}
        args.append(buf)
        out_shape = jax.ShapeDtypeStruct(buf.shape, buf.dtype)
        if axis == 1:
            o_spec = pl.BlockSpec((N_CHIPS, None, ra, tn), lambda i, j, kk: (0, layer, 0, j))
        else:
            per = cb // tn
            o_spec = pl.BlockSpec((None, None, tm, tn), lambda i, j, kk: (j // per, layer, i, j % per))
    return pl.pallas_call(
        body, name=name, grid=(m // tm, n // tn, nk), in_specs=in_specs, out_specs=o_spec,
        out_shape=out_shape, input_output_aliases=aliases,
        scratch_shapes=[pltpu.VMEM((tm, tn) if nk > 1 else (SUBLANE, LANE), F32)],
        compiler_params=_params("parallel", "parallel", "arbitrary"),
    )(*args)


def _rowwise(fn, rows, pars, outs, accs, name, tm_cap=256):
    length = rows[0][0].shape[0]
    tm = _tile(length, tm_cap, LANE)
    n = length // tm
    in_specs, args, counts = [], [], []
    for arr, c0, cw, shift in rows:
        bw = math.gcd(c0, cw) if c0 else cw
        assert bw % LANE == 0 or (c0 == 0 and cw == arr.shape[1]), (name, c0, cw)
        cnt = cw // bw
        counts.append(cnt)
        for j in range(cnt):
            in_specs.append(pl.BlockSpec(
                (tm, bw), lambda i, j=j, c0=c0, bw=bw, shift=shift: (jnp.clip(i + shift, 0, n - 1), c0 // bw + j)))
            args.append(arr)
    for p in pars:
        in_specs.append(pl.BlockSpec(p.shape, lambda i: (0, 0)))
        args.append(p)
    out_shape = [jax.ShapeDtypeStruct((length, w), dt) for w, dt in outs]
    out_specs = [pl.BlockSpec((tm, w), lambda i: (i, 0)) for w, _ in outs]
    out_shape += [jax.ShapeDtypeStruct((r, w), F32) for r, w in accs]
    out_specs += [pl.BlockSpec((r, w), lambda i: (0, 0)) for r, w in accs]
    n_in, n_out = len(args), len(outs)

    def body(*refs):
        i = pl.program_id(0)
        vals, p = [], 0
        for cnt in counts:
            blocks = [refs[p + j][...] for j in range(cnt)]
            p += cnt
            vals.append(blocks[0] if cnt == 1 else jnp.concatenate(blocks, axis=1))
        for _ in pars:
            vals.append(refs[p][...])
            p += 1
        res = fn((i, n), *vals)
        out_refs = refs[n_in:n_in + n_out]
        acc_refs = refs[n_in + n_out:]
        for r, v in zip(out_refs, res[:n_out]):
            r[...] = v.astype(r.dtype)
        if acc_refs:
            @pl.when(i == 0)
            def _():
                for r in acc_refs:
                    r[...] = jnp.zeros_like(r)
            for r, v in zip(acc_refs, res[n_out:]):
                r[...] += v

    res = pl.pallas_call(
        body, name=name, grid=(n,), in_specs=in_specs, out_specs=out_specs, out_shape=out_shape,
        compiler_params=_params("arbitrary"),
    )(*args)
    return res


def _rms(x, g):
    return x * lax.rsqrt(jnp.mean(x * x, axis=-1, keepdims=True) + NORM_EPS) * g


def _rms_fwd(x, g, name):
    return _rowwise(lambda ctx, xv, gv: (_rms(xv, gv),), [(x, 0, D_MODEL, 0)], [g],
                    [(D_MODEL, BF16)], [], name)[0]


def _rms_bwd(x, g, dh, dres, name):
    def fn(ctx, xv, dhv, drv, gv):
        _, vjp = jax.vjp(_rms, xv, gv)
        dx, dg = vjp(dhv)
        return dx + drv, dg
    return _rowwise(fn, [(x, 0, D_MODEL, 0), (dh, 0, D_MODEL, 0), (dres, 0, D_MODEL, 0)], [g],
                    [(D_MODEL, F32)], [(1, D_MODEL)], name)


def _rope_tables(length):
    pos = jnp.arange(length, dtype=F32)
    inv_freq = ROPE_THETA ** (-jnp.arange(0, ROT_DIM, 2, dtype=F32) / ROT_DIM)
    ang = pos[:, None] * inv_freq[None, :]
    cos, sin = jnp.cos(ang), jnp.sin(ang)
    half = ROT_DIM // 2
    ones = jnp.ones((length, HEAD_DIM - ROT_DIM), F32)
    zeros = jnp.zeros_like(ones)
    zh = jnp.zeros((length, half), F32)
    c64 = jnp.concatenate([cos, cos, ones], axis=1)
    s1 = jnp.concatenate([-sin, zh, zeros], axis=1)
    s2 = jnp.concatenate([zh, sin, zeros], axis=1)
    tile2 = lambda t: jnp.concatenate([t, t], axis=1)
    return tile2(c64), tile2(s1), tile2(s2)


def _lane_chunks(t):
    return [t[:, j * LANE:(j + 1) * LANE] for j in range(t.shape[1] // LANE)]


def _rope(t, c, s1, s2, n_rot):
    half = ROT_DIM // 2
    out = []
    for j, ch in enumerate(_lane_chunks(t)):
        if j < n_rot:
            ch = ch * c + pltpu.roll(ch, LANE - half, 1) * s1 + pltpu.roll(ch, half, 1) * s2
        out.append(ch)
    return jnp.concatenate(out, axis=1)


def _unrope(d, c, s1, s2, n_rot):
    half = ROT_DIM // 2
    out = []
    for j, ch in enumerate(_lane_chunks(d)):
        if j < n_rot:
            ch = ch * c + pltpu.roll(ch * s1, half, 1) + pltpu.roll(ch * s2, LANE - half, 1)
        out.append(ch)
    return jnp.concatenate(out, axis=1)


N_ROT_CHUNKS = (ATTN_WIDTH + KV_WIDTH) // LANE
QKV_WIDTH = ATTN_WIDTH + 2 * KV_WIDTH


def _split_fwd(proj, tabs, name):
    def fn(ctx, t, c, s1, s2):
        return (_rope(t, c, s1, s2, N_ROT_CHUNKS),)
    rows = [(proj, 0, QKV_WIDTH, 0)] + [(t, 0, LANE, 0) for t in tabs]
    return _rowwise(fn, rows, [], [(QKV_WIDTH, BF16)], [], name, tm_cap=BLOCK)[0]


def _split_bwd(d, tabs, name):
    def fn(ctx, dqv, dkcv, dkpv, dvcv, dvpv, c, s1, s2):
        i, n = ctx
        keep = (i < n - 1).astype(F32)
        dd = jnp.concatenate([dqv, dkcv + keep * dkpv, dvcv + keep * dvpv], axis=1)
        return (_unrope(dd, c, s1, s2, N_ROT_CHUNKS),)
    col = lambda row: row * HEAD_DIM
    rows = [(d, 0, ATTN_WIDTH, 0), (d, col(D_KC), KV_WIDTH, 0), (d, col(D_KP), KV_WIDTH, 1),
            (d, col(D_VC), KV_WIDTH, 0), (d, col(D_VP), KV_WIDTH, 1)] + [(t, 0, LANE, 0) for t in tabs]
    return _rowwise(fn, rows, [], [(QKV_WIDTH, BF16)], [], name, tm_cap=BLOCK)[0]


N_HEAD_ROWS = N_Q_HEADS + 2 * N_KV_HEADS


def _att_scores(j, q_ref, kvp_ref, kvc_ref, sink_ref):
    n = pl.program_id(0)
    rows = Q_GROUP * BLOCK
    qs = q_ref[Q_GROUP * j:Q_GROUP * (j + 1)].reshape(rows, HEAD_DIM)
    kb = jnp.concatenate([kvp_ref[j], kvc_ref[j]], axis=0)
    s = lax.dot_general(qs, kb, _DOT_DIMS["nt"], preferred_element_type=F32) * (HEAD_DIM ** -0.5)
    r = lax.broadcasted_iota(jnp.int32, (rows, 2 * BLOCK), 0)
    kj = lax.broadcasted_iota(jnp.int32, (rows, 2 * BLOCK), 1)
    delta = (r % BLOCK) + BLOCK - kj
    ok = (delta >= 0) & (delta < WINDOW) & ((kj >= BLOCK) | (n > 0))
    s = jnp.where(ok, s, NEG_INF)
    rh = lax.broadcasted_iota(jnp.int32, (rows, 1), 0) // BLOCK
    sinks = sink_ref[...]
    lane = lax.broadcasted_iota(jnp.int32, sinks.shape, 1)
    srow = lax.broadcasted_iota(jnp.int32, sinks.shape, 0)
    sink = jnp.zeros((rows, 1), F32)
    for g in range(Q_GROUP):
        val = jnp.sum(jnp.where((lane == g) & (srow == j), sinks, 0.0), keepdims=True)
        sink = jnp.where(rh == g, val, sink)
    m = jnp.maximum(jnp.max(s, axis=-1, keepdims=True), sink)
    p = jnp.exp(s - m)
    psink = jnp.exp(sink - m)
    denom = jnp.sum(p, axis=-1, keepdims=True) + psink
    vb = jnp.concatenate([kvp_ref[N_KV_HEADS + j], kvc_ref[N_KV_HEADS + j]], axis=0)
    return qs, kb, vb, p / denom, psink / denom, rh


def _att_specs(length):
    nb = length // BLOCK
    kv_rows = 2 * N_KV_HEADS
    q_spec = pl.BlockSpec((N_Q_HEADS, BLOCK, HEAD_DIM), lambda n: (0, n, 0))
    prev = pl.BlockSpec((kv_rows, BLOCK, HEAD_DIM), lambda n: (N_Q_HEADS // kv_rows, jnp.maximum(n - 1, 0), 0))
    cur = pl.BlockSpec((kv_rows, BLOCK, HEAD_DIM), lambda n: (N_Q_HEADS // kv_rows, n, 0))
    sink_spec = pl.BlockSpec((N_KV_HEADS, Q_GROUP), lambda n: (0, 0))
    return nb, q_spec, prev, cur, sink_spec


def _att_fwd(heads, sinks, name):
    length = heads.shape[1]
    nb, q_spec, prev, cur, sink_spec = _att_specs(length)

    def body(q_ref, kvp_ref, kvc_ref, sink_ref, o_ref):
        for j in range(N_KV_HEADS):
            _, _, vb, p, _, _ = _att_scores(j, q_ref, kvp_ref, kvc_ref, sink_ref)
            o = jnp.dot(p.astype(BF16), vb, preferred_element_type=F32)
            o_ref[Q_GROUP * j:Q_GROUP * (j + 1)] = o.reshape(Q_GROUP, BLOCK, HEAD_DIM).astype(o_ref.dtype)

    return pl.pallas_call(
        body, name=name, grid=(nb,), in_specs=[q_spec, prev, cur, sink_spec], out_specs=q_spec,
        out_shape=jax.ShapeDtypeStruct((N_Q_HEADS, length, HEAD_DIM), BF16),
        compiler_params=_params("arbitrary"),
    )(heads, heads, heads, sinks)


D_KC, D_KP, D_VC, D_VP = (N_Q_HEADS + i * N_KV_HEADS for i in range(4))
N_DHEAD_ROWS = N_Q_HEADS + 4 * N_KV_HEADS


def _att_bwd(heads, sinks, do, name):
    length = heads.shape[1]
    nb, q_spec, prev, cur, sink_spec = _att_specs(length)

    def body(q_ref, kvp_ref, kvc_ref, sink_ref, do_ref, d_ref, dsink_ref):
        @pl.when(pl.program_id(0) == 0)
        def _():
            dsink_ref[...] = jnp.zeros_like(dsink_ref)

        for j in range(N_KV_HEADS):
            qs, kb, vb, p, psink, rh = _att_scores(j, q_ref, kvp_ref, kvc_ref, sink_ref)
            dob = do_ref[Q_GROUP * j:Q_GROUP * (j + 1)].reshape(Q_GROUP * BLOCK, HEAD_DIM).astype(BF16)
            dv = lax.dot_general(p.astype(BF16), dob, _DOT_DIMS["tn"], preferred_element_type=F32)
            dp = lax.dot_general(dob, vb, _DOT_DIMS["nt"], preferred_element_type=F32)
            dsum = jnp.sum(p * dp, axis=-1, keepdims=True)
            ds = (p * (dp - dsum) * (HEAD_DIM ** -0.5)).astype(BF16)
            dq = jnp.dot(ds, kb, preferred_element_type=F32)
            dk = lax.dot_general(ds, qs, _DOT_DIMS["tn"], preferred_element_type=F32)
            d_ref[Q_GROUP * j:Q_GROUP * (j + 1)] = dq.reshape(Q_GROUP, BLOCK, HEAD_DIM)
            d_ref[D_KP + j] = dk[:BLOCK]
            d_ref[D_KC + j] = dk[BLOCK:]
            d_ref[D_VP + j] = dv[:BLOCK]
            d_ref[D_VC + j] = dv[BLOCK:]
            dsr = -psink * dsum
            row = lax.broadcasted_iota(jnp.int32, (SUBLANE, LANE), 0)
            upd = jnp.zeros((SUBLANE, LANE), F32)
            for g in range(Q_GROUP):
                val = jnp.sum(jnp.where(rh == g, dsr, 0.0), keepdims=True)
                upd = jnp.where(row == g, val, upd)
            dsink_ref[j] += upd

    d_spec = pl.BlockSpec((N_DHEAD_ROWS, BLOCK, HEAD_DIM), lambda n: (0, n, 0))
    return pl.pallas_call(
        body, name=name, grid=(nb,),
        in_specs=[q_spec, prev, cur, sink_spec, q_spec],
        out_specs=[d_spec, pl.BlockSpec((N_KV_HEADS, SUBLANE, LANE), lambda n: (0, 0, 0))],
        out_shape=[jax.ShapeDtypeStruct((N_DHEAD_ROWS, length, HEAD_DIM), F32),
                   jax.ShapeDtypeStruct((N_KV_HEADS, SUBLANE, LANE), F32)],
        compiler_params=_params("arbitrary"),
    )(heads, heads, heads, sinks, do)


def _to_heads(t, heads):
    return t.reshape(t.shape[0], heads, HEAD_DIM).transpose(1, 0, 2)


def _from_heads(t):
    return t.transpose(1, 0, 2).reshape(t.shape[1], t.shape[0] * HEAD_DIM)


def _shift_down(z, s):
    t = lax.broadcasted_iota(jnp.int32, z.shape, 0)
    return jnp.where(t >= s, pltpu.roll(z, s, 0), 0.0)


def _shift_up(z, s):
    t = lax.broadcasted_iota(jnp.int32, z.shape, 0)
    return jnp.where(t < z.shape[0] - s, pltpu.roll(z, z.shape[0] - s, 0), 0.0)


def _conv_specs(length):
    col = lambda c0: pl.BlockSpec((length, LANE), lambda j, c0=c0: (0, c0 // LANE + j))
    w_spec = pl.BlockSpec((CONV_K, LANE), lambda j: (0, j))
    o_spec = pl.BlockSpec((length, LANE), lambda j: (0, j))
    return col, w_spec, o_spec


def _conv_fwd(proj, w, name):
    length = proj.shape[0]
    col, w_spec, o_spec = _conv_specs(length)

    def body(cb_ref, cc_ref, cx_ref, w_ref, o_ref):
        z = cc_ref[...] * cx_ref[...]
        s = w_ref[0:1, :] * _shift_down(z, 2) + w_ref[1:2, :] * _shift_down(z, 1) + w_ref[2:3, :] * z
        o_ref[...] = (cb_ref[...] * s).astype(o_ref.dtype)

    return pl.pallas_call(
        body, name=name, grid=(CONV_WIDTH // LANE,),
        in_specs=[col(C_CB), col(C_CC), col(C_CX), w_spec], out_specs=o_spec,
        out_shape=jax.ShapeDtypeStruct((length, CONV_WIDTH), BF16),
        compiler_params=_params("arbitrary"),
    )(proj, proj, proj, w)


def _conv_bwd(proj, w, dy, name):
    length = proj.shape[0]
    col, w_spec, o_spec = _conv_specs(length)
    dw_spec = pl.BlockSpec((1, LANE), lambda j: (0, j))

    def body(cb_ref, cc_ref, cx_ref, w_ref, dy_ref, dcb_ref, dcc_ref, dcx_ref, dw0_ref, dw1_ref, dw2_ref):
        cc, cx, dyv = cc_ref[...], cx_ref[...], dy_ref[...]
        z = cc * cx
        w0, w1, w2 = w_ref[0:1, :], w_ref[1:2, :], w_ref[2:3, :]
        z1, z2 = _shift_down(z, 1), _shift_down(z, 2)
        s = w0 * z2 + w1 * z1 + w2 * z
        dcb_ref[...] = (dyv * s).astype(dcb_ref.dtype)
        ds = dyv * cb_ref[...]
        dw0_ref[...] = jnp.sum(ds * z2, axis=0, keepdims=True)
        dw1_ref[...] = jnp.sum(ds * z1, axis=0, keepdims=True)
        dw2_ref[...] = jnp.sum(ds * z, axis=0, keepdims=True)
        dz = w2 * ds + w1 * _shift_up(ds, 1) + w0 * _shift_up(ds, 2)
        dcc_ref[...] = (dz * cx).astype(dcc_ref.dtype)
        dcx_ref[...] = (dz * cc).astype(dcx_ref.dtype)

    act = jax.ShapeDtypeStruct((length, CONV_WIDTH), BF16)
    dws = jax.ShapeDtypeStruct((1, CONV_WIDTH), F32)
    return pl.pallas_call(
        body, name=name, grid=(CONV_WIDTH // LANE,),
        in_specs=[col(C_CB), col(C_CC), col(C_CX), w_spec, o_spec],
        out_specs=[o_spec, o_spec, o_spec, dw_spec, dw_spec, dw_spec],
        out_shape=[act, act, act, dws, dws, dws],
        compiler_params=_params("arbitrary"),
    )(proj, proj, proj, w, dy)


def _cmul(ar, ai, br, bi):
    return ar * br - ai * bi, ar * bi + ai * br


def _scan_tables(lr, li, reverse):
    pr, pi = [lr], [li]
    for _ in range(SCAN_ROWS - 1):
        nr, ni = _cmul(pr[-1], pi[-1], lr, li)
        pr.append(nr)
        pi.append(ni)
    row = jnp.arange(SCAN_ROWS)[:, None]
    mr, mi = [], []
    for s in (1, 2, 4):
        live = (row + s < SCAN_ROWS) if reverse else (row >= s)
        mr.append(jnp.where(live, pr[s - 1], 0.0))
        mi.append(jnp.where(live, pi[s - 1], 0.0))
    order = range(SCAN_ROWS - 1, -1, -1) if reverse else range(SCAN_ROWS)
    carry_r = jnp.concatenate([pr[d] for d in order], axis=0)
    carry_i = jnp.concatenate([pi[d] for d in order], axis=0)
    return jnp.stack(mr), jnp.stack(mi), carry_r, carry_i


N_SSM_CHUNKS = 4
CHUNK_STATES = SSM_LANES // N_SSM_CHUNKS
CHUNK_CHANNELS = SSM_WIDTH // N_SSM_CHUNKS


def _scan(b, lr, li, reverse, name, states=None):
    length = b.shape[0]
    mr, mi, cr, ci = _scan_tables(lr, li, reverse)
    nchunk = length // SCAN_CHUNK
    nblk = SCAN_CHUNK // SCAN_ROWS
    cw = CHUNK_STATES
    with_dlam = states is not None

    def body(*refs):
        if with_dlam:
            b_ref, mr_ref, mi_ref, cr_ref, ci_ref, s_ref, sp_ref, o_ref, dl_ref, carry = refs
        else:
            b_ref, mr_ref, mi_ref, cr_ref, ci_ref, o_ref, carry = refs
        step = pl.program_id(0)

        @pl.when(step == 0)
        def _():
            carry[...] = jnp.zeros_like(carry)
            if with_dlam:
                dl_ref[...] = jnp.zeros_like(dl_ref)

        blocks = range(nblk - 1, -1, -1) if reverse else range(nblk)
        for j in range(N_SSM_CHUNKS):
            re, im = slice(2 * cw * j, 2 * cw * j + cw), slice(2 * cw * j + cw, 2 * cw * (j + 1))
            tl = slice(cw * j, cw * (j + 1))
            c_r, c_i = carry[0:1, re], carry[0:1, im]
            acc_r = acc_i = jnp.zeros((SCAN_ROWS, cw), F32)
            for blk in blocks:
                r0 = blk * SCAN_ROWS
                xr = b_ref[r0:r0 + SCAN_ROWS, re]
                xi = b_ref[r0:r0 + SCAN_ROWS, im]
                for kk, s in enumerate((1, 2, 4)):
                    sh = SCAN_ROWS - s if reverse else s
                    rr, ri = pltpu.roll(xr, sh, 0), pltpu.roll(xi, sh, 0)
                    ar, ai = _cmul(mr_ref[kk, :, tl], mi_ref[kk, :, tl], rr, ri)
                    xr, xi = xr + ar, xi + ai
                ar, ai = _cmul(cr_ref[:, tl], ci_ref[:, tl], c_r, c_i)
                xr, xi = xr + ar, xi + ai
                o_ref[r0:r0 + SCAN_ROWS, re] = xr
                o_ref[r0:r0 + SCAN_ROWS, im] = xi
                edge = r0 if reverse else r0 + SCAN_ROWS - 1
                c_r = o_ref[edge:edge + 1, re]
                c_i = o_ref[edge:edge + 1, im]
                if with_dlam:
                    if r0 > 0:
                        pr, pi = s_ref[r0 - 1:r0 + SCAN_ROWS - 1, re], s_ref[r0 - 1:r0 + SCAN_ROWS - 1, im]
                    else:
                        live = (step < nchunk - 1).astype(F32)
                        row = lax.broadcasted_iota(jnp.int32, (SCAN_ROWS, cw), 0)
                        pr = jnp.where(row == 0, sp_ref[SCAN_ROWS - 1:SCAN_ROWS, re] * live,
                                       pltpu.roll(s_ref[0:SCAN_ROWS, re], 1, 0))
                        pi = jnp.where(row == 0, sp_ref[SCAN_ROWS - 1:SCAN_ROWS, im] * live,
                                       pltpu.roll(s_ref[0:SCAN_ROWS, im], 1, 0))
                    acc_r = acc_r + xr * pr + xi * pi
                    acc_i = acc_i + xi * pr - xr * pi
            carry[0:1, re] = c_r
            carry[0:1, im] = c_i
            if with_dlam:
                dl_ref[:, re] += acc_r
                dl_ref[:, im] += acc_i

    width = 2 * SSM_LANES
    chunk = (lambda i: (nchunk - 1 - i, 0)) if reverse else (lambda i: (i, 0))
    blk_spec = pl.BlockSpec((SCAN_CHUNK, width), chunk)
    m_spec = pl.BlockSpec((3, SCAN_ROWS, SSM_LANES), lambda i: (0, 0, 0))
    c_spec = pl.BlockSpec((SCAN_ROWS, SSM_LANES), lambda i: (0, 0))
    in_specs, args = [blk_spec, m_spec, m_spec, c_spec, c_spec], [b, mr, mi, cr, ci]
    out_specs, out_shape = blk_spec, jax.ShapeDtypeStruct(b.shape, F32)
    if with_dlam:
        assert reverse
        per = SCAN_CHUNK // SCAN_ROWS
        before = pl.BlockSpec((SCAN_ROWS, width), lambda i: (jnp.maximum((nchunk - 1 - i) * per - 1, 0), 0))
        in_specs += [blk_spec, before]
        args += [states, states]
        out_specs = [blk_spec, pl.BlockSpec((SCAN_ROWS, width), lambda i: (0, 0))]
        out_shape = [out_shape, jax.ShapeDtypeStruct((SCAN_ROWS, width), F32)]
    return pl.pallas_call(
        body, name=name, grid=(nchunk,), in_specs=in_specs, out_specs=out_specs, out_shape=out_shape,
        scratch_shapes=[pltpu.VMEM((SUBLANE, width), F32)],
        compiler_params=_params("arbitrary"),
    )(*args)


def _mm_bd(a, b, mode, name, out_dtype=F32, add=None):
    nc = N_SSM_CHUNKS
    if mode == "tn":
        k, wa, wb = a.shape[0], a.shape[1] // nc, b.shape[1] // nc
        tk = _tile(k, 1024, LANE)

        def body(a_ref, b_ref, o_ref):
            @pl.when(pl.program_id(1) == 0)
            def _():
                o_ref[...] = jnp.zeros_like(o_ref)

            o_ref[...] += lax.dot_general(a_ref[...].astype(BF16), b_ref[...].astype(BF16), _DOT_DIMS["tn"],
                                          preferred_element_type=F32)

        return pl.pallas_call(
            body, name=name, grid=(nc, k // tk),
            in_specs=[pl.BlockSpec((tk, wa), lambda j, kk: (kk, j)), pl.BlockSpec((tk, wb), lambda j, kk: (kk, j))],
            out_specs=pl.BlockSpec((None, wa, wb), lambda j, kk: (j, 0, 0)),
            out_shape=jax.ShapeDtypeStruct((nc, wa, wb), F32),
            compiler_params=_params("parallel", "arbitrary"),
        )(a, b)
    m, wa = a.shape[0], a.shape[1] // nc
    wo = b.shape[2] if mode == "nn" else b.shape[1]
    tm = _tile(m, 512, LANE)

    def body(a_ref, b_ref, *rest):
        r = lax.dot_general(a_ref[...].astype(BF16), b_ref[...].astype(BF16), _DOT_DIMS[mode],
                            preferred_element_type=F32)
        if add is not None:
            r = r + rest[0][...]
        rest[-1][...] = r.astype(out_dtype)

    o_spec = pl.BlockSpec((tm, wo), lambda i, j: (i, j))
    in_specs = [pl.BlockSpec((tm, wa), lambda i, j: (i, j)), pl.BlockSpec((None,) + b.shape[1:], lambda i, j: (j, 0, 0))]
    args = [a, b]
    if add is not None:
        in_specs.append(o_spec)
        args.append(add)
    return pl.pallas_call(
        body, name=name, grid=(m // tm, nc), in_specs=in_specs, out_specs=o_spec,
        out_shape=jax.ShapeDtypeStruct((m, nc * wo), out_dtype),
        compiler_params=_params("parallel", "parallel"),
    )(*args)


def _block_diag(t):
    g, a, b = t.shape
    per = g // N_SSM_CHUNKS
    eye = jnp.eye(per, dtype=t.dtype)
    t = t.reshape(N_SSM_CHUNKS, per, a, b)
    return (t[:, :, :, None, :] * eye[None, :, None, :, None]).reshape(N_SSM_CHUNKS, per * a, per * b)


def _ssm_prep(a_re, a_im, b_re, b_im, c_re, c_im, log_dt):
    dt = jnp.exp(log_dt)[:, None]
    er = jnp.exp(a_re * dt)
    lr, li = er * jnp.cos(a_im * dt), er * jnp.sin(a_im * dt)
    nr, ni = lr - 1.0, li
    den = a_re * a_re + a_im * a_im
    qr, qi = (nr * a_re + ni * a_im) / den, (ni * a_re - nr * a_im) / den
    bbr = qr[..., None] * b_re - qi[..., None] * b_im
    bbi = qr[..., None] * b_im + qi[..., None] * b_re
    bmat = jnp.concatenate([_block_diag(bbr.transpose(0, 2, 1)), _block_diag(bbi.transpose(0, 2, 1))], axis=2)
    cmat = jnp.concatenate([_block_diag(c_re.transpose(0, 2, 1)), -_block_diag(c_im.transpose(0, 2, 1))], axis=1)
    return lr.reshape(1, SSM_LANES), li.reshape(1, SSM_LANES), bmat, cmat


def _ssm_act(yc, u, d):
    return jax.nn.gelu(yc + d * u)


def _glu(ys, z):
    return ys * jax.nn.sigmoid(z)


def _merge(ya, yc, ys, gl, b):
    gates = jax.nn.sigmoid(gl + b)
    return gates[:, :D_MODEL] * ya + gates[:, D_MODEL:2 * D_MODEL] * yc + gates[:, 2 * D_MODEL:] * ys


def _swiglu(gt, up):
    return jax.nn.silu(gt) * up


def _mm_swiglu(a, b, layer, name):
    m, k = a.shape
    hid = b.shape[2] // 2
    tm, tn = _tile(m, 512, LANE), _tile(hid, 1536, LANE)
    nj = hid // tn

    def body(a_ref, bg_ref, bu_ref, g_ref, u_ref, act_ref):
        av = a_ref[...].astype(BF16)
        gt = jnp.dot(av, bg_ref[...].astype(BF16), preferred_element_type=F32)
        up = jnp.dot(av, bu_ref[...].astype(BF16), preferred_element_type=F32)
        g_ref[...] = gt
        u_ref[...] = up
        act_ref[...] = _swiglu(gt, up).astype(act_ref.dtype)

    o_spec = pl.BlockSpec((tm, tn), lambda i, j: (i, j))
    half = jax.ShapeDtypeStruct((m, hid), F32)
    return pl.pallas_call(
        body, name=name, grid=(m // tm, nj),
        in_specs=[pl.BlockSpec((tm, k), lambda i, j: (i, 0)),
                  pl.BlockSpec((None, k, tn), lambda i, j: (layer, 0, j)),
                  pl.BlockSpec((None, k, tn), lambda i, j: (layer, 0, j + nj))],
        out_specs=[o_spec, o_spec, o_spec], out_shape=[half, half, jax.ShapeDtypeStruct((m, hid), BF16)],
        compiler_params=_params("parallel", "parallel"),
    )(a, b, b)


def _mm_dswiglu(dy, b, layer, gt, up, name):
    m, n = dy.shape
    hid = b.shape[1]
    tm = _tile(m, 256, LANE)

    def body(dy_ref, b_ref, g_ref, u_ref, o_ref):
        dact = lax.dot_general(dy_ref[...].astype(BF16), b_ref[...].astype(BF16), _DOT_DIMS["nt"],
                               preferred_element_type=F32)
        _, vjp = jax.vjp(_swiglu, g_ref[...], u_ref[...])
        o_ref[...] = jnp.concatenate(vjp(dact), axis=1).astype(o_ref.dtype)

    row = lambda w: pl.BlockSpec((tm, w), lambda i: (i, 0))
    return pl.pallas_call(
        body, name=name, grid=(m // tm,),
        in_specs=[row(n), pl.BlockSpec((None, hid, n), lambda i: (layer, 0, 0)), row(hid), row(hid)],
        out_specs=row(2 * hid), out_shape=jax.ShapeDtypeStruct((m, 2 * hid), BF16),
        compiler_params=_params("parallel"),
    )(dy, b, gt, up)


def _loss_fn(x, g, t):
    e = _rms(x, g) - t
    per_tok = jnp.mean(e * e, axis=-1, keepdims=True)
    return 0.5 * jnp.sum(per_tok, axis=0, keepdims=True)


def _vjp_rowwise(f, n_row, cot_dtype=F32):
    def fn(ctx, *vals):
        prim = vals[:n_row] + vals[n_row + 1:]
        _, vjp = jax.vjp(f, *prim)
        return vjp(vals[n_row].astype(cot_dtype))
    return fn


def _layer_fwd(i, x, w, tabs):
    nm = lambda s: "l%d_%s" % (i, s)
    sv = {"x": x}
    h = _rms_fwd(x, w["norm_mix"][i:i + 1], nm("rms_mix"))
    proj = _mm(h, w["w_in"], "nn", nm("mm_in"), b_layer=i)
    qkv = _split_fwd(proj, tabs, nm("rope"))
    heads = _to_heads(qkv, N_HEAD_ROWS)
    sinks = w["attn_sinks"][i].reshape(N_KV_HEADS, Q_GROUP)
    att = _from_heads(_att_fwd(heads, sinks, nm("att")))
    conv = _conv_fwd(proj, w["conv_w"][i], nm("conv"))
    lr, li, bmat, cmat = w["ssm"][i]
    u = proj[:, C_U:C_G]
    bu = _mm_bd(u, bmat, "nn", nm("mm_bu"))
    states = _scan(bu, lr, li, False, nm("scan"))
    yc = _mm_bd(states, cmat, "nn", nm("mm_c"))
    d = w["ssm_d"][i:i + 1]
    ys = _rowwise(lambda ctx, a, b, c: (_ssm_act(a, b, c),), [(yc, 0, SSM_WIDTH, 0), (u, 0, SSM_WIDTH, 0)], [d],
                  [(SSM_WIDTH, F32)], [], nm("ssm_act"))[0]
    z = _mm(ys, w["w_ssm_glu"], "nn", nm("mm_glu"), b_layer=i)
    sg = _rowwise(lambda ctx, a, b: (_glu(a, b),), [(ys, 0, SSM_WIDTH, 0), (z, 0, SSM_WIDTH, 0)], [],
                  [(SSM_WIDTH, BF16)], [], nm("glu"))[0]
    ya = _mm(att, w["w_attn_o"], "nn", nm("mm_ao"), b_layer=i)
    yv = _mm(conv, w["w_conv_o"], "nn", nm("mm_co"), b_layer=i)
    ym = _mm(sg, w["w_ssm_o"], "nn", nm("mm_so"), b_layer=i)
    bg = w["b_gate"][i:i + 1]
    merged = _rowwise(lambda ctx, a, b, c, gl, bb: (_merge(a, b, c, gl, bb),),
                      [(ya, 0, D_MODEL, 0), (yv, 0, D_MODEL, 0), (ym, 0, D_MODEL, 0), (proj, C_G, GATE_WIDTH, 0)],
                      [bg], [(D_MODEL, BF16)], [], nm("merge"))[0]
    x1 = _mm(merged, w["w_mix_o"], "nn", nm("mm_mix"), add=x, b_layer=i)
    h2 = _rms_fwd(x1, w["norm_ffn"][i:i + 1], nm("rms_ffn"))
    gt, up, act = _mm_swiglu(h2, w["w_ffn_in"], i, nm("mm_ffn_in"))
    x2 = _mm(act, w["w_ffn_out"], "nn", nm("mm_ffn_out"), add=x1, b_layer=i, tk_cap=3072)
    sv.update(h=h, proj=proj, heads=heads, att=att, conv=conv, u=u, states=states, yc=yc, ys=ys, z=z, sg=sg,
              ya=ya, yv=yv, ym=ym, merged=merged, x1=x1, h2=h2, gt=gt, up=up, act=act)
    return x2, sv


def _layer_bwd(i, dx2, sv, w, tabs, gb):
    nm = lambda s: "l%d_b_%s" % (i, s)
    g = {}

    def wgrad(n, lhs, rhs, label, **kw):
        if n in gb:
            gb[n] = _mm(lhs, rhs, "tn", nm(label), into=(gb[n], i, BIG_AXIS[n]), **kw)
        else:
            g[n] = _mm(lhs, rhs, "tn", nm(label), **kw)

    wgrad("w_ffn_out", sv["act"], dx2, "mm_gw_ffn_out", tn_cap=512)
    dgu = _mm_dswiglu(dx2, w["w_ffn_out"], i, sv["gt"], sv["up"], nm("mm_dswiglu"))
    dh2 = _mm(dgu, w["w_ffn_in"], "nt", nm("mm_dh2"), tk_cap=3072, b_layer=i)
    wgrad("w_ffn_in", sv["h2"], dgu, "mm_gw_ffn_in", tm_cap=1024)
    dx1, g["norm_ffn"] = _rms_bwd(sv["x1"], w["norm_ffn"][i:i + 1], dh2, dx2, nm("rms_ffn"))
    dmerged = _mm(dx1, w["w_mix_o"], "nt", nm("mm_dmerged"), b_layer=i)
    wgrad("w_mix_o", sv["merged"], dx1, "mm_gw_mix", tn_cap=512)
    proj = sv["proj"]
    bg = w["b_gate"][i:i + 1]
    dya, dyv, dym, dgl, g["b_gate"] = _rowwise(
        _vjp_rowwise(_merge, 4),
        [(sv["ya"], 0, D_MODEL, 0), (sv["yv"], 0, D_MODEL, 0), (sv["ym"], 0, D_MODEL, 0),
         (proj, C_G, GATE_WIDTH, 0), (dmerged, 0, D_MODEL, 0)], [bg],
        [(D_MODEL, BF16), (D_MODEL, BF16), (D_MODEL, BF16), (GATE_WIDTH, BF16)], [(1, GATE_WIDTH)], nm("merge"))
    dsg = _mm(dym, w["w_ssm_o"], "nt", nm("mm_dsg"), b_layer=i)
    wgrad("w_ssm_o", sv["sg"], dym, "mm_gw_so")
    dys0, dz = _rowwise(_vjp_rowwise(_glu, 2), [(sv["ys"], 0, SSM_WIDTH, 0), (sv["z"], 0, SSM_WIDTH, 0),
                                                 (dsg, 0, SSM_WIDTH, 0)], [],
                        [(SSM_WIDTH, F32), (SSM_WIDTH, BF16)], [], nm("glu"))
    dys = _mm(dz, w["w_ssm_glu"], "nt", nm("mm_dys"), add=dys0, b_layer=i)
    wgrad("w_ssm_glu", sv["ys"], dz, "mm_gw_glu")
    d = w["ssm_d"][i:i + 1]
    dyc, du0, g["ssm_d"] = _rowwise(
        _vjp_rowwise(_ssm_act, 2), [(sv["yc"], 0, SSM_WIDTH, 0), (sv["u"], 0, SSM_WIDTH, 0), (dys, 0, SSM_WIDTH, 0)],
        [d], [(SSM_WIDTH, F32), (SSM_WIDTH, F32)], [(1, SSM_WIDTH)], nm("ssm_act"))
    lr, li, bmat, cmat = w["ssm"][i]
    dstates = _mm_bd(dyc, cmat, "nt", nm("mm_dstates"))
    g_cmat = _mm_bd(sv["states"], dyc, "tn", nm("mm_gc"))
    gs, dl = _scan(dstates, lr, -li, True, nm("scan"), states=sv["states"])
    g_lam = jnp.sum(dl, axis=0).reshape(N_SSM_CHUNKS, 2, CHUNK_STATES)
    du = _mm_bd(gs, bmat, "nt", nm("mm_du"), out_dtype=BF16, add=du0)
    g_bmat = _mm_bd(sv["u"], gs, "tn", nm("mm_gb"))
    g["ssm"] = (g_lam[:, 0].reshape(1, SSM_LANES), g_lam[:, 1].reshape(1, SSM_LANES), g_bmat, g_cmat)
    dconv = _mm(dyv, w["w_conv_o"], "nt", nm("mm_dconv"), b_layer=i)
    wgrad("w_conv_o", sv["conv"], dyv, "mm_gw_co")
    dcb, dcc, dcx, dw0, dw1, dw2 = _conv_bwd(proj, w["conv_w"][i], dconv, nm("conv"))
    g["conv_w"] = jnp.concatenate([dw0, dw1, dw2], axis=0)
    datt = _mm(dya, w["w_attn_o"], "nt", nm("mm_datt"), b_layer=i)
    wgrad("w_attn_o", sv["att"], dya, "mm_gw_ao")
    sinks = w["attn_sinks"][i].reshape(N_KV_HEADS, Q_GROUP)
    dheads, dsk = _att_bwd(sv["heads"], sinks, _to_heads(datt, N_Q_HEADS), nm("att"))
    g["attn_sinks"] = dsk[:, :Q_GROUP, 0].reshape(N_Q_HEADS)
    dqkv = _split_bwd(_from_heads(dheads), tabs, nm("rope"))
    dproj = jnp.concatenate([dqkv, dcb, dcc, dcx, du, dgl], axis=1)
    dh = _mm(dproj, w["w_in"], "nt", nm("mm_dh"), tk_cap=3072, b_layer=i)
    dproj_shards = dproj.reshape(dproj.shape[0], N_CHIPS, IN_COLS // N_CHIPS).transpose(1, 0, 2)
    wgrad("w_in", sv["h"], dproj_shards, "mm_gw_in", b_shards=True, tm_cap=1024)
    dx, g["norm_mix"] = _rms_bwd(sv["x"], w["norm_mix"][i:i + 1], dh, dx1, nm("rms_mix"))
    return dx, g


def _local_step(x, target, w):
    length = x.shape[0]
    tabs = _rope_tables(length)
    ssm_names = ("ssm_a_re", "ssm_a_im", "ssm_b_re", "ssm_b_im", "ssm_c_re", "ssm_c_im", "ssm_log_dt")
    w = dict(w)
    preps = [jax.vjp(_ssm_prep, *[w[n][i] for n in ssm_names]) for i in range(DEPTH)]
    w["ssm"] = [p[0] for p in preps]
    saved = []
    for i in range(DEPTH):
        x, sv = _layer_fwd(i, x, w, tabs)
        saved.append(sv)

    def loss_fn(ctx, xv, tv, gv):
        val, vjp = jax.vjp(_loss_fn, xv, gv, tv)
        dx, dg, _ = vjp(jnp.ones((1, 1), F32))
        return dx, dg, val + jnp.zeros((1, LANE), F32)

    gfin = w["norm_final"].reshape(1, D_MODEL)
    dx, g_final, loss = _rowwise(loss_fn, [(x, 0, D_MODEL, 0), (target, 0, D_MODEL, 0)], [gfin],
                                 [(D_MODEL, F32)], [(1, D_MODEL), (1, LANE)], "loss")
    gb = {}
    for n in BIG:
        depth, ra, cb = w[n].shape
        gb[n] = lax.empty((N_CHIPS, depth) + ((ra // N_CHIPS, cb) if BIG_AXIS[n] == 1 else (ra, cb // N_CHIPS)), F32)
    layer_grads = [None] * DEPTH
    for i in reversed(range(DEPTH)):
        dx, layer_grads[i] = _layer_bwd(i, dx, saved[i], w, tabs, gb)
    grads = dict(gb)
    for n in layer_grads[0]:
        if n != "ssm":
            grads[n] = jnp.stack([lg[n] for lg in layer_grads])
    ssm_g = [preps[i][1](layer_grads[i]["ssm"]) for i in range(DEPTH)]
    for j, n in enumerate(ssm_names):
        grads[n] = jnp.stack([sg[j] for sg in ssm_g])
    grads["norm_final"] = g_final.reshape(D_MODEL)
    for n in ("norm_mix", "norm_ffn", "b_gate", "ssm_d"):
        grads[n] = grads[n].reshape(grads[n].shape[0], -1)
    return loss, dx, grads


COLS = 1024
ANY_SPEC = pl.BlockSpec(memory_space=pl.ANY)


def _place():
    return lax.axis_index("x"), lax.axis_index("y"), lax.axis_index("c")


def _other_chips(x, y):
    return [(1 - x, y), (x, 1 - y), (1 - x, 1 - y)]


def _remote(src, dst, send_sems, recv_sems, k, to):
    return pltpu.make_async_remote_copy(src_ref=src, dst_ref=dst, send_sem=send_sems.at[k], recv_sem=recv_sems.at[k],
                                        device_id=to, device_id_type=MESH_ID)


def _comm_call(body, name, out_shape, n_sems, n_local, args):
    return pl.pallas_call(
        body, name=name, out_shape=out_shape, in_specs=[ANY_SPEC] * len(args),
        out_specs=[ANY_SPEC] * len(out_shape),
        scratch_shapes=[pltpu.SemaphoreType.DMA((n_sems,)), pltpu.SemaphoreType.DMA((n_sems,)),
                        pltpu.SemaphoreType.DMA((n_local,))],
    )(*args)


def _gather_weights(shards, name):
    nt = len(shards)
    per = 9
    hds = [s.shape[0] // 2 for s in shards]

    def body(*refs):
        srcs, outs = refs[:nt], refs[nt:2 * nt]
        send_sems, recv_sems, _ = refs[2 * nt:]
        x, y, c = _place()
        me, sibling = (x, y, c), (x, y, 1 - c)
        xn, yn, dg = _other_chips(x, y)

        def blk(t, chip, hc, part=None):
            lo, n = hc * hds[t], hds[t]
            if part is not None:
                first_n = (n + 1) // 2
                lo, n = (lo, first_n) if part == 0 else (lo + first_n, n - first_n)
            return outs[t].at[2 * chip[0] + chip[1], pl.ds(lo, n)] if n else None

        def copy(t, k, src, dst, to):
            return _remote(src, dst, send_sems, recv_sems, per * t + k, to)

        def arrived(t, k, ref):
            copy(t, k, ref, ref, me).wait_recv()

        sent = []

        def start(t, k, ref, to):
            if ref is not None:
                sent.append(copy(t, k, ref, ref, to))
                sent[-1].start()

        for t in range(nt):
            own = srcs[t].at[pl.ds(c * hds[t], hds[t])]
            for k, chip in enumerate((xn, yn)):
                sent.append(copy(t, k, own, blk(t, (x, y), c), (*chip, c)))
                sent[-1].start()
        for t in range(nt):
            sent.append(copy(t, 8, srcs[t], outs[t].at[2 * x + y], sibling))
            sent[-1].start()
        for t in range(nt):
            arrived(t, 0, blk(t, xn, c))
            start(t, 3, blk(t, xn, c, 1), (*yn, c))
            start(t, 4, blk(t, xn, c), sibling)
            arrived(t, 1, blk(t, yn, c))
            start(t, 2, blk(t, yn, c, 0), (*xn, c))
            start(t, 5, blk(t, yn, c), sibling)
        for t in range(nt):
            arrived(t, 2, blk(t, dg, c, 0))
            start(t, 6, blk(t, dg, c, 0), sibling)
            if blk(t, dg, c, 1) is not None:
                arrived(t, 3, blk(t, dg, c, 1))
                start(t, 7, blk(t, dg, c, 1), sibling)
        for t in range(nt):
            arrived(t, 4, blk(t, xn, 1 - c))
            arrived(t, 5, blk(t, yn, 1 - c))
            arrived(t, 6, blk(t, dg, 1 - c, 0))
            if blk(t, dg, 1 - c, 1) is not None:
                arrived(t, 7, blk(t, dg, 1 - c, 1))
            arrived(t, 8, outs[t].at[2 * x + y])
        for cp in sent:
            cp.wait_send()

    out_shape = [jax.ShapeDtypeStruct((N_CHIPS,) + s.shape, s.dtype) for s in shards]
    return _comm_call(body, name, out_shape, per * nt, 1, shards)


def _swap_halves(gs, name):
    nt = len(gs)
    hds = [g.shape[1] // 2 for g in gs]

    def body(*refs):
        srcs, outs = refs[:nt], refs[nt:2 * nt]
        send_sems, recv_sems, _ = refs[2 * nt:]
        x, y, c = _place()
        cps = [_remote(srcs[t].at[s, pl.ds((1 - c) * hds[t], hds[t])], outs[t].at[s], send_sems, recv_sems,
                       N_CHIPS * t + s, (x, y, 1 - c)) for t in range(nt) for s in range(N_CHIPS)]
        for cp in cps:
            cp.start()
        for cp in cps:
            cp.wait()

    out_shape = [jax.ShapeDtypeStruct((N_CHIPS, g.shape[1] // 2) + g.shape[2:], g.dtype) for g in gs]
    return _comm_call(body, name, out_shape, N_CHIPS * nt, 1, gs)


def _exchange_shards(parts, name, swap=()):
    nt, ns = len(parts), len(swap)
    hds = [g.shape[1] // 2 for g in swap]

    def body(*refs):
        srcs, swap_srcs = refs[:nt], refs[nt:nt + ns]
        outs, swap_outs = refs[nt + ns:2 * nt + ns], refs[2 * nt + ns:2 * (nt + ns)]
        send_sems, recv_sems, _ = refs[2 * (nt + ns):]
        x, y, c = _place()
        cps = [_remote(srcs[t].at[2 * chip[0] + chip[1]], outs[t].at[j], send_sems, recv_sems, 3 * t + j, (*chip, c))
               for t in range(nt) for j, chip in enumerate(_other_chips(x, y))]
        cps += [_remote(swap_srcs[t].at[s, pl.ds((1 - c) * hds[t], hds[t])], swap_outs[t].at[s], send_sems, recv_sems,
                        3 * nt + N_CHIPS * t + s, (x, y, 1 - c)) for t in range(ns) for s in range(N_CHIPS)]
        for cp in cps:
            cp.start()
        for cp in cps:
            cp.wait()

    out_shape = [jax.ShapeDtypeStruct((N_CHIPS - 1,) + p.shape[1:], p.dtype) for p in parts]
    out_shape += [jax.ShapeDtypeStruct((N_CHIPS, g.shape[1] // 2) + g.shape[2:], g.dtype) for g in swap]
    res = _comm_call(body, name, out_shape, 3 * nt + N_CHIPS * ns, 1, list(parts) + list(swap))
    return res[:nt], res[nt:]


def _join_halves(reds, name):
    nt = len(reds)
    hds = [r.shape[1] for r in reds]

    def body(*refs):
        srcs, outs = refs[:nt], refs[nt:2 * nt]
        send_sems, recv_sems, _ = refs[2 * nt:]
        x, y, c = _place()
        cps = [_remote(srcs[t].at[0], outs[t].at[pl.ds(c * hds[t], hds[t])], send_sems, recv_sems, t, (x, y, 1 - c))
               for t in range(nt)]
        for cp in cps:
            cp.start()
        for t in range(nt):
            _remote(srcs[t].at[0], outs[t].at[pl.ds((1 - c) * hds[t], hds[t])], send_sems, recv_sems, t,
                    (x, y, c)).wait_recv()
        for cp in cps:
            cp.wait_send()

    out_shape = [jax.ShapeDtypeStruct((2 * r.shape[1],) + r.shape[2:], r.dtype) for r in reds]
    return _comm_call(body, name, out_shape, nt, 1, reds)


SUM_BLOCK_ELEMS = 768 * 1024
ADAMW_BLOCK_ELEMS = 512 * 1024


def _sum_windows(parts, lead, name, out_dtypes=(F32,)):
    a, b = parts[0][0].shape[2:]
    ta = _tile(a, max(SUBLANE, SUM_BLOCK_ELEMS // b), 2 * SUBLANE)
    offs = jnp.stack([jnp.stack([jnp.asarray(o, jnp.int32) for o in off]) for _, off in parts])
    n_in = len(parts)

    def body(off_ref, *refs):
        acc = refs[0][...].astype(F32)
        for r in refs[1:n_in]:
            acc = acc + r[...].astype(F32)
        for r in refs[n_in:]:
            r[...] = acc.astype(r.dtype)

    in_specs = [pl.BlockSpec((1, 1, ta, b), lambda p, q, i, off, k=k: (off[k, 0] + p, off[k, 1] + q, i, 0))
                for k in range(n_in)]
    o_spec = pl.BlockSpec((1, 1, ta, b), lambda p, q, i, off: (p, q, i, 0))
    return pl.pallas_call(
        body, name=name, out_shape=[jax.ShapeDtypeStruct(tuple(lead) + (a, b), dt) for dt in out_dtypes],
        grid_spec=pltpu.PrefetchScalarGridSpec(
            num_scalar_prefetch=1, grid=tuple(lead) + (a // ta,), in_specs=in_specs,
            out_specs=[o_spec] * len(out_dtypes)),
        compiler_params=_params("arbitrary", "arbitrary", "arbitrary"),
    )(offs, *[arr for arr, _ in parts])


def _reduce_scatter(gs, names, wire, groups):
    x, y, c = _place()
    theirs = dict(zip(groups[0], _swap_halves([gs[i] for i in groups[0]], "rs_swap_halves")))
    pairs, others = {}, {}
    for k, group in enumerate(groups):
        for i in group:
            g = gs[i]
            pairs[i] = _sum_windows([(g, (0, c * (g.shape[1] // 2))), (theirs[i], (0, 0))],
                                    (N_CHIPS, g.shape[1] // 2), "rs_sum_pair_" + names[i], (F32, wire[i]))
        nxt = groups[k + 1] if k + 1 < len(groups) else []
        got, swapped = _exchange_shards([pairs[i][1] for i in group], "rs_exchange_%d" % k, [gs[i] for i in nxt])
        others.update(zip(group, got))
        theirs.update(zip(nxt, swapped))
    pairs, others = [pairs[i] for i in range(len(gs))], [others[i] for i in range(len(gs))]
    reds = [_sum_windows([(p[0], (2 * x + y, 0))] + [(o, (j, 0)) for j in range(N_CHIPS - 1)], (1, p[0].shape[1]),
                         "rs_sum_chips_" + n)[0] for p, o, n in zip(pairs, others, names)]
    joined = _join_halves(reds, "rs_join")
    return [lax.dynamic_update_slice_in_dim(j, r[0], c * r.shape[1], axis=0) for j, r in zip(joined, reds)]


def _adamw(wt, g, m, v, name):
    cols = wt.shape[-1]
    r = wt.size // cols
    tr = _tile(r, max(SUBLANE, ADAMW_BLOCK_ELEMS // max(cols, LANE)), SUBLANE)

    def body(w_ref, g_ref, m_ref, v_ref, d_ref, nm_ref, nv_ref):
        gv = g_ref[...]
        mn = ADAM_B1 * m_ref[...] + (1.0 - ADAM_B1) * gv
        vn = ADAM_B2 * v_ref[...] + (1.0 - ADAM_B2) * jnp.square(gv)
        m_hat = mn / (1.0 - ADAM_B1 ** ADAM_STEP)
        v_hat = vn / (1.0 - ADAM_B2 ** ADAM_STEP)
        d_ref[...] = -ADAM_LR * (m_hat / (jnp.sqrt(v_hat) + ADAM_EPS) + ADAM_WD * w_ref[...])
        nm_ref[...] = mn
        nv_ref[...] = vn

    spec = pl.BlockSpec((tr, cols), lambda i: (i, 0))
    shp = jax.ShapeDtypeStruct((r, cols), F32)
    res = pl.pallas_call(
        body, name=name, grid=(r // tr,), in_specs=[spec] * 4, out_specs=[spec] * 3, out_shape=[shp] * 3,
        compiler_params=_params("parallel"),
    )(*[t.reshape(r, cols) for t in (wt, g, m, v)])
    return [t.reshape(wt.shape) for t in res]


def _join_shards(n, piece):
    _, depth, a, b = piece.shape
    if BIG_AXIS[n] == 2:
        return piece.transpose(1, 2, 0, 3).reshape(depth, a, N_CHIPS * b)
    return piece.transpose(1, 0, 2, 3).reshape(depth, N_CHIPS * a, b)


RS_GROUPS = (("w_ffn_out", "w_mix_o", "small"), ("w_ffn_in",), ("w_in", "w_attn_o", "w_conv_o", "w_ssm_glu", "w_ssm_o"))


SMALL_PART_ROWS = 2 * SUBLANE


def _rows_of(t):
    return -(-t.size // COLS)


def _pack_small(ts):
    rows = [jnp.pad(t.reshape(-1), (0, _rows_of(t) * COLS - t.size)).reshape(-1, COLS) for t in ts]
    total = sum(r.shape[0] for r in rows)
    part = -(-total // (N_DEV * SMALL_PART_ROWS)) * SMALL_PART_ROWS
    rows.append(jnp.zeros((N_DEV * part - total, COLS), F32))
    return jnp.concatenate(rows, axis=0).reshape(N_CHIPS, 2, part, COLS)


def _unpack_small(buf, like):
    buf, out, r0 = buf.reshape(-1, COLS), [], 0
    for t in like:
        out.append(buf[r0:r0 + _rows_of(t)].reshape(-1)[:t.size].reshape(t.shape))
        r0 += _rows_of(t)
    return out


def kernel(x, norm_mix, w_in, b_gate, attn_sinks, w_attn_o, conv_w, w_conv_o, ssm_a_re, ssm_a_im, ssm_b_re, ssm_b_im, ssm_c_re, ssm_c_im, ssm_d, ssm_log_dt, w_ssm_glu, w_ssm_o, w_mix_o, norm_ffn, w_ffn_in, w_ffn_out, norm_final, loss_target, m_norm_mix, m_w_in, m_b_gate, m_attn_sinks, m_w_attn_o, m_conv_w, m_w_conv_o, m_ssm_a_re, m_ssm_a_im, m_ssm_b_re, m_ssm_b_im, m_ssm_c_re, m_ssm_c_im, m_ssm_d, m_ssm_log_dt, m_w_ssm_glu, m_w_ssm_o, m_w_mix_o, m_norm_ffn, m_w_ffn_in, m_w_ffn_out, m_norm_final, v_norm_mix, v_w_in, v_b_gate, v_attn_sinks, v_w_attn_o, v_conv_w, v_w_conv_o, v_ssm_a_re, v_ssm_a_im, v_ssm_b_re, v_ssm_b_im, v_ssm_c_re, v_ssm_c_im, v_ssm_d, v_ssm_log_dt, v_w_ssm_glu, v_w_ssm_o, v_w_mix_o, v_norm_ffn, v_w_ffn_in, v_w_ffn_out, v_norm_final):
    a = dict(zip(ARG_NAMES, (
        x, norm_mix, w_in, b_gate, attn_sinks, w_attn_o, conv_w, w_conv_o, ssm_a_re, ssm_a_im, ssm_b_re, ssm_b_im,
        ssm_c_re, ssm_c_im, ssm_d, ssm_log_dt, w_ssm_glu, w_ssm_o, w_mix_o, norm_ffn, w_ffn_in, w_ffn_out, norm_final,
        loss_target, m_norm_mix, m_w_in, m_b_gate, m_attn_sinks, m_w_attn_o, m_conv_w, m_w_conv_o, m_ssm_a_re,
        m_ssm_a_im, m_ssm_b_re, m_ssm_b_im, m_ssm_c_re, m_ssm_c_im, m_ssm_d, m_ssm_log_dt, m_w_ssm_glu, m_w_ssm_o,
        m_w_mix_o, m_norm_ffn, m_w_ffn_in, m_w_ffn_out, m_norm_final, v_norm_mix, v_w_in, v_b_gate, v_attn_sinks,
        v_w_attn_o, v_conv_w, v_w_conv_o, v_ssm_a_re, v_ssm_a_im, v_ssm_b_re, v_ssm_b_im, v_ssm_c_re, v_ssm_c_im,
        v_ssm_d, v_ssm_log_dt, v_w_ssm_glu, v_w_ssm_o, v_w_mix_o, v_norm_ffn, v_w_ffn_in, v_w_ffn_out, v_norm_final)))
    px, py, _ = _place()
    chip = 2 * px + py

    gathered = BIG + ("conv_w",)
    own = [a[n].astype(BF16) for n in BIG] + [a["conv_w"]]
    gath = _gather_weights(own, "gather_weights")
    w = {n: _join_shards(n, p) for n, p in zip(gathered, gath)}
    for n in SMALL:
        w[n] = a[n]

    loss, dx, grads = _local_step(a["x"][0], a["loss_target"][0], w)

    small_names = SMALL + ("conv_w",)
    small = _pack_small([grads[n] for n in small_names])
    rs_names = BIG + ("small",)
    groups = [[rs_names.index(n) for n in grp] for grp in RS_GROUPS]
    rs = _reduce_scatter([grads[n] for n in BIG] + [small], rs_names, (BF16,) * len(BIG) + (F32,), groups)
    red = dict(zip(BIG, rs))
    small_all = _gather_weights([rs[-1]], "gather_small")[0]
    red.update(zip(small_names, _unpack_small(small_all, [grads[n] for n in small_names])))
    lane = a["conv_w"].shape[2]
    red["conv_w"] = lax.dynamic_slice_in_dim(red["conv_w"], chip * lane, lane, axis=2)

    loss_all = lax.psum(loss[0, 0], ("x", "y", "c"))
    deltas, new_m, new_v = [], [], []
    for n in WEIGHTS:
        d, mn, vn = _adamw(a[n], red[n], a["m_" + n], a["v_" + n], "adamw_" + n)
        deltas.append(d)
        new_m.append(mn)
        new_v.append(vn)
    return (loss_all, dx[None], *[red[n] for n in WEIGHTS], *deltas, *new_m, *new_v)
```

```python
import math

import jax
import jax.numpy as jnp
from jax import lax
from jax.experimental import pallas as pl
from jax.experimental.pallas import tpu as pltpu

F32 = jnp.float32
BF16 = jnp.bfloat16

D_MODEL = 1024
DEPTH = 4
N_Q_HEADS = 8
N_KV_HEADS = 2
HEAD_DIM = 64
Q_GROUP = N_Q_HEADS // N_KV_HEADS
WINDOW = 128
BLOCK = 128
ROPE_THETA = 500000.0
ROT_DIM = HEAD_DIM // 4
ATTN_WIDTH = N_Q_HEADS * HEAD_DIM
KV_WIDTH = N_KV_HEADS * HEAD_DIM
NEG_INF = -1e30
CONV_WIDTH = 512
CONV_K = 3
SSM_WIDTH = 512
SSM_GROUP = 16
SSM_GROUPS = 32
SSM_STATE = 64
SSM_LANES = SSM_GROUPS * SSM_STATE
GATE_WIDTH = 3 * D_MODEL
FFN_HIDDEN = 2816
NORM_EPS = 1e-6
IN_COLS = 5888
C_Q, C_K, C_V, C_CB, C_CC, C_CX, C_U, C_G = 0, 512, 640, 768, 1280, 1792, 2304, 2816

ADAM_LR = 0.001
ADAM_B1 = 0.9
ADAM_B2 = 0.999
ADAM_EPS = 1e-08
ADAM_WD = 0.01
ADAM_STEP = 10

N_CHIPS = 4
N_DEV = 8
MESH_ID = pl.DeviceIdType.MESH

VMEM_LIMIT_BYTES = 48 * 1024 * 1024
LANE = 128
SUBLANE = 8
SCAN_ROWS = 8
SCAN_CHUNK = 128

BIG = ("w_in", "w_attn_o", "w_conv_o", "w_ssm_glu", "w_ssm_o", "w_mix_o", "w_ffn_in", "w_ffn_out")
BIG_AXIS = {"w_in": 2, "w_attn_o": 2, "w_conv_o": 2, "w_ssm_glu": 1, "w_ssm_o": 2, "w_mix_o": 1,
            "w_ffn_in": 2, "w_ffn_out": 1, "conv_w": 2}
SMALL = ("norm_mix", "b_gate", "attn_sinks", "ssm_a_re", "ssm_a_im", "ssm_b_re", "ssm_b_im",
         "ssm_c_re", "ssm_c_im", "ssm_d", "ssm_log_dt", "norm_ffn", "norm_final")
WEIGHTS = ("norm_mix", "w_in", "b_gate", "attn_sinks", "w_attn_o", "conv_w", "w_conv_o", "ssm_a_re",
           "ssm_a_im", "ssm_b_re", "ssm_b_im", "ssm_c_re", "ssm_c_im", "ssm_d", "ssm_log_dt",
           "w_ssm_glu", "w_ssm_o", "w_mix_o", "norm_ffn", "w_ffn_in", "w_ffn_out", "norm_final")
ARG_NAMES = ("x",) + WEIGHTS + ("loss_target",) + tuple("m_" + n for n in WEIGHTS) + tuple(
    "v_" + n for n in WEIGHTS)


def _params(*sem):
    return pltpu.CompilerParams(dimension_semantics=sem if sem else None,
                                vmem_limit_bytes=VMEM_LIMIT_BYTES)


def _tile(dim, cap, align):
    t = min(cap, dim) // align * align
    while t >= align:
        if dim % t == 0:
            return t
        t -= align
    return dim


_DOT_DIMS = {"nn": (((1,), (0,)), ((), ())), "nt": (((1,), (1,)), ((), ())), "tn": (((0,), (0,)), ((), ()))}


def _mm(a, b, mode, name, out_dtype=F32, add=None, tm_cap=512, tn_cap=3072, tk_cap=1024, b_layer=None, into=None,
        b_shards=False):
    bshape = b.shape if b_layer is None else b.shape[1:]
    if b_shards:
        assert mode == "tn" and b_layer is None and into is not None and into[2] == 2
        bshape = (b.shape[1], N_CHIPS * b.shape[2])
    if mode == "nn":
        (m, k), (k2, n) = a.shape, bshape
    elif mode == "nt":
        (m, k), (n, k2) = a.shape, bshape
    else:
        (k, m), (k2, n) = a.shape, bshape
    assert k == k2, (name, a.shape, b.shape)
    tm, tn, tk = _tile(m, tm_cap, LANE), _tile(n, tn_cap, LANE), _tile(k, tk_cap, LANE)
    if into is not None:
        buf, layer, axis = into
        _, _, ra, cb = buf.shape
        if axis == 1:
            tm = m
        elif b_shards:
            tn = cb
        else:
            tn = _tile(cb, tn_cap, LANE)
            assert cb % tn == 0 and tn % LANE == 0, (name, cb, tn)
    nk = k // tk
    dims = _DOT_DIMS[mode]

    def body(a_ref, b_ref, *rest):
        rest = list(rest)
        add_ref = rest.pop(0) if add is not None else None
        if into is not None:
            rest.pop(0)
        o_ref, acc = rest
        kk = pl.program_id(2)
        part = lax.dot_general(a_ref[...].astype(BF16), b_ref[...].astype(BF16), dims, preferred_element_type=F32)

        def finish(r):
            if add is not None:
                r = r + add_ref[...]
            o_ref[...] = r.astype(o_ref.dtype).reshape(o_ref.shape)

        if nk == 1:
            finish(part)
            return

        @pl.when(kk == 0)
        def _():
            acc[...] = part

        @pl.when(kk > 0)
        def _():
            acc[...] += part

        @pl.when(kk == nk - 1)
        def _():
            finish(acc[...])

    if mode == "tn":
        a_spec = pl.BlockSpec((tk, tm), lambda i, j, kk: (kk, i))
    else:
        a_spec = pl.BlockSpec((tm, tk), lambda i, j, kk: (i, kk))
    lead = () if b_layer is None else (None,)
    at = (lambda *ix: ix) if b_layer is None else (lambda *ix: (b_layer,) + ix)
    if b_shards:
        b_spec = pl.BlockSpec((None, tk, tn), lambda i, j, kk: (j, kk, 0))
    elif mode == "nt":
        b_spec = pl.BlockSpec(lead + (tn, tk), lambda i, j, kk: at(j, kk))
    else:
        b_spec = pl.BlockSpec(lead + (tk, tn), lambda i, j, kk: at(kk, j))
    o_spec = pl.BlockSpec((tm, tn), lambda i, j, kk: (i, j))
    in_specs, args = [a_spec, b_spec], [a, b]
    if add is not None:
        in_specs.append(o_spec)
        args.append(add)
    out_shape, aliases = jax.ShapeDtypeStruct((m, n), out_dtype), {}
    if into is not None:
        in_specs.append(pl.BlockSpec(memory_space=pl.ANY))
        aliases = {len(args): 0}
        args.append(buf)
        out_shape = jax.ShapeDtypeStruct(buf.shape, buf.dtype)
        if axis == 1:
            o_spec = pl.BlockSpec((N_CHIPS, None, ra, tn), lambda i, j, kk: (0, layer, 0, j))
        else:
            per = cb // tn
            o_spec = pl.BlockSpec((None, None, tm, tn), lambda i, j, kk: (j // per, layer, i, j % per))
    grid = (m // tm, n // tn, nk)
    if nk == 1 and mode != "tn":
        def swapped(spec):
            if spec.index_map is None:
                return spec
            return pl.BlockSpec(spec.block_shape, lambda j, i, kk, f=spec.index_map: f(i, j, kk))
        in_specs, o_spec, grid = [swapped(s) for s in in_specs], swapped(o_spec), (n // tn, m // tm, nk)
    return pl.pallas_call(
        body, name=name, grid=grid, in_specs=in_specs, out_specs=o_spec,
        out_shape=out_shape, input_output_aliases=aliases,
        scratch_shapes=[pltpu.VMEM((tm, tn) if nk > 1 else (SUBLANE, LANE), F32)],
        compiler_params=_params("parallel", "parallel", "arbitrary"),
    )(*args)


def _rowwise(fn, rows, pars, outs, accs, name, tm_cap=256):
    length = rows[0][0].shape[0]
    tm = _tile(length, tm_cap, LANE)
    n = length // tm
    in_specs, args, counts = [], [], []
    for arr, c0, cw, shift in rows:
        bw = math.gcd(c0, cw) if c0 else cw
        assert bw % LANE == 0 or (c0 == 0 and cw == arr.shape[1]), (name, c0, cw)
        cnt = cw // bw
        counts.append(cnt)
        for j in range(cnt):
            in_specs.append(pl.BlockSpec(
                (tm, bw), lambda i, j=j, c0=c0, bw=bw, shift=shift: (jnp.clip(i + shift, 0, n - 1), c0 // bw + j)))
            args.append(arr)
    for p in pars:
        in_specs.append(pl.BlockSpec(p.shape, lambda i: (0, 0)))
        args.append(p)
    out_shape = [jax.ShapeDtypeStruct((length, w), dt) for w, dt in outs]
    out_specs = [pl.BlockSpec((tm, w), lambda i: (i, 0)) for w, _ in outs]
    out_shape += [jax.ShapeDtypeStruct((r, w), F32) for r, w in accs]
    out_specs += [pl.BlockSpec((r, w), lambda i: (0, 0)) for r, w in accs]
    n_in, n_out = len(args), len(outs)

    def body(*refs):
        i = pl.program_id(0)
        vals, p = [], 0
        for cnt in counts:
            blocks = [refs[p + j][...] for j in range(cnt)]
            p += cnt
            vals.append(blocks[0] if cnt == 1 else jnp.concatenate(blocks, axis=1))
        for _ in pars:
            vals.append(refs[p][...])
            p += 1
        res = fn((i, n), *vals)
        out_refs = refs[n_in:n_in + n_out]
        acc_refs = refs[n_in + n_out:]
        for r, v in zip(out_refs, res[:n_out]):
            r[...] = v.astype(r.dtype)
        if acc_refs:
            @pl.when(i == 0)
            def _():
                for r in acc_refs:
                    r[...] = jnp.zeros_like(r)
            for r, v in zip(acc_refs, res[n_out:]):
                r[...] += v

    res = pl.pallas_call(
        body, name=name, grid=(n,), in_specs=in_specs, out_specs=out_specs, out_shape=out_shape,
        compiler_params=_params("arbitrary"),
    )(*args)
    return res


def _rms(x, g):
    return x * lax.rsqrt(jnp.mean(x * x, axis=-1, keepdims=True) + NORM_EPS) * g


def _rms_fwd(x, g, name):
    return _rowwise(lambda ctx, xv, gv: (_rms(xv, gv),), [(x, 0, D_MODEL, 0)], [g],
                    [(D_MODEL, BF16)], [], name)[0]


def _rms_bwd(x, g, dh, dres, name):
    def fn(ctx, xv, dhv, drv, gv):
        _, vjp = jax.vjp(_rms, xv, gv)
        dx, dg = vjp(dhv)
        return dx + drv, dg
    return _rowwise(fn, [(x, 0, D_MODEL, 0), (dh, 0, D_MODEL, 0), (dres, 0, D_MODEL, 0)], [g],
                    [(D_MODEL, F32)], [(1, D_MODEL)], name)


def _rope_tables(length):
    pos = jnp.arange(length, dtype=F32)
    inv_freq = ROPE_THETA ** (-jnp.arange(0, ROT_DIM, 2, dtype=F32) / ROT_DIM)
    ang = pos[:, None] * inv_freq[None, :]
    cos, sin = jnp.cos(ang), jnp.sin(ang)
    half = ROT_DIM // 2
    ones = jnp.ones((length, HEAD_DIM - ROT_DIM), F32)
    zeros = jnp.zeros_like(ones)
    zh = jnp.zeros((length, half), F32)
    c64 = jnp.concatenate([cos, cos, ones], axis=1)
    s1 = jnp.concatenate([-sin, zh, zeros], axis=1)
    s2 = jnp.concatenate([zh, sin, zeros], axis=1)
    tile2 = lambda t: jnp.concatenate([t, t], axis=1)
    return tile2(c64), tile2(s1), tile2(s2)


def _lane_chunks(t):
    return [t[:, j * LANE:(j + 1) * LANE] for j in range(t.shape[1] // LANE)]


def _rope(t, c, s1, s2, n_rot):
    half = ROT_DIM // 2
    out = []
    for j, ch in enumerate(_lane_chunks(t)):
        if j < n_rot:
            ch = ch * c + pltpu.roll(ch, LANE - half, 1) * s1 + pltpu.roll(ch, half, 1) * s2
        out.append(ch)
    return jnp.concatenate(out, axis=1)


def _unrope(d, c, s1, s2, n_rot):
    half = ROT_DIM // 2
    out = []
    for j, ch in enumerate(_lane_chunks(d)):
        if j < n_rot:
            ch = ch * c + pltpu.roll(ch * s1, half, 1) + pltpu.roll(ch * s2, LANE - half, 1)
        out.append(ch)
    return jnp.concatenate(out, axis=1)


N_ROT_CHUNKS = (ATTN_WIDTH + KV_WIDTH) // LANE
QKV_WIDTH = ATTN_WIDTH + 2 * KV_WIDTH


def _split_fwd(proj, tabs, name):
    def fn(ctx, t, c, s1, s2):
        return (_rope(t, c, s1, s2, N_ROT_CHUNKS),)
    rows = [(proj, 0, QKV_WIDTH, 0)] + [(t, 0, LANE, 0) for t in tabs]
    return _rowwise(fn, rows, [], [(QKV_WIDTH, BF16)], [], name, tm_cap=BLOCK)[0]


def _split_bwd(d, tabs, name):
    def fn(ctx, dqv, dkcv, dkpv, dvcv, dvpv, c, s1, s2):
        i, n = ctx
        keep = (i < n - 1).astype(F32)
        dd = jnp.concatenate([dqv, dkcv + keep * dkpv, dvcv + keep * dvpv], axis=1)
        return (_unrope(dd, c, s1, s2, N_ROT_CHUNKS),)
    col = lambda row: row * HEAD_DIM
    rows = [(d, 0, ATTN_WIDTH, 0), (d, col(D_KC), KV_WIDTH, 0), (d, col(D_KP), KV_WIDTH, 1),
            (d, col(D_VC), KV_WIDTH, 0), (d, col(D_VP), KV_WIDTH, 1)] + [(t, 0, LANE, 0) for t in tabs]
    return _rowwise(fn, rows, [], [(QKV_WIDTH, BF16)], [], name, tm_cap=BLOCK)[0]


N_HEAD_ROWS = N_Q_HEADS + 2 * N_KV_HEADS


def _att_scores(j, q_ref, kvp_ref, kvc_ref, sink_ref):
    n = pl.program_id(0)
    rows = Q_GROUP * BLOCK
    qs = q_ref[Q_GROUP * j:Q_GROUP * (j + 1)].reshape(rows, HEAD_DIM)
    kb = jnp.concatenate([kvp_ref[j], kvc_ref[j]], axis=0)
    s = lax.dot_general(qs, kb, _DOT_DIMS["nt"], preferred_element_type=F32) * (HEAD_DIM ** -0.5)
    r = lax.broadcasted_iota(jnp.int32, (rows, 2 * BLOCK), 0)
    kj = lax.broadcasted_iota(jnp.int32, (rows, 2 * BLOCK), 1)
    delta = (r % BLOCK) + BLOCK - kj
    ok = (delta >= 0) & (delta < WINDOW) & ((kj >= BLOCK) | (n > 0))
    s = jnp.where(ok, s, NEG_INF)
    rh = lax.broadcasted_iota(jnp.int32, (rows, 1), 0) // BLOCK
    sinks = sink_ref[...]
    lane = lax.broadcasted_iota(jnp.int32, sinks.shape, 1)
    srow = lax.broadcasted_iota(jnp.int32, sinks.shape, 0)
    sink = jnp.zeros((rows, 1), F32)
    for g in range(Q_GROUP):
        val = jnp.sum(jnp.where((lane == g) & (srow == j), sinks, 0.0), keepdims=True)
        sink = jnp.where(rh == g, val, sink)
    m = jnp.maximum(jnp.max(s, axis=-1, keepdims=True), sink)
    p = jnp.exp(s - m)
    psink = jnp.exp(sink - m)
    denom = jnp.sum(p, axis=-1, keepdims=True) + psink
    vb = jnp.concatenate([kvp_ref[N_KV_HEADS + j], kvc_ref[N_KV_HEADS + j]], axis=0)
    return qs, kb, vb, p / denom, psink / denom, rh


def _att_specs(length):
    nb = length // BLOCK
    kv_rows = 2 * N_KV_HEADS
    q_spec = pl.BlockSpec((N_Q_HEADS, BLOCK, HEAD_DIM), lambda n: (0, n, 0))
    prev = pl.BlockSpec((kv_rows, BLOCK, HEAD_DIM), lambda n: (N_Q_HEADS // kv_rows, jnp.maximum(n - 1, 0), 0))
    cur = pl.BlockSpec((kv_rows, BLOCK, HEAD_DIM), lambda n: (N_Q_HEADS // kv_rows, n, 0))
    sink_spec = pl.BlockSpec((N_KV_HEADS, Q_GROUP), lambda n: (0, 0))
    return nb, q_spec, prev, cur, sink_spec


def _att_fwd(heads, sinks, name):
    length = heads.shape[1]
    nb, q_spec, prev, cur, sink_spec = _att_specs(length)

    def body(q_ref, kvp_ref, kvc_ref, sink_ref, o_ref):
        for j in range(N_KV_HEADS):
            _, _, vb, p, _, _ = _att_scores(j, q_ref, kvp_ref, kvc_ref, sink_ref)
            o = jnp.dot(p.astype(BF16), vb, preferred_element_type=F32)
            o_ref[Q_GROUP * j:Q_GROUP * (j + 1)] = o.reshape(Q_GROUP, BLOCK, HEAD_DIM).astype(o_ref.dtype)

    return pl.pallas_call(
        body, name=name, grid=(nb,), in_specs=[q_spec, prev, cur, sink_spec], out_specs=q_spec,
        out_shape=jax.ShapeDtypeStruct((N_Q_HEADS, length, HEAD_DIM), BF16),
        compiler_params=_params("arbitrary"),
    )(heads, heads, heads, sinks)


D_KC, D_KP, D_VC, D_VP = (N_Q_HEADS + i * N_KV_HEADS for i in range(4))
N_DHEAD_ROWS = N_Q_HEADS + 4 * N_KV_HEADS


def _att_bwd(heads, sinks, do, name):
    length = heads.shape[1]
    nb, q_spec, prev, cur, sink_spec = _att_specs(length)

    def body(q_ref, kvp_ref, kvc_ref, sink_ref, do_ref, d_ref, dsink_ref):
        @pl.when(pl.program_id(0) == 0)
        def _():
            dsink_ref[...] = jnp.zeros_like(dsink_ref)

        for j in range(N_KV_HEADS):
            qs, kb, vb, p, psink, rh = _att_scores(j, q_ref, kvp_ref, kvc_ref, sink_ref)
            dob = do_ref[Q_GROUP * j:Q_GROUP * (j + 1)].reshape(Q_GROUP * BLOCK, HEAD_DIM).astype(BF16)
            dv = lax.dot_general(p.astype(BF16), dob, _DOT_DIMS["tn"], preferred_element_type=F32)
            dp = lax.dot_general(dob, vb, _DOT_DIMS["nt"], preferred_element_type=F32)
            dsum = jnp.sum(p * dp, axis=-1, keepdims=True)
            ds = (p * (dp - dsum) * (HEAD_DIM ** -0.5)).astype(BF16)
            dq = jnp.dot(ds, kb, preferred_element_type=F32)
            dk = lax.dot_general(ds, qs, _DOT_DIMS["tn"], preferred_element_type=F32)
            d_ref[Q_GROUP * j:Q_GROUP * (j + 1)] = dq.reshape(Q_GROUP, BLOCK, HEAD_DIM)
            d_ref[D_KP + j] = dk[:BLOCK]
            d_ref[D_KC + j] = dk[BLOCK:]
            d_ref[D_VP + j] = dv[:BLOCK]
            d_ref[D_VC + j] = dv[BLOCK:]
            dsr = -psink * dsum
            row = lax.broadcasted_iota(jnp.int32, (SUBLANE, LANE), 0)
            upd = jnp.zeros((SUBLANE, LANE), F32)
            for g in range(Q_GROUP):
                val = jnp.sum(jnp.where(rh == g, dsr, 0.0), keepdims=True)
                upd = jnp.where(row == g, val, upd)
            dsink_ref[j] += upd

    d_spec = pl.BlockSpec((N_DHEAD_ROWS, BLOCK, HEAD_DIM), lambda n: (0, n, 0))
    return pl.pallas_call(
        body, name=name, grid=(nb,),
        in_specs=[q_spec, prev, cur, sink_spec, q_spec],
        out_specs=[d_spec, pl.BlockSpec((N_KV_HEADS, SUBLANE, LANE), lambda n: (0, 0, 0))],
        out_shape=[jax.ShapeDtypeStruct((N_DHEAD_ROWS, length, HEAD_DIM), F32),
                   jax.ShapeDtypeStruct((N_KV_HEADS, SUBLANE, LANE), F32)],
        compiler_params=_params("arbitrary"),
    )(heads, heads, heads, sinks, do)


def _to_heads(t, heads):
    return t.reshape(t.shape[0], heads, HEAD_DIM).transpose(1, 0, 2)


def _from_heads(t):
    return t.transpose(1, 0, 2).reshape(t.shape[1], t.shape[0] * HEAD_DIM)


def _shift_down(z, s):
    t = lax.broadcasted_iota(jnp.int32, z.shape, 0)
    return jnp.where(t >= s, pltpu.roll(z, s, 0), 0.0)


def _shift_up(z, s):
    t = lax.broadcasted_iota(jnp.int32, z.shape, 0)
    return jnp.where(t < z.shape[0] - s, pltpu.roll(z, z.shape[0] - s, 0), 0.0)


def _conv_specs(length):
    col = lambda c0: pl.BlockSpec((length, LANE), lambda j, c0=c0: (0, c0 // LANE + j))
    w_spec = pl.BlockSpec((CONV_K, LANE), lambda j: (0, j))
    o_spec = pl.BlockSpec((length, LANE), lambda j: (0, j))
    return col, w_spec, o_spec


def _conv_fwd(proj, w, name):
    length = proj.shape[0]
    col, w_spec, o_spec = _conv_specs(length)

    def body(cb_ref, cc_ref, cx_ref, w_ref, o_ref):
        z = cc_ref[...] * cx_ref[...]
        s = w_ref[0:1, :] * _shift_down(z, 2) + w_ref[1:2, :] * _shift_down(z, 1) + w_ref[2:3, :] * z
        o_ref[...] = (cb_ref[...] * s).astype(o_ref.dtype)

    return pl.pallas_call(
        body, name=name, grid=(CONV_WIDTH // LANE,),
        in_specs=[col(C_CB), col(C_CC), col(C_CX), w_spec], out_specs=o_spec,
        out_shape=jax.ShapeDtypeStruct((length, CONV_WIDTH), BF16),
        compiler_params=_params("arbitrary"),
    )(proj, proj, proj, w)


def _conv_bwd(proj, w, dy, name):
    length = proj.shape[0]
    col, w_spec, o_spec = _conv_specs(length)
    dw_spec = pl.BlockSpec((1, LANE), lambda j: (0, j))

    def body(cb_ref, cc_ref, cx_ref, w_ref, dy_ref, dcb_ref, dcc_ref, dcx_ref, dw0_ref, dw1_ref, dw2_ref):
        cc, cx, dyv = cc_ref[...], cx_ref[...], dy_ref[...]
        z = cc * cx
        w0, w1, w2 = w_ref[0:1, :], w_ref[1:2, :], w_ref[2:3, :]
        z1, z2 = _shift_down(z, 1), _shift_down(z, 2)
        s = w0 * z2 + w1 * z1 + w2 * z
        dcb_ref[...] = (dyv * s).astype(dcb_ref.dtype)
        ds = dyv * cb_ref[...]
        dw0_ref[...] = jnp.sum(ds * z2, axis=0, keepdims=True)
        dw1_ref[...] = jnp.sum(ds * z1, axis=0, keepdims=True)
        dw2_ref[...] = jnp.sum(ds * z, axis=0, keepdims=True)
        dz = w2 * ds + w1 * _shift_up(ds, 1) + w0 * _shift_up(ds, 2)
        dcc_ref[...] = (dz * cx).astype(dcc_ref.dtype)
        dcx_ref[...] = (dz * cc).astype(dcx_ref.dtype)

    act = jax.ShapeDtypeStruct((length, CONV_WIDTH), BF16)
    dws = jax.ShapeDtypeStruct((1, CONV_WIDTH), F32)
    return pl.pallas_call(
        body, name=name, grid=(CONV_WIDTH // LANE,),
        in_specs=[col(C_CB), col(C_CC), col(C_CX), w_spec, o_spec],
        out_specs=[o_spec, o_spec, o_spec, dw_spec, dw_spec, dw_spec],
        out_shape=[act, act, act, dws, dws, dws],
        compiler_params=_params("arbitrary"),
    )(proj, proj, proj, w, dy)


def _cmul(ar, ai, br, bi):
    return ar * br - ai * bi, ar * bi + ai * br


def _scan_tables(lr, li, reverse):
    pr, pi = [lr], [li]
    for _ in range(SCAN_ROWS - 1):
        nr, ni = _cmul(pr[-1], pi[-1], lr, li)
        pr.append(nr)
        pi.append(ni)
    row = jnp.arange(SCAN_ROWS)[:, None]
    mr, mi = [], []
    for s in (1, 2, 4):
        live = (row + s < SCAN_ROWS) if reverse else (row >= s)
        mr.append(jnp.where(live, pr[s - 1], 0.0))
        mi.append(jnp.where(live, pi[s - 1], 0.0))
    order = range(SCAN_ROWS - 1, -1, -1) if reverse else range(SCAN_ROWS)
    carry_r = jnp.concatenate([pr[d] for d in order], axis=0)
    carry_i = jnp.concatenate([pi[d] for d in order], axis=0)
    return jnp.stack(mr), jnp.stack(mi), carry_r, carry_i


N_SSM_CHUNKS = 4
CHUNK_STATES = SSM_LANES // N_SSM_CHUNKS
CHUNK_CHANNELS = SSM_WIDTH // N_SSM_CHUNKS


def _scan(b, lr, li, reverse, name, states=None):
    length = b.shape[0]
    mr, mi, cr, ci = _scan_tables(lr, li, reverse)
    nchunk = length // SCAN_CHUNK
    nblk = SCAN_CHUNK // SCAN_ROWS
    cw = CHUNK_STATES
    with_dlam = states is not None

    def body(*refs):
        if with_dlam:
            b_ref, mr_ref, mi_ref, cr_ref, ci_ref, s_ref, sp_ref, o_ref, dl_ref, carry = refs
        else:
            b_ref, mr_ref, mi_ref, cr_ref, ci_ref, o_ref, carry = refs
        step = pl.program_id(0)

        @pl.when(step == 0)
        def _():
            carry[...] = jnp.zeros_like(carry)
            if with_dlam:
                dl_ref[...] = jnp.zeros_like(dl_ref)

        blocks = range(nblk - 1, -1, -1) if reverse else range(nblk)
        for j in range(N_SSM_CHUNKS):
            re, im = slice(2 * cw * j, 2 * cw * j + cw), slice(2 * cw * j + cw, 2 * cw * (j + 1))
            tl = slice(cw * j, cw * (j + 1))
            c_r, c_i = carry[0:1, re], carry[0:1, im]
            acc_r = acc_i = jnp.zeros((SCAN_ROWS, cw), F32)
            for blk in blocks:
                r0 = blk * SCAN_ROWS
                xr = b_ref[r0:r0 + SCAN_ROWS, re]
                xi = b_ref[r0:r0 + SCAN_ROWS, im]
                for kk, s in enumerate((1, 2, 4)):
                    sh = SCAN_ROWS - s if reverse else s
                    rr, ri = pltpu.roll(xr, sh, 0), pltpu.roll(xi, sh, 0)
                    ar, ai = _cmul(mr_ref[kk, :, tl], mi_ref[kk, :, tl], rr, ri)
                    xr, xi = xr + ar, xi + ai
                ar, ai = _cmul(cr_ref[:, tl], ci_ref[:, tl], c_r, c_i)
                xr, xi = xr + ar, xi + ai
                o_ref[r0:r0 + SCAN_ROWS, re] = xr
                o_ref[r0:r0 + SCAN_ROWS, im] = xi
                edge = r0 if reverse else r0 + SCAN_ROWS - 1
                c_r = o_ref[edge:edge + 1, re]
                c_i = o_ref[edge:edge + 1, im]
                if with_dlam:
                    if r0 > 0:
                        pr, pi = s_ref[r0 - 1:r0 + SCAN_ROWS - 1, re], s_ref[r0 - 1:r0 + SCAN_ROWS - 1, im]
                    else:
                        live = (step < nchunk - 1).astype(F32)
                        row = lax.broadcasted_iota(jnp.int32, (SCAN_ROWS, cw), 0)
                        pr = jnp.where(row == 0, sp_ref[SCAN_ROWS - 1:SCAN_ROWS, re] * live,
                                       pltpu.roll(s_ref[0:SCAN_ROWS, re], 1, 0))
                        pi = jnp.where(row == 0, sp_ref[SCAN_ROWS - 1:SCAN_ROWS, im] * live,
                                       pltpu.roll(s_ref[0:SCAN_ROWS, im], 1, 0))
                    acc_r = acc_r + xr * pr + xi * pi
                    acc_i = acc_i + xi * pr - xr * pi
            carry[0:1, re] = c_r
            carry[0:1, im] = c_i
            if with_dlam:
                dl_ref[:, re] += acc_r
                dl_ref[:, im] += acc_i

    width = 2 * SSM_LANES
    chunk = (lambda i: (nchunk - 1 - i, 0)) if reverse else (lambda i: (i, 0))
    blk_spec = pl.BlockSpec((SCAN_CHUNK, width), chunk)
    m_spec = pl.BlockSpec((3, SCAN_ROWS, SSM_LANES), lambda i: (0, 0, 0))
    c_spec = pl.BlockSpec((SCAN_ROWS, SSM_LANES), lambda i: (0, 0))
    in_specs, args = [blk_spec, m_spec, m_spec, c_spec, c_spec], [b, mr, mi, cr, ci]
    out_specs, out_shape = blk_spec, jax.ShapeDtypeStruct(b.shape, F32)
    if with_dlam:
        assert reverse
        per = SCAN_CHUNK // SCAN_ROWS
        before = pl.BlockSpec((SCAN_ROWS, width), lambda i: (jnp.maximum((nchunk - 1 - i) * per - 1, 0), 0))
        in_specs += [blk_spec, before]
        args += [states, states]
        out_specs = [blk_spec, pl.BlockSpec((SCAN_ROWS, width), lambda i: (0, 0))]
        out_shape = [out_shape, jax.ShapeDtypeStruct((SCAN_ROWS, width), F32)]
    return pl.pallas_call(
        body, name=name, grid=(nchunk,), in_specs=in_specs, out_specs=out_specs, out_shape=out_shape,
        scratch_shapes=[pltpu.VMEM((SUBLANE, width), F32)],
        compiler_params=_params("arbitrary"),
    )(*args)


def _mm_bd(a, b, mode, name, out_dtype=F32, add=None):
    nc = N_SSM_CHUNKS
    if mode == "tn":
        k, wa, wb = a.shape[0], a.shape[1] // nc, b.shape[1] // nc
        tk = _tile(k, 1024, LANE)

        def body(a_ref, b_ref, o_ref):
            @pl.when(pl.program_id(1) == 0)
            def _():
                o_ref[...] = jnp.zeros_like(o_ref)

            o_ref[...] += lax.dot_general(a_ref[...].astype(BF16), b_ref[...].astype(BF16), _DOT_DIMS["tn"],
                                          preferred_element_type=F32)

        return pl.pallas_call(
            body, name=name, grid=(nc, k // tk),
            in_specs=[pl.BlockSpec((tk, wa), lambda j, kk: (kk, j)), pl.BlockSpec((tk, wb), lambda j, kk: (kk, j))],
            out_specs=pl.BlockSpec((None, wa, wb), lambda j, kk: (j, 0, 0)),
            out_shape=jax.ShapeDtypeStruct((nc, wa, wb), F32),
            compiler_params=_params("parallel", "arbitrary"),
        )(a, b)
    m, wa = a.shape[0], a.shape[1] // nc
    wo = b.shape[2] if mode == "nn" else b.shape[1]
    tm = _tile(m, 512, LANE)

    def body(a_ref, b_ref, *rest):
        r = lax.dot_general(a_ref[...].astype(BF16), b_ref[...].astype(BF16), _DOT_DIMS[mode],
                            preferred_element_type=F32)
        if add is not None:
            r = r + rest[0][...]
        rest[-1][...] = r.astype(out_dtype)

    o_spec = pl.BlockSpec((tm, wo), lambda i, j: (i, j))
    in_specs = [pl.BlockSpec((tm, wa), lambda i, j: (i, j)), pl.BlockSpec((None,) + b.shape[1:], lambda i, j: (j, 0, 0))]
    args = [a, b]
    if add is not None:
        in_specs.append(o_spec)
        args.append(add)
    return pl.pallas_call(
        body, name=name, grid=(m // tm, nc), in_specs=in_specs, out_specs=o_spec,
        out_shape=jax.ShapeDtypeStruct((m, nc * wo), out_dtype),
        compiler_params=_params("parallel", "parallel"),
    )(*args)


def _block_diag(t):
    g, a, b = t.shape
    per = g // N_SSM_CHUNKS
    eye = jnp.eye(per, dtype=t.dtype)
    t = t.reshape(N_SSM_CHUNKS, per, a, b)
    return (t[:, :, :, None, :] * eye[None, :, None, :, None]).reshape(N_SSM_CHUNKS, per * a, per * b)


def _ssm_prep(a_re, a_im, b_re, b_im, c_re, c_im, log_dt):
    dt = jnp.exp(log_dt)[:, None]
    er = jnp.exp(a_re * dt)
    lr, li = er * jnp.cos(a_im * dt), er * jnp.sin(a_im * dt)
    nr, ni = lr - 1.0, li
    den = a_re * a_re + a_im * a_im
    qr, qi = (nr * a_re + ni * a_im) / den, (ni * a_re - nr * a_im) / den
    bbr = qr[..., None] * b_re - qi[..., None] * b_im
    bbi = qr[..., None] * b_im + qi[..., None] * b_re
    bmat = jnp.concatenate([_block_diag(bbr.transpose(0, 2, 1)), _block_diag(bbi.transpose(0, 2, 1))], axis=2)
    cmat = jnp.concatenate([_block_diag(c_re.transpose(0, 2, 1)), -_block_diag(c_im.transpose(0, 2, 1))], axis=1)
    return lr.reshape(1, SSM_LANES), li.reshape(1, SSM_LANES), bmat, cmat


def _ssm_act(yc, u, d):
    return jax.nn.gelu(yc + d * u)


def _glu(ys, z):
    return ys * jax.nn.sigmoid(z)


def _merge(ya, yc, ys, gl, b):
    gates = jax.nn.sigmoid(gl + b)
    return gates[:, :D_MODEL] * ya + gates[:, D_MODEL:2 * D_MODEL] * yc + gates[:, 2 * D_MODEL:] * ys


def _swiglu(gt, up):
    return jax.nn.silu(gt) * up


def _mm_swiglu(a, b, layer, name):
    m, k = a.shape
    hid = b.shape[2] // 2
    tm, tn = _tile(m, 512, LANE), _tile(hid, 1536, LANE)
    nj = hid // tn

    def body(a_ref, bg_ref, bu_ref, g_ref, u_ref, act_ref):
        av = a_ref[...].astype(BF16)
        gt = jnp.dot(av, bg_ref[...].astype(BF16), preferred_element_type=F32)
        up = jnp.dot(av, bu_ref[...].astype(BF16), preferred_element_type=F32)
        g_ref[...] = gt
        u_ref[...] = up
        act_ref[...] = _swiglu(gt, up).astype(act_ref.dtype)

    o_spec = pl.BlockSpec((tm, tn), lambda i, j: (i, j))
    half = jax.ShapeDtypeStruct((m, hid), F32)
    return pl.pallas_call(
        body, name=name, grid=(m // tm, nj),
        in_specs=[pl.BlockSpec((tm, k), lambda i, j: (i, 0)),
                  pl.BlockSpec((None, k, tn), lambda i, j: (layer, 0, j)),
                  pl.BlockSpec((None, k, tn), lambda i, j: (layer, 0, j + nj))],
        out_specs=[o_spec, o_spec, o_spec], out_shape=[half, half, jax.ShapeDtypeStruct((m, hid), BF16)],
        compiler_params=_params("parallel", "parallel"),
    )(a, b, b)


def _mm_dswiglu(dy, b, layer, gt, up, name):
    m, n = dy.shape
    hid = b.shape[1]
    tm = _tile(m, 256, LANE)

    def body(dy_ref, b_ref, g_ref, u_ref, o_ref):
        dact = lax.dot_general(dy_ref[...].astype(BF16), b_ref[...].astype(BF16), _DOT_DIMS["nt"],
                               preferred_element_type=F32)
        _, vjp = jax.vjp(_swiglu, g_ref[...], u_ref[...])
        o_ref[...] = jnp.concatenate(vjp(dact), axis=1).astype(o_ref.dtype)

    row = lambda w: pl.BlockSpec((tm, w), lambda i: (i, 0))
    return pl.pallas_call(
        body, name=name, grid=(m // tm,),
        in_specs=[row(n), pl.BlockSpec((None, hid, n), lambda i: (layer, 0, 0)), row(hid), row(hid)],
        out_specs=row(2 * hid), out_shape=jax.ShapeDtypeStruct((m, 2 * hid), BF16),
        compiler_params=_params("parallel"),
    )(dy, b, gt, up)


def _loss_fn(x, g, t):
    e = _rms(x, g) - t
    per_tok = jnp.mean(e * e, axis=-1, keepdims=True)
    return 0.5 * jnp.sum(per_tok, axis=0, keepdims=True)


def _vjp_rowwise(f, n_row, cot_dtype=F32):
    def fn(ctx, *vals):
        prim = vals[:n_row] + vals[n_row + 1:]
        _, vjp = jax.vjp(f, *prim)
        return vjp(vals[n_row].astype(cot_dtype))
    return fn


def _layer_fwd(i, x, w, tabs):
    nm = lambda s: "l%d_%s" % (i, s)
    sv = {"x": x}
    h = _rms_fwd(x, w["norm_mix"][i:i + 1], nm("rms_mix"))
    proj = _mm(h, w["w_in"], "nn", nm("mm_in"), b_layer=i)
    qkv = _split_fwd(proj, tabs, nm("rope"))
    heads = _to_heads(qkv, N_HEAD_ROWS)
    sinks = w["attn_sinks"][i].reshape(N_KV_HEADS, Q_GROUP)
    att = _from_heads(_att_fwd(heads, sinks, nm("att")))
    conv = _conv_fwd(proj, w["conv_w"][i], nm("conv"))
    lr, li, bmat, cmat = w["ssm"][i]
    u = proj[:, C_U:C_G]
    bu = _mm_bd(u, bmat, "nn", nm("mm_bu"))
    states = _scan(bu, lr, li, False, nm("scan"))
    yc = _mm_bd(states, cmat, "nn", nm("mm_c"))
    d = w["ssm_d"][i:i + 1]
    ys = _rowwise(lambda ctx, a, b, c: (_ssm_act(a, b, c),), [(yc, 0, SSM_WIDTH, 0), (u, 0, SSM_WIDTH, 0)], [d],
                  [(SSM_WIDTH, F32)], [], nm("ssm_act"))[0]
    z = _mm(ys, w["w_ssm_glu"], "nn", nm("mm_glu"), b_layer=i)
    sg = _rowwise(lambda ctx, a, b: (_glu(a, b),), [(ys, 0, SSM_WIDTH, 0), (z, 0, SSM_WIDTH, 0)], [],
                  [(SSM_WIDTH, BF16)], [], nm("glu"))[0]
    ya = _mm(att, w["w_attn_o"], "nn", nm("mm_ao"), b_layer=i)
    yv = _mm(conv, w["w_conv_o"], "nn", nm("mm_co"), b_layer=i)
    ym = _mm(sg, w["w_ssm_o"], "nn", nm("mm_so"), b_layer=i)
    bg = w["b_gate"][i:i + 1]
    merged = _rowwise(lambda ctx, a, b, c, gl, bb: (_merge(a, b, c, gl, bb),),
                      [(ya, 0, D_MODEL, 0), (yv, 0, D_MODEL, 0), (ym, 0, D_MODEL, 0), (proj, C_G, GATE_WIDTH, 0)],
                      [bg], [(D_MODEL, BF16)], [], nm("merge"))[0]
    x1 = _mm(merged, w["w_mix_o"], "nn", nm("mm_mix"), add=x, b_layer=i)
    h2 = _rms_fwd(x1, w["norm_ffn"][i:i + 1], nm("rms_ffn"))
    gt, up, act = _mm_swiglu(h2, w["w_ffn_in"], i, nm("mm_ffn_in"))
    x2 = _mm(act, w["w_ffn_out"], "nn", nm("mm_ffn_out"), add=x1, b_layer=i, tk_cap=3072)
    sv.update(h=h, proj=proj, heads=heads, att=att, conv=conv, u=u, states=states, yc=yc, ys=ys, z=z, sg=sg,
              ya=ya, yv=yv, ym=ym, merged=merged, x1=x1, h2=h2, gt=gt, up=up, act=act)
    return x2, sv


def _layer_bwd(i, dx2, sv, w, tabs, gb):
    nm = lambda s: "l%d_b_%s" % (i, s)
    g = {}

    def wgrad(n, lhs, rhs, label, **kw):
        if n in gb:
            gb[n] = _mm(lhs, rhs, "tn", nm(label), into=(gb[n], i, BIG_AXIS[n]), **kw)
        else:
            g[n] = _mm(lhs, rhs, "tn", nm(label), **kw)

    wgrad("w_ffn_out", sv["act"], dx2, "mm_gw_ffn_out", tn_cap=512)
    dgu = _mm_dswiglu(dx2, w["w_ffn_out"], i, sv["gt"], sv["up"], nm("mm_dswiglu"))
    dh2 = _mm(dgu, w["w_ffn_in"], "nt", nm("mm_dh2"), tk_cap=3072, b_layer=i)
    wgrad("w_ffn_in", sv["h2"], dgu, "mm_gw_ffn_in", tm_cap=1024)
    dx1, g["norm_ffn"] = _rms_bwd(sv["x1"], w["norm_ffn"][i:i + 1], dh2, dx2, nm("rms_ffn"))
    dmerged = _mm(dx1, w["w_mix_o"], "nt", nm("mm_dmerged"), b_layer=i)
    wgrad("w_mix_o", sv["merged"], dx1, "mm_gw_mix", tn_cap=512)
    proj = sv["proj"]
    bg = w["b_gate"][i:i + 1]
    dya, dyv, dym, dgl, g["b_gate"] = _rowwise(
        _vjp_rowwise(_merge, 4),
        [(sv["ya"], 0, D_MODEL, 0), (sv["yv"], 0, D_MODEL, 0), (sv["ym"], 0, D_MODEL, 0),
         (proj, C_G, GATE_WIDTH, 0), (dmerged, 0, D_MODEL, 0)], [bg],
        [(D_MODEL, BF16), (D_MODEL, BF16), (D_MODEL, BF16), (GATE_WIDTH, BF16)], [(1, GATE_WIDTH)], nm("merge"))
    dsg = _mm(dym, w["w_ssm_o"], "nt", nm("mm_dsg"), b_layer=i)
    wgrad("w_ssm_o", sv["sg"], dym, "mm_gw_so")
    dys0, dz = _rowwise(_vjp_rowwise(_glu, 2), [(sv["ys"], 0, SSM_WIDTH, 0), (sv["z"], 0, SSM_WIDTH, 0),
                                                 (dsg, 0, SSM_WIDTH, 0)], [],
                        [(SSM_WIDTH, F32), (SSM_WIDTH, BF16)], [], nm("glu"))
    dys = _mm(dz, w["w_ssm_glu"], "nt", nm("mm_dys"), add=dys0, b_layer=i)
    wgrad("w_ssm_glu", sv["ys"], dz, "mm_gw_glu")
    d = w["ssm_d"][i:i + 1]
    dyc, du0, g["ssm_d"] = _rowwise(
        _vjp_rowwise(_ssm_act, 2), [(sv["yc"], 0, SSM_WIDTH, 0), (sv["u"], 0, SSM_WIDTH, 0), (dys, 0, SSM_WIDTH, 0)],
        [d], [(SSM_WIDTH, F32), (SSM_WIDTH, F32)], [(1, SSM_WIDTH)], nm("ssm_act"))
    lr, li, bmat, cmat = w["ssm"][i]
    dstates = _mm_bd(dyc, cmat, "nt", nm("mm_dstates"))
    g_cmat = _mm_bd(sv["states"], dyc, "tn", nm("mm_gc"))
    gs, dl = _scan(dstates, lr, -li, True, nm("scan"), states=sv["states"])
    g_lam = jnp.sum(dl, axis=0).reshape(N_SSM_CHUNKS, 2, CHUNK_STATES)
    du = _mm_bd(gs, bmat, "nt", nm("mm_du"), out_dtype=BF16, add=du0)
    g_bmat = _mm_bd(sv["u"], gs, "tn", nm("mm_gb"))
    g["ssm"] = (g_lam[:, 0].reshape(1, SSM_LANES), g_lam[:, 1].reshape(1, SSM_LANES), g_bmat, g_cmat)
    dconv = _mm(dyv, w["w_conv_o"], "nt", nm("mm_dconv"), b_layer=i)
    wgrad("w_conv_o", sv["conv"], dyv, "mm_gw_co")
    dcb, dcc, dcx, dw0, dw1, dw2 = _conv_bwd(proj, w["conv_w"][i], dconv, nm("conv"))
    g["conv_w"] = jnp.concatenate([dw0, dw1, dw2], axis=0)
    datt = _mm(dya, w["w_attn_o"], "nt", nm("mm_datt"), b_layer=i)
    wgrad("w_attn_o", sv["att"], dya, "mm_gw_ao")
    sinks = w["attn_sinks"][i].reshape(N_KV_HEADS, Q_GROUP)
    dheads, dsk = _att_bwd(sv["heads"], sinks, _to_heads(datt, N_Q_HEADS), nm("att"))
    g["attn_sinks"] = dsk[:, :Q_GROUP, 0].reshape(N_Q_HEADS)
    dqkv = _split_bwd(_from_heads(dheads), tabs, nm("rope"))
    dproj = jnp.concatenate([dqkv, dcb, dcc, dcx, du, dgl], axis=1)
    dh = _mm(dproj, w["w_in"], "nt", nm("mm_dh"), tk_cap=3072, b_layer=i)
    dproj_shards = dproj.reshape(dproj.shape[0], N_CHIPS, IN_COLS // N_CHIPS).transpose(1, 0, 2)
    wgrad("w_in", sv["h"], dproj_shards, "mm_gw_in", b_shards=True, tm_cap=1024)
    dx, g["norm_mix"] = _rms_bwd(sv["x"], w["norm_mix"][i:i + 1], dh, dx1, nm("rms_mix"))
    return dx, g


def _local_step(x, target, w):
    length = x.shape[0]
    tabs = _rope_tables(length)
    ssm_names = ("ssm_a_re", "ssm_a_im", "ssm_b_re", "ssm_b_im", "ssm_c_re", "ssm_c_im", "ssm_log_dt")
    w = dict(w)
    preps = [jax.vjp(_ssm_prep, *[w[n][i] for n in ssm_names]) for i in range(DEPTH)]
    w["ssm"] = [p[0] for p in preps]
    saved = []
    for i in range(DEPTH):
        x, sv = _layer_fwd(i, x, w, tabs)
        saved.append(sv)

    def loss_fn(ctx, xv, tv, gv):
        val, vjp = jax.vjp(_loss_fn, xv, gv, tv)
        dx, dg, _ = vjp(jnp.ones((1, 1), F32))
        return dx, dg, val + jnp.zeros((1, LANE), F32)

    gfin = w["norm_final"].reshape(1, D_MODEL)
    dx, g_final, loss = _rowwise(loss_fn, [(x, 0, D_MODEL, 0), (target, 0, D_MODEL, 0)], [gfin],
                                 [(D_MODEL, F32)], [(1, D_MODEL), (1, LANE)], "loss")
    gb = {}
    for n in BIG:
        depth, ra, cb = w[n].shape
        gb[n] = lax.empty((N_CHIPS, depth) + ((ra // N_CHIPS, cb) if BIG_AXIS[n] == 1 else (ra, cb // N_CHIPS)), F32)
    layer_grads = [None] * DEPTH
    for i in reversed(range(DEPTH)):
        dx, layer_grads[i] = _layer_bwd(i, dx, saved[i], w, tabs, gb)
    grads = dict(gb)
    for n in layer_grads[0]:
        if n != "ssm":
            grads[n] = jnp.stack([lg[n] for lg in layer_grads])
    ssm_g = [preps[i][1](layer_grads[i]["ssm"]) for i in range(DEPTH)]
    for j, n in enumerate(ssm_names):
        grads[n] = jnp.stack([sg[j] for sg in ssm_g])
    grads["norm_final"] = g_final.reshape(D_MODEL)
    for n in ("norm_mix", "norm_ffn", "b_gate", "ssm_d"):
        grads[n] = grads[n].reshape(grads[n].shape[0], -1)
    return loss, dx, grads


COLS = 1024
ANY_SPEC = pl.BlockSpec(memory_space=pl.ANY)


def _place():
    return lax.axis_index("x"), lax.axis_index("y"), lax.axis_index("c")


def _other_chips(x, y):
    return [(1 - x, y), (x, 1 - y), (1 - x, 1 - y)]


def _remote(src, dst, send_sems, recv_sems, k, to):
    return pltpu.make_async_remote_copy(src_ref=src, dst_ref=dst, send_sem=send_sems.at[k], recv_sem=recv_sems.at[k],
                                        device_id=to, device_id_type=MESH_ID)


def _comm_call(body, name, out_shape, n_sems, n_local, args):
    return pl.pallas_call(
        body, name=name, out_shape=out_shape, in_specs=[ANY_SPEC] * len(args),
        out_specs=[ANY_SPEC] * len(out_shape),
        scratch_shapes=[pltpu.SemaphoreType.DMA((n_sems,)), pltpu.SemaphoreType.DMA((n_sems,)),
                        pltpu.SemaphoreType.DMA((n_local,))],
    )(*args)


def _gather_weights(shards, name):
    nt = len(shards)
    per = 9
    hds = [s.shape[0] // 2 for s in shards]

    def body(*refs):
        srcs, outs = refs[:nt], refs[nt:2 * nt]
        send_sems, recv_sems, _ = refs[2 * nt:]
        x, y, c = _place()
        me, sibling = (x, y, c), (x, y, 1 - c)
        xn, yn, dg = _other_chips(x, y)

        def blk(t, chip, hc, part=None):
            lo, n = hc * hds[t], hds[t]
            if part is not None:
                first_n = (n + 1) // 2
                lo, n = (lo, first_n) if part == 0 else (lo + first_n, n - first_n)
            return outs[t].at[2 * chip[0] + chip[1], pl.ds(lo, n)] if n else None

        def copy(t, k, src, dst, to):
            return _remote(src, dst, send_sems, recv_sems, per * t + k, to)

        def arrived(t, k, ref):
            copy(t, k, ref, ref, me).wait_recv()

        sent = []

        def start(t, k, ref, to):
            if ref is not None:
                sent.append(copy(t, k, ref, ref, to))
                sent[-1].start()

        for t in range(nt):
            own = srcs[t].at[pl.ds(c * hds[t], hds[t])]
            for k, chip in enumerate((xn, yn)):
                sent.append(copy(t, k, own, blk(t, (x, y), c), (*chip, c)))
                sent[-1].start()
        for t in range(nt):
            sent.append(copy(t, 8, srcs[t], outs[t].at[2 * x + y], sibling))
            sent[-1].start()
        for t in range(nt):
            arrived(t, 0, blk(t, xn, c))
            start(t, 3, blk(t, xn, c, 1), (*yn, c))
            start(t, 4, blk(t, xn, c), sibling)
            arrived(t, 1, blk(t, yn, c))
            start(t, 2, blk(t, yn, c, 0), (*xn, c))
            start(t, 5, blk(t, yn, c), sibling)
        for t in range(nt):
            arrived(t, 2, blk(t, dg, c, 0))
            start(t, 6, blk(t, dg, c, 0), sibling)
            if blk(t, dg, c, 1) is not None:
                arrived(t, 3, blk(t, dg, c, 1))
                start(t, 7, blk(t, dg, c, 1), sibling)
        for t in range(nt):
            arrived(t, 4, blk(t, xn, 1 - c))
            arrived(t, 5, blk(t, yn, 1 - c))
            arrived(t, 6, blk(t, dg, 1 - c, 0))
            if blk(t, dg, 1 - c, 1) is not None:
                arrived(t, 7, blk(t, dg, 1 - c, 1))
            arrived(t, 8, outs[t].at[2 * x + y])
        for cp in sent:
            cp.wait_send()

    out_shape = [jax.ShapeDtypeStruct((N_CHIPS,) + s.shape, s.dtype) for s in shards]
    return _comm_call(body, name, out_shape, per * nt, 1, shards)


def _swap_halves(gs, name):
    nt = len(gs)
    hds = [g.shape[1] // 2 for g in gs]

    def body(*refs):
        srcs, outs = refs[:nt], refs[nt:2 * nt]
        send_sems, recv_sems, _ = refs[2 * nt:]
        x, y, c = _place()
        cps = [_remote(srcs[t].at[s, pl.ds((1 - c) * hds[t], hds[t])], outs[t].at[s], send_sems, recv_sems,
                       N_CHIPS * t + s, (x, y, 1 - c)) for t in range(nt) for s in range(N_CHIPS)]
        for cp in cps:
            cp.start()
        for cp in cps:
            cp.wait()

    out_shape = [jax.ShapeDtypeStruct((N_CHIPS, g.shape[1] // 2) + g.shape[2:], g.dtype) for g in gs]
    return _comm_call(body, name, out_shape, N_CHIPS * nt, 1, gs)


def _exchange_shards(parts, name, swap=()):
    nt, ns = len(parts), len(swap)
    hds = [g.shape[1] // 2 for g in swap]

    def body(*refs):
        srcs, swap_srcs = refs[:nt], refs[nt:nt + ns]
        outs, swap_outs = refs[nt + ns:2 * nt + ns], refs[2 * nt + ns:2 * (nt + ns)]
        send_sems, recv_sems, _ = refs[2 * (nt + ns):]
        x, y, c = _place()
        cps = [_remote(srcs[t].at[2 * chip[0] + chip[1]], outs[t].at[j], send_sems, recv_sems, 3 * t + j, (*chip, c))
               for t in range(nt) for j, chip in enumerate(_other_chips(x, y))]
        cps += [_remote(swap_srcs[t].at[s, pl.ds((1 - c) * hds[t], hds[t])], swap_outs[t].at[s], send_sems, recv_sems,
                        3 * nt + N_CHIPS * t + s, (x, y, 1 - c)) for t in range(ns) for s in range(N_CHIPS)]
        for cp in cps:
            cp.start()
        for cp in cps:
            cp.wait()

    out_shape = [jax.ShapeDtypeStruct((N_CHIPS - 1,) + p.shape[1:], p.dtype) for p in parts]
    out_shape += [jax.ShapeDtypeStruct((N_CHIPS, g.shape[1] // 2) + g.shape[2:], g.dtype) for g in swap]
    res = _comm_call(body, name, out_shape, 3 * nt + N_CHIPS * ns, 1, list(parts) + list(swap))
    return res[:nt], res[nt:]


def _join_halves(reds, name):
    nt = len(reds)
    hds = [r.shape[1] for r in reds]

    def body(*refs):
        srcs, outs = refs[:nt], refs[nt:2 * nt]
        send_sems, recv_sems, _ = refs[2 * nt:]
        x, y, c = _place()
        cps = [_remote(srcs[t].at[0], outs[t].at[pl.ds(c * hds[t], hds[t])], send_sems, recv_sems, t, (x, y, 1 - c))
               for t in range(nt)]
        for cp in cps:
            cp.start()
        for t in range(nt):
            _remote(srcs[t].at[0], outs[t].at[pl.ds((1 - c) * hds[t], hds[t])], send_sems, recv_sems, t,
                    (x, y, c)).wait_recv()
        for cp in cps:
            cp.wait_send()

    out_shape = [jax.ShapeDtypeStruct((2 * r.shape[1],) + r.shape[2:], r.dtype) for r in reds]
    return _comm_call(body, name, out_shape, nt, 1, reds)


SUM_BLOCK_ELEMS = 768 * 1024
ADAMW_BLOCK_ELEMS = 512 * 1024


def _sum_windows(parts, lead, name, out_dtypes=(F32,)):
    a, b = parts[0][0].shape[2:]
    ta = _tile(a, max(SUBLANE, SUM_BLOCK_ELEMS // b), 2 * SUBLANE)
    offs = jnp.stack([jnp.stack([jnp.asarray(o, jnp.int32) for o in off]) for _, off in parts])
    n_in = len(parts)

    def body(off_ref, *refs):
        acc = refs[0][...].astype(F32)
        for r in refs[1:n_in]:
            acc = acc + r[...].astype(F32)
        for r in refs[n_in:]:
            r[...] = acc.astype(r.dtype)

    in_specs = [pl.BlockSpec((1, 1, ta, b), lambda p, q, i, off, k=k: (off[k, 0] + p, off[k, 1] + q, i, 0))
                for k in range(n_in)]
    o_spec = pl.BlockSpec((1, 1, ta, b), lambda p, q, i, off: (p, q, i, 0))
    return pl.pallas_call(
        body, name=name, out_shape=[jax.ShapeDtypeStruct(tuple(lead) + (a, b), dt) for dt in out_dtypes],
        grid_spec=pltpu.PrefetchScalarGridSpec(
            num_scalar_prefetch=1, grid=tuple(lead) + (a // ta,), in_specs=in_specs,
            out_specs=[o_spec] * len(out_dtypes)),
        compiler_params=_params("arbitrary", "arbitrary", "arbitrary"),
    )(offs, *[arr for arr, _ in parts])


def _reduce_scatter(gs, names, wire, groups):
    x, y, c = _place()
    theirs = dict(zip(groups[0], _swap_halves([gs[i] for i in groups[0]], "rs_swap_halves")))
    pairs, others = {}, {}
    for k, group in enumerate(groups):
        for i in group:
            g = gs[i]
            pairs[i] = _sum_windows([(g, (0, c * (g.shape[1] // 2))), (theirs[i], (0, 0))],
                                    (N_CHIPS, g.shape[1] // 2), "rs_sum_pair_" + names[i], (F32, wire[i]))
        nxt = groups[k + 1] if k + 1 < len(groups) else []
        got, swapped = _exchange_shards([pairs[i][1] for i in group], "rs_exchange_%d" % k, [gs[i] for i in nxt])
        others.update(zip(group, got))
        theirs.update(zip(nxt, swapped))
    pairs, others = [pairs[i] for i in range(len(gs))], [others[i] for i in range(len(gs))]
    reds = [_sum_windows([(p[0], (2 * x + y, 0))] + [(o, (j, 0)) for j in range(N_CHIPS - 1)], (1, p[0].shape[1]),
                         "rs_sum_chips_" + n)[0] for p, o, n in zip(pairs, others, names)]
    joined = _join_halves(reds, "rs_join")
    return [lax.dynamic_update_slice_in_dim(j, r[0], c * r.shape[1], axis=0) for j, r in zip(joined, reds)]


def _adamw(wt, g, m, v, name):
    cols = wt.shape[-1]
    r = wt.size // cols
    tr = _tile(r, max(SUBLANE, ADAMW_BLOCK_ELEMS // max(cols, LANE)), SUBLANE)

    def body(w_ref, g_ref, m_ref, v_ref, d_ref, nm_ref, nv_ref):
        gv = g_ref[...]
        mn = ADAM_B1 * m_ref[...] + (1.0 - ADAM_B1) * gv
        vn = ADAM_B2 * v_ref[...] + (1.0 - ADAM_B2) * jnp.square(gv)
        m_hat = mn / (1.0 - ADAM_B1 ** ADAM_STEP)
        v_hat = vn / (1.0 - ADAM_B2 ** ADAM_STEP)
        d_ref[...] = -ADAM_LR * (m_hat / (jnp.sqrt(v_hat) + ADAM_EPS) + ADAM_WD * w_ref[...])
        nm_ref[...] = mn
        nv_ref[...] = vn

    spec = pl.BlockSpec((tr, cols), lambda i: (i, 0))
    shp = jax.ShapeDtypeStruct((r, cols), F32)
    res = pl.pallas_call(
        body, name=name, grid=(r // tr,), in_specs=[spec] * 4, out_specs=[spec] * 3, out_shape=[shp] * 3,
        compiler_params=_params("parallel"),
    )(*[t.reshape(r, cols) for t in (wt, g, m, v)])
    return [t.reshape(wt.shape) for t in res]


def _join_shards(n, piece):
    _, depth, a, b = piece.shape
    if BIG_AXIS[n] == 2:
        return piece.transpose(1, 2, 0, 3).reshape(depth, a, N_CHIPS * b)
    return piece.transpose(1, 0, 2, 3).reshape(depth, N_CHIPS * a, b)


RS_GROUPS = (("w_ffn_out", "w_mix_o", "small"), ("w_ffn_in",), ("w_in", "w_attn_o", "w_conv_o", "w_ssm_glu", "w_ssm_o"))


SMALL_PART_ROWS = 2 * SUBLANE


def _rows_of(t):
    return -(-t.size // COLS)


def _pack_small(ts):
    rows = [jnp.pad(t.reshape(-1), (0, _rows_of(t) * COLS - t.size)).reshape(-1, COLS) for t in ts]
    total = sum(r.shape[0] for r in rows)
    part = -(-total // (N_DEV * SMALL_PART_ROWS)) * SMALL_PART_ROWS
    rows.append(jnp.zeros((N_DEV * part - total, COLS), F32))
    return jnp.concatenate(rows, axis=0).reshape(N_CHIPS, 2, part, COLS)


def _unpack_small(buf, like):
    buf, out, r0 = buf.reshape(-1, COLS), [], 0
    for t in like:
        out.append(buf[r0:r0 + _rows_of(t)].reshape(-1)[:t.size].reshape(t.shape))
        r0 += _rows_of(t)
    return out


def kernel(x, norm_mix, w_in, b_gate, attn_sinks, w_attn_o, conv_w, w_conv_o, ssm_a_re, ssm_a_im, ssm_b_re, ssm_b_im, ssm_c_re, ssm_c_im, ssm_d, ssm_log_dt, w_ssm_glu, w_ssm_o, w_mix_o, norm_ffn, w_ffn_in, w_ffn_out, norm_final, loss_target, m_norm_mix, m_w_in, m_b_gate, m_attn_sinks, m_w_attn_o, m_conv_w, m_w_conv_o, m_ssm_a_re, m_ssm_a_im, m_ssm_b_re, m_ssm_b_im, m_ssm_c_re, m_ssm_c_im, m_ssm_d, m_ssm_log_dt, m_w_ssm_glu, m_w_ssm_o, m_w_mix_o, m_norm_ffn, m_w_ffn_in, m_w_ffn_out, m_norm_final, v_norm_mix, v_w_in, v_b_gate, v_attn_sinks, v_w_attn_o, v_conv_w, v_w_conv_o, v_ssm_a_re, v_ssm_a_im, v_ssm_b_re, v_ssm_b_im, v_ssm_c_re, v_ssm_c_im, v_ssm_d, v_ssm_log_dt, v_w_ssm_glu, v_w_ssm_o, v_w_mix_o, v_norm_ffn, v_w_ffn_in, v_w_ffn_out, v_norm_final):
    a = dict(zip(ARG_NAMES, (
        x, norm_mix, w_in, b_gate, attn_sinks, w_attn_o, conv_w, w_conv_o, ssm_a_re, ssm_a_im, ssm_b_re, ssm_b_im,
        ssm_c_re, ssm_c_im, ssm_d, ssm_log_dt, w_ssm_glu, w_ssm_o, w_mix_o, norm_ffn, w_ffn_in, w_ffn_out, norm_final,
        loss_target, m_norm_mix, m_w_in, m_b_gate, m_attn_sinks, m_w_attn_o, m_conv_w, m_w_conv_o, m_ssm_a_re,
        m_ssm_a_im, m_ssm_b_re, m_ssm_b_im, m_ssm_c_re, m_ssm_c_im, m_ssm_d, m_ssm_log_dt, m_w_ssm_glu, m_w_ssm_o,
        m_w_mix_o, m_norm_ffn, m_w_ffn_in, m_w_ffn_out, m_norm_final, v_norm_mix, v_w_in, v_b_gate, v_attn_sinks,
        v_w_attn_o, v_conv_w, v_w_conv_o, v_ssm_a_re, v_ssm_a_im, v_ssm_b_re, v_ssm_b_im, v_ssm_c_re, v_ssm_c_im,
        v_ssm_d, v_ssm_log_dt, v_w_ssm_glu, v_w_ssm_o, v_w_mix_o, v_norm_ffn, v_w_ffn_in, v_w_ffn_out, v_norm_final)))
    px, py, _ = _place()
    chip = 2 * px + py

    gathered = BIG + ("conv_w",)
    own = [a[n].astype(BF16) for n in BIG] + [a["conv_w"]]
    gath = _gather_weights(own, "gather_weights")
    w = {n: _join_shards(n, p) for n, p in zip(gathered, gath)}
    for n in SMALL:
        w[n] = a[n]

    loss, dx, grads = _local_step(a["x"][0], a["loss_target"][0], w)

    small_names = SMALL + ("conv_w",)
    small = _pack_small([grads[n] for n in small_names])
    rs_names = BIG + ("small",)
    groups = [[rs_names.index(n) for n in grp] for grp in RS_GROUPS]
    rs = _reduce_scatter([grads[n] for n in BIG] + [small], rs_names, (BF16,) * len(BIG) + (F32,), groups)
    red = dict(zip(BIG, rs))
    small_all = _gather_weights([rs[-1]], "gather_small")[0]
    red.update(zip(small_names, _unpack_small(small_all, [grads[n] for n in small_names])))
    lane = a["conv_w"].shape[2]
    red["conv_w"] = lax.dynamic_slice_in_dim(red["conv_w"], chip * lane, lane, axis=2)

    loss_all = lax.psum(loss[0, 0], ("x", "y", "c"))
    deltas, new_m, new_v = [], [], []
    for n in WEIGHTS:
        d, mn, vn = _adamw(a[n], red[n], a["m_" + n], a["v_" + n], "adamw_" + n)
        deltas.append(d)
        new_m.append(mn)
        new_v.append(vn)
    return (loss_all, dx[None], *[red[n] for n in WEIGHTS], *deltas, *new_m, *new_v)
```
